```python
import jax, jax.numpy as jnp
from jax import lax
import numpy as np


D_MODEL = 1024
BATCH = 8
SEQ = 4096
DEPTH = 1

HEAD_DIM = 64
ATTN_HEADS = 8
RWKV_HEADS = 8
ATTN_WIDTH = ATTN_HEADS * HEAD_DIM
RWKV_WIDTH = RWKV_HEADS * HEAD_DIM
MIX_WIDTH = ATTN_WIDTH + RWKV_WIDTH
ATTN_PATTERNS = ((128, 1), (512, 4), (2048, 16))
Q_BLOCK = 128
ROPE_THETA = 10000.0
DECAY_LORA = 64
AAA_LORA = 64
GATE_LORA = 160
RWKV_FEATS = 3 * RWKV_WIDTH + DECAY_LORA + AAA_LORA + GATE_LORA
IN_WIDTH = 3 * ATTN_WIDTH + RWKV_FEATS
GN_EPS = 64e-5
LN_EPS = 1e-5
N_EXPERTS = 256
TOP_K = 8
N_GROUPS = 8
TOPK_GROUPS = 4
ROUTED_SCALE = 2.5
EXPERT_FF = 256
SHARED_FF = 256
EXPERT_BLOCK = 128
ALPHA = (2.0 * DEPTH) ** 0.25
BETA = (8.0 * DEPTH) ** -0.25

kernel_name = 'hybrid_dilated_rwkv7_moe_deepnorm'


def layer_norm(x, g, b):
    xf = x.astype(jnp.float32)
    mu = xf.mean(-1, keepdims=True)
    var = jnp.square(xf - mu).mean(-1, keepdims=True)
    y = (xf - mu) * lax.rsqrt(var + LN_EPS) * g.astype(jnp.float32) + b.astype(jnp.float32)
    return y.astype(x.dtype)


def rope(z, pos):
    half = HEAD_DIM // 2
    inv_freq = ROPE_THETA ** (-jnp.arange(half, dtype=jnp.float32) * 2.0 / HEAD_DIM)
    ang = pos[:, None] * inv_freq[None, :]
    cos = jnp.cos(ang)[None, :, None, :]
    sin = jnp.sin(ang)[None, :, None, :]
    z = z.astype(jnp.float32)
    z1, z2 = z[..., :half], z[..., half:]
    return jnp.concatenate([z1 * cos - z2 * sin, z2 * cos + z1 * sin], axis=-1)


def dilated_window_attention(q, k, v, dilation, n_back):
    B, S, H, Dh = q.shape
    L = S // dilation
    nb = -(-L // Q_BLOCK)
    Lp = nb * Q_BLOCK

    def split(z):
        z = z.reshape(B, L, dilation, H, Dh).transpose(0, 2, 3, 1, 4)
        z = jnp.pad(z, ((0, 0), (0, 0), (0, 0), (0, Lp - L), (0, 0)))
        return z.reshape(B, dilation, H, nb, Q_BLOCK, Dh)

    def with_prev(z):
        prev = jnp.pad(z, ((0, 0), (0, 0), (0, 0), (1, 0), (0, 0), (0, 0)))[:, :, :, :-1]
        return jnp.concatenate([prev, z], axis=4)

    qb = split(q)
    kc = with_prev(split(k))
    vc = with_prev(split(v))
    s = jnp.einsum('brhnqc,brhnkc->brhnqk', qb, kc) * (Dh ** -0.5)
    qi = jnp.arange(Q_BLOCK)[:, None]
    ki = jnp.arange(2 * Q_BLOCK)[None, :]
    dist = Q_BLOCK + qi - ki
    blk = jnp.arange(nb)[:, None, None]
    valid = (dist >= 0) & (dist <= n_back) & (blk * Q_BLOCK + ki - Q_BLOCK >= 0)
    s = jnp.where(valid, s, -jnp.inf)
    m = s.max(-1, keepdims=True)
    p = jnp.exp(s - m)
    l = p.sum(-1, keepdims=True)
    o = jnp.einsum('brhnqk,brhnkc->brhnqc', p, vc) / l
    lse = m + jnp.log(l)

    def merge(z):
        z = z.reshape(B, dilation, H, Lp, -1)[:, :, :, :L]
        return z.transpose(0, 3, 1, 2, 4).reshape(B, S, H, -1)

    return merge(o), merge(lse)[..., 0]


def rwkv7_time_mix(feats, mu, w0, w_decay_up, a0, w_aaa_up, w_gate_up, k_k, k_a, r_k, lnx_g, lnx_b):
    B, S, _ = feats.shape
    H, N, C = RWKV_HEADS, HEAD_DIM, RWKV_WIDTH
    f = feats.astype(jnp.float32)
    prev = jnp.pad(f, ((0, 0), (1, 0), (0, 0)))[:, :-1]
    f = f + (prev - f) * mu
    r, k, v, wl, al, gl = jnp.split(
        f, [C, 2 * C, 3 * C, 3 * C + DECAY_LORA, 3 * C + DECAY_LORA + AAA_LORA], axis=-1)
    w = -jax.nn.softplus(-(w0 + jnp.tanh(wl) @ w_decay_up)) - 0.5
    decay = jnp.exp(-jnp.exp(w))
    a = jax.nn.sigmoid(a0 + al @ w_aaa_up)
    g = jax.nn.sigmoid(gl) @ w_gate_up

    def heads(z):
        return z.reshape(B, S, H, N)

    kk = heads(k * k_k)
    kk = kk / jnp.maximum(jnp.linalg.norm(kk, axis=-1, keepdims=True), 1e-12)
    k = k * (1.0 + (a - 1.0) * k_a)
    r_h, k_h, v_h, a_h = heads(r), heads(k), heads(v), heads(a)

    def step(state, inp):
        r_t, w_t, k_t, v_t, a_t, b_t = inp
        sa = jnp.einsum('bhvk,bhk->bhv', state, a_t)
        state = (state * w_t[:, :, None, :] + sa[..., None] * b_t[:, :, None, :]
                 + v_t[..., None] * k_t[:, :, None, :])
        return state, jnp.einsum('bhvk,bhk->bhv', state, r_t)

    seq = [z.transpose(1, 0, 2, 3) for z in (r_h, heads(decay), k_h, v_h, -kk, kk * a_h)]
    state0 = jnp.zeros((B, H, N, N), jnp.float32)
    _, y = lax.scan(step, state0, (seq[0], seq[1], seq[2], seq[3], seq[4], seq[5]))
    y = y.transpose(1, 0, 2, 3)
    ym = y.mean(-1, keepdims=True)
    yv = jnp.square(y - ym).mean(-1, keepdims=True)
    y = ((y - ym) * lax.rsqrt(yv + GN_EPS)).reshape(B, S, C) * lnx_g + lnx_b
    bonus = (r_h * k_h * r_k).sum(-1, keepdims=True) * v_h
    return (y + bonus.reshape(B, S, C)) * g


def hybrid_mixer(x, w_in, mu_shift, w0, w_decay_up, a0, w_aaa_up, w_gate_up, k_k, k_a, r_k,
                 lnx_g, lnx_b, w_out):
    B, S, _ = x.shape
    h = x @ w_in
    q, k, v, feats = jnp.split(h, [ATTN_WIDTH, 2 * ATTN_WIDTH, 3 * ATTN_WIDTH], axis=-1)
    pos = jnp.arange(S, dtype=jnp.float32)

    def heads(z):
        return z.reshape(B, S, ATTN_HEADS, HEAD_DIM)

    q = rope(heads(q), pos)
    k = rope(heads(k), pos)
    v = heads(v).astype(jnp.float32)
    outs, lses = [], []
    for window, dilation in ATTN_PATTERNS:
        o, lse = dilated_window_attention(q, k, v, dilation, window // dilation)
        outs.append(o)
        lses.append(lse)
    wts = jax.nn.softmax(jnp.stack(lses), axis=0)
    attn = jnp.sum(wts[..., None] * jnp.stack(outs), axis=0).reshape(B, S, ATTN_WIDTH)
    rw = rwkv7_time_mix(feats, mu_shift, w0, w_decay_up, a0, w_aaa_up, w_gate_up,
                        k_k, k_a, r_k, lnx_g, lnx_b)
    mixed = jnp.concatenate([attn, rw], axis=-1).astype(x.dtype)
    return mixed @ w_out


def moe_ffn(x, router_w, router_bias, e_gate, e_up, e_down, s_gate, s_up, s_down):
    B, S, D = x.shape
    T = B * S
    E = N_EXPERTS
    xt = x.reshape(T, D)
    aff = jax.nn.sigmoid(xt.astype(jnp.float32) @ router_w.astype(jnp.float32))
    sel = aff + router_bias.astype(jnp.float32)
    grp_score = lax.top_k(sel.reshape(T, N_GROUPS, E // N_GROUPS), 2)[0].sum(-1)
    _, top_grp = lax.top_k(grp_score, TOPK_GROUPS)
    grp_mask = jax.nn.one_hot(top_grp, N_GROUPS, dtype=jnp.float32).sum(1) > 0
    sel = jnp.where(jnp.repeat(grp_mask, E // N_GROUPS, axis=1), sel, -jnp.inf)
    _, idx = lax.top_k(sel, TOP_K)
    gate = jnp.take_along_axis(aff, idx, axis=1)
    gate = gate / gate.sum(-1, keepdims=True) * ROUTED_SCALE

    n_assign = T * TOP_K
    flat_e = idx.reshape(-1)
    order = jnp.argsort(flat_e)
    e_sorted = flat_e[order]
    tok_sorted = (order // TOP_K).astype(jnp.int32)
    gate_sorted = gate.reshape(-1)[order]
    counts = jnp.zeros((E,), jnp.int32).at[flat_e].add(1)
    padded = (counts + EXPERT_BLOCK - 1) // EXPERT_BLOCK * EXPERT_BLOCK
    pad_end = jnp.cumsum(padded)
    pad_start = pad_end - padded
    seg_start = jnp.cumsum(counts) - counts
    dest = pad_start[e_sorted] + jnp.arange(n_assign, dtype=jnp.int32) - seg_start[e_sorted]
    n_blocks = -(-(n_assign + E * (EXPERT_BLOCK - 1)) // EXPERT_BLOCK)
    n_rows = n_blocks * EXPERT_BLOCK
    row_tok = jnp.zeros((n_rows,), jnp.int32).at[dest].set(tok_sorted)
    row_gate = jnp.zeros((n_rows,), jnp.float32).at[dest].set(gate_sorted)
    blk_start = jnp.arange(n_blocks, dtype=jnp.int32) * EXPERT_BLOCK
    blk_expert = jnp.minimum(jnp.searchsorted(pad_end, blk_start, side='right'), E - 1)

    def expert_block(args):
        toks, gts, e = args
        xb = xt[toks]
        hb = jax.nn.silu(xb @ e_gate[e]) * (xb @ e_up[e])
        return (hb @ e_down[e]).astype(jnp.float32) * gts[:, None]

    y = lax.map(expert_block, (row_tok.reshape(n_blocks, EXPERT_BLOCK),
                               row_gate.reshape(n_blocks, EXPERT_BLOCK), blk_expert))
    routed = jnp.zeros((T, D), jnp.float32).at[row_tok].add(y.reshape(n_rows, D))
    shared = (jax.nn.silu(xt @ s_gate) * (xt @ s_up)) @ s_down
    return (routed.astype(x.dtype) + shared).reshape(B, S, D)


def setup_inputs(seed: int = 0) -> dict:
    key = jax.random.key(seed)
    ks = jax.random.split(key, 26)
    f32 = jnp.float32
    L, D, E = DEPTH, D_MODEL, N_EXPERTS

    def nrm(k, shape, scale):
        return jax.random.normal(k, shape, f32) * scale

    return {
        'x': nrm(ks[0], (BATCH, SEQ, D), 1.0),
        'w_in': nrm(ks[1], (L, D, IN_WIDTH), D ** -0.5),
        'mu_shift': jax.random.uniform(ks[2], (L, RWKV_FEATS), f32),
        'w0': jax.random.uniform(ks[3], (L, RWKV_WIDTH), f32, -6.5, -1.5),
        'w_decay_up': nrm(ks[4], (L, DECAY_LORA, RWKV_WIDTH), 0.1),
        'a0': nrm(ks[5], (L, RWKV_WIDTH), 0.1),
        'w_aaa_up': nrm(ks[6], (L, AAA_LORA, RWKV_WIDTH), 0.1),
        'w_gate_up': nrm(ks[7], (L, GATE_LORA, RWKV_WIDTH), GATE_LORA ** -0.5),
        'k_k': 0.85 + nrm(ks[8], (L, RWKV_WIDTH), 0.05),
        'k_a': 1.0 + nrm(ks[9], (L, RWKV_WIDTH), 0.05),
        'r_k': nrm(ks[10], (L, RWKV_HEADS, HEAD_DIM), 0.1),
        'lnx_g': 1.0 + nrm(ks[11], (L, RWKV_WIDTH), 0.05),
        'lnx_b': nrm(ks[12], (L, RWKV_WIDTH), 0.01),
        'w_out': nrm(ks[13], (L, MIX_WIDTH, D), BETA * MIX_WIDTH ** -0.5),
        'ln1_g': 1.0 + nrm(ks[14], (L, D), 0.05),
        'ln1_b': nrm(ks[15], (L, D), 0.01),
        'router_w': nrm(ks[16], (L, D, E), D ** -0.5),
        'router_bias': nrm(ks[17], (L, E), 0.01),
        'e_gate': nrm(ks[18], (L, E, D, EXPERT_FF), D ** -0.5),
        'e_up': nrm(ks[19], (L, E, D, EXPERT_FF), D ** -0.5),
        'e_down': nrm(ks[20], (L, E, EXPERT_FF, D), BETA * EXPERT_FF ** -0.5),
        's_gate': nrm(ks[21], (L, D, SHARED_FF), D ** -0.5),
        's_up': nrm(ks[22], (L, D, SHARED_FF), D ** -0.5),
        's_down': nrm(ks[23], (L, SHARED_FF, D), BETA * SHARED_FF ** -0.5),
        'ln2_g': 1.0 + nrm(ks[24], (L, D), 0.05),
        'ln2_b': nrm(ks[25], (L, D), 0.01),
    }


def reference(x, w_in, mu_shift, w0, w_decay_up, a0, w_aaa_up, w_gate_up, k_k, k_a, r_k,
              lnx_g, lnx_b, w_out, ln1_g, ln1_b, router_w, router_bias, e_gate, e_up, e_down,
              s_gate, s_up, s_down, ln2_g, ln2_b):
    for l in range(DEPTH):
        mix = hybrid_mixer(x, w_in[l], mu_shift[l], w0[l], w_decay_up[l], a0[l], w_aaa_up[l],
                           w_gate_up[l], k_k[l], k_a[l], r_k[l], lnx_g[l], lnx_b[l], w_out[l])
        x = layer_norm(ALPHA * x + mix, ln1_g[l], ln1_b[l])
        ffn = moe_ffn(x, router_w[l], router_bias[l], e_gate[l], e_up[l], e_down[l],
                      s_gate[l], s_up[l], s_down[l])
        x = layer_norm(ALPHA * x + ffn, ln2_g[l], ln2_b[l])
    return x
```

```python
import functools

import jax
import jax.numpy as jnp
from jax import lax
from jax.experimental import pallas as pl
from jax.experimental.pallas import tpu as pltpu

F32 = jnp.float32
BF16 = jnp.bfloat16
I32 = jnp.int32

LANES = 128
SUBLANES = 8
VMEM_LIMIT_BYTES = 56 * 1024 * 1024

HEAD_DIM = 64
ATTN_HEADS = 8
RWKV_HEADS = 8
ATTN_WIDTH = ATTN_HEADS * HEAD_DIM
RWKV_WIDTH = RWKV_HEADS * HEAD_DIM
ATTN_PATTERNS = ((128, 1), (512, 4), (2048, 16))
Q_BLOCK = 128
ROPE_THETA = 10000.0
DECAY_LORA = 64
AAA_LORA = 64
GATE_LORA = 160
GN_EPS = 64e-5
LN_EPS = 1e-5
TOP_K = 8
N_GROUPS = 8
TOPK_GROUPS = 4
ROUTED_SCALE = 2.5
DEPTH = 1
ALPHA = (2.0 * DEPTH) ** 0.25

RWKV_CHUNK = 64
PAIR = 2 * HEAD_DIM
LORA_PAD = 2 * LANES


def _cparams(*sem):
    return pltpu.CompilerParams(dimension_semantics=sem, vmem_limit_bytes=VMEM_LIMIT_BYTES)


def _split_bf16(a):
    hi = a.astype(BF16)
    lo = (a - hi.astype(F32)).astype(BF16)
    return hi, lo


def _dot(a, b):
    return jnp.dot(a, b, preferred_element_type=F32)


def _dot_nt(a, b):
    return lax.dot_general(a, b, (((1,), (1,)), ((), ())), preferred_element_type=F32)


def _dot_hl(a_f32, b_bf16):
    hi, lo = _split_bf16(a_f32)
    return _dot(hi, b_bf16) + _dot(lo, b_bf16)


def _softplus(z):
    return jnp.maximum(z, 0.0) + jnp.log(1.0 + jnp.exp(-jnp.abs(z)))


def _sigmoid(z):
    return 1.0 / (1.0 + jnp.exp(-z))


def _inproj_kernel(x_ref, w_ref, cos_ref, sin_ref, mu_ref, wd_ref, wa_ref, wg_ref, vec_ref, bd_ref,
                   q_ref, k_ref, v_ref, r_ref, ld_ref, kp_ref, vv_ref, kk_ref, b_ref, g_ref, bon_ref,
                   carry_ref):
    s = pl.program_id(1)
    tm = x_ref.shape[0]
    aw = ATTN_WIDTH
    rw = RWKV_WIDTH

    @pl.when(s == 0)
    def _():
        carry_ref[...] = jnp.zeros_like(carry_ref)

    h = _dot(x_ref[...].astype(BF16), w_ref[...])
    reps = aw // cos_ref.shape[1]
    cos = jnp.concatenate([cos_ref[...]] * reps, axis=1)
    sin = jnp.concatenate([sin_ref[...]] * reps, axis=1)
    q_ref[...] = ((h[:, 0:aw] * cos + h[:, aw:2 * aw] * sin) * (HEAD_DIM ** -0.5)).astype(q_ref.dtype)
    k_ref[...] = (h[:, 2 * aw:3 * aw] * cos + h[:, 3 * aw:4 * aw] * sin).astype(k_ref.dtype)
    v_ref[...] = h[:, 4 * aw:5 * aw].astype(v_ref.dtype)

    f = h[:, 5 * aw:]
    rows = lax.broadcasted_iota(I32, f.shape, 0)
    prev = jnp.where(rows == 0, carry_ref[SUBLANES - 1:SUBLANES, :], pltpu.roll(f, 1, axis=0))
    carry_ref[...] = f[tm - SUBLANES:tm, :]
    f = f + (prev - f) * mu_ref[...]

    r = f[:, 0:rw]
    k = f[:, rw:2 * rw]
    v = f[:, 2 * rw:3 * rw]
    la = f[:, 3 * rw:3 * rw + LANES]
    gl = f[:, 3 * rw + LANES:]
    w0, a0, k_k, k_a, r_k = (vec_ref[i:i + 1, :] for i in range(5))

    z = w0 + _dot(jnp.tanh(la).astype(BF16), wd_ref[...])
    w = -_softplus(-z) - 0.5
    ld_ref[...] = -jnp.exp(w)
    a = _sigmoid(a0 + _dot(la.astype(BF16), wa_ref[...]))
    g_ref[...] = _dot(_sigmoid(gl).astype(BF16), wg_ref[...])

    bd = bd_ref[...]
    kk = k * k_k
    nrm = jnp.sqrt(_dot_hl(kk * kk, bd))
    kk = kk / jnp.maximum(nrm, 1e-12)
    kp = k * (1.0 + (a - 1.0) * k_a)
    r_ref[...] = r
    kp_ref[...] = kp
    vv_ref[...] = v
    kk_ref[...] = kk
    b_ref[...] = kk * a
    bon_ref[...] = _dot_hl(r * kp * r_k, bd) * v


def _rot_half_cols(w):
    d, n = w.shape
    w4 = w.reshape(d, n // HEAD_DIM, 2, HEAD_DIM // 2)
    return jnp.stack([-w4[:, :, 1, :], w4[:, :, 0, :]], axis=2).reshape(d, n)


def _inproj(x2, seq, w_in, mu_shift, w0, w_decay_up, a0, w_aaa_up, w_gate_up, k_k, k_a, r_k, tm):
    t, d = x2.shape
    aw, rw = ATTN_WIDTH, RWKV_WIDTH
    wq, wk, wv = w_in[:, 0:aw], w_in[:, aw:2 * aw], w_in[:, 2 * aw:3 * aw]
    wf = w_in[:, 3 * aw:]
    gpad = LORA_PAD - GATE_LORA
    w_all = jnp.concatenate(
        [wq, _rot_half_cols(wq), wk, _rot_half_cols(wk), wv, wf, jnp.zeros((d, gpad), F32)], axis=1).astype(BF16)
    mu = jnp.concatenate([mu_shift, jnp.zeros((gpad,), F32)])[None, :]
    nf = mu.shape[1]
    wd = jnp.concatenate([w_decay_up, jnp.zeros((AAA_LORA, rw), F32)], axis=0).astype(BF16)
    wa = jnp.concatenate([jnp.zeros((DECAY_LORA, rw), F32), w_aaa_up], axis=0).astype(BF16)
    wg = jnp.concatenate([w_gate_up, jnp.zeros((gpad, rw), F32)], axis=0).astype(BF16)
    vec = jnp.stack([w0, a0, k_k, k_a, r_k.reshape(-1), w0 * 0, w0 * 0, w0 * 0])
    head = jnp.arange(rw) // HEAD_DIM
    bd = (head[:, None] == head[None, :]).astype(BF16)
    half = HEAD_DIM // 2
    inv_freq = ROPE_THETA ** (-jnp.arange(half, dtype=F32) * 2.0 / HEAD_DIM)
    ang = jnp.arange(seq, dtype=F32)[:, None] * inv_freq[None, :]
    cos = jnp.tile(jnp.cos(ang), (1, LANES // half))
    sin = jnp.tile(jnp.sin(ang), (1, LANES // half))

    nst = seq // tm
    tok = lambda b, s: (b * nst + s, 0)
    const = lambda b, s: (0, 0)
    full = lambda a: pl.BlockSpec(a.shape, const)
    out_bf = jax.ShapeDtypeStruct((t, aw), BF16)
    out_f = jax.ShapeDtypeStruct((t, rw), F32)
    return pl.pallas_call(
        _inproj_kernel,
        grid=(t // seq, nst),
        in_specs=[pl.BlockSpec((tm, d), tok), full(w_all),
                  pl.BlockSpec((tm, LANES), lambda b, s: (s, 0)), pl.BlockSpec((tm, LANES), lambda b, s: (s, 0)),
                  full(mu), full(wd), full(wa), full(wg), full(vec), full(bd)],
        out_specs=[pl.BlockSpec((tm, aw), tok)] * 3 + [pl.BlockSpec((tm, rw), tok)] * 8,
        out_shape=[out_bf] * 3 + [out_f] * 8,
        scratch_shapes=[pltpu.VMEM((SUBLANES, nf), F32)],
        compiler_params=_cparams("arbitrary", "arbitrary"),
        name="inproj",
    )(x2, w_all, cos, sin, mu, wd, wa, wg, vec, bd)


def _dotf(a, b, nt=False):
    dot = _dot_nt if nt else _dot
    ah, al = _split_bf16(a)
    bh, bl = _split_bf16(b)
    return dot(ah, bh) + (dot(ah, bl) + dot(al, bh))


def _dot3(a_bf16, b):
    b1 = b.astype(BF16)
    r1 = b - b1.astype(F32)
    b2 = r1.astype(BF16)
    b3 = (r1 - b2.astype(F32)).astype(BF16)
    return _dot(a_bf16, b1) + (_dot(a_bf16, b2) + _dot(a_bf16, b3))


def _rwkv_kernel(r_ref, ld_ref, kp_ref, v_ref, kk_ref, b_ref, g_ref, bon_ref, gn_ref, bd_ref, o_ref, h_ref):
    c = pl.program_id(1)
    ch = r_ref.shape[0]
    npairs = r_ref.shape[1] // PAIR

    @pl.when(c == 0)
    def _():
        h_ref[...] = jnp.zeros_like(h_ref)

    ld = ld_ref[...]
    ri = lax.broadcasted_iota(I32, (ch, ch), 0)
    ci = lax.broadcasted_iota(I32, (ch, ch), 1)
    cum = _dot3((ri >= ci).astype(BF16), ld)
    tot = cum[ch - 1:ch, :]
    a_t = -kk_ref[...] * jnp.exp(cum - ld)
    pinv = jnp.exp(-cum)
    b_t = b_ref[...] * pinv
    k_t = kp_ref[...] * pinv
    r_t = r_ref[...] * jnp.exp(cum)
    pend = jnp.exp(tot - cum)
    b_end = b_ref[...] * pend
    k_end = kp_ref[...] * pend
    p_tot = jnp.exp(tot)
    v_all = v_ref[...]

    row = lax.broadcasted_iota(I32, (ch, PAIR), 0)
    col = lax.broadcasted_iota(I32, (ch, PAIR), 1)
    first = col < HEAD_DIM
    jj = col & (HEAD_DIM - 1)
    strict = jj < row
    incl = jj <= row
    eye = (jj == row).astype(F32)
    row2 = lax.broadcasted_iota(I32, (PAIR, PAIR), 0)
    col2 = lax.broadcasted_iota(I32, (PAIR, PAIR), 1)
    same_head = (row2 < HEAD_DIM) == (col2 < HEAD_DIM)
    diag2 = row2 == col2
    zeros_cp = jnp.zeros((ch, PAIR), F32)

    def bdiag(y):
        return jnp.concatenate([jnp.where(first, y, 0.0), jnp.where(first, 0.0, y)], axis=0)

    ys = []
    for p in range(npairs):
        sl = slice(p * PAIR, (p + 1) * PAIR)
        a2, r2, b2, k2, v2 = a_t[:, sl], r_t[:, sl], b_t[:, sl], k_t[:, sl], v_all[:, sl]
        s8 = _dotf(jnp.concatenate([a2, r2], axis=0), jnp.concatenate([bdiag(b2), bdiag(k2)], axis=0), nt=True)
        lmat = jnp.where(strict, s8[0:ch, 0:PAIR], 0.0)
        a_ak = jnp.where(strict, s8[0:ch, PAIR:], 0.0)
        m_rb = jnp.where(incl, s8[ch:, 0:PAIR], 0.0)
        m_rk = jnp.where(incl, s8[ch:, PAIR:], 0.0)

        tinv = eye + lmat
        lk = _dotf(lmat, bdiag(lmat))
        steps = max(ch.bit_length() - 3, 0)
        for _ in range(steps):
            both = _dotf(jnp.concatenate([tinv, lk], axis=0), bdiag(lk))
            tinv = tinv + both[0:ch]
            lk = both[ch:]
        tinv = tinv + _dotf(tinv, bdiag(lk))

        w1 = _dotf(a_ak, bdiag(v2))
        au = _dotf(tinv, jnp.concatenate([bdiag(a2), bdiag(w1)], axis=1))
        a_hat, u_loc = au[:, 0:PAIR], au[:, PAIR:]
        rhs = jnp.concatenate([
            jnp.concatenate([bdiag(a_hat), bdiag(u_loc)], axis=1),
            jnp.concatenate([jnp.zeros((2 * ch, PAIR), F32), bdiag(v2)], axis=1)], axis=0)
        ry = _dotf(jnp.concatenate([m_rb, m_rk], axis=1), rhs)
        r_hat = r2 + ry[:, 0:PAIR]
        y_loc = ry[:, PAIR:]

        bkt = jnp.concatenate([b_end[:, sl], k_end[:, sl]], axis=0).T
        gh = _dotf(bkt, jnp.concatenate([jnp.concatenate([a_hat, u_loc], axis=1),
                                         jnp.concatenate([zeros_cp, v2], axis=1)], axis=0))
        g_mat = jnp.where(same_head, gh[:, 0:PAIR], 0.0) + jnp.where(diag2, p_tot[:, sl], 0.0)
        h_loc = jnp.where(same_head, gh[:, PAIR:], 0.0)

        yh = _dotf(jnp.concatenate([r_hat, g_mat], axis=0), h_ref[p])
        ys.append(yh[0:ch] + y_loc)
        h_ref[p] = yh[ch:] + h_loc

    y = jnp.concatenate(ys, axis=1)
    bd = bd_ref[...]
    inv_n = 1.0 / HEAD_DIM
    yc = y - _dot_hl(y, bd) * inv_n
    var = _dot_hl(yc * yc, bd) * inv_n
    yn = yc * lax.rsqrt(var + GN_EPS) * gn_ref[0:1, :] + gn_ref[1:2, :]
    o_ref[...] = (yn + bon_ref[...]) * g_ref[...]


def _rwkv(feats, seq, lnx_g, lnx_b):
    r, ld, kp, vv, kk, b, g, bon = feats
    t, rw = r.shape
    ch = RWKV_CHUNK
    nch = seq // ch
    gn = jnp.concatenate([lnx_g[None], lnx_b[None], jnp.zeros((SUBLANES - 2, rw), F32)], axis=0)
    head = jnp.arange(rw) // HEAD_DIM
    bd = (head[:, None] == head[None, :]).astype(BF16)
    tok = pl.BlockSpec((ch, rw), lambda bb, c: (bb * nch + c, 0))
    const = lambda a: pl.BlockSpec(a.shape, lambda bb, c: (0, 0))
    return pl.pallas_call(
        _rwkv_kernel,
        grid=(t // seq, nch),
        in_specs=[tok] * 8 + [const(gn), const(bd)],
        out_specs=tok,
        out_shape=jax.ShapeDtypeStruct((t, rw), F32),
        scratch_shapes=[pltpu.VMEM((rw // PAIR, PAIR, PAIR), F32)],
        compiler_params=_cparams("arbitrary", "arbitrary"),
        name="rwkv",
    )(r, ld, kp, vv, kk, b, g, bon, gn, bd)


NEG_BIG = -1e30


def _attn_kernel(n_back, q_ref, kc_ref, kp_ref, vc_ref, vp_ref, o_ref, lse_ref):
    n = pl.program_id(2)
    qb = q_ref.shape[0]
    qi = lax.broadcasted_iota(I32, (qb, 2 * qb), 0)
    ki = lax.broadcasted_iota(I32, (qb, 2 * qb), 1)
    dist = qb + qi - ki
    has_prev = jnp.where(n > 0, 0, qb)
    valid = (dist >= 0) & (dist <= n_back) & (ki >= has_prev)
    first = lax.broadcasted_iota(I32, (qb, PAIR), 1) < HEAD_DIM
    for p in range(q_ref.shape[1] // PAIR):
        sl = slice(p * PAIR, (p + 1) * PAIR)
        q2 = q_ref[:, sl]
        k2 = jnp.concatenate([kp_ref[:, sl], kc_ref[:, sl]], axis=0)
        v2 = jnp.concatenate([vp_ref[:, sl], vc_ref[:, sl]], axis=0)
        o_h, lse_h = [], []
        for hh in range(2):
            keep = first if hh == 0 else jnp.logical_not(first)
            s = _dot_nt(jnp.where(keep, q2, jnp.zeros_like(q2)), k2)
            s = jnp.where(valid, s, NEG_BIG)
            m = jnp.max(s, axis=1, keepdims=True)
            pe = jnp.where(valid, jnp.exp(s - m), 0.0)
            l = jnp.sum(pe, axis=1, keepdims=True)
            o_h.append(_dot(pe.astype(BF16), v2) / l)
            lse_h.append(m + jnp.log(l))
        o_ref[:, sl] = jnp.where(first, o_h[0], o_h[1])
        lse_ref[:, sl] = jnp.where(first, lse_h[0], lse_h[1])


def _attention(q, k, v, bsz, seq, window, dilation):
    t, aw = q.shape
    ln = seq // dilation
    nb = ln // Q_BLOCK
    view = lambda z: z.reshape(bsz, ln, dilation * aw)
    cur = pl.BlockSpec((None, Q_BLOCK, aw), lambda b, r, n: (b, n, r))
    prv = pl.BlockSpec((None, Q_BLOCK, aw), lambda b, r, n: (b, jnp.maximum(n - 1, 0), r))
    shp = jax.ShapeDtypeStruct((bsz, ln, dilation * aw), F32)
    o, lse = pl.pallas_call(
        functools.partial(_attn_kernel, window // dilation),
        grid=(bsz, dilation, nb),
        in_specs=[cur, cur, prv, cur, prv],
        out_specs=[cur, cur],
        out_shape=[shp, shp],
        compiler_params=_cparams("arbitrary", "arbitrary", "arbitrary"),
        name=f"attn_d{dilation}",
    )(view(q), view(k), view(k), view(v), view(v))
    return o.reshape(t, aw), lse.reshape(t, aw)


def _layer_norm(y, g, b):
    mu = jnp.mean(y, axis=-1, keepdims=True)
    yc = y - mu
    var = jnp.mean(yc * yc, axis=-1, keepdims=True)
    return yc * lax.rsqrt(var + LN_EPS) * g + b


def _outproj_kernel(npat, *refs):
    o_refs = refs[0:npat]
    lse_refs = refs[npat:2 * npat]
    rw_ref, x_ref, wo_ref, ln_ref, y_ref, yb_ref = refs[2 * npat:]
    lses = [r[...] for r in lse_refs]
    m = functools.reduce(jnp.maximum, lses)
    es = [jnp.exp(z - m) for z in lses]
    den = functools.reduce(lambda a, b: a + b, es)
    attn = functools.reduce(lambda a, b: a + b, [(e / den) * r[...] for e, r in zip(es, o_refs)])
    aw = attn.shape[1]
    mix = _dot(attn.astype(BF16), wo_ref[0:aw, :]) + _dot(rw_ref[...].astype(BF16), wo_ref[aw:, :])
    y = _layer_norm(ALPHA * x_ref[...] + mix, ln_ref[0:1, :], ln_ref[1:2, :])
    y_ref[...] = y
    yb_ref[...] = y.astype(yb_ref.dtype)


def _outproj(os_, lses, rw, x2, w_out, ln_g, ln_b, tm):
    t, d = x2.shape
    npat = len(os_)
    ln = jnp.concatenate([ln_g[None], ln_b[None], jnp.zeros((SUBLANES - 2, d), F32)], axis=0)
    wo = w_out.astype(BF16)
    tok = lambda w: pl.BlockSpec((tm, w), lambda i: (i, 0))
    const = lambda a: pl.BlockSpec(a.shape, lambda i: (0, 0))
    return pl.pallas_call(
        functools.partial(_outproj_kernel, npat),
        grid=(t // tm,),
        in_specs=[tok(os_[0].shape[1])] * (2 * npat) + [tok(rw.shape[1]), tok(d), const(wo), const(ln)],
        out_specs=[tok(d), tok(d)],
        out_shape=[jax.ShapeDtypeStruct((t, d), F32), jax.ShapeDtypeStruct((t, d), BF16)],
        compiler_params=_cparams("arbitrary"),
        name="outproj",
    )(*os_, *lses, rw, x2, wo, ln)


EXPERT_ROWS = 256


def _lanes(col_rep, n):
    return jnp.concatenate([col_rep] * (n // LANES), axis=1)


def _route_kernel(x_ref, rwt_ref, bias_ref, tri_ref, ones_ref, idx_ref, gate_ref, rank_ref, cnt_ref, base_ref):
    step = pl.program_id(0)
    ne = rwt_ref.shape[0]
    tm = x_ref.shape[0]
    gsz = ne // N_GROUPS
    ninf = -jnp.inf

    @pl.when(step == 0)
    def _():
        base_ref[...] = jnp.zeros_like(base_ref)

    aff = _sigmoid(_dotf(rwt_ref[...], x_ref[...], nt=True))
    sel = aff + _lanes(bias_ref[...], tm)

    sel3 = sel.reshape(N_GROUPS, gsz, tm)
    rid = lax.broadcasted_iota(I32, sel3.shape, 1).astype(F32)
    m1 = jnp.max(sel3, axis=1, keepdims=True)
    i1 = jnp.min(jnp.where(sel3 == m1, rid, float(gsz)), axis=1, keepdims=True)
    m2 = jnp.max(jnp.where(rid == i1, ninf, sel3), axis=1, keepdims=True)
    gsc = (m1 + m2).reshape(N_GROUPS, tm)
    gid = lax.broadcasted_iota(I32, gsc.shape, 0).astype(F32)
    keep = jnp.zeros(gsc.shape, F32)
    for _ in range(TOPK_GROUPS):
        gm = jnp.max(gsc, axis=0, keepdims=True)
        gi = jnp.min(jnp.where(gsc == gm, gid, float(N_GROUPS)), axis=0, keepdims=True)
        hit = gid == gi
        keep = jnp.where(hit, 1.0, keep)
        gsc = jnp.where(hit, ninf, gsc)
    cand = jnp.where(keep.reshape(N_GROUPS, 1, tm) > 0.0, sel3, ninf).reshape(ne, tm)

    eid = lax.broadcasted_iota(I32, (ne, tm), 0).astype(F32)
    hits, idxs, graw = [], [], []
    for _ in range(TOP_K):
        m = jnp.max(cand, axis=0, keepdims=True)
        ij = jnp.min(jnp.where(cand == m, eid, float(ne)), axis=0, keepdims=True)
        hit = eid == ij
        hits.append(hit)
        idxs.append(ij)
        graw.append(jnp.sum(jnp.where(hit, aff, 0.0), axis=0, keepdims=True))
        cand = jnp.where(hit, ninf, cand)
    gsum = functools.reduce(lambda a, b: a + b, graw)
    idx_ref[...] = jnp.concatenate(idxs, axis=0).astype(I32)
    gate_ref[...] = jnp.concatenate([g / gsum * ROUTED_SCALE for g in graw], axis=0)

    onehot = functools.reduce(lambda a, b: a + b, [h.astype(F32) for h in hits]).astype(BF16)
    before = _dot(onehot, tri_ref[...]) + _lanes(base_ref[...], tm)
    rank_ref[...] = jnp.concatenate(
        [jnp.sum(jnp.where(h, before, 0.0), axis=0, keepdims=True) for h in hits], axis=0).astype(I32)
    total = base_ref[...] + _dot(onehot, ones_ref[...])
    base_ref[...] = total
    cnt_ref[...] = total


def _route(x1, router_w, router_bias, tm):
    t, d = x1.shape
    ne = router_w.shape[1]
    rwt = router_w.T
    bias = jnp.broadcast_to(router_bias[:, None], (ne, LANES))
    pos = jnp.arange(tm)
    tri = (pos[:, None] < pos[None, :]).astype(BF16)
    ones = jnp.ones((tm, LANES), BF16)
    const = lambda a: pl.BlockSpec(a.shape, lambda i: (0, 0))
    tokt = pl.BlockSpec((TOP_K, tm), lambda i: (0, i))
    return pl.pallas_call(
        _route_kernel,
        grid=(t // tm,),
        in_specs=[pl.BlockSpec((tm, d), lambda i: (i, 0)), const(rwt), const(bias), const(tri), const(ones)],
        out_specs=[tokt, tokt, tokt, pl.BlockSpec((ne, LANES), lambda i: (0, 0))],
        out_shape=[jax.ShapeDtypeStruct((TOP_K, t), I32), jax.ShapeDtypeStruct((TOP_K, t), F32),
                   jax.ShapeDtypeStruct((TOP_K, t), I32), jax.ShapeDtypeStruct((ne, LANES), F32)],
        scratch_shapes=[pltpu.VMEM((ne, LANES), F32)],
        compiler_params=_cparams("arbitrary"),
        name="route",
    )(x1, rwt, bias, tri, ones)


def _dest_kernel(nblk, idx_ref, rank_ref, cnt_ref, tril_ref, dest_ref, meta_ref):
    ne = cnt_ref.shape[0]
    tm = idx_ref.shape[1]
    padded = jnp.ceil(cnt_ref[...] * (1.0 / EXPERT_ROWS)) * EXPERT_ROWS
    pad_end = _dot3(tril_ref[...], padded)
    pad_start = _lanes(pad_end - padded, tm)
    eid = lax.broadcasted_iota(I32, (ne, tm), 0)
    idx = idx_ref[...]
    rows = [jnp.sum(jnp.where(eid == idx[j:j + 1, :], pad_start, 0.0), axis=0, keepdims=True) for j in range(TOP_K)]
    dest_ref[...] = jnp.concatenate(rows, axis=0).astype(I32) + rank_ref[...]

    blk_start = (lax.broadcasted_iota(I32, (ne, nblk), 1) * EXPERT_ROWS).astype(F32)
    owner = jnp.sum((_lanes(pad_end, nblk) <= blk_start).astype(F32), axis=0, keepdims=True)
    owner = jnp.minimum(owner, float(ne - 1))
    used = jnp.max(_lanes(pad_end, nblk), axis=0, keepdims=True) * (1.0 / EXPERT_ROWS)
    meta_ref[...] = jnp.concatenate([owner, used, jnp.zeros((SUBLANES - 2, nblk), F32)], axis=0).astype(I32)


def _dest(idx, rank, cnt, nblk, tm):
    t = idx.shape[1]
    ne = cnt.shape[0]
    eid = jnp.arange(ne)
    tril = (eid[:, None] >= eid[None, :]).astype(BF16)
    tokt = pl.BlockSpec((TOP_K, tm), lambda i: (0, i))
    const = lambda a: pl.BlockSpec(a.shape, lambda i: (0, 0))
    return pl.pallas_call(
        functools.partial(_dest_kernel, nblk),
        grid=(t // tm,),
        in_specs=[tokt, tokt, const(cnt), const(tril)],
        out_specs=[tokt, pl.BlockSpec((SUBLANES, nblk), lambda i: (0, 0))],
        out_shape=[jax.ShapeDtypeStruct((TOP_K, t), I32), jax.ShapeDtypeStruct((SUBLANES, nblk), I32)],
        compiler_params=_cparams("arbitrary"),
        name="dest",
    )(idx, rank, cnt, tril)


def _row_copy(src, i, dst, j, sem):
    return pltpu.make_async_copy(src.at[pl.ds(i, 1), :], dst.at[pl.ds(j, 1), :], sem)


def _dispatch_kernel(dest_ref, x_ref, xs_in_ref, xs_ref, sem):
    del xs_in_ref
    tm = x_ref.shape[0]

    def issue(t, carry):
        for j in range(TOP_K):
            _row_copy(x_ref, t, xs_ref, dest_ref[j, t], sem).start()
        return carry

    def drain(t, carry):
        for j in range(TOP_K):
            _row_copy(x_ref, 0, xs_ref, 0, sem).wait()
        return carry

    lax.fori_loop(0, tm, issue, 0)
    lax.fori_loop(0, tm, drain, 0)


def _dispatch(x1, dest, nrows, tm):
    t, d = x1.shape
    xs0 = jnp.zeros((nrows, d), F32)
    return pl.pallas_call(
        _dispatch_kernel,
        grid=(t // tm,),
        in_specs=[pl.BlockSpec((TOP_K, tm), lambda i: (0, i), memory_space=pltpu.SMEM),
                  pl.BlockSpec((tm, d), lambda i: (i, 0)),
                  pl.BlockSpec(memory_space=pl.ANY)],
        out_specs=pl.BlockSpec(memory_space=pl.ANY),
        out_shape=jax.ShapeDtypeStruct((nrows, d), F32),
        scratch_shapes=[pltpu.SemaphoreType.DMA(())],
        input_output_aliases={2: 0},
        compiler_params=_cparams("arbitrary"),
        name="dispatch",
    )(dest, x1, xs0)


def _expert_kernel(owner_ref, used_ref, xs_ref, wg_ref, wu_ref, wd_ref, ys_ref, wgb_ref, wub_ref, wdb_ref):
    i = pl.program_id(0)
    prev = owner_ref[jnp.maximum(i - 1, 0)]

    @pl.when((i == 0) | (owner_ref[i] != prev))
    def _():
        wgb_ref[...] = wg_ref[...].astype(BF16)
        wub_ref[...] = wu_ref[...].astype(BF16)
        wdb_ref[...] = wd_ref[...].astype(BF16)

    @pl.when(i < used_ref[0])
    def _():
        xb = xs_ref[...].astype(BF16)
        hg = _dot(xb, wgb_ref[...])
        hb = hg * _sigmoid(hg) * _dot(xb, wub_ref[...])
        ys_ref[...] = _dot(hb.astype(BF16), wdb_ref[...])


def _experts(xs, owner, used, e_gate, e_up, e_down):
    nrows, d = xs.shape
    ff = e_gate.shape[2]
    nblk = nrows // EXPERT_ROWS
    blk = lambda i, owner_ref, used_ref: (jnp.minimum(i, used_ref[0] - 1), 0)
    wsel = lambda i, owner_ref, used_ref: (owner_ref[jnp.minimum(i, used_ref[0] - 1)], 0, 0)
    grid_spec = pltpu.PrefetchScalarGridSpec(
        num_scalar_prefetch=2,
        grid=(nblk,),
        in_specs=[pl.BlockSpec((EXPERT_ROWS, d), blk),
                  pl.BlockSpec((None, d, ff), wsel), pl.BlockSpec((None, d, ff), wsel),
                  pl.BlockSpec((None, ff, d), wsel)],
        out_specs=pl.BlockSpec((EXPERT_ROWS, d), blk),
        scratch_shapes=[pltpu.VMEM((d, ff), BF16), pltpu.VMEM((d, ff), BF16), pltpu.VMEM((ff, d), BF16)],
    )
    return pl.pallas_call(
        _expert_kernel,
        grid_spec=grid_spec,
        out_shape=jax.ShapeDtypeStruct((nrows, d), F32),
        compiler_params=_cparams("arbitrary"),
        name="experts",
    )(owner, used, xs, e_gate, e_up, e_down)


def _final_kernel(dest_ref, x_ref, xb_ref, gate_ref, ys_ref, sg_ref, su_ref, sd_ref, ln_ref, o_ref, buf_ref, sem):
    tm = x_ref.shape[0]

    def issue(t, carry):
        for j in range(TOP_K):
            _row_copy(ys_ref, dest_ref[j, t], buf_ref.at[j], t, sem).start()
        return carry

    def drain(t, carry):
        for j in range(TOP_K):
            _row_copy(ys_ref, 0, buf_ref.at[j], 0, sem).wait()
        return carry

    lax.fori_loop(0, tm, issue, 0)
    xb = xb_ref[...]
    hg = _dot(xb, sg_ref[...])
    hs = hg * _sigmoid(hg) * _dot(xb, su_ref[...])
    ffn = _dot(hs.astype(BF16), sd_ref[...])
    gates = gate_ref[...].T
    lax.fori_loop(0, tm, drain, 0)
    routed = functools.reduce(lambda a, b: a + b, [gates[:, j:j + 1] * buf_ref[j] for j in range(TOP_K)])
    o_ref[...] = _layer_norm(ALPHA * x_ref[...] + (routed + ffn), ln_ref[0:1, :], ln_ref[1:2, :])


def _final(x1, x1b, dest, gate, ys, s_gate, s_up, s_down, ln_g, ln_b, tm):
    t, d = x1.shape
    ln = jnp.concatenate([ln_g[None], ln_b[None], jnp.zeros((SUBLANES - 2, d), F32)], axis=0)
    sg, su, sd = s_gate.astype(BF16), s_up.astype(BF16), s_down.astype(BF16)
    tok = pl.BlockSpec((tm, d), lambda i: (i, 0))
    const = lambda a: pl.BlockSpec(a.shape, lambda i: (0, 0))
    return pl.pallas_call(
        _final_kernel,
        grid=(t // tm,),
        in_specs=[pl.BlockSpec((TOP_K, tm), lambda i: (0, i), memory_space=pltpu.SMEM),
                  tok, tok, pl.BlockSpec((TOP_K, tm), lambda i: (0, i)),
                  pl.BlockSpec(memory_space=pl.ANY), const(sg), const(su), const(sd), const(ln)],
        out_specs=tok,
        out_shape=jax.ShapeDtypeStruct((t, d), F32),
        scratch_shapes=[pltpu.VMEM((TOP_K, tm, d), F32), pltpu.SemaphoreType.DMA(())],
        compiler_params=_cparams("arbitrary"),
        name="final",
    )(dest, x1, x1b, gate, ys, sg, su, sd, ln)


def _moe(x1, x1b, router_w, router_bias, e_gate, e_up, e_down, s_gate, s_up, s_down, ln_g, ln_b):
    t, d = x1.shape
    ne = router_w.shape[1]
    nblk = -(-(t * TOP_K + ne * (EXPERT_ROWS - 1)) // EXPERT_ROWS)
    nblk_pad = -(-nblk // LANES) * LANES
    idx, gate, rank, cnt = _route(x1, router_w, router_bias, tm=512)
    dest, meta = _dest(idx, rank, cnt, nblk_pad, tm=512)
    xs = _dispatch(x1, dest, nblk * EXPERT_ROWS, tm=256)
    ys = _experts(xs, meta[0], meta[1, 0:1], e_gate, e_up, e_down)
    return _final(x1, x1b, dest, gate, ys, s_gate, s_up, s_down, ln_g, ln_b, tm=256)


def kernel(x, w_in, mu_shift, w0, w_decay_up, a0, w_aaa_up, w_gate_up, k_k, k_a, r_k, lnx_g, lnx_b, w_out,
           ln1_g, ln1_b, router_w, router_bias, e_gate, e_up, e_down, s_gate, s_up, s_down, ln2_g, ln2_b):
    bsz, seq, d = x.shape
    x2 = x.reshape(bsz * seq, d)
    outs = _inproj(x2, seq, w_in[0], mu_shift[0], w0[0], w_decay_up[0], a0[0], w_aaa_up[0], w_gate_up[0],
                   k_k[0], k_a[0], r_k[0], tm=256)
    rw = _rwkv(outs[3:], seq, lnx_g[0], lnx_b[0])
    x1, x1b = _mixer_tail(outs[:3], rw, x2, bsz, seq, w_out[0], ln1_g[0], ln1_b[0])
    out = _moe(x1, x1b, router_w[0], router_bias[0], e_gate[0], e_up[0], e_down[0], s_gate[0], s_up[0], s_down[0],
               ln2_g[0], ln2_b[0])
    return out.reshape(bsz, seq, d)


def _mixer_tail(qkv, rw, x2, bsz, seq, w_out, ln_g, ln_b):
    q, k, v = qkv
    res = [_attention(q, k, v, bsz, seq, window, dilation) for window, dilation in ATTN_PATTERNS]
    return _outproj([o for o, _ in res], [l for _, l in res], rw, x2, w_out, ln_g, ln_b, tm=512)
```

```python
import functools

import jax
import jax.numpy as jnp
from jax import lax
from jax.experimental import pallas as pl
from jax.experimental.pallas import tpu as pltpu

F32 = jnp.float32
BF16 = jnp.bfloat16
I32 = jnp.int32

LANES = 128
SUBLANES = 8
VMEM_LIMIT_BYTES = 56 * 1024 * 1024

HEAD_DIM = 64
ATTN_HEADS = 8
RWKV_HEADS = 8
ATTN_WIDTH = ATTN_HEADS * HEAD_DIM
RWKV_WIDTH = RWKV_HEADS * HEAD_DIM
ATTN_PATTERNS = ((128, 1), (512, 4), (2048, 16))
Q_BLOCK = 128
ROPE_THETA = 10000.0
DECAY_LORA = 64
AAA_LORA = 64
GATE_LORA = 160
GN_EPS = 64e-5
LN_EPS = 1e-5
TOP_K = 8
N_GROUPS = 8
TOPK_GROUPS = 4
ROUTED_SCALE = 2.5
DEPTH = 1
ALPHA = (2.0 * DEPTH) ** 0.25

RWKV_CHUNK = 64
PAIR = 2 * HEAD_DIM
LORA_PAD = 2 * LANES


def _cparams(*sem):
    return pltpu.CompilerParams(dimension_semantics=sem, vmem_limit_bytes=VMEM_LIMIT_BYTES)


def _split_bf16(a):
    hi = a.astype(BF16)
    lo = (a - hi.astype(F32)).astype(BF16)
    return hi, lo


def _dot(a, b):
    return jnp.dot(a, b, preferred_element_type=F32)


def _dot_nt(a, b):
    return lax.dot_general(a, b, (((1,), (1,)), ((), ())), preferred_element_type=F32)


def _dot_hl(a_f32, b_bf16):
    hi, lo = _split_bf16(a_f32)
    return _dot(hi, b_bf16) + _dot(lo, b_bf16)


def _softplus(z):
    return jnp.maximum(z, 0.0) + jnp.log(1.0 + jnp.exp(-jnp.abs(z)))


def _sigmoid(z):
    return 1.0 / (1.0 + jnp.exp(-z))


def _inproj_kernel(dils, x_ref, w_ref, cos_ref, sin_ref, mu_ref, wd_ref, wa_ref, wg_ref, vec_ref, bd_ref, *refs):
    nq = 3 * len(dils)
    qkv_refs = refs[0:nq]
    r_ref, ld_ref, kp_ref, vv_ref, kk_ref, b_ref, g_ref, bon_ref, carry_ref, qkv_scr = refs[nq:]
    s = pl.program_id(1)
    tm = x_ref.shape[0]
    aw = ATTN_WIDTH
    rw = RWKV_WIDTH

    @pl.when(s == 0)
    def _():
        carry_ref[...] = jnp.zeros_like(carry_ref)

    h = _dot(x_ref[...].astype(BF16), w_ref[...])
    reps = aw // cos_ref.shape[1]
    cos = jnp.concatenate([cos_ref[...]] * reps, axis=1)
    sin = jnp.concatenate([sin_ref[...]] * reps, axis=1)
    qkv = [(h[:, 0:aw] * cos + h[:, aw:2 * aw] * sin) * (HEAD_DIM ** -0.5),
           h[:, 2 * aw:3 * aw] * cos + h[:, 3 * aw:4 * aw] * sin,
           h[:, 4 * aw:5 * aw]]
    nslab = aw // LANES
    for i in range(3):
        for j in range(nslab):
            qkv_scr[i * nslab + j] = qkv[i][:, j * LANES:(j + 1) * LANES]
    for di, d in enumerate(dils):
        for i in range(3):
            o_ref = qkv_refs[3 * di + i]
            if d == 1:
                o_ref[...] = qkv[i].astype(o_ref.dtype)
                continue
            for res in range(d):
                for j in range(nslab):
                    rows = qkv_scr[i * nslab + j, pl.ds(res, tm // d, stride=d), :]
                    o_ref[:, res * aw + j * LANES:res * aw + (j + 1) * LANES] = rows.astype(o_ref.dtype)

    f = h[:, 5 * aw:]
    rows = lax.broadcasted_iota(I32, f.shape, 0)
    prev = jnp.where(rows == 0, carry_ref[SUBLANES - 1:SUBLANES, :], pltpu.roll(f, 1, axis=0))
    carry_ref[...] = f[tm - SUBLANES:tm, :]
    f = f + (prev - f) * mu_ref[...]

    r = f[:, 0:rw]
    k = f[:, rw:2 * rw]
    v = f[:, 2 * rw:3 * rw]
    la = f[:, 3 * rw:3 * rw + LANES]
    gl = f[:, 3 * rw + LANES:]
    w0, a0, k_k, k_a, r_k = (vec_ref[i:i + 1, :] for i in range(5))

    z = w0 + _dot(jnp.tanh(la).astype(BF16), wd_ref[...])
    w = -_softplus(-z) - 0.5
    ld_ref[...] = -jnp.exp(w)
    a = _sigmoid(a0 + _dot(la.astype(BF16), wa_ref[...]))
    g_ref[...] = _dot(_sigmoid(gl).astype(BF16), wg_ref[...])

    bd = bd_ref[...]
    kk = k * k_k
    nrm = jnp.sqrt(_dot_hl(kk * kk, bd))
    kk = kk / jnp.maximum(nrm, 1e-12)
    kp = k * (1.0 + (a - 1.0) * k_a)
    r_ref[...] = r
    kp_ref[...] = kp
    vv_ref[...] = v
    kk_ref[...] = kk
    b_ref[...] = kk * a
    bon_ref[...] = _dot_hl(r * kp * r_k, bd) * v


def _rot_half_cols(w):
    d, n = w.shape
    w4 = w.reshape(d, n // HEAD_DIM, 2, HEAD_DIM // 2)
    return jnp.stack([-w4[:, :, 1, :], w4[:, :, 0, :]], axis=2).reshape(d, n)


def _inproj(x2, seq, w_in, mu_shift, w0, w_decay_up, a0, w_aaa_up, w_gate_up, k_k, k_a, r_k, tm):
    t, d = x2.shape
    aw, rw = ATTN_WIDTH, RWKV_WIDTH
    wq, wk, wv = w_in[:, 0:aw], w_in[:, aw:2 * aw], w_in[:, 2 * aw:3 * aw]
    wf = w_in[:, 3 * aw:]
    gpad = LORA_PAD - GATE_LORA
    w_all = jnp.concatenate(
        [wq, _rot_half_cols(wq), wk, _rot_half_cols(wk), wv, wf, jnp.zeros((d, gpad), F32)], axis=1).astype(BF16)
    mu = jnp.concatenate([mu_shift, jnp.zeros((gpad,), F32)])[None, :]
    nf = mu.shape[1]
    wd = jnp.concatenate([w_decay_up, jnp.zeros((AAA_LORA, rw), F32)], axis=0).astype(BF16)
    wa = jnp.concatenate([jnp.zeros((DECAY_LORA, rw), F32), w_aaa_up], axis=0).astype(BF16)
    wg = jnp.concatenate([w_gate_up, jnp.zeros((gpad, rw), F32)], axis=0).astype(BF16)
    vec = jnp.stack([w0, a0, k_k, k_a, r_k.reshape(-1), w0 * 0, w0 * 0, w0 * 0])
    head = jnp.arange(rw) // HEAD_DIM
    bd = (head[:, None] == head[None, :]).astype(BF16)
    half = HEAD_DIM // 2
    inv_freq = ROPE_THETA ** (-jnp.arange(half, dtype=F32) * 2.0 / HEAD_DIM)
    ang = jnp.arange(seq, dtype=F32)[:, None] * inv_freq[None, :]
    cos = jnp.tile(jnp.cos(ang), (1, LANES // half))
    sin = jnp.tile(jnp.sin(ang), (1, LANES // half))

    nst = seq // tm
    tok = lambda b, s: (b * nst + s, 0)
    const = lambda b, s: (0, 0)
    full = lambda a: pl.BlockSpec(a.shape, const)
    dils = tuple(dl for _, dl in ATTN_PATTERNS)
    qkv_specs, qkv_shapes = [], []
    for dl in dils:
        qkv_specs += [pl.BlockSpec((tm // dl, dl * aw), tok)] * 3
        qkv_shapes += [jax.ShapeDtypeStruct((t // dl, dl * aw), BF16)] * 3
    out_f = jax.ShapeDtypeStruct((t, rw), F32)
    outs = pl.pallas_call(
        functools.partial(_inproj_kernel, dils),
        grid=(t // seq, nst),
        in_specs=[pl.BlockSpec((tm, d), tok), full(w_all),
                  pl.BlockSpec((tm, LANES), lambda b, s: (s, 0)), pl.BlockSpec((tm, LANES), lambda b, s: (s, 0)),
                  full(mu), full(wd), full(wa), full(wg), full(vec), full(bd)],
        out_specs=qkv_specs + [pl.BlockSpec((tm, rw), tok)] * 8,
        out_shape=qkv_shapes + [out_f] * 8,
        scratch_shapes=[pltpu.VMEM((SUBLANES, nf), F32), pltpu.VMEM((3 * aw // LANES, tm, LANES), F32)],
        compiler_params=_cparams("arbitrary", "arbitrary"),
        name="inproj",
    )(x2, w_all, cos, sin, mu, wd, wa, wg, vec, bd)
    nq = 3 * len(dils)
    return [outs[3 * i:3 * i + 3] for i in range(len(dils))], outs[nq:]


def _dotf(a, b, nt=False, passes=3):
    dot = _dot_nt if nt else _dot
    if passes == 1:
        return dot(a.astype(BF16), b.astype(BF16))
    ah, al = _split_bf16(a)
    if passes == 2:
        bh = b.astype(BF16)
        return dot(ah, bh) + dot(al, bh)
    bh, bl = _split_bf16(b)
    return dot(ah, bh) + (dot(ah, bl) + dot(al, bh))


RWKV_PASSES = dict(s8=1, inv=1, w1=1, au=1, ry=1, gh=1, yh=1)


def _dot3(a_bf16, b):
    b1 = b.astype(BF16)
    r1 = b - b1.astype(F32)
    b2 = r1.astype(BF16)
    b3 = (r1 - b2.astype(F32)).astype(BF16)
    return _dot(a_bf16, b1) + (_dot(a_bf16, b2) + _dot(a_bf16, b3))


def _rwkv_kernel(r_ref, ld_ref, kp_ref, v_ref, kk_ref, b_ref, g_ref, bon_ref, gn_ref, bd_ref, o_ref, h_ref):
    c = pl.program_id(1)
    nbat, ch, rw = r_ref.shape
    npairs = rw // PAIR

    @pl.when(c == 0)
    def _():
        h_ref[...] = jnp.zeros_like(h_ref)

    ri = lax.broadcasted_iota(I32, (ch, ch), 0)
    ci = lax.broadcasted_iota(I32, (ch, ch), 1)
    tril = (ri >= ci).astype(BF16)
    bd = bd_ref[...]
    row = lax.broadcasted_iota(I32, (ch, PAIR), 0)
    col = lax.broadcasted_iota(I32, (ch, PAIR), 1)
    first = col < HEAD_DIM
    jj = col & (HEAD_DIM - 1)
    strict = jj < row
    incl = jj <= row
    eye = (jj == row).astype(F32)
    row2 = lax.broadcasted_iota(I32, (PAIR, PAIR), 0)
    col2 = lax.broadcasted_iota(I32, (PAIR, PAIR), 1)
    same_head = (row2 < HEAD_DIM) == (col2 < HEAD_DIM)
    diag2 = row2 == col2
    zeros_cp = jnp.zeros((ch, PAIR), F32)

    def bdiag(y):
        return jnp.concatenate([jnp.where(first, y, 0.0), jnp.where(first, 0.0, y)], axis=0)

    ps = RWKV_PASSES
    units = []
    for bi in range(nbat):
        ld = ld_ref[bi]
        cum = _dot3(tril, ld)
        tot = cum[ch - 1:ch, :]
        a_t = -kk_ref[bi] * jnp.exp(cum - ld)
        pinv = jnp.exp(-cum)
        b_t = b_ref[bi] * pinv
        k_t = kp_ref[bi] * pinv
        r_t = r_ref[bi] * jnp.exp(cum)
        pend = jnp.exp(tot - cum)
        b_end = b_ref[bi] * pend
        k_end = kp_ref[bi] * pend
        p_tot = jnp.exp(tot)
        v_all = v_ref[bi]
        for p in range(npairs):
            sl = slice(p * PAIR, (p + 1) * PAIR)
            units.append(dict(h=bi * npairs + p, a=a_t[:, sl], r=r_t[:, sl], b=b_t[:, sl], k=k_t[:, sl],
                              v=v_all[:, sl], be=b_end[:, sl], ke=k_end[:, sl], pt=p_tot[:, sl]))

    for u in units:
        u["s8"] = _dotf(jnp.concatenate([u["a"], u["r"]], axis=0),
                        jnp.concatenate([bdiag(u["b"]), bdiag(u["k"])], axis=0), nt=True, passes=ps["s8"])
    for u in units:
        s8 = u.pop("s8")
        u["l"] = jnp.where(strict, s8[0:ch, 0:PAIR], 0.0)
        u["ak"] = jnp.where(strict, s8[0:ch, PAIR:], 0.0)
        u["mrb"] = jnp.where(incl, s8[ch:, 0:PAIR], 0.0)
        u["mrk"] = jnp.where(incl, s8[ch:, PAIR:], 0.0)
        u["t"] = eye + u["l"]
    for u in units:
        u["lk"] = _dotf(u["l"], bdiag(u["l"]), passes=ps["inv"])
        u["w1"] = _dotf(u["ak"], bdiag(u["v"]), passes=ps["w1"])
    for _ in range(max(ch.bit_length() - 3, 0)):
        for u in units:
            u["both"] = _dotf(jnp.concatenate([u["t"], u["lk"]], axis=0), bdiag(u["lk"]), passes=ps["inv"])
        for u in units:
            both = u.pop("both")
            u["t"] = u["t"] + both[0:ch]
            u["lk"] = both[ch:]
    for u in units:
        u["t"] = u["t"] + _dotf(u["t"], bdiag(u["lk"]), passes=ps["inv"])
    for u in units:
        u["au"] = _dotf(u["t"], jnp.concatenate([bdiag(u["a"]), bdiag(u["w1"])], axis=1), passes=ps["au"])
    for u in units:
        a_hat, u_loc = u["au"][:, 0:PAIR], u["au"][:, PAIR:]
        rhs = jnp.concatenate([
            jnp.concatenate([bdiag(a_hat), bdiag(u_loc)], axis=1),
            jnp.concatenate([jnp.zeros((2 * ch, PAIR), F32), bdiag(u["v"])], axis=1)], axis=0)
        u["ry"] = _dotf(jnp.concatenate([u["mrb"], u["mrk"]], axis=1), rhs, passes=ps["ry"])
        bkt = jnp.concatenate([u["be"], u["ke"]], axis=0).T
        u["gh"] = _dotf(bkt, jnp.concatenate([u["au"], jnp.concatenate([zeros_cp, u["v"]], axis=1)], axis=0),
                        passes=ps["gh"])
    for u in units:
        r_hat = u["r"] + u["ry"][:, 0:PAIR]
        g_mat = jnp.where(same_head, u["gh"][:, 0:PAIR], 0.0) + jnp.where(diag2, u["pt"], 0.0)
        u["yh"] = _dotf(jnp.concatenate([r_hat, g_mat], axis=0), h_ref[u["h"]], passes=ps["yh"])
    for u in units:
        h_ref[u["h"]] = u["yh"][ch:] + jnp.where(same_head, u["gh"][:, PAIR:], 0.0)
        u["y"] = u["yh"][0:ch] + u["ry"][:, PAIR:]

    inv_n = 1.0 / HEAD_DIM
    for bi in range(nbat):
        y = jnp.concatenate([u["y"] for u in units[bi * npairs:(bi + 1) * npairs]], axis=1)
        yc = y - _dot_hl(y, bd) * inv_n
        var = _dot_hl(yc * yc, bd) * inv_n
        yn = yc * lax.rsqrt(var + GN_EPS) * gn_ref[0:1, :] + gn_ref[1:2, :]
        o_ref[bi] = (yn + bon_ref[bi]) * g_ref[bi]


def _rwkv(feats, bsz, seq, lnx_g, lnx_b, nbat):
    t, rw = feats[0].shape
    ch = RWKV_CHUNK
    gn = jnp.concatenate([lnx_g[None], lnx_b[None], jnp.zeros((SUBLANES - 2, rw), F32)], axis=0)
    head = jnp.arange(rw) // HEAD_DIM
    bd = (head[:, None] == head[None, :]).astype(BF16)
    tok = pl.BlockSpec((nbat, ch, rw), lambda bb, c: (bb, c, 0))
    const = lambda a: pl.BlockSpec(a.shape, lambda bb, c: (0, 0))
    out = pl.pallas_call(
        _rwkv_kernel,
        grid=(bsz // nbat, seq // ch),
        in_specs=[tok] * 8 + [const(gn), const(bd)],
        out_specs=tok,
        out_shape=jax.ShapeDtypeStruct((bsz, seq, rw), F32),
        scratch_shapes=[pltpu.VMEM((nbat * (rw // PAIR), PAIR, PAIR), F32)],
        compiler_params=_cparams("arbitrary", "arbitrary"),
        name="rwkv",
    )(*[f.reshape(bsz, seq, rw) for f in feats], gn, bd)
    return out.reshape(t, rw)


NEG_BIG = -1e30


def _attn_kernel(n_back, q_ref, kc_ref, kp_ref, vc_ref, vp_ref, o_ref, lse_ref):
    n = pl.program_id(2)
    qb = q_ref.shape[0]
    qi = lax.broadcasted_iota(I32, (qb, 2 * qb), 0)
    ki = lax.broadcasted_iota(I32, (qb, 2 * qb), 1)
    dist = qb + qi - ki
    has_prev = jnp.where(n > 0, 0, qb)
    valid = (dist >= 0) & (dist <= n_back) & (ki >= has_prev)
    first = lax.broadcasted_iota(I32, (qb, PAIR), 1) < HEAD_DIM
    for p in range(q_ref.shape[1] // PAIR):
        sl = slice(p * PAIR, (p + 1) * PAIR)
        q2 = q_ref[:, sl]
        k2 = jnp.concatenate([kp_ref[:, sl], kc_ref[:, sl]], axis=0)
        v2 = jnp.concatenate([vp_ref[:, sl], vc_ref[:, sl]], axis=0)
        o_h, lse_h = [], []
        for hh in range(2):
            keep = first if hh == 0 else jnp.logical_not(first)
            s = _dot_nt(jnp.where(keep, q2, jnp.zeros_like(q2)), k2)
            s = jnp.where(valid, s, NEG_BIG)
            m = jnp.max(s, axis=1, keepdims=True)
            pe = jnp.where(valid, jnp.exp(s - m), 0.0)
            l = jnp.sum(pe, axis=1, keepdims=True)
            o_h.append(_dot(pe.astype(BF16), v2) / l)
            lse_h.append(m + jnp.log(l))
        o_ref[:, sl] = jnp.where(first, o_h[0], o_h[1])
        lse_ref[:, sl] = jnp.where(first, lse_h[0], lse_h[1])


def _attention(q, k, v, bsz, seq, window, dilation):
    aw = q.shape[1] // dilation
    nb = seq // dilation // Q_BLOCK
    cur = pl.BlockSpec((Q_BLOCK, aw), lambda b, r, n: (b * nb + n, r))
    prv = pl.BlockSpec((Q_BLOCK, aw), lambda b, r, n: (b * nb + jnp.maximum(n - 1, 0), r))
    shp = jax.ShapeDtypeStruct(q.shape, F32)
    return pl.pallas_call(
        functools.partial(_attn_kernel, window // dilation),
        grid=(bsz, dilation, nb),
        in_specs=[cur, cur, prv, cur, prv],
        out_specs=[cur, cur],
        out_shape=[shp, shp],
        compiler_params=_cparams("arbitrary", "arbitrary", "arbitrary"),
        name=f"attn_d{dilation}",
    )(q, k, k, v, v)


def _layer_norm(y, g, b):
    mu = jnp.mean(y, axis=-1, keepdims=True)
    yc = y - mu
    var = jnp.mean(yc * yc, axis=-1, keepdims=True)
    return yc * lax.rsqrt(var + LN_EPS) * g + b


def _outproj_kernel(dils, *refs):
    npat = len(dils)
    o_refs = refs[0:npat]
    lse_refs = refs[npat:2 * npat]
    rw_ref, x_ref, wo_ref, ln_ref, y_ref, yb_ref, scr = refs[2 * npat:]
    tm, aw = rw_ref.shape

    def token_major(ref, d, slot):
        if d == 1:
            return ref[...]
        nslab = aw // LANES
        for res in range(d):
            for j in range(nslab):
                scr[slot * nslab + j, pl.ds(res, tm // d, stride=d), :] = (
                    ref[:, res * aw + j * LANES:res * aw + (j + 1) * LANES])
        return jnp.concatenate([scr[slot * nslab + j] for j in range(nslab)], axis=1)

    lses = [token_major(r, d, 2 * i) for i, (r, d) in enumerate(zip(lse_refs, dils))]
    outs = [token_major(r, d, 2 * i + 1) for i, (r, d) in enumerate(zip(o_refs, dils))]
    m = functools.reduce(jnp.maximum, lses)
    es = [jnp.exp(z - m) for z in lses]
    den = functools.reduce(lambda a, b: a + b, es)
    attn = functools.reduce(lambda a, b: a + b, [(e / den) * o for e, o in zip(es, outs)])
    mix = _dot(attn.astype(BF16), wo_ref[0:aw, :]) + _dot(rw_ref[...].astype(BF16), wo_ref[aw:, :])
    y = _layer_norm(ALPHA * x_ref[...] + mix, ln_ref[0:1, :], ln_ref[1:2, :])
    y_ref[...] = y
    yb_ref[...] = y.astype(yb_ref.dtype)


def _outproj(os_, lses, dils, rw, x2, w_out, ln_g, ln_b, tm):
    t, d = x2.shape
    aw = rw.shape[1]
    ln = jnp.concatenate([ln_g[None], ln_b[None], jnp.zeros((SUBLANES - 2, d), F32)], axis=0)
    wo = w_out.astype(BF16)
    tok = lambda w: pl.BlockSpec((tm, w), lambda i: (i, 0))
    view = [pl.BlockSpec((tm // dl, dl * aw), lambda i: (i, 0)) for dl in dils]
    const = lambda a: pl.BlockSpec(a.shape, lambda i: (0, 0))
    return pl.pallas_call(
        functools.partial(_outproj_kernel, dils),
        grid=(t // tm,),
        in_specs=view + view + [tok(aw), tok(d), const(wo), const(ln)],
        out_specs=[tok(d), tok(d)],
        out_shape=[jax.ShapeDtypeStruct((t, d), F32), jax.ShapeDtypeStruct((t, d), BF16)],
        scratch_shapes=[pltpu.VMEM((2 * len(dils) * aw // LANES, tm, LANES), F32)],
        compiler_params=_cparams("arbitrary"),
        name="outproj",
    )(*os_, *lses, rw, x2, wo, ln)


EXPERT_ROWS = 256


def _lanes(col_rep, n):
    return jnp.concatenate([col_rep] * (n // LANES), axis=1)


def _route_kernel(x_ref, rwt_ref, bias_ref, tri_ref, ones_ref, idx_ref, gate_ref, rank_ref, cnt_ref, base_ref):
    step = pl.program_id(0)
    ne = rwt_ref.shape[0]
    tm = x_ref.shape[0]
    gsz = ne // N_GROUPS
    ninf = -jnp.inf

    @pl.when(step == 0)
    def _():
        base_ref[...] = jnp.zeros_like(base_ref)

    aff = _sigmoid(_dotf(rwt_ref[...], x_ref[...], nt=True))
    sel = aff + _lanes(bias_ref[...], tm)

    sel3 = sel.reshape(N_GROUPS, gsz, tm)
    rid = lax.broadcasted_iota(I32, sel3.shape, 1).astype(F32)
    m1 = jnp.max(sel3, axis=1, keepdims=True)
    i1 = jnp.min(jnp.where(sel3 == m1, rid, float(gsz)), axis=1, keepdims=True)
    m2 = jnp.max(jnp.where(rid == i1, ninf, sel3), axis=1, keepdims=True)
    gsc = (m1 + m2).reshape(N_GROUPS, tm)
    gid = lax.broadcasted_iota(I32, gsc.shape, 0).astype(F32)
    keep = jnp.zeros(gsc.shape, F32)
    for _ in range(TOPK_GROUPS):
        gm = jnp.max(gsc, axis=0, keepdims=True)
        gi = jnp.min(jnp.where(gsc == gm, gid, float(N_GROUPS)), axis=0, keepdims=True)
        hit = gid == gi
        keep = jnp.where(hit, 1.0, keep)
        gsc = jnp.where(hit, ninf, gsc)
    cand = jnp.where(keep.reshape(N_GROUPS, 1, tm) > 0.0, sel3, ninf).reshape(ne, tm)

    eid = lax.broadcasted_iota(I32, (ne, tm), 0).astype(F32)
    hits, idxs, graw = [], [], []
    for _ in range(TOP_K):
        m = jnp.max(cand, axis=0, keepdims=True)
        ij = jnp.min(jnp.where(cand == m, eid, float(ne)), axis=0, keepdims=True)
        hit = eid == ij
        hits.append(hit)
        idxs.append(ij)
        graw.append(jnp.sum(jnp.where(hit, aff, 0.0), axis=0, keepdims=True))
        cand = jnp.where(hit, ninf, cand)
    gsum = functools.reduce(lambda a, b: a + b, graw)
    idx_ref[...] = jnp.concatenate(idxs, axis=0).astype(I32)
    gate_ref[...] = jnp.concatenate([g / gsum * ROUTED_SCALE for g in graw], axis=0)

    onehot = functools.reduce(lambda a, b: a + b, [h.astype(F32) for h in hits]).astype(BF16)
    before = _dot(onehot, tri_ref[...]) + _lanes(base_ref[...], tm)
    rank_ref[...] = jnp.concatenate(
        [jnp.sum(jnp.where(h, before, 0.0), axis=0, keepdims=True) for h in hits], axis=0).astype(I32)
    total = base_ref[...] + _dot(onehot, ones_ref[...])
    base_ref[...] = total
    cnt_ref[...] = total


def _route(x1, router_w, router_bias, tm):
    t, d = x1.shape
    ne = router_w.shape[1]
    rwt = router_w.T
    bias = jnp.broadcast_to(router_bias[:, None], (ne, LANES))
    pos = jnp.arange(tm)
    tri = (pos[:, None] < pos[None, :]).astype(BF16)
    ones = jnp.ones((tm, LANES), BF16)
    const = lambda a: pl.BlockSpec(a.shape, lambda i: (0, 0))
    tokt = pl.BlockSpec((TOP_K, tm), lambda i: (0, i))
    return pl.pallas_call(
        _route_kernel,
        grid=(t // tm,),
        in_specs=[pl.BlockSpec((tm, d), lambda i: (i, 0)), const(rwt), const(bias), const(tri), const(ones)],
        out_specs=[tokt, tokt, tokt, pl.BlockSpec((ne, LANES), lambda i: (0, 0))],
        out_shape=[jax.ShapeDtypeStruct((TOP_K, t), I32), jax.ShapeDtypeStruct((TOP_K, t), F32),
                   jax.ShapeDtypeStruct((TOP_K, t), I32), jax.ShapeDtypeStruct((ne, LANES), F32)],
        scratch_shapes=[pltpu.VMEM((ne, LANES), F32)],
        compiler_params=_cparams("arbitrary"),
        name="route",
    )(x1, rwt, bias, tri, ones)


def _dest_kernel(nblk, idx_ref, rank_ref, cnt_ref, tril_ref, dest_ref, meta_ref):
    ne = cnt_ref.shape[0]
    tm = idx_ref.shape[1]
    padded = jnp.ceil(cnt_ref[...] * (1.0 / EXPERT_ROWS)) * EXPERT_ROWS
    pad_end = _dot3(tril_ref[...], padded)
    pad_start = _lanes(pad_end - padded, tm)
    eid = lax.broadcasted_iota(I32, (ne, tm), 0)
    idx = idx_ref[...]
    rows = [jnp.sum(jnp.where(eid == idx[j:j + 1, :], pad_start, 0.0), axis=0, keepdims=True) for j in range(TOP_K)]
    dest_ref[...] = jnp.concatenate(rows, axis=0).astype(I32) + rank_ref[...]

    blk_start = (lax.broadcasted_iota(I32, (ne, nblk), 1) * EXPERT_ROWS).astype(F32)
    owner = jnp.sum((_lanes(pad_end, nblk) <= blk_start).astype(F32), axis=0, keepdims=True)
    owner = jnp.minimum(owner, float(ne - 1))
    used = jnp.max(_lanes(pad_end, nblk), axis=0, keepdims=True) * (1.0 / EXPERT_ROWS)
    meta_ref[...] = jnp.concatenate([owner, used, jnp.zeros((SUBLANES - 2, nblk), F32)], axis=0).astype(I32)


def _dest(idx, rank, cnt, nblk, tm):
    t = idx.shape[1]
    ne = cnt.shape[0]
    eid = jnp.arange(ne)
    tril = (eid[:, None] >= eid[None, :]).astype(BF16)
    tokt = pl.BlockSpec((TOP_K, tm), lambda i: (0, i))
    const = lambda a: pl.BlockSpec(a.shape, lambda i: (0, 0))
    return pl.pallas_call(
        functools.partial(_dest_kernel, nblk),
        grid=(t // tm,),
        in_specs=[tokt, tokt, const(cnt), const(tril)],
        out_specs=[tokt, pl.BlockSpec((SUBLANES, nblk), lambda i: (0, 0))],
        out_shape=[jax.ShapeDtypeStruct((TOP_K, t), I32), jax.ShapeDtypeStruct((SUBLANES, nblk), I32)],
        compiler_params=_cparams("arbitrary"),
        name="dest",
    )(idx, rank, cnt, tril)


def _row_copy(src, i, dst, j, sem):
    return pltpu.make_async_copy(src.at[pl.ds(i, 1), :], dst.at[pl.ds(j, 1), :], sem)


def _dispatch_kernel(owner_ref, used_ref, dest_ref, x_ref, xs_ref, zero_ref, sem, zsem):
    tm = x_ref.shape[0]
    nblk = owner_ref.shape[0]
    used = used_ref[0]

    @pl.when(pl.program_id(0) == 0)
    def _():
        zero_ref[...] = jnp.zeros_like(zero_ref)

        def block_copy(i):
            return pltpu.make_async_copy(zero_ref, xs_ref.at[pl.ds(i * EXPERT_ROWS, EXPERT_ROWS), :], zsem)

        def is_last(i):
            return (i == used - 1) | (owner_ref[jnp.minimum(i + 1, nblk - 1)] != owner_ref[i])

        def start(i, carry):
            @pl.when(is_last(i))
            def _():
                block_copy(i).start()
            return carry

        def wait(i, carry):
            @pl.when(is_last(i))
            def _():
                block_copy(i).wait()
            return carry

        lax.fori_loop(0, used, start, 0)
        lax.fori_loop(0, used, wait, 0)

    def issue(t, carry):
        for j in range(TOP_K):
            _row_copy(x_ref, t, xs_ref, dest_ref[j, t], sem).start(priority=j % 2)
        return carry

    def drain(t, carry):
        for j in range(TOP_K):
            _row_copy(x_ref, 0, xs_ref, 0, sem).wait()
        return carry

    lax.fori_loop(0, tm, issue, 0)
    lax.fori_loop(0, tm, drain, 0)


def _dispatch(x1, dest, owner, used, nrows, tm):
    t, d = x1.shape
    grid_spec = pltpu.PrefetchScalarGridSpec(
        num_scalar_prefetch=2,
        grid=(t // tm,),
        in_specs=[pl.BlockSpec((TOP_K, tm), lambda i, o, u: (0, i), memory_space=pltpu.SMEM),
                  pl.BlockSpec((tm, d), lambda i, o, u: (i, 0))],
        out_specs=pl.BlockSpec(memory_space=pl.ANY),
        scratch_shapes=[pltpu.VMEM((EXPERT_ROWS, d), F32), pltpu.SemaphoreType.DMA(()),
                        pltpu.SemaphoreType.DMA(())],
    )
    return pl.pallas_call(
        _dispatch_kernel,
        grid_spec=grid_spec,
        out_shape=jax.ShapeDtypeStruct((nrows, d), F32),
        compiler_params=_cparams("arbitrary"),
        name="dispatch",
    )(owner, used, dest, x1)


def _expert_kernel(owner_ref, used_ref, xs_ref, wg_ref, wu_ref, wd_ref, ys_ref, wgb_ref, wub_ref, wdb_ref):
    i = pl.program_id(0)
    prev = owner_ref[jnp.maximum(i - 1, 0)]

    @pl.when((i == 0) | (owner_ref[i] != prev))
    def _():
        wgb_ref[...] = wg_ref[...].astype(BF16)
        wub_ref[...] = wu_ref[...].astype(BF16)
        wdb_ref[...] = wd_ref[...].astype(BF16)

    @pl.when(i < used_ref[0])
    def _():
        xb = xs_ref[...].astype(BF16)
        hg = _dot(xb, wgb_ref[...])
        hb = hg * _sigmoid(hg) * _dot(xb, wub_ref[...])
        ys_ref[...] = _dot(hb.astype(BF16), wdb_ref[...])


def _experts(xs, owner, used, e_gate, e_up, e_down):
    nrows, d = xs.shape
    ff = e_gate.shape[2]
    nblk = nrows // EXPERT_ROWS
    blk = lambda i, owner_ref, used_ref: (jnp.minimum(i, used_ref[0] - 1), 0)
    wsel = lambda i, owner_ref, used_ref: (owner_ref[jnp.minimum(i, used_ref[0] - 1)], 0, 0)
    grid_spec = pltpu.PrefetchScalarGridSpec(
        num_scalar_prefetch=2,
        grid=(nblk,),
        in_specs=[pl.BlockSpec((EXPERT_ROWS, d), blk),
                  pl.BlockSpec((None, d, ff), wsel), pl.BlockSpec((None, d, ff), wsel),
                  pl.BlockSpec((None, ff, d), wsel)],
        out_specs=pl.BlockSpec((EXPERT_ROWS, d), blk),
        scratch_shapes=[pltpu.VMEM((d, ff), BF16), pltpu.VMEM((d, ff), BF16), pltpu.VMEM((ff, d), BF16)],
    )
    return pl.pallas_call(
        _expert_kernel,
        grid_spec=grid_spec,
        out_shape=jax.ShapeDtypeStruct((nrows, d), F32),
        compiler_params=_cparams("arbitrary"),
        name="experts",
    )(owner, used, xs, e_gate, e_up, e_down)


def _final_kernel(dest_ref, dnext_ref, x_ref, xb_ref, gate_ref, ys_ref, sg_ref, su_ref, sd_ref, ln_ref, o_ref,
                  buf_ref, sems):
    tm = x_ref.shape[0]
    step = pl.program_id(0)
    slot = step % 2

    def gather(dref, sl):
        def issue(t, carry):
            for j in range(TOP_K):
                _row_copy(ys_ref, dref[j, t], buf_ref.at[sl, j], t, sems.at[sl]).start(priority=j % 2)
            return carry
        lax.fori_loop(0, tm, issue, 0)

    @pl.when(step == 0)
    def _():
        gather(dest_ref, 0)

    @pl.when(step + 1 < pl.num_programs(0))
    def _():
        gather(dnext_ref, 1 - slot)

    xb = xb_ref[...]
    hg = _dot(xb, sg_ref[...])
    hs = hg * _sigmoid(hg) * _dot(xb, su_ref[...])
    ffn = _dot(hs.astype(BF16), sd_ref[...])
    gates = gate_ref[...].T

    def drain(t, carry):
        for j in range(TOP_K):
            _row_copy(ys_ref, 0, buf_ref.at[slot, j], 0, sems.at[slot]).wait()
        return carry

    lax.fori_loop(0, tm, drain, 0)
    routed = functools.reduce(lambda a, b: a + b, [gates[:, j:j + 1] * buf_ref[slot, j] for j in range(TOP_K)])
    o_ref[...] = _layer_norm(ALPHA * x_ref[...] + (routed + ffn), ln_ref[0:1, :], ln_ref[1:2, :])


def _final(x1, x1b, dest, gate, ys, s_gate, s_up, s_down, ln_g, ln_b, tm):
    t, d = x1.shape
    ln = jnp.concatenate([ln_g[None], ln_b[None], jnp.zeros((SUBLANES - 2, d), F32)], axis=0)
    sg, su, sd = s_gate.astype(BF16), s_up.astype(BF16), s_down.astype(BF16)
    tok = pl.BlockSpec((tm, d), lambda i: (i, 0))
    const = lambda a: pl.BlockSpec(a.shape, lambda i: (0, 0))
    nt = t // tm
    return pl.pallas_call(
        _final_kernel,
        grid=(nt,),
        in_specs=[pl.BlockSpec((TOP_K, tm), lambda i: (0, i), memory_space=pltpu.SMEM),
                  pl.BlockSpec((TOP_K, tm), lambda i: (0, jnp.minimum(i + 1, nt - 1)), memory_space=pltpu.SMEM),
                  tok, tok, pl.BlockSpec((TOP_K, tm), lambda i: (0, i)),
                  pl.BlockSpec(memory_space=pl.ANY), const(sg), const(su), const(sd), const(ln)],
        out_specs=tok,
        out_shape=jax.ShapeDtypeStruct((t, d), F32),
        scratch_shapes=[pltpu.VMEM((2, TOP_K, tm, d), F32), pltpu.SemaphoreType.DMA((2,))],
        compiler_params=_cparams("arbitrary"),
        name="final",
    )(dest, dest, x1, x1b, gate, ys, sg, su, sd, ln)


def _moe(x1, x1b, router_w, router_bias, e_gate, e_up, e_down, s_gate, s_up, s_down, ln_g, ln_b):
    t, d = x1.shape
    ne = router_w.shape[1]
    nblk = -(-(t * TOP_K + ne * (EXPERT_ROWS - 1)) // EXPERT_ROWS)
    nblk_pad = -(-nblk // LANES) * LANES
    idx, gate, rank, cnt = _route(x1, router_w, router_bias, tm=512)
    dest, meta = _dest(idx, rank, cnt, nblk_pad, tm=512)
    owner, used = meta[0], meta[1, 0:1]
    xs = _dispatch(x1, dest, owner, used, nblk * EXPERT_ROWS, tm=256)
    ys = _experts(xs, owner, used, e_gate, e_up, e_down)
    return _final(x1, x1b, dest, gate, ys, s_gate, s_up, s_down, ln_g, ln_b, tm=256)


def kernel(x, w_in, mu_shift, w0, w_decay_up, a0, w_aaa_up, w_gate_up, k_k, k_a, r_k, lnx_g, lnx_b, w_out,
           ln1_g, ln1_b, router_w, router_bias, e_gate, e_up, e_down, s_gate, s_up, s_down, ln2_g, ln2_b):
    bsz, seq, d = x.shape
    x2 = x.reshape(bsz * seq, d)
    qkvs, feats = _inproj(x2, seq, w_in[0], mu_shift[0], w0[0], w_decay_up[0], a0[0], w_aaa_up[0], w_gate_up[0],
                          k_k[0], k_a[0], r_k[0], tm=256)
    rw = _rwkv(feats, bsz, seq, lnx_g[0], lnx_b[0], nbat=4)
    x1, x1b = _mixer_tail(qkvs, rw, x2, bsz, seq, w_out[0], ln1_g[0], ln1_b[0])
    out = _moe(x1, x1b, router_w[0], router_bias[0], e_gate[0], e_up[0], e_down[0], s_gate[0], s_up[0], s_down[0],
               ln2_g[0], ln2_b[0])
    return out.reshape(bsz, seq, d)


def _mixer_tail(qkvs, rw, x2, bsz, seq, w_out, ln_g, ln_b):
    res = [_attention(q, k, v, bsz, seq, window, dilation)
           for (q, k, v), (window, dilation) in zip(qkvs, ATTN_PATTERNS)]
    dils = tuple(dl for _, dl in ATTN_PATTERNS)
    return _outproj([o for o, _ in res], [l for _, l in res], dils, rw, x2, w_out, ln_g, ln_b, tm=512)
```

```python
import functools

import jax
import jax.numpy as jnp
from jax import lax
from jax.experimental import pallas as pl
from jax.experimental.pallas import tpu as pltpu

F32 = jnp.float32
BF16 = jnp.bfloat16
I32 = jnp.int32
I16 = jnp.int16
U32 = jnp.uint32

LANES = 128
SUBLANES = 8
VMEM_LIMIT_BYTES = 56 * 1024 * 1024

HEAD_DIM = 64
ATTN_HEADS = 8
RWKV_HEADS = 8
ATTN_WIDTH = ATTN_HEADS * HEAD_DIM
RWKV_WIDTH = RWKV_HEADS * HEAD_DIM
ATTN_PATTERNS = ((128, 1), (512, 4), (2048, 16))
Q_BLOCK = 128
ROPE_THETA = 10000.0
DECAY_LORA = 64
AAA_LORA = 64
GATE_LORA = 160
GN_EPS = 64e-5
LN_EPS = 1e-5
TOP_K = 8
N_GROUPS = 8
TOPK_GROUPS = 4
ROUTED_SCALE = 2.5
DEPTH = 1
ALPHA = (2.0 * DEPTH) ** 0.25

RWKV_CHUNK = 64
PAIR = 2 * HEAD_DIM
LORA_PAD = 2 * LANES


def _cparams(*sem):
    return pltpu.CompilerParams(dimension_semantics=sem, vmem_limit_bytes=VMEM_LIMIT_BYTES)


def _split_bf16(a):
    hi = a.astype(BF16)
    lo = (a - hi.astype(F32)).astype(BF16)
    return hi, lo


def _dot(a, b):
    return jnp.dot(a, b, preferred_element_type=F32)


def _dot_nt(a, b):
    return lax.dot_general(a, b, (((1,), (1,)), ((), ())), preferred_element_type=F32)


def _dot_hl(a_f32, b_bf16):
    hi, lo = _split_bf16(a_f32)
    return _dot(hi, b_bf16) + _dot(lo, b_bf16)


def _softplus(z):
    return jnp.maximum(z, 0.0) + jnp.log(1.0 + jnp.exp(-jnp.abs(z)))


def _sigmoid(z):
    return 1.0 / (1.0 + jnp.exp(-z))


def _inproj_kernel(dils, x_ref, w_ref, cos_ref, sin_ref, mu_ref, wd_ref, wa_ref, wg_ref, vec_ref, bd_ref, *refs):
    nq = 3 * len(dils)
    qkv_refs = refs[0:nq]
    r_ref, ld_ref, kp_ref, vv_ref, kk_ref, b_ref, g_ref, bon_ref, carry_ref, qkv_scr = refs[nq:]
    s = pl.program_id(1)
    tm = x_ref.shape[0]
    aw = ATTN_WIDTH
    rw = RWKV_WIDTH

    @pl.when(s == 0)
    def _():
        carry_ref[...] = jnp.zeros_like(carry_ref)

    h = _dot(x_ref[...].astype(BF16), w_ref[...])
    reps = aw // cos_ref.shape[1]
    cos = jnp.concatenate([cos_ref[...]] * reps, axis=1)
    sin = jnp.concatenate([sin_ref[...]] * reps, axis=1)
    qkv = [(h[:, 0:aw] * cos + h[:, aw:2 * aw] * sin) * (HEAD_DIM ** -0.5),
           h[:, 2 * aw:3 * aw] * cos + h[:, 3 * aw:4 * aw] * sin,
           h[:, 4 * aw:5 * aw]]
    nslab = aw // LANES
    for i in range(3):
        for j in range(nslab):
            qkv_scr[i * nslab + j] = qkv[i][:, j * LANES:(j + 1) * LANES]
    for di, d in enumerate(dils):
        for i in range(3):
            o_ref = qkv_refs[3 * di + i]
            if d == 1:
                o_ref[...] = qkv[i].astype(o_ref.dtype)
                continue
            for res in range(d):
                for j in range(nslab):
                    rows = qkv_scr[i * nslab + j, pl.ds(res, tm // d, stride=d), :]
                    o_ref[:, res * aw + j * LANES:res * aw + (j + 1) * LANES] = rows.astype(o_ref.dtype)

    f = h[:, 5 * aw:]
    rows = lax.broadcasted_iota(I32, f.shape, 0)
    prev = jnp.where(rows == 0, carry_ref[SUBLANES - 1:SUBLANES, :], pltpu.roll(f, 1, axis=0))
    carry_ref[...] = f[tm - SUBLANES:tm, :]
    f = f + (prev - f) * mu_ref[...]

    r = f[:, 0:rw]
    k = f[:, rw:2 * rw]
    v = f[:, 2 * rw:3 * rw]
    la = f[:, 3 * rw:3 * rw + LANES]
    gl = f[:, 3 * rw + LANES:]
    w0, a0, k_k, k_a, r_k = (vec_ref[i:i + 1, :] for i in range(5))

    z = w0 + _dot(jnp.tanh(la).astype(BF16), wd_ref[...])
    w = -_softplus(-z) - 0.5
    ld_ref[...] = -jnp.exp(w)
    a = _sigmoid(a0 + _dot(la.astype(BF16), wa_ref[...]))
    g_ref[...] = _dot(_sigmoid(gl).astype(BF16), wg_ref[...])

    bd = bd_ref[...]
    kk = k * k_k
    nrm = jnp.sqrt(_dot_hl(kk * kk, bd))
    kk = kk / jnp.maximum(nrm, 1e-12)
    kp = k * (1.0 + (a - 1.0) * k_a)
    r_ref[...] = r
    kp_ref[...] = kp
    vv_ref[...] = v
    kk_ref[...] = kk
    b_ref[...] = kk * a
    bon_ref[...] = _dot_hl(r * kp * r_k, bd) * v


def _rot_half_cols(w):
    d, n = w.shape
    w4 = w.reshape(d, n // HEAD_DIM, 2, HEAD_DIM // 2)
    return jnp.stack([-w4[:, :, 1, :], w4[:, :, 0, :]], axis=2).reshape(d, n)


def _inproj(x2, seq, w_in, mu_shift, w0, w_decay_up, a0, w_aaa_up, w_gate_up, k_k, k_a, r_k, tm):
    t, d = x2.shape
    aw, rw = ATTN_WIDTH, RWKV_WIDTH
    wq, wk, wv = w_in[:, 0:aw], w_in[:, aw:2 * aw], w_in[:, 2 * aw:3 * aw]
    wf = w_in[:, 3 * aw:]
    gpad = LORA_PAD - GATE_LORA
    w_all = jnp.concatenate(
        [wq, _rot_half_cols(wq), wk, _rot_half_cols(wk), wv, wf, jnp.zeros((d, gpad), F32)], axis=1).astype(BF16)
    mu = jnp.concatenate([mu_shift, jnp.zeros((gpad,), F32)])[None, :]
    nf = mu.shape[1]
    wd = jnp.concatenate([w_decay_up, jnp.zeros((AAA_LORA, rw), F32)], axis=0).astype(BF16)
    wa = jnp.concatenate([jnp.zeros((DECAY_LORA, rw), F32), w_aaa_up], axis=0).astype(BF16)
    wg = jnp.concatenate([w_gate_up, jnp.zeros((gpad, rw), F32)], axis=0).astype(BF16)
    vec = jnp.stack([w0, a0, k_k, k_a, r_k.reshape(-1), w0 * 0, w0 * 0, w0 * 0])
    head = jnp.arange(rw) // HEAD_DIM
    bd = (head[:, None] == head[None, :]).astype(BF16)
    half = HEAD_DIM // 2
    inv_freq = ROPE_THETA ** (-jnp.arange(half, dtype=F32) * 2.0 / HEAD_DIM)
    ang = jnp.arange(seq, dtype=F32)[:, None] * inv_freq[None, :]
    cos = jnp.tile(jnp.cos(ang), (1, LANES // half))
    sin = jnp.tile(jnp.sin(ang), (1, LANES // half))

    nst = seq // tm
    tok = lambda b, s: (b * nst + s, 0)
    const = lambda b, s: (0, 0)
    full = lambda a: pl.BlockSpec(a.shape, const)
    dils = tuple(dl for _, dl in ATTN_PATTERNS)
    qkv_specs, qkv_shapes = [], []
    for dl in dils:
        qkv_specs += [pl.BlockSpec((tm // dl, dl * aw), tok)] * 3
        qkv_shapes += [jax.ShapeDtypeStruct((t // dl, dl * aw), BF16)] * 3
    out_f = jax.ShapeDtypeStruct((t, rw), F32)
    outs = pl.pallas_call(
        functools.partial(_inproj_kernel, dils),
        grid=(t // seq, nst),
        in_specs=[pl.BlockSpec((tm, d), tok), full(w_all),
                  pl.BlockSpec((tm, LANES), lambda b, s: (s, 0)), pl.BlockSpec((tm, LANES), lambda b, s: (s, 0)),
                  full(mu), full(wd), full(wa), full(wg), full(vec), full(bd)],
        out_specs=qkv_specs + [pl.BlockSpec((tm, rw), tok)] * 8,
        out_shape=qkv_shapes + [out_f] * 8,
        scratch_shapes=[pltpu.VMEM((SUBLANES, nf), F32), pltpu.VMEM((3 * aw // LANES, tm, LANES), F32)],
        compiler_params=_cparams("arbitrary", "arbitrary"),
        name="inproj",
    )(x2, w_all, cos, sin, mu, wd, wa, wg, vec, bd)
    nq = 3 * len(dils)
    return [outs[3 * i:3 * i + 3] for i in range(len(dils))], outs[nq:]


def _dotf(a, b, nt=False, passes=3):
    dot = _dot_nt if nt else _dot
    if passes == 1:
        return dot(a.astype(BF16), b.astype(BF16))
    ah, al = _split_bf16(a)
    if passes == 2:
        bh = b.astype(BF16)
        return dot(ah, bh) + dot(al, bh)
    bh, bl = _split_bf16(b)
    return dot(ah, bh) + (dot(ah, bl) + dot(al, bh))


RWKV_PASSES = dict(s8=1, inv=1, w1=1, au=1, ry=1, gh=1, yh=1)


def _dot3(a_bf16, b):
    b1 = b.astype(BF16)
    r1 = b - b1.astype(F32)
    b2 = r1.astype(BF16)
    b3 = (r1 - b2.astype(F32)).astype(BF16)
    return _dot(a_bf16, b1) + (_dot(a_bf16, b2) + _dot(a_bf16, b3))


def _rwkv_kernel(r_ref, ld_ref, kp_ref, v_ref, kk_ref, b_ref, g_ref, bon_ref, gn_ref, bd_ref, o_ref, h_ref):
    c = pl.program_id(1)
    nbat, ch, rw = r_ref.shape
    npairs = rw // PAIR

    @pl.when(c == 0)
    def _():
        h_ref[...] = jnp.zeros_like(h_ref)

    ri = lax.broadcasted_iota(I32, (ch, ch), 0)
    ci = lax.broadcasted_iota(I32, (ch, ch), 1)
    tril = (ri >= ci).astype(BF16)
    bd = bd_ref[...]
    row = lax.broadcasted_iota(I32, (ch, PAIR), 0)
    col = lax.broadcasted_iota(I32, (ch, PAIR), 1)
    first = col < HEAD_DIM
    jj = col & (HEAD_DIM - 1)
    strict = jj < row
    incl = jj <= row
    eye = (jj == row).astype(F32)
    row2 = lax.broadcasted_iota(I32, (PAIR, PAIR), 0)
    col2 = lax.broadcasted_iota(I32, (PAIR, PAIR), 1)
    same_head = (row2 < HEAD_DIM) == (col2 < HEAD_DIM)
    diag2 = row2 == col2
    zeros_cp = jnp.zeros((ch, PAIR), F32)

    def bdiag(y):
        return jnp.concatenate([jnp.where(first, y, 0.0), jnp.where(first, 0.0, y)], axis=0)

    ps = RWKV_PASSES
    units = []
    for bi in range(nbat):
        ld = ld_ref[bi]
        cum = _dot3(tril, ld)
        tot = cum[ch - 1:ch, :]
        a_t = -kk_ref[bi] * jnp.exp(cum - ld)
        pinv = jnp.exp(-cum)
        b_t = b_ref[bi] * pinv
        k_t = kp_ref[bi] * pinv
        r_t = r_ref[bi] * jnp.exp(cum)
        pend = jnp.exp(tot - cum)
        b_end = b_ref[bi] * pend
        k_end = kp_ref[bi] * pend
        p_tot = jnp.exp(tot)
        v_all = v_ref[bi]
        for p in range(npairs):
            sl = slice(p * PAIR, (p + 1) * PAIR)
            units.append(dict(h=bi * npairs + p, a=a_t[:, sl], r=r_t[:, sl], b=b_t[:, sl], k=k_t[:, sl],
                              v=v_all[:, sl], be=b_end[:, sl], ke=k_end[:, sl], pt=p_tot[:, sl]))

    for u in units:
        u["s8"] = _dotf(jnp.concatenate([u["a"], u["r"]], axis=0),
                        jnp.concatenate([bdiag(u["b"]), bdiag(u["k"])], axis=0), nt=True, passes=ps["s8"])
    for u in units:
        s8 = u.pop("s8")
        u["l"] = jnp.where(strict, s8[0:ch, 0:PAIR], 0.0)
        u["ak"] = jnp.where(strict, s8[0:ch, PAIR:], 0.0)
        u["mrb"] = jnp.where(incl, s8[ch:, 0:PAIR], 0.0)
        u["mrk"] = jnp.where(incl, s8[ch:, PAIR:], 0.0)
        u["t"] = eye + u["l"]
    for u in units:
        u["lk"] = _dotf(u["l"], bdiag(u["l"]), passes=ps["inv"])
        u["w1"] = _dotf(u["ak"], bdiag(u["v"]), passes=ps["w1"])
    for _ in range(max(ch.bit_length() - 3, 0)):
        for u in units:
            u["both"] = _dotf(jnp.concatenate([u["t"], u["lk"]], axis=0), bdiag(u["lk"]), passes=ps["inv"])
        for u in units:
            both = u.pop("both")
            u["t"] = u["t"] + both[0:ch]
            u["lk"] = both[ch:]
    for u in units:
        u["t"] = u["t"] + _dotf(u["t"], bdiag(u["lk"]), passes=ps["inv"])
    for u in units:
        u["au"] = _dotf(u["t"], jnp.concatenate([bdiag(u["a"]), bdiag(u["w1"])], axis=1), passes=ps["au"])
    for u in units:
        a_hat, u_loc = u["au"][:, 0:PAIR], u["au"][:, PAIR:]
        rhs = jnp.concatenate([
            jnp.concatenate([bdiag(a_hat), bdiag(u_loc)], axis=1),
            jnp.concatenate([jnp.zeros((2 * ch, PAIR), F32), bdiag(u["v"])], axis=1)], axis=0)
        u["ry"] = _dotf(jnp.concatenate([u["mrb"], u["mrk"]], axis=1), rhs, passes=ps["ry"])
        bkt = jnp.concatenate([u["be"], u["ke"]], axis=0).T
        u["gh"] = _dotf(bkt, jnp.concatenate([u["au"], jnp.concatenate([zeros_cp, u["v"]], axis=1)], axis=0),
                        passes=ps["gh"])
    for u in units:
        r_hat = u["r"] + u["ry"][:, 0:PAIR]
        g_mat = jnp.where(same_head, u["gh"][:, 0:PAIR], 0.0) + jnp.where(diag2, u["pt"], 0.0)
        u["yh"] = _dotf(jnp.concatenate([r_hat, g_mat], axis=0), h_ref[u["h"]], passes=ps["yh"])
    for u in units:
        h_ref[u["h"]] = u["yh"][ch:] + jnp.where(same_head, u["gh"][:, PAIR:], 0.0)
        u["y"] = u["yh"][0:ch] + u["ry"][:, PAIR:]

    inv_n = 1.0 / HEAD_DIM
    for bi in range(nbat):
        y = jnp.concatenate([u["y"] for u in units[bi * npairs:(bi + 1) * npairs]], axis=1)
        yc = y - _dot_hl(y, bd) * inv_n
        var = _dot_hl(yc * yc, bd) * inv_n
        yn = yc * lax.rsqrt(var + GN_EPS) * gn_ref[0:1, :] + gn_ref[1:2, :]
        o_ref[bi] = (yn + bon_ref[bi]) * g_ref[bi]


def _rwkv(feats, bsz, seq, lnx_g, lnx_b, nbat):
    t, rw = feats[0].shape
    ch = RWKV_CHUNK
    gn = jnp.concatenate([lnx_g[None], lnx_b[None], jnp.zeros((SUBLANES - 2, rw), F32)], axis=0)
    head = jnp.arange(rw) // HEAD_DIM
    bd = (head[:, None] == head[None, :]).astype(BF16)
    tok = pl.BlockSpec((nbat, ch, rw), lambda bb, c: (bb, c, 0))
    const = lambda a: pl.BlockSpec(a.shape, lambda bb, c: (0, 0))
    out = pl.pallas_call(
        _rwkv_kernel,
        grid=(bsz // nbat, seq // ch),
        in_specs=[tok] * 8 + [const(gn), const(bd)],
        out_specs=tok,
        out_shape=jax.ShapeDtypeStruct((bsz, seq, rw), F32),
        scratch_shapes=[pltpu.VMEM((nbat * (rw // PAIR), PAIR, PAIR), F32)],
        compiler_params=_cparams("arbitrary", "arbitrary"),
        name="rwkv",
    )(*[f.reshape(bsz, seq, rw) for f in feats], gn, bd)
    return out.reshape(t, rw)


NEG_BIG = -1e30


def _attn_kernel(n_back, q_ref, kc_ref, kp_ref, vc_ref, vp_ref, o_ref, lse_ref):
    n = pl.program_id(2)
    qb = q_ref.shape[0]
    qi = lax.broadcasted_iota(I32, (qb, 2 * qb), 0)
    ki = lax.broadcasted_iota(I32, (qb, 2 * qb), 1)
    dist = qb + qi - ki
    has_prev = jnp.where(n > 0, 0, qb)
    valid = (dist >= 0) & (dist <= n_back) & (ki >= has_prev)
    first = lax.broadcasted_iota(I32, (qb, PAIR), 1) < HEAD_DIM
    for p in range(q_ref.shape[1] // PAIR):
        sl = slice(p * PAIR, (p + 1) * PAIR)
        q2 = q_ref[:, sl]
        k2 = jnp.concatenate([kp_ref[:, sl], kc_ref[:, sl]], axis=0)
        v2 = jnp.concatenate([vp_ref[:, sl], vc_ref[:, sl]], axis=0)
        o_h, lse_h = [], []
        for hh in range(2):
            keep = first if hh == 0 else jnp.logical_not(first)
            s = _dot_nt(jnp.where(keep, q2, jnp.zeros_like(q2)), k2)
            s = jnp.where(valid, s, NEG_BIG)
            m = jnp.max(s, axis=1, keepdims=True)
            pe = jnp.where(valid, jnp.exp(s - m), 0.0)
            l = jnp.sum(pe, axis=1, keepdims=True)
            o_h.append(_dot(pe.astype(BF16), v2) / l)
            lse_h.append(m + jnp.log(l))
        o_ref[:, sl] = jnp.where(first, o_h[0], o_h[1])
        lse_ref[:, sl] = jnp.where(first, lse_h[0], lse_h[1])


def _attention(q, k, v, bsz, seq, window, dilation):
    aw = q.shape[1] // dilation
    nb = seq // dilation // Q_BLOCK
    cur = pl.BlockSpec((Q_BLOCK, aw), lambda b, r, n: (b * nb + n, r))
    prv = pl.BlockSpec((Q_BLOCK, aw), lambda b, r, n: (b * nb + jnp.maximum(n - 1, 0), r))
    shp = jax.ShapeDtypeStruct(q.shape, F32)
    return pl.pallas_call(
        functools.partial(_attn_kernel, window // dilation),
        grid=(bsz, dilation, nb),
        in_specs=[cur, cur, prv, cur, prv],
        out_specs=[cur, cur],
        out_shape=[shp, shp],
        compiler_params=_cparams("arbitrary", "arbitrary", "arbitrary"),
        name=f"attn_d{dilation}",
    )(q, k, k, v, v)


def _layer_norm(y, g, b):
    mu = jnp.mean(y, axis=-1, keepdims=True)
    yc = y - mu
    var = jnp.mean(yc * yc, axis=-1, keepdims=True)
    return yc * lax.rsqrt(var + LN_EPS) * g + b


def _outproj_kernel(dils, *refs):
    npat = len(dils)
    o_refs = refs[0:npat]
    lse_refs = refs[npat:2 * npat]
    rw_ref, x_ref, wo_ref, ln_ref, y_ref, yb_ref, scr = refs[2 * npat:]
    tm, aw = rw_ref.shape

    def token_major(ref, d, slot):
        if d == 1:
            return ref[...]
        nslab = aw // LANES
        for res in range(d):
            for j in range(nslab):
                scr[slot * nslab + j, pl.ds(res, tm // d, stride=d), :] = (
                    ref[:, res * aw + j * LANES:res * aw + (j + 1) * LANES])
        return jnp.concatenate([scr[slot * nslab + j] for j in range(nslab)], axis=1)

    lses = [token_major(r, d, 2 * i) for i, (r, d) in enumerate(zip(lse_refs, dils))]
    outs = [token_major(r, d, 2 * i + 1) for i, (r, d) in enumerate(zip(o_refs, dils))]
    m = functools.reduce(jnp.maximum, lses)
    es = [jnp.exp(z - m) for z in lses]
    den = functools.reduce(lambda a, b: a + b, es)
    attn = functools.reduce(lambda a, b: a + b, [(e / den) * o for e, o in zip(es, outs)])
    mix = _dot(attn.astype(BF16), wo_ref[0:aw, :]) + _dot(rw_ref[...].astype(BF16), wo_ref[aw:, :])
    y = _layer_norm(ALPHA * x_ref[...] + mix, ln_ref[0:1, :], ln_ref[1:2, :])
    y_ref[...] = y
    yb_ref[...] = y.astype(yb_ref.dtype)


def _outproj(os_, lses, dils, rw, x2, w_out, ln_g, ln_b, tm):
    t, d = x2.shape
    aw = rw.shape[1]
    ln = jnp.concatenate([ln_g[None], ln_b[None], jnp.zeros((SUBLANES - 2, d), F32)], axis=0)
    wo = w_out.astype(BF16)
    tok = lambda w: pl.BlockSpec((tm, w), lambda i: (i, 0))
    view = [pl.BlockSpec((tm // dl, dl * aw), lambda i: (i, 0)) for dl in dils]
    const = lambda a: pl.BlockSpec(a.shape, lambda i: (0, 0))
    return pl.pallas_call(
        functools.partial(_outproj_kernel, dils),
        grid=(t // tm,),
        in_specs=view + view + [tok(aw), tok(d), const(wo), const(ln)],
        out_specs=[tok(d), tok(d)],
        out_shape=[jax.ShapeDtypeStruct((t, d), F32), jax.ShapeDtypeStruct((t, d), BF16)],
        scratch_shapes=[pltpu.VMEM((2 * len(dils) * aw // LANES, tm, LANES), F32)],
        compiler_params=_cparams("arbitrary"),
        name="outproj",
    )(*os_, *lses, rw, x2, wo, ln)


def _lanes(col_rep, n):
    return jnp.concatenate([col_rep] * (n // LANES), axis=1)


MOE_TILE = 512
ROW_CHUNK = 8
SORT_ROWS = 256
EXPERT_ROWS = 256


def _tile_rows(ne):
    raw = MOE_TILE * TOP_K + ne * (ROW_CHUNK - 1)
    return -(-raw // SORT_ROWS) * SORT_ROWS


def _route_kernel(x_ref, rwt_ref, bias_ref, tri_ref, ones_ref, low_ref, dloc_ref, gate_ref, tab_ref):
    ne = rwt_ref.shape[0]
    tm = x_ref.shape[0]
    gsz = ne // N_GROUPS
    ninf = -jnp.inf

    aff = _sigmoid(_dotf(rwt_ref[...], x_ref[...], nt=True))
    sel = aff + _lanes(bias_ref[...], tm)

    sel3 = sel.reshape(N_GROUPS, gsz, tm)
    rid = lax.broadcasted_iota(I32, sel3.shape, 1).astype(F32)
    m1 = jnp.max(sel3, axis=1, keepdims=True)
    i1 = jnp.min(jnp.where(sel3 == m1, rid, float(gsz)), axis=1, keepdims=True)
    m2 = jnp.max(jnp.where(rid == i1, ninf, sel3), axis=1, keepdims=True)
    gsc = (m1 + m2).reshape(N_GROUPS, tm)
    gid = lax.broadcasted_iota(I32, gsc.shape, 0).astype(F32)
    keep = jnp.zeros(gsc.shape, F32)
    for _ in range(TOPK_GROUPS):
        gm = jnp.max(gsc, axis=0, keepdims=True)
        gi = jnp.min(jnp.where(gsc == gm, gid, float(N_GROUPS)), axis=0, keepdims=True)
        hit = gid == gi
        keep = jnp.where(hit, 1.0, keep)
        gsc = jnp.where(hit, ninf, gsc)
    cand = jnp.where(keep.reshape(N_GROUPS, 1, tm) > 0.0, sel3, ninf).reshape(ne, tm)

    eid = lax.broadcasted_iota(I32, (ne, tm), 0).astype(F32)
    hits, graw = [], []
    for _ in range(TOP_K):
        m = jnp.max(cand, axis=0, keepdims=True)
        ij = jnp.min(jnp.where(cand == m, eid, float(ne)), axis=0, keepdims=True)
        hit = eid == ij
        hits.append(hit)
        graw.append(jnp.sum(jnp.where(hit, aff, 0.0), axis=0, keepdims=True))
        cand = jnp.where(hit, ninf, cand)
    gsum = functools.reduce(lambda a, b: a + b, graw)
    gate_ref[...] = jnp.concatenate([g / gsum * ROUTED_SCALE for g in graw], axis=0)

    onehot = functools.reduce(lambda a, b: a + b, [h.astype(F32) for h in hits]).astype(BF16)
    before = _dot(onehot, tri_ref[...])
    count = _dot(onehot, ones_ref[...])
    padded = jnp.ceil(count * (1.0 / ROW_CHUNK)) * ROW_CHUNK
    start = _dot3(low_ref[...], padded)
    row = _lanes(start, tm) + before
    dloc_ref[...] = jnp.concatenate(
        [jnp.sum(jnp.where(h, row, 0.0), axis=0, keepdims=True) for h in hits], axis=0).astype(I32)
    lane = lax.broadcasted_iota(I32, padded.shape, 1)
    both = jnp.where(lane == 0, padded, jnp.where(lane == 1, start, 0.0))
    tab_ref[...] = both.T[0:SUBLANES, :].astype(I32)


def _route(x1, router_w, router_bias):
    t, d = x1.shape
    ne = router_w.shape[1]
    tm = MOE_TILE
    rwt = router_w.T
    bias = jnp.broadcast_to(router_bias[:, None], (ne, LANES))
    pos = jnp.arange(tm)
    tri = (pos[:, None] < pos[None, :]).astype(BF16)
    ones = jnp.ones((tm, LANES), BF16)
    eid = jnp.arange(ne)
    low = (eid[:, None] > eid[None, :]).astype(BF16)
    const = lambda a: pl.BlockSpec(a.shape, lambda i: (0, 0))
    tokt = pl.BlockSpec((TOP_K, tm), lambda i: (0, i))
    return pl.pallas_call(
        _route_kernel,
        grid=(t // tm,),
        in_specs=[pl.BlockSpec((tm, d), lambda i: (i, 0)), const(rwt), const(bias), const(tri), const(ones),
                  const(low)],
        out_specs=[tokt, tokt, pl.BlockSpec((None, SUBLANES, ne), lambda i: (i, 0, 0))],
        out_shape=[jax.ShapeDtypeStruct((TOP_K, t), I32), jax.ShapeDtypeStruct((TOP_K, t), F32),
                   jax.ShapeDtypeStruct((t // tm, SUBLANES, ne), I32)],
        compiler_params=_cparams("arbitrary"),
        name="route",
    )(x1, rwt, bias, tri, ones, low)


def _pack_bf16_pairs(v, exact=False):
    half = v.shape[1] // 2
    u = pltpu.bitcast(v if exact else v.astype(BF16).astype(F32), U32)
    return (u[:, 0:half] >> 16) | (u[:, half:] & jnp.uint32(0xFFFF0000))


def _unpack_bf16_pairs(w):
    lo = pltpu.bitcast(w << 16, F32)
    hi = pltpu.bitcast(w & jnp.uint32(0xFFFF0000), F32)
    return jnp.concatenate([lo, hi], axis=1)


def _dot3l(a, b_bf16):
    a1 = a.astype(BF16)
    r1 = a - a1.astype(F32)
    a2 = r1.astype(BF16)
    a3 = (r1 - a2.astype(F32)).astype(BF16)
    return _dot(a1, b_bf16) + (_dot(a2, b_bf16) + _dot(a3, b_bf16))


def _plan_kernel(nblk, cnt_ref, low_ref, upper_ref, gstart_ref, meta_ref):
    c = cnt_ref[...].astype(F32)
    nt, ne = c.shape
    earlier = _dot3(low_ref[...], c)
    total = jnp.sum(c, axis=0, keepdims=True)
    padded = jnp.ceil(total * (1.0 / EXPERT_ROWS)) * EXPERT_ROWS
    pad_end = _dot3l(jnp.broadcast_to(padded, (SUBLANES, ne)), upper_ref[...])
    gstart_ref[...] = ((pad_end - padded)[0:1, :] + earlier).astype(I32)
    end_col = jnp.broadcast_to(pad_end.T[:, 0:1], (ne, nblk))
    blk_start = (lax.broadcasted_iota(I32, (ne, nblk), 1) * EXPERT_ROWS).astype(F32)
    owner = jnp.minimum(jnp.sum((end_col <= blk_start).astype(F32), axis=0, keepdims=True), float(ne - 1))
    used = jnp.max(end_col, axis=0, keepdims=True) * (1.0 / EXPERT_ROWS)
    meta_ref[...] = jnp.concatenate([owner, used, jnp.zeros((SUBLANES - 2, nblk), F32)], axis=0).astype(I32)


def _plan(cnt, nblk_pad):
    nt, ne = cnt.shape
    ti = jnp.arange(nt)
    low = (ti[:, None] > ti[None, :]).astype(BF16)
    ei = jnp.arange(ne)
    upper = (ei[:, None] <= ei[None, :]).astype(BF16)
    return pl.pallas_call(
        functools.partial(_plan_kernel, nblk_pad),
        out_shape=[jax.ShapeDtypeStruct((nt, ne), I32), jax.ShapeDtypeStruct((SUBLANES, nblk_pad), I32)],
        compiler_params=pltpu.CompilerParams(vmem_limit_bytes=VMEM_LIMIT_BYTES),
        name="plan",
    )(cnt, low, upper)


def _group_copies(cnt_ref, loc_ref, gstart_ref, tile, ne, local, remote, sem, to_remote, wait):
    def one(e, carry):
        n = pl.multiple_of(cnt_ref[tile * ne + e], ROW_CHUNK)

        @pl.when(n > 0)
        def _():
            lo = pl.multiple_of(loc_ref[tile * ne + e], ROW_CHUNK)
            go = pl.multiple_of(gstart_ref[tile * ne + e], ROW_CHUNK)
            a, b = local.at[pl.ds(lo, n), :], remote.at[pl.ds(go, n), :]
            cp = pltpu.make_async_copy(a, b, sem) if to_remote else pltpu.make_async_copy(b, a, sem)
            if wait:
                cp.wait()
            else:
                cp.start()
        return carry

    lax.fori_loop(0, ne, one, 0)


def _dispatch_kernel(ne, cnt_ref, loc_ref, gstart_ref, owner_ref, used_ref, dloc_ref, x_ref, xs_ref,
                     buf_ref, zero_ref, sems, zsem):
    step = pl.program_id(0)
    nsteps = pl.num_programs(0)
    slot = step % 2
    tm = x_ref.shape[0]
    rows = buf_ref.shape[1]
    nblk = owner_ref.shape[0]
    used = used_ref[0]

    @pl.when(step == 0)
    def _():
        zero_ref[...] = jnp.zeros_like(zero_ref)

        def block_copy(i):
            return pltpu.make_async_copy(zero_ref, xs_ref.at[pl.ds(i * EXPERT_ROWS, EXPERT_ROWS), :], zsem)

        def is_last(i):
            return (i == used - 1) | (owner_ref[jnp.minimum(i + 1, nblk - 1)] != owner_ref[i])

        def start(i, carry):
            @pl.when(is_last(i))
            def _():
                block_copy(i).start()
            return carry

        def wait(i, carry):
            @pl.when(is_last(i))
            def _():
                block_copy(i).wait()
            return carry

        lax.fori_loop(0, used, start, 0)
        lax.fori_loop(0, used, wait, 0)

    x = x_ref[...]
    dl = dloc_ref[...].astype(I16)
    riota = lax.broadcasted_iota(I32, (SORT_ROWS, tm), 0).astype(I16)
    one = jnp.ones((SORT_ROWS, tm), BF16)
    for c in range(rows // SORT_ROWS):
        r = riota + jnp.int16(c * SORT_ROWS)
        p = jnp.zeros((SORT_ROWS, tm), BF16)
        for j in range(TOP_K):
            p = jnp.where(r == dl[j:j + 1, :], one, p)
        buf_ref[slot, c * SORT_ROWS:(c + 1) * SORT_ROWS, :] = _pack_bf16_pairs(_dot(p, x), exact=True)

    copies = functools.partial(_group_copies, cnt_ref, loc_ref, gstart_ref)
    copies(step, ne, buf_ref.at[slot], xs_ref, sems.at[slot], True, False)

    @pl.when(step > 0)
    def _():
        copies(step - 1, ne, buf_ref.at[1 - slot], xs_ref, sems.at[1 - slot], True, True)

    @pl.when(step == nsteps - 1)
    def _():
        copies(step, ne, buf_ref.at[slot], xs_ref, sems.at[slot], True, True)


def _dispatch(x1b, dloc, cnt, loc, gstart, owner, used, nrows):
    t, d = x1b.shape
    nt, ne = cnt.shape
    tm = MOE_TILE
    rt = _tile_rows(ne)
    grid_spec = pltpu.PrefetchScalarGridSpec(
        num_scalar_prefetch=5,
        grid=(nt,),
        in_specs=[pl.BlockSpec((TOP_K, tm), lambda i, *_: (0, i)), pl.BlockSpec((tm, d), lambda i, *_: (i, 0))],
        out_specs=pl.BlockSpec(memory_space=pl.ANY),
        scratch_shapes=[pltpu.VMEM((2, rt, d // 2), U32), pltpu.VMEM((EXPERT_ROWS, d // 2), U32),
                        pltpu.SemaphoreType.DMA((2,)), pltpu.SemaphoreType.DMA(())],
    )
    return pl.pallas_call(
        functools.partial(_dispatch_kernel, ne),
        grid_spec=grid_spec,
        out_shape=jax.ShapeDtypeStruct((nrows, d // 2), U32),
        compiler_params=_cparams("arbitrary"),
        name="dispatch",
    )(cnt.reshape(-1), loc.reshape(-1), gstart.reshape(-1), owner, used, dloc, x1b)


def _expert_kernel(owner_ref, used_ref, xs_ref, wg_ref, wu_ref, wd_ref, ys_ref, wgb_ref, wub_ref, wdb_ref):
    i = pl.program_id(0)
    prev = owner_ref[jnp.maximum(i - 1, 0)]

    @pl.when((i == 0) | (owner_ref[i] != prev))
    def _():
        wgb_ref[...] = wg_ref[...].astype(BF16)
        wub_ref[...] = wu_ref[...].astype(BF16)
        wdb_ref[...] = wd_ref[...].astype(BF16)

    @pl.when(i < used_ref[0])
    def _():
        xb = _unpack_bf16_pairs(xs_ref[...]).astype(BF16)
        hg = _dot(xb, wgb_ref[...])
        hb = hg * _sigmoid(hg) * _dot(xb, wub_ref[...])
        ys_ref[...] = _pack_bf16_pairs(_dot(hb.astype(BF16), wdb_ref[...]))


def _experts(xs, owner, used, e_gate, e_up, e_down):
    nrows, dh = xs.shape
    _, d, ff = e_gate.shape
    nblk = nrows // EXPERT_ROWS
    blk = lambda i, owner_ref, used_ref: (jnp.minimum(i, used_ref[0] - 1), 0)
    wsel = lambda i, owner_ref, used_ref: (owner_ref[jnp.minimum(i, used_ref[0] - 1)], 0, 0)
    grid_spec = pltpu.PrefetchScalarGridSpec(
        num_scalar_prefetch=2,
        grid=(nblk,),
        in_specs=[pl.BlockSpec((EXPERT_ROWS, dh), blk),
                  pl.BlockSpec((None, d, ff), wsel), pl.BlockSpec((None, d, ff), wsel),
                  pl.BlockSpec((None, ff, d), wsel)],
        out_specs=pl.BlockSpec((EXPERT_ROWS, dh), blk),
        scratch_shapes=[pltpu.VMEM((d, ff), BF16), pltpu.VMEM((d, ff), BF16), pltpu.VMEM((ff, d), BF16)],
    )
    return pl.pallas_call(
        _expert_kernel,
        grid_spec=grid_spec,
        out_shape=jax.ShapeDtypeStruct((nrows, dh), U32),
        compiler_params=_cparams("arbitrary"),
        name="experts",
    )(owner, used, xs, e_gate, e_up, e_down)


def _final_kernel(ne, cnt_ref, loc_ref, gstart_ref, used_ref, dloc_ref, gate_ref, x_ref, xb_ref, ys_ref,
                  sg_ref, su_ref, sd_ref, ln_ref, o_ref, buf_ref, acc_ref, sems):
    step = pl.program_id(0)
    nsteps = pl.num_programs(0)
    slot = step % 2
    tm = x_ref.shape[0]
    used = used_ref[step]
    copies = functools.partial(_group_copies, cnt_ref, loc_ref, gstart_ref)

    @pl.when(step == 0)
    def _():
        copies(step, ne, buf_ref.at[0], ys_ref, sems.at[0], False, False)

    @pl.when(step + 1 < nsteps)
    def _():
        copies(step + 1, ne, buf_ref.at[1 - slot], ys_ref, sems.at[1 - slot], False, False)

    xb = xb_ref[...]
    hg = _dot(xb, sg_ref[...])
    hs = hg * _sigmoid(hg) * _dot(xb, su_ref[...])
    acc_ref[...] = _dot(hs.astype(BF16), sd_ref[...])

    copies(step, ne, buf_ref.at[slot], ys_ref, sems.at[slot], False, True)

    dl = dloc_ref[...].astype(I16)
    gt = gate_ref[...].astype(BF16)
    rows16 = lax.broadcasted_iota(I32, (SORT_ROWS, tm), 0).astype(I16)
    gts = [jnp.broadcast_to(gt[j:j + 1, :], (SORT_ROWS, tm)) for j in range(TOP_K)]
    riota = lax.broadcasted_iota(I32, (SORT_ROWS, buf_ref.shape[2]), 0)

    def chunk(c, carry):
        r0 = pl.multiple_of(c * SORT_ROWS, SORT_ROWS)
        r = rows16 + r0.astype(I16)
        g = jnp.zeros((SORT_ROWS, tm), BF16)
        for j in range(TOP_K):
            g = jnp.where(r == dl[j:j + 1, :], gts[j], g)
        w = buf_ref[slot, pl.ds(r0, SORT_ROWS), :]
        w = jnp.where(riota + r0 < used, w, jnp.zeros_like(w))
        y = _unpack_bf16_pairs(w).astype(BF16)
        acc_ref[...] += lax.dot_general(g, y, (((0,), (0,)), ((), ())), preferred_element_type=F32)
        return carry

    lax.fori_loop(0, (used + SORT_ROWS - 1) // SORT_ROWS, chunk, 0)
    o_ref[...] = _layer_norm(ALPHA * x_ref[...] + acc_ref[...], ln_ref[0:1, :], ln_ref[1:2, :])


def _final(x1, x1b, dloc, gate, cnt, loc, gstart, used_rows, ys, s_gate, s_up, s_down, ln_g, ln_b):
    t, d = x1.shape
    nt, ne = cnt.shape
    tm = MOE_TILE
    rt = _tile_rows(ne)
    ln = jnp.concatenate([ln_g[None], ln_b[None], jnp.zeros((SUBLANES - 2, d), F32)], axis=0)
    sg, su, sd = s_gate.astype(BF16), s_up.astype(BF16), s_down.astype(BF16)
    tok = pl.BlockSpec((tm, d), lambda i, *_: (i, 0))
    tokt = pl.BlockSpec((TOP_K, tm), lambda i, *_: (0, i))
    const = lambda a: pl.BlockSpec(a.shape, lambda i, *_: (0, 0))
    grid_spec = pltpu.PrefetchScalarGridSpec(
        num_scalar_prefetch=4,
        grid=(nt,),
        in_specs=[tokt, tokt, tok, tok, pl.BlockSpec(memory_space=pl.ANY), const(sg), const(su), const(sd), const(ln)],
        out_specs=tok,
        scratch_shapes=[pltpu.VMEM((2, rt, d // 2), U32), pltpu.VMEM((tm, d), F32), pltpu.SemaphoreType.DMA((2,))],
    )
    return pl.pallas_call(
        functools.partial(_final_kernel, ne),
        grid_spec=grid_spec,
        out_shape=jax.ShapeDtypeStruct((t, d), F32),
        compiler_params=_cparams("arbitrary"),
        name="final",
    )(cnt.reshape(-1), loc.reshape(-1), gstart.reshape(-1), used_rows, dloc, gate, x1, x1b, ys, sg, su, sd, ln)


def _moe(x1, x1b, router_w, router_bias, e_gate, e_up, e_down, s_gate, s_up, s_down, ln_g, ln_b):
    t = x1.shape[0]
    ne = router_w.shape[1]
    nt = t // MOE_TILE
    nblk = -(-(nt * _tile_rows(ne)) // EXPERT_ROWS) + ne
    nblk_pad = -(-nblk // LANES) * LANES
    dloc, gate, tab = _route(x1, router_w, router_bias)
    cnt, loc = tab[:, 0, :], tab[:, 1, :]
    used_rows = loc[:, ne - 1] + cnt[:, ne - 1]
    gstart, meta = _plan(cnt, nblk_pad)
    owner, used = meta[0], meta[1, 0:1]
    xs = _dispatch(x1b, dloc, cnt, loc, gstart, owner, used, nblk * EXPERT_ROWS)
    ys = _experts(xs, owner, used, e_gate, e_up, e_down)
    return _final(x1, x1b, dloc, gate, cnt, loc, gstart, used_rows, ys, s_gate, s_up, s_down, ln_g, ln_b)


def kernel(x, w_in, mu_shift, w0, w_decay_up, a0, w_aaa_up, w_gate_up, k_k, k_a, r_k, lnx_g, lnx_b, w_out,
           ln1_g, ln1_b, router_w, router_bias, e_gate, e_up, e_down, s_gate, s_up, s_down, ln2_g, ln2_b):
    bsz, seq, d = x.shape
    x2 = x.reshape(bsz * seq, d)
    qkvs, feats = _inproj(x2, seq, w_in[0], mu_shift[0], w0[0], w_decay_up[0], a0[0], w_aaa_up[0], w_gate_up[0],
                          k_k[0], k_a[0], r_k[0], tm=256)
    rw = _rwkv(feats, bsz, seq, lnx_g[0], lnx_b[0], nbat=4)
    x1, x1b = _mixer_tail(qkvs, rw, x2, bsz, seq, w_out[0], ln1_g[0], ln1_b[0])
    out = _moe(x1, x1b, router_w[0], router_bias[0], e_gate[0], e_up[0], e_down[0], s_gate[0], s_up[0], s_down[0],
               ln2_g[0], ln2_b[0])
    return out.reshape(bsz, seq, d)


def _mixer_tail(qkvs, rw, x2, bsz, seq, w_out, ln_g, ln_b):
    res = [_attention(q, k, v, bsz, seq, window, dilation)
           for (q, k, v), (window, dilation) in zip(qkvs, ATTN_PATTERNS)]
    dils = tuple(dl for _, dl in ATTN_PATTERNS)
    return _outproj([o for o, _ in res], [l for _, l in res], dils, rw, x2, w_out, ln_g, ln_b, tm=512)
```

```python
import functools

import jax
import jax.numpy as jnp
from jax import lax
from jax.experimental import pallas as pl
from jax.experimental.pallas import tpu as pltpu

F32 = jnp.float32
BF16 = jnp.bfloat16
I32 = jnp.int32
I16 = jnp.int16
U32 = jnp.uint32

LANES = 128
SUBLANES = 8
VMEM_LIMIT_BYTES = 56 * 1024 * 1024

HEAD_DIM = 64
ATTN_HEADS = 8
RWKV_HEADS = 8
ATTN_WIDTH = ATTN_HEADS * HEAD_DIM
RWKV_WIDTH = RWKV_HEADS * HEAD_DIM
ATTN_PATTERNS = ((128, 1), (512, 4), (2048, 16))
Q_BLOCK = 128
ROPE_THETA = 10000.0
DECAY_LORA = 64
AAA_LORA = 64
GATE_LORA = 160
GN_EPS = 64e-5
LN_EPS = 1e-5
TOP_K = 8
N_GROUPS = 8
TOPK_GROUPS = 4
ROUTED_SCALE = 2.5
DEPTH = 1
ALPHA = (2.0 * DEPTH) ** 0.25

RWKV_CHUNK = 64
PAIR = 2 * HEAD_DIM
LORA_PAD = 2 * LANES


def _cparams(*sem):
    return pltpu.CompilerParams(dimension_semantics=sem, vmem_limit_bytes=VMEM_LIMIT_BYTES)


def _split_bf16(a):
    hi = a.astype(BF16)
    lo = (a - hi.astype(F32)).astype(BF16)
    return hi, lo


def _dot(a, b):
    return jnp.dot(a, b, preferred_element_type=F32)


def _dot_nt(a, b):
    return lax.dot_general(a, b, (((1,), (1,)), ((), ())), preferred_element_type=F32)


def _dot_hl(a_f32, b_bf16):
    hi, lo = _split_bf16(a_f32)
    return _dot(hi, b_bf16) + _dot(lo, b_bf16)


def _softplus(z):
    return jnp.maximum(z, 0.0) + jnp.log(1.0 + jnp.exp(-jnp.abs(z)))


def _sigmoid(z):
    return 1.0 / (1.0 + jnp.exp(-z))


def _inproj_kernel(dils, x_ref, w_ref, cos_ref, sin_ref, mu_ref, wd_ref, wa_ref, wg_ref, vec_ref, bd_ref, *refs):
    nq = 3 * len(dils)
    qkv_refs = refs[0:nq]
    r_ref, ld_ref, kp_ref, vv_ref, kk_ref, b_ref, g_ref, bon_ref, carry_ref, qkv_scr = refs[nq:]
    s = pl.program_id(1)
    tm = x_ref.shape[0]
    aw = ATTN_WIDTH
    rw = RWKV_WIDTH

    @pl.when(s == 0)
    def _():
        carry_ref[...] = jnp.zeros_like(carry_ref)

    h = _dot(x_ref[...].astype(BF16), w_ref[...])
    reps = aw // cos_ref.shape[1]
    cos = jnp.concatenate([cos_ref[...]] * reps, axis=1)
    sin = jnp.concatenate([sin_ref[...]] * reps, axis=1)
    qkv = [(h[:, 0:aw] * cos + h[:, aw:2 * aw] * sin) * (HEAD_DIM ** -0.5),
           h[:, 2 * aw:3 * aw] * cos + h[:, 3 * aw:4 * aw] * sin,
           h[:, 4 * aw:5 * aw]]
    nslab = aw // LANES
    for i in range(3):
        for j in range(nslab):
            qkv_scr[i * nslab + j] = qkv[i][:, j * LANES:(j + 1) * LANES]
    for di, d in enumerate(dils):
        for i in range(3):
            o_ref = qkv_refs[3 * di + i]
            if d == 1:
                o_ref[...] = qkv[i].astype(o_ref.dtype)
                continue
            for res in range(d):
                for j in range(nslab):
                    rows = qkv_scr[i * nslab + j, pl.ds(res, tm // d, stride=d), :]
                    o_ref[:, res * aw + j * LANES:res * aw + (j + 1) * LANES] = rows.astype(o_ref.dtype)

    f = h[:, 5 * aw:]
    rows = lax.broadcasted_iota(I32, f.shape, 0)
    prev = jnp.where(rows == 0, carry_ref[SUBLANES - 1:SUBLANES, :], pltpu.roll(f, 1, axis=0))
    carry_ref[...] = f[tm - SUBLANES:tm, :]
    f = f + (prev - f) * mu_ref[...]

    r = f[:, 0:rw]
    k = f[:, rw:2 * rw]
    v = f[:, 2 * rw:3 * rw]
    la = f[:, 3 * rw:3 * rw + LANES]
    gl = f[:, 3 * rw + LANES:]
    w0, a0, k_k, k_a, r_k = (vec_ref[i:i + 1, :] for i in range(5))

    z = w0 + _dot(jnp.tanh(la).astype(BF16), wd_ref[...])
    w = -_softplus(-z) - 0.5
    ld_ref[...] = -jnp.exp(w)
    a = _sigmoid(a0 + _dot(la.astype(BF16), wa_ref[...]))
    g_ref[...] = _dot(_sigmoid(gl).astype(BF16), wg_ref[...])

    bd = bd_ref[...]
    kk = k * k_k
    nrm = jnp.sqrt(_dot_hl(kk * kk, bd))
    kk = kk / jnp.maximum(nrm, 1e-12)
    kp = k * (1.0 + (a - 1.0) * k_a)
    r_ref[...] = r
    kp_ref[...] = kp
    vv_ref[...] = v
    kk_ref[...] = kk
    b_ref[...] = kk * a
    bon_ref[...] = _dot_hl(r * kp * r_k, bd) * v


def _rot_half_cols(w):
    d, n = w.shape
    w4 = w.reshape(d, n // HEAD_DIM, 2, HEAD_DIM // 2)
    return jnp.stack([-w4[:, :, 1, :], w4[:, :, 0, :]], axis=2).reshape(d, n)


def _inproj(x2, seq, w_in, mu_shift, w0, w_decay_up, a0, w_aaa_up, w_gate_up, k_k, k_a, r_k, tm):
    t, d = x2.shape
    aw, rw = ATTN_WIDTH, RWKV_WIDTH
    wq, wk, wv = w_in[:, 0:aw], w_in[:, aw:2 * aw], w_in[:, 2 * aw:3 * aw]
    wf = w_in[:, 3 * aw:]
    gpad = LORA_PAD - GATE_LORA
    w_all = jnp.concatenate(
        [wq, _rot_half_cols(wq), wk, _rot_half_cols(wk), wv, wf, jnp.zeros((d, gpad), F32)], axis=1).astype(BF16)
    mu = jnp.concatenate([mu_shift, jnp.zeros((gpad,), F32)])[None, :]
    nf = mu.shape[1]
    wd = jnp.concatenate([w_decay_up, jnp.zeros((AAA_LORA, rw), F32)], axis=0).astype(BF16)
    wa = jnp.concatenate([jnp.zeros((DECAY_LORA, rw), F32), w_aaa_up], axis=0).astype(BF16)
    wg = jnp.concatenate([w_gate_up, jnp.zeros((gpad, rw), F32)], axis=0).astype(BF16)
    vec = jnp.stack([w0, a0, k_k, k_a, r_k.reshape(-1), w0 * 0, w0 * 0, w0 * 0])
    head = jnp.arange(rw) // HEAD_DIM
    bd = (head[:, None] == head[None, :]).astype(BF16)
    half = HEAD_DIM // 2
    inv_freq = ROPE_THETA ** (-jnp.arange(half, dtype=F32) * 2.0 / HEAD_DIM)
    ang = jnp.arange(seq, dtype=F32)[:, None] * inv_freq[None, :]
    cos = jnp.tile(jnp.cos(ang), (1, LANES // half))
    sin = jnp.tile(jnp.sin(ang), (1, LANES // half))

    nst = seq // tm
    tok = lambda b, s: (b * nst + s, 0)
    const = lambda b, s: (0, 0)
    full = lambda a: pl.BlockSpec(a.shape, const)
    dils = tuple(dl for _, dl in ATTN_PATTERNS)
    qkv_specs, qkv_shapes = [], []
    for dl in dils:
        qkv_specs += [pl.BlockSpec((tm // dl, dl * aw), tok)] * 3
        qkv_shapes += [jax.ShapeDtypeStruct((t // dl, dl * aw), BF16)] * 3
    out_f = jax.ShapeDtypeStruct((t, rw), F32)
    outs = pl.pallas_call(
        functools.partial(_inproj_kernel, dils),
        grid=(t // seq, nst),
        in_specs=[pl.BlockSpec((tm, d), tok), full(w_all),
                  pl.BlockSpec((tm, LANES), lambda b, s: (s, 0)), pl.BlockSpec((tm, LANES), lambda b, s: (s, 0)),
                  full(mu), full(wd), full(wa), full(wg), full(vec), full(bd)],
        out_specs=qkv_specs + [pl.BlockSpec((tm, rw), tok)] * 8,
        out_shape=qkv_shapes + [out_f] * 8,
        scratch_shapes=[pltpu.VMEM((SUBLANES, nf), F32), pltpu.VMEM((3 * aw // LANES, tm, LANES), F32)],
        compiler_params=_cparams("arbitrary", "arbitrary"),
        name="inproj",
    )(x2, w_all, cos, sin, mu, wd, wa, wg, vec, bd)
    nq = 3 * len(dils)
    return [outs[3 * i:3 * i + 3] for i in range(len(dils))], outs[nq:]


def _dotf(a, b, nt=False, passes=3):
    dot = _dot_nt if nt else _dot
    if passes == 1:
        return dot(a.astype(BF16), b.astype(BF16))
    ah, al = _split_bf16(a)
    if passes == 2:
        bh = b.astype(BF16)
        return dot(ah, bh) + dot(al, bh)
    bh, bl = _split_bf16(b)
    return dot(ah, bh) + (dot(ah, bl) + dot(al, bh))


RWKV_PASSES = dict(s8=1, inv=1, w1=1, au=1, ry=1, gh=1, yh=1)


def _dot3(a_bf16, b):
    b1 = b.astype(BF16)
    r1 = b - b1.astype(F32)
    b2 = r1.astype(BF16)
    b3 = (r1 - b2.astype(F32)).astype(BF16)
    return _dot(a_bf16, b1) + (_dot(a_bf16, b2) + _dot(a_bf16, b3))


def _rwkv_kernel(r_ref, ld_ref, kp_ref, v_ref, kk_ref, b_ref, g_ref, bon_ref, gn_ref, bd_ref, o_ref, h_ref):
    c = pl.program_id(1)
    nbat, ch, rw = r_ref.shape
    npairs = rw // PAIR

    @pl.when(c == 0)
    def _():
        h_ref[...] = jnp.zeros_like(h_ref)

    ri = lax.broadcasted_iota(I32, (ch, ch), 0)
    ci = lax.broadcasted_iota(I32, (ch, ch), 1)
    tril = (ri >= ci).astype(BF16)
    bd = bd_ref[...]
    row = lax.broadcasted_iota(I32, (ch, PAIR), 0)
    col = lax.broadcasted_iota(I32, (ch, PAIR), 1)
    first = col < HEAD_DIM
    jj = col & (HEAD_DIM - 1)
    strict = jj < row
    incl = jj <= row
    eye = (jj == row).astype(F32)
    row2 = lax.broadcasted_iota(I32, (PAIR, PAIR), 0)
    col2 = lax.broadcasted_iota(I32, (PAIR, PAIR), 1)
    same_head = (row2 < HEAD_DIM) == (col2 < HEAD_DIM)
    diag2 = row2 == col2
    zeros_cp = jnp.zeros((ch, PAIR), F32)

    def bdiag(y):
        return jnp.concatenate([jnp.where(first, y, 0.0), jnp.where(first, 0.0, y)], axis=0)

    ps = RWKV_PASSES
    units = []
    for bi in range(nbat):
        ld = ld_ref[bi]
        cum = _dot3(tril, ld)
        tot = cum[ch - 1:ch, :]
        a_t = -kk_ref[bi] * jnp.exp(cum - ld)
        pinv = jnp.exp(-cum)
        b_t = b_ref[bi] * pinv
        k_t = kp_ref[bi] * pinv
        r_t = r_ref[bi] * jnp.exp(cum)
        pend = jnp.exp(tot - cum)
        b_end = b_ref[bi] * pend
        k_end = kp_ref[bi] * pend
        p_tot = jnp.exp(tot)
        v_all = v_ref[bi]
        for p in range(npairs):
            sl = slice(p * PAIR, (p + 1) * PAIR)
            units.append(dict(h=bi * npairs + p, a=a_t[:, sl], r=r_t[:, sl], b=b_t[:, sl], k=k_t[:, sl],
                              v=v_all[:, sl], be=b_end[:, sl], ke=k_end[:, sl], pt=p_tot[:, sl]))

    for u in units:
        u["s8"] = _dotf(jnp.concatenate([u["a"], u["r"]], axis=0),
                        jnp.concatenate([bdiag(u["b"]), bdiag(u["k"])], axis=0), nt=True, passes=ps["s8"])
    for u in units:
        s8 = u.pop("s8")
        u["l"] = jnp.where(strict, s8[0:ch, 0:PAIR], 0.0)
        u["ak"] = jnp.where(strict, s8[0:ch, PAIR:], 0.0)
        u["mrb"] = jnp.where(incl, s8[ch:, 0:PAIR], 0.0)
        u["mrk"] = jnp.where(incl, s8[ch:, PAIR:], 0.0)
        u["t"] = eye + u["l"]
    for u in units:
        u["lk"] = _dotf(u["l"], bdiag(u["l"]), passes=ps["inv"])
        u["w1"] = _dotf(u["ak"], bdiag(u["v"]), passes=ps["w1"])
    for _ in range(max(ch.bit_length() - 3, 0)):
        for u in units:
            u["both"] = _dotf(jnp.concatenate([u["t"], u["lk"]], axis=0), bdiag(u["lk"]), passes=ps["inv"])
        for u in units:
            both = u.pop("both")
            u["t"] = u["t"] + both[0:ch]
            u["lk"] = both[ch:]
    for u in units:
        u["t"] = u["t"] + _dotf(u["t"], bdiag(u["lk"]), passes=ps["inv"])
    for u in units:
        u["au"] = _dotf(u["t"], jnp.concatenate([bdiag(u["a"]), bdiag(u["w1"])], axis=1), passes=ps["au"])
    for u in units:
        a_hat, u_loc = u["au"][:, 0:PAIR], u["au"][:, PAIR:]
        rhs = jnp.concatenate([
            jnp.concatenate([bdiag(a_hat), bdiag(u_loc)], axis=1),
            jnp.concatenate([jnp.zeros((2 * ch, PAIR), F32), bdiag(u["v"])], axis=1)], axis=0)
        u["ry"] = _dotf(jnp.concatenate([u["mrb"], u["mrk"]], axis=1), rhs, passes=ps["ry"])
        bkt = jnp.concatenate([u["be"], u["ke"]], axis=0).T
        u["gh"] = _dotf(bkt, jnp.concatenate([u["au"], jnp.concatenate([zeros_cp, u["v"]], axis=1)], axis=0),
                        passes=ps["gh"])
    for u in units:
        r_hat = u["r"] + u["ry"][:, 0:PAIR]
        g_mat = jnp.where(same_head, u["gh"][:, 0:PAIR], 0.0) + jnp.where(diag2, u["pt"], 0.0)
        u["yh"] = _dotf(jnp.concatenate([r_hat, g_mat], axis=0), h_ref[u["h"]], passes=ps["yh"])
    for u in units:
        h_ref[u["h"]] = u["yh"][ch:] + jnp.where(same_head, u["gh"][:, PAIR:], 0.0)
        u["y"] = u["yh"][0:ch] + u["ry"][:, PAIR:]

    inv_n = 1.0 / HEAD_DIM
    for bi in range(nbat):
        y = jnp.concatenate([u["y"] for u in units[bi * npairs:(bi + 1) * npairs]], axis=1)
        yc = y - _dot_hl(y, bd) * inv_n
        var = _dot_hl(yc * yc, bd) * inv_n
        yn = yc * lax.rsqrt(var + GN_EPS) * gn_ref[0:1, :] + gn_ref[1:2, :]
        o_ref[bi] = (yn + bon_ref[bi]) * g_ref[bi]


def _rwkv(feats, bsz, seq, lnx_g, lnx_b, nbat):
    t, rw = feats[0].shape
    ch = RWKV_CHUNK
    gn = jnp.concatenate([lnx_g[None], lnx_b[None], jnp.zeros((SUBLANES - 2, rw), F32)], axis=0)
    head = jnp.arange(rw) // HEAD_DIM
    bd = (head[:, None] == head[None, :]).astype(BF16)
    tok = pl.BlockSpec((nbat, ch, rw), lambda bb, c: (bb, c, 0))
    const = lambda a: pl.BlockSpec(a.shape, lambda bb, c: (0, 0))
    out = pl.pallas_call(
        _rwkv_kernel,
        grid=(bsz // nbat, seq // ch),
        in_specs=[tok] * 8 + [const(gn), const(bd)],
        out_specs=tok,
        out_shape=jax.ShapeDtypeStruct((bsz, seq, rw), F32),
        scratch_shapes=[pltpu.VMEM((nbat * (rw // PAIR), PAIR, PAIR), F32)],
        compiler_params=_cparams("arbitrary", "arbitrary"),
        name="rwkv",
    )(*[f.reshape(bsz, seq, rw) for f in feats], gn, bd)
    return out.reshape(t, rw)


NEG_BIG = -1e30


def _attn_kernel(n_back, q_ref, kc_ref, kp_ref, vc_ref, vp_ref, o_ref, lse_ref):
    n = pl.program_id(2)
    qb = q_ref.shape[0]
    qi = lax.broadcasted_iota(I32, (qb, 2 * qb), 0)
    ki = lax.broadcasted_iota(I32, (qb, 2 * qb), 1)
    dist = qb + qi - ki
    has_prev = jnp.where(n > 0, 0, qb)
    valid = (dist >= 0) & (dist <= n_back) & (ki >= has_prev)
    first = lax.broadcasted_iota(I32, (qb, PAIR), 1) < HEAD_DIM
    npairs = q_ref.shape[1] // PAIR
    heads = []
    for p in range(npairs):
        sl = slice(p * PAIR, (p + 1) * PAIR)
        q2 = q_ref[:, sl]
        k2 = jnp.concatenate([kp_ref[:, sl], kc_ref[:, sl]], axis=0)
        for hh in range(2):
            keep = first if hh == 0 else jnp.logical_not(first)
            s = _dot_nt(jnp.where(keep, q2, jnp.zeros_like(q2)), k2)
            heads.append(dict(sl=sl, s=s))
    for h in heads:
        s = jnp.where(valid, h.pop("s"), NEG_BIG)
        m = jnp.max(s, axis=1, keepdims=True)
        pe = jnp.where(valid, jnp.exp(s - m), 0.0)
        l = jnp.sum(pe, axis=1, keepdims=True)
        h["pe"] = pe.astype(BF16)
        h["l"] = l
        h["lse"] = m + jnp.log(l)
    for h in heads:
        sl = h["sl"]
        v2 = jnp.concatenate([vp_ref[:, sl], vc_ref[:, sl]], axis=0)
        h["o"] = _dot(h.pop("pe"), v2) / h["l"]
    for p in range(npairs):
        h0, h1 = heads[2 * p], heads[2 * p + 1]
        o_ref[:, h0["sl"]] = jnp.where(first, h0["o"], h1["o"])
        lse_ref[:, h0["sl"]] = jnp.where(first, h0["lse"], h1["lse"])


def _attention(q, k, v, bsz, seq, window, dilation):
    aw = q.shape[1] // dilation
    nb = seq // dilation // Q_BLOCK
    cur = pl.BlockSpec((Q_BLOCK, aw), lambda b, r, n: (b * nb + n, r))
    prv = pl.BlockSpec((Q_BLOCK, aw), lambda b, r, n: (b * nb + jnp.maximum(n - 1, 0), r))
    shp = jax.ShapeDtypeStruct(q.shape, F32)
    return pl.pallas_call(
        functools.partial(_attn_kernel, window // dilation),
        grid=(bsz, dilation, nb),
        in_specs=[cur, cur, prv, cur, prv],
        out_specs=[cur, cur],
        out_shape=[shp, shp],
        compiler_params=_cparams("arbitrary", "arbitrary", "arbitrary"),
        name=f"attn_d{dilation}",
    )(q, k, k, v, v)


def _layer_norm(y, g, b):
    mu = jnp.mean(y, axis=-1, keepdims=True)
    yc = y - mu
    var = jnp.mean(yc * yc, axis=-1, keepdims=True)
    return yc * lax.rsqrt(var + LN_EPS) * g + b


def _outproj_kernel(dils, *refs):
    npat = len(dils)
    o_refs = refs[0:npat]
    lse_refs = refs[npat:2 * npat]
    rw_ref, x_ref, wo_ref, ln_ref, y_ref, yb_ref, scr = refs[2 * npat:]
    tm, aw = rw_ref.shape

    def token_major(ref, d, slot):
        if d == 1:
            return ref[...]
        nslab = aw // LANES
        for res in range(d):
            for j in range(nslab):
                scr[slot * nslab + j, pl.ds(res, tm // d, stride=d), :] = (
                    ref[:, res * aw + j * LANES:res * aw + (j + 1) * LANES])
        return jnp.concatenate([scr[slot * nslab + j] for j in range(nslab)], axis=1)

    lses = [token_major(r, d, 2 * i) for i, (r, d) in enumerate(zip(lse_refs, dils))]
    outs = [token_major(r, d, 2 * i + 1) for i, (r, d) in enumerate(zip(o_refs, dils))]
    m = functools.reduce(jnp.maximum, lses)
    es = [jnp.exp(z - m) for z in lses]
    den = functools.reduce(lambda a, b: a + b, es)
    attn = functools.reduce(lambda a, b: a + b, [(e / den) * o for e, o in zip(es, outs)])
    mix = _dot(attn.astype(BF16), wo_ref[0:aw, :]) + _dot(rw_ref[...].astype(BF16), wo_ref[aw:, :])
    y = _layer_norm(ALPHA * x_ref[...] + mix, ln_ref[0:1, :], ln_ref[1:2, :])
    y_ref[...] = y
    yb_ref[...] = y.astype(yb_ref.dtype)


def _outproj(os_, lses, dils, rw, x2, w_out, ln_g, ln_b, tm):
    t, d = x2.shape
    aw = rw.shape[1]
    ln = jnp.concatenate([ln_g[None], ln_b[None], jnp.zeros((SUBLANES - 2, d), F32)], axis=0)
    wo = w_out.astype(BF16)
    tok = lambda w: pl.BlockSpec((tm, w), lambda i: (i, 0))
    view = [pl.BlockSpec((tm // dl, dl * aw), lambda i: (i, 0)) for dl in dils]
    const = lambda a: pl.BlockSpec(a.shape, lambda i: (0, 0))
    return pl.pallas_call(
        functools.partial(_outproj_kernel, dils),
        grid=(t // tm,),
        in_specs=view + view + [tok(aw), tok(d), const(wo), const(ln)],
        out_specs=[tok(d), tok(d)],
        out_shape=[jax.ShapeDtypeStruct((t, d), F32), jax.ShapeDtypeStruct((t, d), BF16)],
        scratch_shapes=[pltpu.VMEM((2 * len(dils) * aw // LANES, tm, LANES), F32)],
        compiler_params=_cparams("arbitrary"),
        name="outproj",
    )(*os_, *lses, rw, x2, wo, ln)


def _lanes(col_rep, n):
    return jnp.concatenate([col_rep] * (n // LANES), axis=1)


MOE_TILE = 512
ROW_CHUNK = 8
SORT_ROWS = 256
COMBINE_ROWS = 512
EXPERT_ROWS = 256


def _tile_rows(ne):
    raw = MOE_TILE * TOP_K + ne * (ROW_CHUNK - 1)
    unit = max(SORT_ROWS, COMBINE_ROWS)
    return -(-raw // unit) * unit


def _route_kernel(x_ref, rwt_ref, bias_ref, tri_ref, ones_ref, low_ref, dloc_ref, gate_ref, tab_ref):
    ne = rwt_ref.shape[0]
    tm = x_ref.shape[0]
    gsz = ne // N_GROUPS
    ninf = -jnp.inf

    aff = _sigmoid(_dotf(rwt_ref[...], x_ref[...], nt=True))
    sel = aff + _lanes(bias_ref[...], tm)

    sel3 = sel.reshape(N_GROUPS, gsz, tm)
    rid = lax.broadcasted_iota(I32, sel3.shape, 1).astype(F32)
    m1 = jnp.max(sel3, axis=1, keepdims=True)
    i1 = jnp.min(jnp.where(sel3 == m1, rid, float(gsz)), axis=1, keepdims=True)
    m2 = jnp.max(jnp.where(rid == i1, ninf, sel3), axis=1, keepdims=True)
    gsc = (m1 + m2).reshape(N_GROUPS, tm)
    gid = lax.broadcasted_iota(I32, gsc.shape, 0).astype(F32)
    keep = jnp.zeros(gsc.shape, F32)
    for _ in range(TOPK_GROUPS):
        gm = jnp.max(gsc, axis=0, keepdims=True)
        gi = jnp.min(jnp.where(gsc == gm, gid, float(N_GROUPS)), axis=0, keepdims=True)
        hit = gid == gi
        keep = jnp.where(hit, 1.0, keep)
        gsc = jnp.where(hit, ninf, gsc)
    cand = jnp.where(keep.reshape(N_GROUPS, 1, tm) > 0.0, sel3, ninf).reshape(ne, tm)

    eid = lax.broadcasted_iota(I32, (ne, tm), 0).astype(F32)
    hits, graw = [], []
    for _ in range(TOP_K):
        m = jnp.max(cand, axis=0, keepdims=True)
        ij = jnp.min(jnp.where(cand == m, eid, float(ne)), axis=0, keepdims=True)
        hit = eid == ij
        hits.append(hit)
        graw.append(jnp.sum(jnp.where(hit, aff, 0.0), axis=0, keepdims=True))
        cand = jnp.where(hit, ninf, cand)
    gsum = functools.reduce(lambda a, b: a + b, graw)
    gate_ref[...] = jnp.concatenate([g / gsum * ROUTED_SCALE for g in graw], axis=0)

    onehot = functools.reduce(lambda a, b: a + b, [h.astype(F32) for h in hits]).astype(BF16)
    before = _dot(onehot, tri_ref[...])
    count = _dot(onehot, ones_ref[...])
    padded = jnp.ceil(count * (1.0 / ROW_CHUNK)) * ROW_CHUNK
    start = _dot3(low_ref[...], padded)
    row = _lanes(start, tm) + before
    dloc_ref[...] = jnp.concatenate(
        [jnp.sum(jnp.where(h, row, 0.0), axis=0, keepdims=True) for h in hits], axis=0).astype(I32)
    lane = lax.broadcasted_iota(I32, padded.shape, 1)
    both = jnp.where(lane == 0, padded, jnp.where(lane == 1, start, 0.0))
    tab_ref[...] = both.T[0:SUBLANES, :].astype(I32)


def _route(x1, router_w, router_bias):
    t, d = x1.shape
    ne = router_w.shape[1]
    tm = MOE_TILE
    rwt = router_w.T
    bias = jnp.broadcast_to(router_bias[:, None], (ne, LANES))
    pos = jnp.arange(tm)
    tri = (pos[:, None] < pos[None, :]).astype(BF16)
    ones = jnp.ones((tm, LANES), BF16)
    eid = jnp.arange(ne)
    low = (eid[:, None] > eid[None, :]).astype(BF16)
    const = lambda a: pl.BlockSpec(a.shape, lambda i: (0, 0))
    tokt = pl.BlockSpec((TOP_K, tm), lambda i: (0, i))
    return pl.pallas_call(
        _route_kernel,
        grid=(t // tm,),
        in_specs=[pl.BlockSpec((tm, d), lambda i: (i, 0)), const(rwt), const(bias), const(tri), const(ones),
                  const(low)],
        out_specs=[tokt, tokt, pl.BlockSpec((None, SUBLANES, ne), lambda i: (i, 0, 0))],
        out_shape=[jax.ShapeDtypeStruct((TOP_K, t), I32), jax.ShapeDtypeStruct((TOP_K, t), F32),
                   jax.ShapeDtypeStruct((t // tm, SUBLANES, ne), I32)],
        compiler_params=_cparams("arbitrary"),
        name="route",
    )(x1, rwt, bias, tri, ones, low)


def _pack_bf16_pairs(v, exact=False):
    half = v.shape[1] // 2
    u = pltpu.bitcast(v if exact else v.astype(BF16).astype(F32), U32)
    return (u[:, 0:half] >> 16) | (u[:, half:] & jnp.uint32(0xFFFF0000))


def _unpack_bf16_pairs(w):
    lo = pltpu.bitcast(w << 16, F32)
    hi = pltpu.bitcast(w & jnp.uint32(0xFFFF0000), F32)
    return jnp.concatenate([lo, hi], axis=1)


def _dot3l(a, b_bf16):
    a1 = a.astype(BF16)
    r1 = a - a1.astype(F32)
    a2 = r1.astype(BF16)
    a3 = (r1 - a2.astype(F32)).astype(BF16)
    return _dot(a1, b_bf16) + (_dot(a2, b_bf16) + _dot(a3, b_bf16))


def _plan_kernel(nblk, cnt_ref, low_ref, upper_ref, gstart_ref, meta_ref, emeta_ref):
    c = cnt_ref[...].astype(F32)
    nt, ne = c.shape
    earlier = _dot3(low_ref[...], c)
    total = jnp.sum(c, axis=0, keepdims=True)
    padded = jnp.ceil(total * (1.0 / EXPERT_ROWS)) * EXPERT_ROWS
    pad_end = _dot3l(jnp.broadcast_to(padded, (SUBLANES, ne)), upper_ref[...])
    pad_start = (pad_end - padded)[0:1, :]
    gstart_ref[...] = (pad_start + earlier).astype(I32)
    emeta_ref[...] = jnp.concatenate(
        [pad_start, padded * (1.0 / EXPERT_ROWS), jnp.zeros((SUBLANES - 2, ne), F32)], axis=0).astype(I32)
    end_col = jnp.broadcast_to(pad_end.T[:, 0:1], (ne, nblk))
    blk_start = (lax.broadcasted_iota(I32, (ne, nblk), 1) * EXPERT_ROWS).astype(F32)
    owner = jnp.minimum(jnp.sum((end_col <= blk_start).astype(F32), axis=0, keepdims=True), float(ne - 1))
    used = jnp.max(end_col, axis=0, keepdims=True) * (1.0 / EXPERT_ROWS)
    meta_ref[...] = jnp.concatenate([owner, used, jnp.zeros((SUBLANES - 2, nblk), F32)], axis=0).astype(I32)


def _plan(cnt, nblk_pad):
    nt, ne = cnt.shape
    ti = jnp.arange(nt)
    low = (ti[:, None] > ti[None, :]).astype(BF16)
    ei = jnp.arange(ne)
    upper = (ei[:, None] <= ei[None, :]).astype(BF16)
    return pl.pallas_call(
        functools.partial(_plan_kernel, nblk_pad),
        out_shape=[jax.ShapeDtypeStruct((nt, ne), I32), jax.ShapeDtypeStruct((SUBLANES, nblk_pad), I32),
                   jax.ShapeDtypeStruct((SUBLANES, ne), I32)],
        compiler_params=pltpu.CompilerParams(vmem_limit_bytes=VMEM_LIMIT_BYTES),
        name="plan",
    )(cnt, low, upper)


def _group_copies(cnt_ref, loc_ref, gstart_ref, tile, ne, local, remote, sem, to_remote, wait):
    def one(e, carry):
        n = pl.multiple_of(cnt_ref[tile * ne + e], ROW_CHUNK)

        @pl.when(n > 0)
        def _():
            lo = pl.multiple_of(loc_ref[tile * ne + e], ROW_CHUNK)
            go = pl.multiple_of(gstart_ref[tile * ne + e], ROW_CHUNK)
            a, b = local.at[pl.ds(lo, n), :], remote.at[pl.ds(go, n), :]
            cp = pltpu.make_async_copy(a, b, sem) if to_remote else pltpu.make_async_copy(b, a, sem)
            if wait:
                cp.wait()
            else:
                cp.start()
        return carry

    lax.fori_loop(0, ne, one, 0, unroll=8)


def _dispatch_kernel(ne, cnt_ref, loc_ref, gstart_ref, owner_ref, used_ref, urows_ref, dloc_ref, x_ref, xs_ref,
                     buf_ref, zero_ref, sems, zsem):
    step = pl.program_id(0)
    nsteps = pl.num_programs(0)
    slot = step % 2
    tm = x_ref.shape[0]
    nblk = owner_ref.shape[0]
    used = used_ref[0]

    @pl.when(step == 0)
    def _():
        zero_ref[...] = jnp.zeros_like(zero_ref)

        def block_copy(i):
            return pltpu.make_async_copy(zero_ref, xs_ref.at[pl.ds(i * EXPERT_ROWS, EXPERT_ROWS), :], zsem)

        def is_last(i):
            return (i == used - 1) | (owner_ref[jnp.minimum(i + 1, nblk - 1)] != owner_ref[i])

        def start(i, carry):
            @pl.when(is_last(i))
            def _():
                block_copy(i).start()
            return carry

        def wait(i, carry):
            @pl.when(is_last(i))
            def _():
                block_copy(i).wait()
            return carry

        lax.fori_loop(0, used, start, 0)
        lax.fori_loop(0, used, wait, 0)

    x = x_ref[...]
    dl = dloc_ref[...].astype(I16)
    riota = lax.broadcasted_iota(I32, (SORT_ROWS, tm), 0).astype(I16)
    one = jnp.ones((SORT_ROWS, tm), BF16)

    def chunk(c, carry):
        r0 = pl.multiple_of(c * SORT_ROWS, SORT_ROWS)
        r = riota + r0.astype(I16)
        p = jnp.zeros((SORT_ROWS, tm), BF16)
        for j in range(TOP_K):
            p = jnp.where(r == dl[j:j + 1, :], one, p)
        buf_ref[slot, pl.ds(r0, SORT_ROWS), :] = _pack_bf16_pairs(_dot(p, x), exact=True)
        return carry

    lax.fori_loop(0, (urows_ref[step] + SORT_ROWS - 1) // SORT_ROWS, chunk, 0)

    copies = functools.partial(_group_copies, cnt_ref, loc_ref, gstart_ref)
    copies(step, ne, buf_ref.at[slot], xs_ref, sems.at[slot], True, False)

    @pl.when(step > 0)
    def _():
        copies(step - 1, ne, buf_ref.at[1 - slot], xs_ref, sems.at[1 - slot], True, True)

    @pl.when(step == nsteps - 1)
    def _():
        copies(step, ne, buf_ref.at[slot], xs_ref, sems.at[slot], True, True)


def _dispatch(x1b, dloc, cnt, loc, gstart, owner, used, used_rows, nrows):
    t, d = x1b.shape
    nt, ne = cnt.shape
    tm = MOE_TILE
    rt = _tile_rows(ne)
    grid_spec = pltpu.PrefetchScalarGridSpec(
        num_scalar_prefetch=6,
        grid=(nt,),
        in_specs=[pl.BlockSpec((TOP_K, tm), lambda i, *_: (0, i)), pl.BlockSpec((tm, d), lambda i, *_: (i, 0))],
        out_specs=pl.BlockSpec(memory_space=pl.ANY),
        scratch_shapes=[pltpu.VMEM((2, rt, d // 2), U32), pltpu.VMEM((EXPERT_ROWS, d // 2), U32),
                        pltpu.SemaphoreType.DMA((2,)), pltpu.SemaphoreType.DMA(())],
    )
    return pl.pallas_call(
        functools.partial(_dispatch_kernel, ne),
        grid_spec=grid_spec,
        out_shape=jax.ShapeDtypeStruct((nrows, d // 2), U32),
        compiler_params=_cparams("arbitrary"),
        name="dispatch",
    )(cnt.reshape(-1), loc.reshape(-1), gstart.reshape(-1), owner, used, used_rows, dloc, x1b)


def _expert_kernel(first_ref, nblk_ref, xs_ref, wg_ref, wu_ref, wd_ref, ys_ref,
                   xbuf, ybuf, wgb_ref, wub_ref, wdb_ref, sem_in, sem_out):
    e = pl.program_id(0)
    nb = nblk_ref[e]
    row0 = first_ref[e]

    def rows(b):
        return pl.ds(pl.multiple_of(row0 + b * EXPERT_ROWS, EXPERT_ROWS), EXPERT_ROWS)

    def in_copy(b, slot):
        return pltpu.make_async_copy(xs_ref.at[rows(b), :], xbuf.at[slot], sem_in.at[slot])

    def out_copy(b, slot):
        return pltpu.make_async_copy(ybuf.at[slot], ys_ref.at[rows(b), :], sem_out.at[slot])

    @pl.when(nb > 0)
    def _():
        in_copy(0, 0).start()
        wgb_ref[...] = wg_ref[...].astype(BF16)
        wub_ref[...] = wu_ref[...].astype(BF16)
        wdb_ref[...] = wd_ref[...].astype(BF16)

        def block(b, carry):
            slot = b % 2

            @pl.when(b + 1 < nb)
            def _():
                in_copy(b + 1, 1 - slot).start()

            in_copy(b, slot).wait()

            @pl.when(b >= 2)
            def _():
                out_copy(b - 2, slot).wait()

            xb = _unpack_bf16_pairs(xbuf[slot]).astype(BF16)
            hg = _dot(xb, wgb_ref[...])
            hb = hg * _sigmoid(hg) * _dot(xb, wub_ref[...])
            ybuf[slot] = _pack_bf16_pairs(_dot(hb.astype(BF16), wdb_ref[...]))
            out_copy(b, slot).start()
            return carry

        lax.fori_loop(0, nb, block, 0)

        def drain(b, carry):
            out_copy(b, b % 2).wait()
            return carry

        lax.fori_loop(jnp.maximum(nb - 2, 0), nb, drain, 0)


def _experts(xs, first_row, nblocks, e_gate, e_up, e_down):
    nrows, dh = xs.shape
    ne, d, ff = e_gate.shape
    wsel = lambda e, *_: (e, 0, 0)
    grid_spec = pltpu.PrefetchScalarGridSpec(
        num_scalar_prefetch=2,
        grid=(ne,),
        in_specs=[pl.BlockSpec(memory_space=pl.ANY),
                  pl.BlockSpec((None, d, ff), wsel), pl.BlockSpec((None, d, ff), wsel),
                  pl.BlockSpec((None, ff, d), wsel)],
        out_specs=pl.BlockSpec(memory_space=pl.ANY),
        scratch_shapes=[pltpu.VMEM((2, EXPERT_ROWS, dh), U32), pltpu.VMEM((2, EXPERT_ROWS, dh), U32),
                        pltpu.VMEM((d, ff), BF16), pltpu.VMEM((d, ff), BF16), pltpu.VMEM((ff, d), BF16),
                        pltpu.SemaphoreType.DMA((2,)), pltpu.SemaphoreType.DMA((2,))],
    )
    return pl.pallas_call(
        _expert_kernel,
        grid_spec=grid_spec,
        out_shape=jax.ShapeDtypeStruct((nrows, dh), U32),
        compiler_params=_cparams("arbitrary"),
        name="experts",
    )(first_row, nblocks, xs, e_gate, e_up, e_down)


def _final_kernel(ne, cnt_ref, loc_ref, gstart_ref, used_ref, dloc_ref, gate_ref, x_ref, xb_ref, ys_ref,
                  sg_ref, su_ref, sd_ref, ln_ref, o_ref, buf_ref, acc_ref, sems):
    step = pl.program_id(0)
    nsteps = pl.num_programs(0)
    slot = step % 2
    tm = x_ref.shape[0]
    used = used_ref[step]
    copies = functools.partial(_group_copies, cnt_ref, loc_ref, gstart_ref)

    @pl.when(step == 0)
    def _():
        copies(step, ne, buf_ref.at[0], ys_ref, sems.at[0], False, False)

    @pl.when(step + 1 < nsteps)
    def _():
        copies(step + 1, ne, buf_ref.at[1 - slot], ys_ref, sems.at[1 - slot], False, False)

    xb = xb_ref[...]
    hg = _dot(xb, sg_ref[...])
    hs = hg * _sigmoid(hg) * _dot(xb, su_ref[...])
    acc_ref[...] = _dot(hs.astype(BF16), sd_ref[...])

    copies(step, ne, buf_ref.at[slot], ys_ref, sems.at[slot], False, True)

    dl = dloc_ref[...].astype(I16)
    gt = gate_ref[...].astype(BF16)
    rows16 = lax.broadcasted_iota(I32, (COMBINE_ROWS, tm), 0).astype(I16)
    gts = [jnp.broadcast_to(gt[j:j + 1, :], (COMBINE_ROWS, tm)) for j in range(TOP_K)]
    riota = lax.broadcasted_iota(I32, (COMBINE_ROWS, buf_ref.shape[2]), 0)

    def chunk(c, carry):
        r0 = pl.multiple_of(c * COMBINE_ROWS, COMBINE_ROWS)
        r = rows16 + r0.astype(I16)
        g = jnp.zeros((COMBINE_ROWS, tm), BF16)
        for j in range(TOP_K):
            g = jnp.where(r == dl[j:j + 1, :], gts[j], g)
        w = buf_ref[slot, pl.ds(r0, COMBINE_ROWS), :]
        w = jnp.where(riota + r0 < used, w, jnp.zeros_like(w))
        y = _unpack_bf16_pairs(w).astype(BF16)
        acc_ref[...] += lax.dot_general(g, y, (((0,), (0,)), ((), ())), preferred_element_type=F32)
        return carry

    lax.fori_loop(0, (used + COMBINE_ROWS - 1) // COMBINE_ROWS, chunk, 0)
    o_ref[...] = _layer_norm(ALPHA * x_ref[...] + acc_ref[...], ln_ref[0:1, :], ln_ref[1:2, :])


def _final(x1, x1b, dloc, gate, cnt, loc, gstart, used_rows, ys, s_gate, s_up, s_down, ln_g, ln_b):
    t, d = x1.shape
    nt, ne = cnt.shape
    tm = MOE_TILE
    rt = _tile_rows(ne)
    ln = jnp.concatenate([ln_g[None], ln_b[None], jnp.zeros((SUBLANES - 2, d), F32)], axis=0)
    sg, su, sd = s_gate.astype(BF16), s_up.astype(BF16), s_down.astype(BF16)
    tok = pl.BlockSpec((tm, d), lambda i, *_: (i, 0))
    tokt = pl.BlockSpec((TOP_K, tm), lambda i, *_: (0, i))
    const = lambda a: pl.BlockSpec(a.shape, lambda i, *_: (0, 0))
    grid_spec = pltpu.PrefetchScalarGridSpec(
        num_scalar_prefetch=4,
        grid=(nt,),
        in_specs=[tokt, tokt, tok, tok, pl.BlockSpec(memory_space=pl.ANY), const(sg), const(su), const(sd), const(ln)],
        out_specs=tok,
        scratch_shapes=[pltpu.VMEM((2, rt, d // 2), U32), pltpu.VMEM((tm, d), F32), pltpu.SemaphoreType.DMA((2,))],
    )
    return pl.pallas_call(
        functools.partial(_final_kernel, ne),
        grid_spec=grid_spec,
        out_shape=jax.ShapeDtypeStruct((t, d), F32),
        compiler_params=_cparams("arbitrary"),
        name="final",
    )(cnt.reshape(-1), loc.reshape(-1), gstart.reshape(-1), used_rows, dloc, gate, x1, x1b, ys, sg, su, sd, ln)


def _moe(x1, x1b, router_w, router_bias, e_gate, e_up, e_down, s_gate, s_up, s_down, ln_g, ln_b):
    t = x1.shape[0]
    ne = router_w.shape[1]
    nt = t // MOE_TILE
    nblk = -(-(nt * _tile_rows(ne)) // EXPERT_ROWS) + ne
    nblk_pad = -(-nblk // LANES) * LANES
    dloc, gate, tab = _route(x1, router_w, router_bias)
    cnt, loc = tab[:, 0, :], tab[:, 1, :]
    used_rows = loc[:, ne - 1] + cnt[:, ne - 1]
    gstart, meta, emeta = _plan(cnt, nblk_pad)
    owner, used = meta[0], meta[1, 0:1]
    xs = _dispatch(x1b, dloc, cnt, loc, gstart, owner, used, used_rows, nblk * EXPERT_ROWS)
    ys = _experts(xs, emeta[0], emeta[1], e_gate, e_up, e_down)
    return _final(x1, x1b, dloc, gate, cnt, loc, gstart, used_rows, ys, s_gate, s_up, s_down, ln_g, ln_b)


def kernel(x, w_in, mu_shift, w0, w_decay_up, a0, w_aaa_up, w_gate_up, k_k, k_a, r_k, lnx_g, lnx_b, w_out,
           ln1_g, ln1_b, router_w, router_bias, e_gate, e_up, e_down, s_gate, s_up, s_down, ln2_g, ln2_b):
    bsz, seq, d = x.shape
    x2 = x.reshape(bsz * seq, d)
    qkvs, feats = _inproj(x2, seq, w_in[0], mu_shift[0], w0[0], w_decay_up[0], a0[0], w_aaa_up[0], w_gate_up[0],
                          k_k[0], k_a[0], r_k[0], tm=256)
    rw = _rwkv(feats, bsz, seq, lnx_g[0], lnx_b[0], nbat=4)
    x1, x1b = _mixer_tail(qkvs, rw, x2, bsz, seq, w_out[0], ln1_g[0], ln1_b[0])
    out = _moe(x1, x1b, router_w[0], router_bias[0], e_gate[0], e_up[0], e_down[0], s_gate[0], s_up[0], s_down[0],
               ln2_g[0], ln2_b[0])
    return out.reshape(bsz, seq, d)


def _mixer_tail(qkvs, rw, x2, bsz, seq, w_out, ln_g, ln_b):
    res = [_attention(q, k, v, bsz, seq, window, dilation)
           for (q, k, v), (window, dilation) in zip(qkvs, ATTN_PATTERNS)]
    dils = tuple(dl for _, dl in ATTN_PATTERNS)
    return _outproj([o for o, _ in res], [l for _, l in res], dils, rw, x2, w_out, ln_g, ln_b, tm=512)
```

```python
import functools

import jax
import jax.numpy as jnp
from jax import lax
from jax.experimental import pallas as pl
from jax.experimental.pallas import tpu as pltpu

F32 = jnp.float32
BF16 = jnp.bfloat16
I32 = jnp.int32
I16 = jnp.int16
U32 = jnp.uint32

LANES = 128
SUBLANES = 8
VMEM_LIMIT_BYTES = 56 * 1024 * 1024

HEAD_DIM = 64
ATTN_HEADS = 8
RWKV_HEADS = 8
ATTN_WIDTH = ATTN_HEADS * HEAD_DIM
RWKV_WIDTH = RWKV_HEADS * HEAD_DIM
ATTN_PATTERNS = ((128, 1), (512, 4), (2048, 16))
Q_BLOCK = 128
ROPE_THETA = 10000.0
DECAY_LORA = 64
AAA_LORA = 64
GATE_LORA = 160
GN_EPS = 64e-5
LN_EPS = 1e-5
TOP_K = 8
N_GROUPS = 8
TOPK_GROUPS = 4
ROUTED_SCALE = 2.5
DEPTH = 1
ALPHA = (2.0 * DEPTH) ** 0.25

RWKV_CHUNK = 64
PAIR = 2 * HEAD_DIM
LORA_PAD = 2 * LANES


def _cparams(*sem):
    return pltpu.CompilerParams(dimension_semantics=sem, vmem_limit_bytes=VMEM_LIMIT_BYTES)


def _split_bf16(a):
    hi = a.astype(BF16)
    lo = (a - hi.astype(F32)).astype(BF16)
    return hi, lo


def _dot(a, b):
    return jnp.dot(a, b, preferred_element_type=F32)


def _dot_nt(a, b):
    return lax.dot_general(a, b, (((1,), (1,)), ((), ())), preferred_element_type=F32)


def _dot_hl(a_f32, b_bf16):
    hi, lo = _split_bf16(a_f32)
    return _dot(hi, b_bf16) + _dot(lo, b_bf16)


def _softplus(z):
    return jnp.maximum(z, 0.0) + jnp.log(1.0 + jnp.exp(-jnp.abs(z)))


def _sigmoid(z):
    return 1.0 / (1.0 + jnp.exp(-z))


def _inproj_kernel(dils, x_ref, w_ref, cos_ref, sin_ref, mu_ref, wd_ref, wa_ref, wg_ref, vec_ref, bd_ref, *refs):
    nq = 3 * len(dils)
    qkv_refs = refs[0:nq]
    r_ref, ld_ref, kp_ref, vv_ref, kk_ref, b_ref, g_ref, bon_ref, carry_ref, qkv_scr = refs[nq:]
    s = pl.program_id(1)
    tm = x_ref.shape[0]
    aw = ATTN_WIDTH
    rw = RWKV_WIDTH

    @pl.when(s == 0)
    def _():
        carry_ref[...] = jnp.zeros_like(carry_ref)

    h = _dot(x_ref[...].astype(BF16), w_ref[...])
    reps = aw // cos_ref.shape[1]
    cos = jnp.concatenate([cos_ref[...]] * reps, axis=1)
    sin = jnp.concatenate([sin_ref[...]] * reps, axis=1)
    qkv = [(h[:, 0:aw] * cos + h[:, aw:2 * aw] * sin) * (HEAD_DIM ** -0.5),
           h[:, 2 * aw:3 * aw] * cos + h[:, 3 * aw:4 * aw] * sin,
           h[:, 4 * aw:5 * aw]]
    nslab = aw // LANES
    for i in range(3):
        for j in range(nslab):
            qkv_scr[i * nslab + j] = qkv[i][:, j * LANES:(j + 1) * LANES]
    for di, d in enumerate(dils):
        for i in range(3):
            o_ref = qkv_refs[3 * di + i]
            if d == 1:
                o_ref[...] = qkv[i].astype(o_ref.dtype)
                continue
            for res in range(d):
                for j in range(nslab):
                    rows = qkv_scr[i * nslab + j, pl.ds(res, tm // d, stride=d), :]
                    o_ref[:, res * aw + j * LANES:res * aw + (j + 1) * LANES] = rows.astype(o_ref.dtype)

    f = h[:, 5 * aw:]
    rows = lax.broadcasted_iota(I32, f.shape, 0)
    prev = jnp.where(rows == 0, carry_ref[SUBLANES - 1:SUBLANES, :], pltpu.roll(f, 1, axis=0))
    carry_ref[...] = f[tm - SUBLANES:tm, :]
    f = f + (prev - f) * mu_ref[...]

    r = f[:, 0:rw]
    k = f[:, rw:2 * rw]
    v = f[:, 2 * rw:3 * rw]
    la = f[:, 3 * rw:3 * rw + LANES]
    gl = f[:, 3 * rw + LANES:]
    w0, a0, k_k, k_a, r_k = (vec_ref[i:i + 1, :] for i in range(5))

    z = w0 + _dot(jnp.tanh(la).astype(BF16), wd_ref[...])
    w = -_softplus(-z) - 0.5
    ld_ref[...] = -jnp.exp(w)
    a = _sigmoid(a0 + _dot(la.astype(BF16), wa_ref[...]))
    g_ref[...] = _dot(_sigmoid(gl).astype(BF16), wg_ref[...])

    bd = bd_ref[...]
    kk = k * k_k
    nrm = jnp.sqrt(_dot_hl(kk * kk, bd))
    kk = kk / jnp.maximum(nrm, 1e-12)
    kp = k * (1.0 + (a - 1.0) * k_a)
    r_ref[...] = r
    kp_ref[...] = kp
    vv_ref[...] = v
    kk_ref[...] = kk
    b_ref[...] = kk * a
    bon_ref[...] = _dot_hl(r * kp * r_k, bd) * v


def _rot_half_cols(w):
    d, n = w.shape
    w4 = w.reshape(d, n // HEAD_DIM, 2, HEAD_DIM // 2)
    return jnp.stack([-w4[:, :, 1, :], w4[:, :, 0, :]], axis=2).reshape(d, n)


def _inproj(x2, seq, w_in, mu_shift, w0, w_decay_up, a0, w_aaa_up, w_gate_up, k_k, k_a, r_k, tm):
    t, d = x2.shape
    aw, rw = ATTN_WIDTH, RWKV_WIDTH
    wq, wk, wv = w_in[:, 0:aw], w_in[:, aw:2 * aw], w_in[:, 2 * aw:3 * aw]
    wf = w_in[:, 3 * aw:]
    gpad = LORA_PAD - GATE_LORA
    w_all = jnp.concatenate(
        [wq, _rot_half_cols(wq), wk, _rot_half_cols(wk), wv, wf, jnp.zeros((d, gpad), F32)], axis=1).astype(BF16)
    mu = jnp.concatenate([mu_shift, jnp.zeros((gpad,), F32)])[None, :]
    nf = mu.shape[1]
    wd = jnp.concatenate([w_decay_up, jnp.zeros((AAA_LORA, rw), F32)], axis=0).astype(BF16)
    wa = jnp.concatenate([jnp.zeros((DECAY_LORA, rw), F32), w_aaa_up], axis=0).astype(BF16)
    wg = jnp.concatenate([w_gate_up, jnp.zeros((gpad, rw), F32)], axis=0).astype(BF16)
    vec = jnp.stack([w0, a0, k_k, k_a, r_k.reshape(-1), w0 * 0, w0 * 0, w0 * 0])
    head = jnp.arange(rw) // HEAD_DIM
    bd = (head[:, None] == head[None, :]).astype(BF16)
    half = HEAD_DIM // 2
    inv_freq = ROPE_THETA ** (-jnp.arange(half, dtype=F32) * 2.0 / HEAD_DIM)
    ang = jnp.arange(seq, dtype=F32)[:, None] * inv_freq[None, :]
    cos = jnp.tile(jnp.cos(ang), (1, LANES // half))
    sin = jnp.tile(jnp.sin(ang), (1, LANES // half))

    nst = seq // tm
    tok = lambda b, s: (b * nst + s, 0)
    const = lambda b, s: (0, 0)
    full = lambda a: pl.BlockSpec(a.shape, const)
    dils = tuple(dl for _, dl in ATTN_PATTERNS)
    qkv_specs, qkv_shapes = [], []
    for dl in dils:
        qkv_specs += [pl.BlockSpec((tm // dl, dl * aw), tok)] * 3
        qkv_shapes += [jax.ShapeDtypeStruct((t // dl, dl * aw), BF16)] * 3
    out_f = jax.ShapeDtypeStruct((t, rw), F32)
    outs = pl.pallas_call(
        functools.partial(_inproj_kernel, dils),
        grid=(t // seq, nst),
        in_specs=[pl.BlockSpec((tm, d), tok), full(w_all),
                  pl.BlockSpec((tm, LANES), lambda b, s: (s, 0)), pl.BlockSpec((tm, LANES), lambda b, s: (s, 0)),
                  full(mu), full(wd), full(wa), full(wg), full(vec), full(bd)],
        out_specs=qkv_specs + [pl.BlockSpec((tm, rw), tok)] * 8,
        out_shape=qkv_shapes + [out_f] * 8,
        scratch_shapes=[pltpu.VMEM((SUBLANES, nf), F32), pltpu.VMEM((3 * aw // LANES, tm, LANES), F32)],
        compiler_params=_cparams("arbitrary", "arbitrary"),
        name="inproj",
    )(x2, w_all, cos, sin, mu, wd, wa, wg, vec, bd)
    nq = 3 * len(dils)
    return [outs[3 * i:3 * i + 3] for i in range(len(dils))], outs[nq:]


def _dotf(a, b, nt=False, passes=3):
    dot = _dot_nt if nt else _dot
    if passes == 1:
        return dot(a.astype(BF16), b.astype(BF16))
    ah, al = _split_bf16(a)
    if passes == 2:
        bh = b.astype(BF16)
        return dot(ah, bh) + dot(al, bh)
    bh, bl = _split_bf16(b)
    return dot(ah, bh) + (dot(ah, bl) + dot(al, bh))


RWKV_PASSES = dict(s8=1, inv=1, w1=1, au=1, ry=1, gh=1, yh=1)


def _dot3(a_bf16, b):
    b1 = b.astype(BF16)
    r1 = b - b1.astype(F32)
    b2 = r1.astype(BF16)
    b3 = (r1 - b2.astype(F32)).astype(BF16)
    return _dot(a_bf16, b1) + (_dot(a_bf16, b2) + _dot(a_bf16, b3))


def _rwkv_kernel(r_ref, ld_ref, kp_ref, v_ref, kk_ref, b_ref, g_ref, bon_ref, gn_ref, bd_ref, o_ref, h_ref):
    c = pl.program_id(1)
    nbat, ch, rw = r_ref.shape
    npairs = rw // PAIR

    @pl.when(c == 0)
    def _():
        h_ref[...] = jnp.zeros_like(h_ref)

    ri = lax.broadcasted_iota(I32, (ch, ch), 0)
    ci = lax.broadcasted_iota(I32, (ch, ch), 1)
    tril = (ri >= ci).astype(BF16)
    bd = bd_ref[...]
    row = lax.broadcasted_iota(I32, (ch, PAIR), 0)
    col = lax.broadcasted_iota(I32, (ch, PAIR), 1)
    first = col < HEAD_DIM
    jj = col & (HEAD_DIM - 1)
    strict = jj < row
    incl = jj <= row
    eye = (jj == row).astype(F32)
    row2 = lax.broadcasted_iota(I32, (PAIR, PAIR), 0)
    col2 = lax.broadcasted_iota(I32, (PAIR, PAIR), 1)
    same_head = (row2 < HEAD_DIM) == (col2 < HEAD_DIM)
    diag2 = row2 == col2
    zeros_cp = jnp.zeros((ch, PAIR), F32)

    def bdiag(y):
        return jnp.concatenate([jnp.where(first, y, 0.0), jnp.where(first, 0.0, y)], axis=0)

    ps = RWKV_PASSES
    units = []
    for bi in range(nbat):
        ld = ld_ref[bi]
        cum = _dot3(tril, ld)
        tot = cum[ch - 1:ch, :]
        a_t = -kk_ref[bi] * jnp.exp(cum - ld)
        pinv = jnp.exp(-cum)
        b_t = b_ref[bi] * pinv
        k_t = kp_ref[bi] * pinv
        r_t = r_ref[bi] * jnp.exp(cum)
        pend = jnp.exp(tot - cum)
        b_end = b_ref[bi] * pend
        k_end = kp_ref[bi] * pend
        p_tot = jnp.exp(tot)
        v_all = v_ref[bi]
        for p in range(npairs):
            sl = slice(p * PAIR, (p + 1) * PAIR)
            units.append(dict(h=bi * npairs + p, a=a_t[:, sl], r=r_t[:, sl], b=b_t[:, sl], k=k_t[:, sl],
                              v=v_all[:, sl], be=b_end[:, sl], ke=k_end[:, sl], pt=p_tot[:, sl]))

    for u in units:
        u["s8"] = _dotf(jnp.concatenate([u["a"], u["r"]], axis=0),
                        jnp.concatenate([bdiag(u["b"]), bdiag(u["k"])], axis=0), nt=True, passes=ps["s8"])
    for u in units:
        s8 = u.pop("s8")
        u["l"] = jnp.where(strict, s8[0:ch, 0:PAIR], 0.0)
        u["ak"] = jnp.where(strict, s8[0:ch, PAIR:], 0.0)
        u["mrb"] = jnp.where(incl, s8[ch:, 0:PAIR], 0.0)
        u["mrk"] = jnp.where(incl, s8[ch:, PAIR:], 0.0)
        u["t"] = eye + u["l"]
    for u in units:
        u["lk"] = _dotf(u["l"], bdiag(u["l"]), passes=ps["inv"])
        u["w1"] = _dotf(u["ak"], bdiag(u["v"]), passes=ps["w1"])
    for _ in range(max(ch.bit_length() - 3, 0)):
        for u in units:
            u["both"] = _dotf(jnp.concatenate([u["t"], u["lk"]], axis=0), bdiag(u["lk"]), passes=ps["inv"])
        for u in units:
            both = u.pop("both")
            u["t"] = u["t"] + both[0:ch]
            u["lk"] = both[ch:]
    for u in units:
        u["t"] = u["t"] + _dotf(u["t"], bdiag(u["lk"]), passes=ps["inv"])
    for u in units:
        u["au"] = _dotf(u["t"], jnp.concatenate([bdiag(u["a"]), bdiag(u["w1"])], axis=1), passes=ps["au"])
    for u in units:
        a_hat, u_loc = u["au"][:, 0:PAIR], u["au"][:, PAIR:]
        rhs = jnp.concatenate([
            jnp.concatenate([bdiag(a_hat), bdiag(u_loc)], axis=1),
            jnp.concatenate([jnp.zeros((2 * ch, PAIR), F32), bdiag(u["v"])], axis=1)], axis=0)
        u["ry"] = _dotf(jnp.concatenate([u["mrb"], u["mrk"]], axis=1), rhs, passes=ps["ry"])
        bkt = jnp.concatenate([u["be"], u["ke"]], axis=0).T
        u["gh"] = _dotf(bkt, jnp.concatenate([u["au"], jnp.concatenate([zeros_cp, u["v"]], axis=1)], axis=0),
                        passes=ps["gh"])
    for u in units:
        r_hat = u["r"] + u["ry"][:, 0:PAIR]
        g_mat = jnp.where(same_head, u["gh"][:, 0:PAIR], 0.0) + jnp.where(diag2, u["pt"], 0.0)
        u["yh"] = _dotf(jnp.concatenate([r_hat, g_mat], axis=0), h_ref[u["h"]], passes=ps["yh"])
    for u in units:
        h_ref[u["h"]] = u["yh"][ch:] + jnp.where(same_head, u["gh"][:, PAIR:], 0.0)
        u["y"] = u["yh"][0:ch] + u["ry"][:, PAIR:]

    inv_n = 1.0 / HEAD_DIM
    for bi in range(nbat):
        y = jnp.concatenate([u["y"] for u in units[bi * npairs:(bi + 1) * npairs]], axis=1)
        yc = y - _dot_hl(y, bd) * inv_n
        var = _dot_hl(yc * yc, bd) * inv_n
        yn = yc * lax.rsqrt(var + GN_EPS) * gn_ref[0:1, :] + gn_ref[1:2, :]
        o_ref[bi] = (yn + bon_ref[bi]) * g_ref[bi]


def _rwkv(feats, bsz, seq, lnx_g, lnx_b, nbat):
    t, rw = feats[0].shape
    ch = RWKV_CHUNK
    gn = jnp.concatenate([lnx_g[None], lnx_b[None], jnp.zeros((SUBLANES - 2, rw), F32)], axis=0)
    head = jnp.arange(rw) // HEAD_DIM
    bd = (head[:, None] == head[None, :]).astype(BF16)
    tok = pl.BlockSpec((nbat, ch, rw), lambda bb, c: (bb, c, 0))
    const = lambda a: pl.BlockSpec(a.shape, lambda bb, c: (0, 0))
    out = pl.pallas_call(
        _rwkv_kernel,
        grid=(bsz // nbat, seq // ch),
        in_specs=[tok] * 8 + [const(gn), const(bd)],
        out_specs=tok,
        out_shape=jax.ShapeDtypeStruct((bsz, seq, rw), F32),
        scratch_shapes=[pltpu.VMEM((nbat * (rw // PAIR), PAIR, PAIR), F32)],
        compiler_params=_cparams("arbitrary", "arbitrary"),
        name="rwkv",
    )(*[f.reshape(bsz, seq, rw) for f in feats], gn, bd)
    return out.reshape(t, rw)


NEG_BIG = -1e30


def _attn_kernel(n_back, q_ref, kc_ref, kp_ref, vc_ref, vp_ref, o_ref, lse_ref):
    n = pl.program_id(2)
    qb = q_ref.shape[0]
    qi = lax.broadcasted_iota(I32, (qb, 2 * qb), 0)
    ki = lax.broadcasted_iota(I32, (qb, 2 * qb), 1)
    dist = qb + qi - ki
    has_prev = jnp.where(n > 0, 0, qb)
    valid = (dist >= 0) & (dist <= n_back) & (ki >= has_prev)
    first = lax.broadcasted_iota(I32, (qb, PAIR), 1) < HEAD_DIM
    npairs = q_ref.shape[1] // PAIR
    heads = []
    for p in range(npairs):
        sl = slice(p * PAIR, (p + 1) * PAIR)
        q2 = q_ref[:, sl]
        k2 = jnp.concatenate([kp_ref[:, sl], kc_ref[:, sl]], axis=0)
        for hh in range(2):
            keep = first if hh == 0 else jnp.logical_not(first)
            s = _dot_nt(jnp.where(keep, q2, jnp.zeros_like(q2)), k2)
            heads.append(dict(sl=sl, s=s))
    for h in heads:
        s = jnp.where(valid, h.pop("s"), NEG_BIG)
        m = jnp.max(s, axis=1, keepdims=True)
        pe = jnp.where(valid, jnp.exp(s - m), 0.0)
        l = jnp.sum(pe, axis=1, keepdims=True)
        h["pe"] = pe.astype(BF16)
        h["l"] = l
        h["lse"] = m + jnp.log(l)
    for h in heads:
        sl = h["sl"]
        v2 = jnp.concatenate([vp_ref[:, sl], vc_ref[:, sl]], axis=0)
        h["o"] = _dot(h.pop("pe"), v2) / h["l"]
    for p in range(npairs):
        h0, h1 = heads[2 * p], heads[2 * p + 1]
        o_ref[:, h0["sl"]] = jnp.where(first, h0["o"], h1["o"])
        lse_ref[:, h0["sl"]] = jnp.where(first, h0["lse"], h1["lse"])


def _attention(q, k, v, bsz, seq, window, dilation):
    aw = q.shape[1] // dilation
    nb = seq // dilation // Q_BLOCK
    cur = pl.BlockSpec((Q_BLOCK, aw), lambda b, r, n: (b * nb + n, r))
    prv = pl.BlockSpec((Q_BLOCK, aw), lambda b, r, n: (b * nb + jnp.maximum(n - 1, 0), r))
    shp = jax.ShapeDtypeStruct(q.shape, F32)
    return pl.pallas_call(
        functools.partial(_attn_kernel, window // dilation),
        grid=(bsz, dilation, nb),
        in_specs=[cur, cur, prv, cur, prv],
        out_specs=[cur, cur],
        out_shape=[shp, shp],
        compiler_params=_cparams("arbitrary", "arbitrary", "arbitrary"),
        name=f"attn_d{dilation}",
    )(q, k, k, v, v)


def _layer_norm(y, g, b):
    mu = jnp.mean(y, axis=-1, keepdims=True)
    yc = y - mu
    var = jnp.mean(yc * yc, axis=-1, keepdims=True)
    return yc * lax.rsqrt(var + LN_EPS) * g + b


def _outproj_kernel(dils, *refs):
    npat = len(dils)
    o_refs = refs[0:npat]
    lse_refs = refs[npat:2 * npat]
    rw_ref, x_ref, wo_ref, ln_ref, y_ref, yb_ref, scr = refs[2 * npat:]
    tm, aw = rw_ref.shape

    def token_major(ref, d, slot):
        if d == 1:
            return ref[...]
        nslab = aw // LANES
        for res in range(d):
            for j in range(nslab):
                scr[slot * nslab + j, pl.ds(res, tm // d, stride=d), :] = (
                    ref[:, res * aw + j * LANES:res * aw + (j + 1) * LANES])
        return jnp.concatenate([scr[slot * nslab + j] for j in range(nslab)], axis=1)

    lses = [token_major(r, d, 2 * i) for i, (r, d) in enumerate(zip(lse_refs, dils))]
    outs = [token_major(r, d, 2 * i + 1) for i, (r, d) in enumerate(zip(o_refs, dils))]
    m = functools.reduce(jnp.maximum, lses)
    es = [jnp.exp(z - m) for z in lses]
    den = functools.reduce(lambda a, b: a + b, es)
    attn = functools.reduce(lambda a, b: a + b, [(e / den) * o for e, o in zip(es, outs)])
    mix = _dot(attn.astype(BF16), wo_ref[0:aw, :]) + _dot(rw_ref[...].astype(BF16), wo_ref[aw:, :])
    y = _layer_norm(ALPHA * x_ref[...] + mix, ln_ref[0:1, :], ln_ref[1:2, :])
    y_ref[...] = y
    yb_ref[...] = y.astype(yb_ref.dtype)


def _outproj(os_, lses, dils, rw, x2, w_out, ln_g, ln_b, tm):
    t, d = x2.shape
    aw = rw.shape[1]
    ln = jnp.concatenate([ln_g[None], ln_b[None], jnp.zeros((SUBLANES - 2, d), F32)], axis=0)
    wo = w_out.astype(BF16)
    tok = lambda w: pl.BlockSpec((tm, w), lambda i: (i, 0))
    view = [pl.BlockSpec((tm // dl, dl * aw), lambda i: (i, 0)) for dl in dils]
    const = lambda a: pl.BlockSpec(a.shape, lambda i: (0, 0))
    return pl.pallas_call(
        functools.partial(_outproj_kernel, dils),
        grid=(t // tm,),
        in_specs=view + view + [tok(aw), tok(d), const(wo), const(ln)],
        out_specs=[tok(d), tok(d)],
        out_shape=[jax.ShapeDtypeStruct((t, d), F32), jax.ShapeDtypeStruct((t, d), BF16)],
        scratch_shapes=[pltpu.VMEM((2 * len(dils) * aw // LANES, tm, LANES), F32)],
        compiler_params=_cparams("arbitrary"),
        name="outproj",
    )(*os_, *lses, rw, x2, wo, ln)


def _lanes(col_rep, n):
    return jnp.concatenate([col_rep] * (n // LANES), axis=1)


MOE_TILE = 512
ROW_CHUNK = 8
SORT_ROWS = 256
COMBINE_ROWS = 512
EXPERT_ROWS = 256


def _tile_rows(ne):
    raw = MOE_TILE * TOP_K + ne * (ROW_CHUNK - 1)
    unit = max(SORT_ROWS, COMBINE_ROWS)
    return -(-raw // unit) * unit


def _route_kernel(x_ref, rwt_ref, bias_ref, tri_ref, ones_ref, low_ref, dloc_ref, gate_ref, tab_ref):
    ne = rwt_ref.shape[0]
    tm = x_ref.shape[0]
    gsz = ne // N_GROUPS
    ninf = -jnp.inf

    aff = _sigmoid(_dotf(rwt_ref[...], x_ref[...], nt=True))
    sel = aff + _lanes(bias_ref[...], tm)

    sel3 = sel.reshape(N_GROUPS, gsz, tm)
    rid = lax.broadcasted_iota(I32, sel3.shape, 1).astype(F32)
    m1 = jnp.max(sel3, axis=1, keepdims=True)
    i1 = jnp.min(jnp.where(sel3 == m1, rid, float(gsz)), axis=1, keepdims=True)
    m2 = jnp.max(jnp.where(rid == i1, ninf, sel3), axis=1, keepdims=True)
    gsc = (m1 + m2).reshape(N_GROUPS, tm)
    gid = lax.broadcasted_iota(I32, gsc.shape, 0).astype(F32)
    keep = jnp.zeros(gsc.shape, F32)
    for _ in range(TOPK_GROUPS):
        gm = jnp.max(gsc, axis=0, keepdims=True)
        gi = jnp.min(jnp.where(gsc == gm, gid, float(N_GROUPS)), axis=0, keepdims=True)
        hit = gid == gi
        keep = jnp.where(hit, 1.0, keep)
        gsc = jnp.where(hit, ninf, gsc)
    cand = jnp.where(keep.reshape(N_GROUPS, 1, tm) > 0.0, sel3, ninf).reshape(ne, tm)

    eid = lax.broadcasted_iota(I32, (ne, tm), 0).astype(F32)
    hits, graw = [], []
    for _ in range(TOP_K):
        m = jnp.max(cand, axis=0, keepdims=True)
        ij = jnp.min(jnp.where(cand == m, eid, float(ne)), axis=0, keepdims=True)
        hit = eid == ij
        hits.append(hit)
        graw.append(jnp.sum(jnp.where(hit, aff, 0.0), axis=0, keepdims=True))
        cand = jnp.where(hit, ninf, cand)
    gsum = functools.reduce(lambda a, b: a + b, graw)
    gate_ref[...] = jnp.concatenate([g / gsum * ROUTED_SCALE for g in graw], axis=0)

    onehot = functools.reduce(lambda a, b: a + b, [h.astype(F32) for h in hits]).astype(BF16)
    before = _dot(onehot, tri_ref[...])
    count = _dot(onehot, ones_ref[...])
    padded = jnp.ceil(count * (1.0 / ROW_CHUNK)) * ROW_CHUNK
    start = _dot3(low_ref[...], padded)
    row = _lanes(start, tm) + before
    dloc_ref[...] = jnp.concatenate(
        [jnp.sum(jnp.where(h, row, 0.0), axis=0, keepdims=True) for h in hits], axis=0).astype(I32)
    lane = lax.broadcasted_iota(I32, padded.shape, 1)
    both = jnp.where(lane == 0, padded, jnp.where(lane == 1, start, 0.0))
    tab_ref[...] = both.T[0:SUBLANES, :].astype(I32)


def _route(x1, router_w, router_bias):
    t, d = x1.shape
    ne = router_w.shape[1]
    tm = MOE_TILE
    rwt = router_w.T
    bias = jnp.broadcast_to(router_bias[:, None], (ne, LANES))
    pos = jnp.arange(tm)
    tri = (pos[:, None] < pos[None, :]).astype(BF16)
    ones = jnp.ones((tm, LANES), BF16)
    eid = jnp.arange(ne)
    low = (eid[:, None] > eid[None, :]).astype(BF16)
    const = lambda a: pl.BlockSpec(a.shape, lambda i: (0, 0))
    tokt = pl.BlockSpec((TOP_K, tm), lambda i: (0, i))
    return pl.pallas_call(
        _route_kernel,
        grid=(t // tm,),
        in_specs=[pl.BlockSpec((tm, d), lambda i: (i, 0)), const(rwt), const(bias), const(tri), const(ones),
                  const(low)],
        out_specs=[tokt, tokt, pl.BlockSpec((None, SUBLANES, ne), lambda i: (i, 0, 0))],
        out_shape=[jax.ShapeDtypeStruct((TOP_K, t), I32), jax.ShapeDtypeStruct((TOP_K, t), F32),
                   jax.ShapeDtypeStruct((t // tm, SUBLANES, ne), I32)],
        compiler_params=_cparams("arbitrary"),
        name="route",
    )(x1, rwt, bias, tri, ones, low)


def _pack_bf16_pairs(v, exact=False):
    half = v.shape[1] // 2
    u = pltpu.bitcast(v if exact else v.astype(BF16).astype(F32), U32)
    return (u[:, 0:half] >> 16) | (u[:, half:] & jnp.uint32(0xFFFF0000))


def _unpack_bf16_pairs(w):
    lo = pltpu.bitcast(w << 16, F32)
    hi = pltpu.bitcast(w & jnp.uint32(0xFFFF0000), F32)
    return jnp.concatenate([lo, hi], axis=1)


def _dot3l(a, b_bf16):
    a1 = a.astype(BF16)
    r1 = a - a1.astype(F32)
    a2 = r1.astype(BF16)
    a3 = (r1 - a2.astype(F32)).astype(BF16)
    return _dot(a1, b_bf16) + (_dot(a2, b_bf16) + _dot(a3, b_bf16))


def _plan_kernel(nblk, cnt_ref, low_ref, upper_ref, gstart_ref, meta_ref, emeta_ref):
    c = cnt_ref[...].astype(F32)
    nt, ne = c.shape
    earlier = _dot3(low_ref[...], c)
    total = jnp.sum(c, axis=0, keepdims=True)
    padded = jnp.ceil(total * (1.0 / EXPERT_ROWS)) * EXPERT_ROWS
    pad_end = _dot3l(jnp.broadcast_to(padded, (SUBLANES, ne)), upper_ref[...])
    pad_start = (pad_end - padded)[0:1, :]
    gstart_ref[...] = (pad_start + earlier).astype(I32)
    emeta_ref[...] = jnp.concatenate(
        [pad_start, padded * (1.0 / EXPERT_ROWS), jnp.zeros((SUBLANES - 2, ne), F32)], axis=0).astype(I32)
    end_col = jnp.broadcast_to(pad_end.T[:, 0:1], (ne, nblk))
    blk_start = (lax.broadcasted_iota(I32, (ne, nblk), 1) * EXPERT_ROWS).astype(F32)
    owner = jnp.minimum(jnp.sum((end_col <= blk_start).astype(F32), axis=0, keepdims=True), float(ne - 1))
    used = jnp.max(end_col, axis=0, keepdims=True) * (1.0 / EXPERT_ROWS)
    meta_ref[...] = jnp.concatenate([owner, used, jnp.zeros((SUBLANES - 2, nblk), F32)], axis=0).astype(I32)


def _plan(cnt, nblk_pad):
    nt, ne = cnt.shape
    ti = jnp.arange(nt)
    low = (ti[:, None] > ti[None, :]).astype(BF16)
    ei = jnp.arange(ne)
    upper = (ei[:, None] <= ei[None, :]).astype(BF16)
    return pl.pallas_call(
        functools.partial(_plan_kernel, nblk_pad),
        out_shape=[jax.ShapeDtypeStruct((nt, ne), I32), jax.ShapeDtypeStruct((SUBLANES, nblk_pad), I32),
                   jax.ShapeDtypeStruct((SUBLANES, ne), I32)],
        compiler_params=pltpu.CompilerParams(vmem_limit_bytes=VMEM_LIMIT_BYTES),
        name="plan",
    )(cnt, low, upper)


def _group_copies(cnt_ref, loc_ref, gstart_ref, tile, ne, local, remote, sem, to_remote, wait):
    def one(e, carry):
        n = pl.multiple_of(cnt_ref[tile * ne + e], ROW_CHUNK)

        @pl.when(n > 0)
        def _():
            lo = pl.multiple_of(loc_ref[tile * ne + e], ROW_CHUNK)
            go = pl.multiple_of(gstart_ref[tile * ne + e], ROW_CHUNK)
            a, b = local.at[pl.ds(lo, n), :], remote.at[pl.ds(go, n), :]
            cp = pltpu.make_async_copy(a, b, sem) if to_remote else pltpu.make_async_copy(b, a, sem)
            if wait:
                cp.wait()
            else:
                cp.start()
        return carry

    lax.fori_loop(0, ne, one, 0, unroll=8)


def _dispatch_kernel(ne, cnt_ref, loc_ref, gstart_ref, owner_ref, used_ref, urows_ref, dloc_ref, x_ref, xs_ref,
                     buf_ref, zero_ref, sems, zsem):
    step = pl.program_id(0)
    nsteps = pl.num_programs(0)
    slot = step % 2
    tm = x_ref.shape[0]
    nblk = owner_ref.shape[0]
    used = used_ref[0]

    @pl.when(step == 0)
    def _():
        zero_ref[...] = jnp.zeros_like(zero_ref)

        def block_copy(i):
            return pltpu.make_async_copy(zero_ref, xs_ref.at[pl.ds(i * EXPERT_ROWS, EXPERT_ROWS), :], zsem)

        def is_last(i):
            return (i == used - 1) | (owner_ref[jnp.minimum(i + 1, nblk - 1)] != owner_ref[i])

        def start(i, carry):
            @pl.when(is_last(i))
            def _():
                block_copy(i).start()
            return carry

        def wait(i, carry):
            @pl.when(is_last(i))
            def _():
                block_copy(i).wait()
            return carry

        lax.fori_loop(0, used, start, 0)
        lax.fori_loop(0, used, wait, 0)

    x = x_ref[...]
    dl = dloc_ref[...].astype(I16)
    riota = lax.broadcasted_iota(I32, (SORT_ROWS, tm), 0).astype(I16)
    one = jnp.ones((SORT_ROWS, tm), BF16)

    def chunk(c, carry):
        r0 = pl.multiple_of(c * SORT_ROWS, SORT_ROWS)
        r = riota + r0.astype(I16)
        p = jnp.zeros((SORT_ROWS, tm), BF16)
        for j in range(TOP_K):
            p = jnp.where(r == dl[j:j + 1, :], one, p)
        buf_ref[slot, pl.ds(r0, SORT_ROWS), :] = _pack_bf16_pairs(_dot(p, x), exact=True)
        return carry

    lax.fori_loop(0, (urows_ref[step] + SORT_ROWS - 1) // SORT_ROWS, chunk, 0)

    copies = functools.partial(_group_copies, cnt_ref, loc_ref, gstart_ref)
    copies(step, ne, buf_ref.at[slot], xs_ref, sems.at[slot], True, False)

    @pl.when(step > 0)
    def _():
        copies(step - 1, ne, buf_ref.at[1 - slot], xs_ref, sems.at[1 - slot], True, True)

    @pl.when(step == nsteps - 1)
    def _():
        copies(step, ne, buf_ref.at[slot], xs_ref, sems.at[slot], True, True)


def _dispatch(x1b, dloc, cnt, loc, gstart, owner, used, used_rows, nrows):
    t, d = x1b.shape
    nt, ne = cnt.shape
    tm = MOE_TILE
    rt = _tile_rows(ne)
    grid_spec = pltpu.PrefetchScalarGridSpec(
        num_scalar_prefetch=6,
        grid=(nt,),
        in_specs=[pl.BlockSpec((TOP_K, tm), lambda i, *_: (0, i)), pl.BlockSpec((tm, d), lambda i, *_: (i, 0))],
        out_specs=pl.BlockSpec(memory_space=pl.ANY),
        scratch_shapes=[pltpu.VMEM((2, rt, d // 2), U32), pltpu.VMEM((EXPERT_ROWS, d // 2), U32),
                        pltpu.SemaphoreType.DMA((2,)), pltpu.SemaphoreType.DMA(())],
    )
    return pl.pallas_call(
        functools.partial(_dispatch_kernel, ne),
        grid_spec=grid_spec,
        out_shape=jax.ShapeDtypeStruct((nrows, d // 2), U32),
        compiler_params=_cparams("arbitrary"),
        name="dispatch",
    )(cnt.reshape(-1), loc.reshape(-1), gstart.reshape(-1), owner, used, used_rows, dloc, x1b)


EXPERT_PIECE = 1024


def _expert_kernel(first_ref, nblk_ref, xs_ref, wg_ref, wu_ref, wd_ref, ys_ref,
                   xbuf, ybuf, wgb_ref, wub_ref, wdb_ref, sem_in, sem_out, done_ref):
    e = pl.program_id(0)
    last = pl.num_programs(0) - 1
    per_piece = EXPERT_PIECE // EXPERT_ROWS

    def pieces(ex):
        return (nblk_ref[ex] + per_piece - 1) // per_piece

    def span(ex, s):
        n = jnp.minimum(EXPERT_PIECE, nblk_ref[ex] * EXPERT_ROWS - s * EXPERT_PIECE)
        n = pl.multiple_of(n, EXPERT_ROWS)
        return pl.ds(pl.multiple_of(first_ref[ex] + s * EXPERT_PIECE, EXPERT_ROWS), n), pl.ds(0, n)

    def in_copy(ex, s, slot):
        far, near = span(ex, s)
        return pltpu.make_async_copy(xs_ref.at[far, :], xbuf.at[slot, near, :], sem_in.at[slot])

    def out_copy(ex, s, slot):
        far, near = span(ex, s)
        return pltpu.make_async_copy(ybuf.at[slot, near, :], ys_ref.at[far, :], sem_out.at[slot])

    @pl.when(e == 0)
    def _():
        done_ref[0] = 0

        @pl.when(nblk_ref[0] > 0)
        def _():
            in_copy(0, 0, 0).start()

    nb = nblk_ref[e]
    ns = pieces(e)
    g0 = done_ref[0]

    @pl.when(nb > 0)
    def _():
        wgb_ref[...] = wg_ref[...].astype(BF16)
        wub_ref[...] = wu_ref[...].astype(BF16)
        wdb_ref[...] = wd_ref[...].astype(BF16)

        def piece(s, carry):
            slot = (g0 + s) % 2

            @pl.when(s + 1 < ns)
            def _():
                in_copy(e, s + 1, 1 - slot).start()

            in_copy(e, s, slot).wait()

            @pl.when(s >= 2)
            def _():
                out_copy(e, s - 2, slot).wait()

            def block(b, c2):
                rows = pl.ds(pl.multiple_of(b * EXPERT_ROWS, EXPERT_ROWS), EXPERT_ROWS)
                xb = _unpack_bf16_pairs(xbuf[slot, rows, :]).astype(BF16)
                hg = _dot(xb, wgb_ref[...])
                hb = hg * _sigmoid(hg) * _dot(xb, wub_ref[...])
                ybuf[slot, rows, :] = _pack_bf16_pairs(_dot(hb.astype(BF16), wdb_ref[...]))
                return c2

            lax.fori_loop(0, jnp.minimum(per_piece, nb - s * per_piece), block, 0)
            out_copy(e, s, slot).start()
            return carry

        lax.fori_loop(0, ns, piece, 0)

    g1 = g0 + ns
    nxt = jnp.minimum(e + 1, last)

    @pl.when((e < last) & (nblk_ref[nxt] > 0))
    def _():
        in_copy(nxt, 0, g1 % 2).start()

    def drain(s, carry):
        out_copy(e, s, (g0 + s) % 2).wait()
        return carry

    lax.fori_loop(jnp.maximum(ns - 2, 0), ns, drain, 0)
    done_ref[0] = g1


def _experts(xs, first_row, nblocks, e_gate, e_up, e_down):
    nrows, dh = xs.shape
    ne, d, ff = e_gate.shape
    wsel = lambda e, *_: (e, 0, 0)
    grid_spec = pltpu.PrefetchScalarGridSpec(
        num_scalar_prefetch=2,
        grid=(ne,),
        in_specs=[pl.BlockSpec(memory_space=pl.ANY),
                  pl.BlockSpec((None, d, ff), wsel), pl.BlockSpec((None, d, ff), wsel),
                  pl.BlockSpec((None, ff, d), wsel)],
        out_specs=pl.BlockSpec(memory_space=pl.ANY),
        scratch_shapes=[pltpu.VMEM((2, EXPERT_PIECE, dh), U32), pltpu.VMEM((2, EXPERT_PIECE, dh), U32),
                        pltpu.VMEM((d, ff), BF16), pltpu.VMEM((d, ff), BF16), pltpu.VMEM((ff, d), BF16),
                        pltpu.SemaphoreType.DMA((2,)), pltpu.SemaphoreType.DMA((2,)), pltpu.SMEM((1,), I32)],
    )
    return pl.pallas_call(
        _expert_kernel,
        grid_spec=grid_spec,
        out_shape=jax.ShapeDtypeStruct((nrows, dh), U32),
        compiler_params=_cparams("arbitrary"),
        name="experts",
    )(first_row, nblocks, xs, e_gate, e_up, e_down)


def _final_kernel(ne, cnt_ref, loc_ref, gstart_ref, used_ref, dloc_ref, gate_ref, x_ref, xb_ref, ys_ref,
                  sg_ref, su_ref, sd_ref, ln_ref, o_ref, buf_ref, acc_ref, sems):
    step = pl.program_id(0)
    nsteps = pl.num_programs(0)
    slot = step % 2
    tm = x_ref.shape[0]
    used = used_ref[step]
    copies = functools.partial(_group_copies, cnt_ref, loc_ref, gstart_ref)

    @pl.when(step == 0)
    def _():
        copies(step, ne, buf_ref.at[0], ys_ref, sems.at[0], False, False)

    @pl.when(step + 1 < nsteps)
    def _():
        copies(step + 1, ne, buf_ref.at[1 - slot], ys_ref, sems.at[1 - slot], False, False)

    xb = xb_ref[...]
    hg = _dot(xb, sg_ref[...])
    hs = hg * _sigmoid(hg) * _dot(xb, su_ref[...])
    acc_ref[...] = _dot(hs.astype(BF16), sd_ref[...])

    copies(step, ne, buf_ref.at[slot], ys_ref, sems.at[slot], False, True)

    dl = dloc_ref[...].astype(I16)
    gt = gate_ref[...].astype(BF16)
    rows16 = lax.broadcasted_iota(I32, (COMBINE_ROWS, tm), 0).astype(I16)
    gts = [jnp.broadcast_to(gt[j:j + 1, :], (COMBINE_ROWS, tm)) for j in range(TOP_K)]
    riota = lax.broadcasted_iota(I32, (COMBINE_ROWS, buf_ref.shape[2]), 0)

    def chunk(c, carry):
        r0 = pl.multiple_of(c * COMBINE_ROWS, COMBINE_ROWS)
        r = rows16 + r0.astype(I16)
        g = jnp.zeros((COMBINE_ROWS, tm), BF16)
        for j in range(TOP_K):
            g = jnp.where(r == dl[j:j + 1, :], gts[j], g)
        w = buf_ref[slot, pl.ds(r0, COMBINE_ROWS), :]
        w = jnp.where(riota + r0 < used, w, jnp.zeros_like(w))
        y = _unpack_bf16_pairs(w).astype(BF16)
        acc_ref[...] += lax.dot_general(g, y, (((0,), (0,)), ((), ())), preferred_element_type=F32)
        return carry

    lax.fori_loop(0, (used + COMBINE_ROWS - 1) // COMBINE_ROWS, chunk, 0)
    o_ref[...] = _layer_norm(ALPHA * x_ref[...] + acc_ref[...], ln_ref[0:1, :], ln_ref[1:2, :])


def _final(x1, x1b, dloc, gate, cnt, loc, gstart, used_rows, ys, s_gate, s_up, s_down, ln_g, ln_b):
    t, d = x1.shape
    nt, ne = cnt.shape
    tm = MOE_TILE
    rt = _tile_rows(ne)
    ln = jnp.concatenate([ln_g[None], ln_b[None], jnp.zeros((SUBLANES - 2, d), F32)], axis=0)
    sg, su, sd = s_gate.astype(BF16), s_up.astype(BF16), s_down.astype(BF16)
    tok = pl.BlockSpec((tm, d), lambda i, *_: (i, 0))
    tokt = pl.BlockSpec((TOP_K, tm), lambda i, *_: (0, i))
    const = lambda a: pl.BlockSpec(a.shape, lambda i, *_: (0, 0))
    grid_spec = pltpu.PrefetchScalarGridSpec(
        num_scalar_prefetch=4,
        grid=(nt,),
        in_specs=[tokt, tokt, tok, tok, pl.BlockSpec(memory_space=pl.ANY), const(sg), const(su), const(sd), const(ln)],
        out_specs=tok,
        scratch_shapes=[pltpu.VMEM((2, rt, d // 2), U32), pltpu.VMEM((tm, d), F32), pltpu.SemaphoreType.DMA((2,))],
    )
    return pl.pallas_call(
        functools.partial(_final_kernel, ne),
        grid_spec=grid_spec,
        out_shape=jax.ShapeDtypeStruct((t, d), F32),
        compiler_params=_cparams("arbitrary"),
        name="final",
    )(cnt.reshape(-1), loc.reshape(-1), gstart.reshape(-1), used_rows, dloc, gate, x1, x1b, ys, sg, su, sd, ln)


def _moe(x1, x1b, router_w, router_bias, e_gate, e_up, e_down, s_gate, s_up, s_down, ln_g, ln_b):
    t = x1.shape[0]
    ne = router_w.shape[1]
    nt = t // MOE_TILE
    nblk = -(-(nt * _tile_rows(ne)) // EXPERT_ROWS) + ne
    nblk_pad = -(-nblk // LANES) * LANES
    dloc, gate, tab = _route(x1, router_w, router_bias)
    cnt, loc = tab[:, 0, :], tab[:, 1, :]
    used_rows = loc[:, ne - 1] + cnt[:, ne - 1]
    gstart, meta, emeta = _plan(cnt, nblk_pad)
    owner, used = meta[0], meta[1, 0:1]
    xs = _dispatch(x1b, dloc, cnt, loc, gstart, owner, used, used_rows, nblk * EXPERT_ROWS)
    ys = _experts(xs, emeta[0], emeta[1], e_gate, e_up, e_down)
    return _final(x1, x1b, dloc, gate, cnt, loc, gstart, used_rows, ys, s_gate, s_up, s_down, ln_g, ln_b)


def kernel(x, w_in, mu_shift, w0, w_decay_up, a0, w_aaa_up, w_gate_up, k_k, k_a, r_k, lnx_g, lnx_b, w_out,
           ln1_g, ln1_b, router_w, router_bias, e_gate, e_up, e_down, s_gate, s_up, s_down, ln2_g, ln2_b):
    bsz, seq, d = x.shape
    x2 = x.reshape(bsz * seq, d)
    qkvs, feats = _inproj(x2, seq, w_in[0], mu_shift[0], w0[0], w_decay_up[0], a0[0], w_aaa_up[0], w_gate_up[0],
                          k_k[0], k_a[0], r_k[0], tm=256)
    rw = _rwkv(feats, bsz, seq, lnx_g[0], lnx_b[0], nbat=4)
    x1, x1b = _mixer_tail(qkvs, rw, x2, bsz, seq, w_out[0], ln1_g[0], ln1_b[0])
    out = _moe(x1, x1b, router_w[0], router_bias[0], e_gate[0], e_up[0], e_down[0], s_gate[0], s_up[0], s_down[0],
               ln2_g[0], ln2_b[0])
    return out.reshape(bsz, seq, d)


def _mixer_tail(qkvs, rw, x2, bsz, seq, w_out, ln_g, ln_b):
    res = [_attention(q, k, v, bsz, seq, window, dilation)
           for (q, k, v), (window, dilation) in zip(qkvs, ATTN_PATTERNS)]
    dils = tuple(dl for _, dl in ATTN_PATTERNS)
    return _outproj([o for o, _ in res], [l for _, l in res], dils, rw, x2, w_out, ln_g, ln_b, tm=512)
```

```python
import functools

import jax
import jax.numpy as jnp
from jax import lax
from jax.experimental import pallas as pl
from jax.experimental.pallas import tpu as pltpu

F32 = jnp.float32
BF16 = jnp.bfloat16
I32 = jnp.int32
I16 = jnp.int16
U32 = jnp.uint32

LANES = 128
SUBLANES = 8
VMEM_LIMIT_BYTES = 56 * 1024 * 1024

HEAD_DIM = 64
ATTN_HEADS = 8
RWKV_HEADS = 8
ATTN_WIDTH = ATTN_HEADS * HEAD_DIM
RWKV_WIDTH = RWKV_HEADS * HEAD_DIM
ATTN_PATTERNS = ((128, 1), (512, 4), (2048, 16))
Q_BLOCK = 128
ROPE_THETA = 10000.0
DECAY_LORA = 64
AAA_LORA = 64
GATE_LORA = 160
GN_EPS = 64e-5
LN_EPS = 1e-5
TOP_K = 8
N_GROUPS = 8
TOPK_GROUPS = 4
ROUTED_SCALE = 2.5
DEPTH = 1
ALPHA = (2.0 * DEPTH) ** 0.25

RWKV_CHUNK = 64
PAIR = 2 * HEAD_DIM
LORA_PAD = 2 * LANES


def _cparams(*sem):
    return pltpu.CompilerParams(dimension_semantics=sem, vmem_limit_bytes=VMEM_LIMIT_BYTES)


def _split_bf16(a):
    hi = a.astype(BF16)
    lo = (a - hi.astype(F32)).astype(BF16)
    return hi, lo


def _dot(a, b):
    return jnp.dot(a, b, preferred_element_type=F32)


def _dot_nt(a, b):
    return lax.dot_general(a, b, (((1,), (1,)), ((), ())), preferred_element_type=F32)


def _dot_hl(a_f32, b_bf16):
    hi, lo = _split_bf16(a_f32)
    return _dot(hi, b_bf16) + _dot(lo, b_bf16)


def _softplus(z):
    return jnp.maximum(z, 0.0) + jnp.log(1.0 + jnp.exp(-jnp.abs(z)))


def _sigmoid(z):
    return 1.0 / (1.0 + jnp.exp(-z))


def _inproj_kernel(dils, x_ref, w_ref, cos_ref, sin_ref, mu_ref, wd_ref, wa_ref, wg_ref, vec_ref, bd_ref, *refs):
    nq = 3 * len(dils)
    qkv_refs = refs[0:nq]
    r_ref, ld_ref, kp_ref, vv_ref, kk_ref, b_ref, g_ref, bon_ref, carry_ref, qkv_scr = refs[nq:]
    s = pl.program_id(1)
    tm = x_ref.shape[0]
    aw = ATTN_WIDTH
    rw = RWKV_WIDTH

    @pl.when(s == 0)
    def _():
        carry_ref[...] = jnp.zeros_like(carry_ref)

    h = _dot(x_ref[...].astype(BF16), w_ref[...])
    reps = aw // cos_ref.shape[1]
    cos = jnp.concatenate([cos_ref[...]] * reps, axis=1)
    sin = jnp.concatenate([sin_ref[...]] * reps, axis=1)
    qkv = [(h[:, 0:aw] * cos + h[:, aw:2 * aw] * sin) * (HEAD_DIM ** -0.5),
           h[:, 2 * aw:3 * aw] * cos + h[:, 3 * aw:4 * aw] * sin,
           h[:, 4 * aw:5 * aw]]
    nslab = aw // LANES
    for i in range(3):
        for j in range(nslab):
            qkv_scr[i * nslab + j] = qkv[i][:, j * LANES:(j + 1) * LANES]
    for di, d in enumerate(dils):
        for i in range(3):
            o_ref = qkv_refs[3 * di + i]
            if d == 1:
                o_ref[...] = qkv[i].astype(o_ref.dtype)
                continue
            for res in range(d):
                for j in range(nslab):
                    rows = qkv_scr[i * nslab + j, pl.ds(res, tm // d, stride=d), :]
                    o_ref[:, res * aw + j * LANES:res * aw + (j + 1) * LANES] = rows.astype(o_ref.dtype)

    f = h[:, 5 * aw:]
    rows = lax.broadcasted_iota(I32, f.shape, 0)
    prev = jnp.where(rows == 0, carry_ref[SUBLANES - 1:SUBLANES, :], pltpu.roll(f, 1, axis=0))
    carry_ref[...] = f[tm - SUBLANES:tm, :]
    f = f + (prev - f) * mu_ref[...]

    r = f[:, 0:rw]
    k = f[:, rw:2 * rw]
    v = f[:, 2 * rw:3 * rw]
    la = f[:, 3 * rw:3 * rw + LANES]
    gl = f[:, 3 * rw + LANES:]
    w0, a0, k_k, k_a, r_k = (vec_ref[i:i + 1, :] for i in range(5))

    z = w0 + _dot(jnp.tanh(la).astype(BF16), wd_ref[...])
    w = -_softplus(-z) - 0.5
    ld_ref[...] = -jnp.exp(w)
    a = _sigmoid(a0 + _dot(la.astype(BF16), wa_ref[...]))
    g_ref[...] = _dot(_sigmoid(gl).astype(BF16), wg_ref[...])

    bd = bd_ref[...]
    kk = k * k_k
    nrm = jnp.sqrt(_dot_hl(kk * kk, bd))
    kk = kk / jnp.maximum(nrm, 1e-12)
    kp = k * (1.0 + (a - 1.0) * k_a)
    r_ref[...] = r
    kp_ref[...] = kp
    vv_ref[...] = v
    kk_ref[...] = kk
    b_ref[...] = kk * a
    bon_ref[...] = _dot_hl(r * kp * r_k, bd) * v


def _rot_half_cols(w):
    d, n = w.shape
    w4 = w.reshape(d, n // HEAD_DIM, 2, HEAD_DIM // 2)
    return jnp.stack([-w4[:, :, 1, :], w4[:, :, 0, :]], axis=2).reshape(d, n)


def _inproj(x2, seq, w_in, mu_shift, w0, w_decay_up, a0, w_aaa_up, w_gate_up, k_k, k_a, r_k, tm):
    t, d = x2.shape
    aw, rw = ATTN_WIDTH, RWKV_WIDTH
    wq, wk, wv = w_in[:, 0:aw], w_in[:, aw:2 * aw], w_in[:, 2 * aw:3 * aw]
    wf = w_in[:, 3 * aw:]
    gpad = LORA_PAD - GATE_LORA
    w_all = jnp.concatenate(
        [wq, _rot_half_cols(wq), wk, _rot_half_cols(wk), wv, wf, jnp.zeros((d, gpad), F32)], axis=1).astype(BF16)
    mu = jnp.concatenate([mu_shift, jnp.zeros((gpad,), F32)])[None, :]
    nf = mu.shape[1]
    wd = jnp.concatenate([w_decay_up, jnp.zeros((AAA_LORA, rw), F32)], axis=0).astype(BF16)
    wa = jnp.concatenate([jnp.zeros((DECAY_LORA, rw), F32), w_aaa_up], axis=0).astype(BF16)
    wg = jnp.concatenate([w_gate_up, jnp.zeros((gpad, rw), F32)], axis=0).astype(BF16)
    vec = jnp.stack([w0, a0, k_k, k_a, r_k.reshape(-1), w0 * 0, w0 * 0, w0 * 0])
    head = jnp.arange(rw) // HEAD_DIM
    bd = (head[:, None] == head[None, :]).astype(BF16)
    half = HEAD_DIM // 2
    inv_freq = ROPE_THETA ** (-jnp.arange(half, dtype=F32) * 2.0 / HEAD_DIM)
    ang = jnp.arange(seq, dtype=F32)[:, None] * inv_freq[None, :]
    cos = jnp.tile(jnp.cos(ang), (1, LANES // half))
    sin = jnp.tile(jnp.sin(ang), (1, LANES // half))

    nst = seq // tm
    tok = lambda b, s: (b * nst + s, 0)
    const = lambda b, s: (0, 0)
    full = lambda a: pl.BlockSpec(a.shape, const)
    dils = tuple(dl for _, dl in ATTN_PATTERNS)
    qkv_specs, qkv_shapes = [], []
    for dl in dils:
        qkv_specs += [pl.BlockSpec((tm // dl, dl * aw), tok)] * 3
        qkv_shapes += [jax.ShapeDtypeStruct((t // dl, dl * aw), BF16)] * 3
    out_f = jax.ShapeDtypeStruct((t, rw), F32)
    outs = pl.pallas_call(
        functools.partial(_inproj_kernel, dils),
        grid=(t // seq, nst),
        in_specs=[pl.BlockSpec((tm, d), tok), full(w_all),
                  pl.BlockSpec((tm, LANES), lambda b, s: (s, 0)), pl.BlockSpec((tm, LANES), lambda b, s: (s, 0)),
                  full(mu), full(wd), full(wa), full(wg), full(vec), full(bd)],
        out_specs=qkv_specs + [pl.BlockSpec((tm, rw), tok)] * 8,
        out_shape=qkv_shapes + [out_f] * 8,
        scratch_shapes=[pltpu.VMEM((SUBLANES, nf), F32), pltpu.VMEM((3 * aw // LANES, tm, LANES), F32)],
        compiler_params=_cparams("arbitrary", "arbitrary"),
        name="inproj",
    )(x2, w_all, cos, sin, mu, wd, wa, wg, vec, bd)
    nq = 3 * len(dils)
    return [outs[3 * i:3 * i + 3] for i in range(len(dils))], outs[nq:]


def _dotf(a, b, nt=False, passes=3):
    dot = _dot_nt if nt else _dot
    if passes == 1:
        return dot(a.astype(BF16), b.astype(BF16))
    ah, al = _split_bf16(a)
    if passes == 2:
        bh = b.astype(BF16)
        return dot(ah, bh) + dot(al, bh)
    bh, bl = _split_bf16(b)
    return dot(ah, bh) + (dot(ah, bl) + dot(al, bh))


RWKV_PASSES = dict(s8=1, inv=1, w1=1, au=1, ry=1, gh=1, yh=1)


def _dot3(a_bf16, b):
    b1 = b.astype(BF16)
    r1 = b - b1.astype(F32)
    b2 = r1.astype(BF16)
    b3 = (r1 - b2.astype(F32)).astype(BF16)
    return _dot(a_bf16, b1) + (_dot(a_bf16, b2) + _dot(a_bf16, b3))


def _rwkv_kernel(r_ref, ld_ref, kp_ref, v_ref, kk_ref, b_ref, g_ref, bon_ref, gn_ref, bd_ref, o_ref, h_ref):
    c = pl.program_id(1)
    nbat, ch, rw = r_ref.shape
    npairs = rw // PAIR

    @pl.when(c == 0)
    def _():
        h_ref[...] = jnp.zeros_like(h_ref)

    ri = lax.broadcasted_iota(I32, (ch, ch), 0)
    ci = lax.broadcasted_iota(I32, (ch, ch), 1)
    tril = (ri >= ci).astype(BF16)
    bd = bd_ref[...]
    row = lax.broadcasted_iota(I32, (ch, PAIR), 0)
    col = lax.broadcasted_iota(I32, (ch, PAIR), 1)
    first = col < HEAD_DIM
    jj = col & (HEAD_DIM - 1)
    strict = jj < row
    incl = jj <= row
    eye = (jj == row).astype(F32)
    row2 = lax.broadcasted_iota(I32, (PAIR, PAIR), 0)
    col2 = lax.broadcasted_iota(I32, (PAIR, PAIR), 1)
    same_head = (row2 < HEAD_DIM) == (col2 < HEAD_DIM)
    diag2 = row2 == col2
    zeros_cp = jnp.zeros((ch, PAIR), F32)

    def bdiag(y):
        return jnp.concatenate([jnp.where(first, y, 0.0), jnp.where(first, 0.0, y)], axis=0)

    ps = RWKV_PASSES
    units = []
    for bi in range(nbat):
        ld = ld_ref[bi]
        cum = _dot3(tril, ld)
        tot = cum[ch - 1:ch, :]
        a_t = -kk_ref[bi] * jnp.exp(cum - ld)
        pinv = jnp.exp(-cum)
        b_t = b_ref[bi] * pinv
        k_t = kp_ref[bi] * pinv
        r_t = r_ref[bi] * jnp.exp(cum)
        pend = jnp.exp(tot - cum)
        b_end = b_ref[bi] * pend
        k_end = kp_ref[bi] * pend
        p_tot = jnp.exp(tot)
        v_all = v_ref[bi]
        for p in range(npairs):
            sl = slice(p * PAIR, (p + 1) * PAIR)
            units.append(dict(h=bi * npairs + p, a=a_t[:, sl], r=r_t[:, sl], b=b_t[:, sl], k=k_t[:, sl],
                              v=v_all[:, sl], be=b_end[:, sl], ke=k_end[:, sl], pt=p_tot[:, sl]))

    for u in units:
        u["s8"] = _dotf(jnp.concatenate([u["a"], u["r"]], axis=0),
                        jnp.concatenate([bdiag(u["b"]), bdiag(u["k"])], axis=0), nt=True, passes=ps["s8"])
    for u in units:
        s8 = u.pop("s8")
        u["l"] = jnp.where(strict, s8[0:ch, 0:PAIR], 0.0)
        u["ak"] = jnp.where(strict, s8[0:ch, PAIR:], 0.0)
        u["mrb"] = jnp.where(incl, s8[ch:, 0:PAIR], 0.0)
        u["mrk"] = jnp.where(incl, s8[ch:, PAIR:], 0.0)
        u["t"] = eye + u["l"]
    for u in units:
        u["lk"] = _dotf(u["l"], bdiag(u["l"]), passes=ps["inv"])
        u["w1"] = _dotf(u["ak"], bdiag(u["v"]), passes=ps["w1"])
    for _ in range(max(ch.bit_length() - 3, 0)):
        for u in units:
            u["both"] = _dotf(jnp.concatenate([u["t"], u["lk"]], axis=0), bdiag(u["lk"]), passes=ps["inv"])
        for u in units:
            both = u.pop("both")
            u["t"] = u["t"] + both[0:ch]
            u["lk"] = both[ch:]
    for u in units:
        u["t"] = u["t"] + _dotf(u["t"], bdiag(u["lk"]), passes=ps["inv"])
    for u in units:
        u["au"] = _dotf(u["t"], jnp.concatenate([bdiag(u["a"]), bdiag(u["w1"])], axis=1), passes=ps["au"])
    for u in units:
        a_hat, u_loc = u["au"][:, 0:PAIR], u["au"][:, PAIR:]
        rhs = jnp.concatenate([
            jnp.concatenate([bdiag(a_hat), bdiag(u_loc)], axis=1),
            jnp.concatenate([jnp.zeros((2 * ch, PAIR), F32), bdiag(u["v"])], axis=1)], axis=0)
        u["ry"] = _dotf(jnp.concatenate([u["mrb"], u["mrk"]], axis=1), rhs, passes=ps["ry"])
        bkt = jnp.concatenate([u["be"], u["ke"]], axis=0).T
        u["gh"] = _dotf(bkt, jnp.concatenate([u["au"], jnp.concatenate([zeros_cp, u["v"]], axis=1)], axis=0),
                        passes=ps["gh"])
    for u in units:
        r_hat = u["r"] + u["ry"][:, 0:PAIR]
        g_mat = jnp.where(same_head, u["gh"][:, 0:PAIR], 0.0) + jnp.where(diag2, u["pt"], 0.0)
        u["yh"] = _dotf(jnp.concatenate([r_hat, g_mat], axis=0), h_ref[u["h"]], passes=ps["yh"])
    for u in units:
        h_ref[u["h"]] = u["yh"][ch:] + jnp.where(same_head, u["gh"][:, PAIR:], 0.0)
        u["y"] = u["yh"][0:ch] + u["ry"][:, PAIR:]

    inv_n = 1.0 / HEAD_DIM
    for bi in range(nbat):
        y = jnp.concatenate([u["y"] for u in units[bi * npairs:(bi + 1) * npairs]], axis=1)
        yc = y - _dot_hl(y, bd) * inv_n
        var = _dot_hl(yc * yc, bd) * inv_n
        yn = yc * lax.rsqrt(var + GN_EPS) * gn_ref[0:1, :] + gn_ref[1:2, :]
        o_ref[bi] = (yn + bon_ref[bi]) * g_ref[bi]


def _rwkv(feats, bsz, seq, lnx_g, lnx_b, nbat):
    t, rw = feats[0].shape
    ch = RWKV_CHUNK
    gn = jnp.concatenate([lnx_g[None], lnx_b[None], jnp.zeros((SUBLANES - 2, rw), F32)], axis=0)
    head = jnp.arange(rw) // HEAD_DIM
    bd = (head[:, None] == head[None, :]).astype(BF16)
    tok = pl.BlockSpec((nbat, ch, rw), lambda bb, c: (bb, c, 0))
    const = lambda a: pl.BlockSpec(a.shape, lambda bb, c: (0, 0))
    out = pl.pallas_call(
        _rwkv_kernel,
        grid=(bsz // nbat, seq // ch),
        in_specs=[tok] * 8 + [const(gn), const(bd)],
        out_specs=tok,
        out_shape=jax.ShapeDtypeStruct((bsz, seq, rw), F32),
        scratch_shapes=[pltpu.VMEM((nbat * (rw // PAIR), PAIR, PAIR), F32)],
        compiler_params=_cparams("arbitrary", "arbitrary"),
        name="rwkv",
    )(*[f.reshape(bsz, seq, rw) for f in feats], gn, bd)
    return out.reshape(t, rw)


NEG_BIG = -1e30


def _attn_kernel(n_back, q_ref, kc_ref, kp_ref, vc_ref, vp_ref, o_ref, lse_ref):
    n = pl.program_id(2)
    qb = q_ref.shape[0]
    qi = lax.broadcasted_iota(I32, (qb, 2 * qb), 0)
    ki = lax.broadcasted_iota(I32, (qb, 2 * qb), 1)
    dist = qb + qi - ki
    has_prev = jnp.where(n > 0, 0, qb)
    valid = (dist >= 0) & (dist <= n_back) & (ki >= has_prev)
    first = lax.broadcasted_iota(I32, (qb, PAIR), 1) < HEAD_DIM
    npairs = q_ref.shape[1] // PAIR
    heads = []
    for p in range(npairs):
        sl = slice(p * PAIR, (p + 1) * PAIR)
        q2 = q_ref[:, sl]
        k2 = jnp.concatenate([kp_ref[:, sl], kc_ref[:, sl]], axis=0)
        for hh in range(2):
            keep = first if hh == 0 else jnp.logical_not(first)
            s = _dot_nt(jnp.where(keep, q2, jnp.zeros_like(q2)), k2)
            heads.append(dict(sl=sl, s=s))
    for h in heads:
        s = jnp.where(valid, h.pop("s"), NEG_BIG)
        m = jnp.max(s, axis=1, keepdims=True)
        pe = jnp.where(valid, jnp.exp(s - m), 0.0)
        l = jnp.sum(pe, axis=1, keepdims=True)
        h["pe"] = pe.astype(BF16)
        h["l"] = l
        h["lse"] = m + jnp.log(l)
    for h in heads:
        sl = h["sl"]
        v2 = jnp.concatenate([vp_ref[:, sl], vc_ref[:, sl]], axis=0)
        h["o"] = _dot(h.pop("pe"), v2) / h["l"]
    for p in range(npairs):
        h0, h1 = heads[2 * p], heads[2 * p + 1]
        o_ref[:, h0["sl"]] = jnp.where(first, h0["o"], h1["o"])
        lse_ref[:, h0["sl"]] = jnp.where(first, h0["lse"], h1["lse"])


def _attention(q, k, v, bsz, seq, window, dilation):
    aw = q.shape[1] // dilation
    nb = seq // dilation // Q_BLOCK
    cur = pl.BlockSpec((Q_BLOCK, aw), lambda b, r, n: (b * nb + n, r))
    prv = pl.BlockSpec((Q_BLOCK, aw), lambda b, r, n: (b * nb + jnp.maximum(n - 1, 0), r))
    shp = jax.ShapeDtypeStruct(q.shape, F32)
    return pl.pallas_call(
        functools.partial(_attn_kernel, window // dilation),
        grid=(bsz, dilation, nb),
        in_specs=[cur, cur, prv, cur, prv],
        out_specs=[cur, cur],
        out_shape=[shp, shp],
        compiler_params=_cparams("arbitrary", "arbitrary", "arbitrary"),
        name=f"attn_d{dilation}",
    )(q, k, k, v, v)


def _layer_norm(y, g, b):
    mu = jnp.mean(y, axis=-1, keepdims=True)
    yc = y - mu
    var = jnp.mean(yc * yc, axis=-1, keepdims=True)
    return yc * lax.rsqrt(var + LN_EPS) * g + b


def _outproj_kernel(dils, *refs):
    npat = len(dils)
    o_refs = refs[0:npat]
    lse_refs = refs[npat:2 * npat]
    rw_ref, x_ref, wo_ref, ln_ref, y_ref, yb_ref, scr = refs[2 * npat:]
    tm, aw = rw_ref.shape

    def token_major(ref, d, slot):
        if d == 1:
            return ref[...]
        nslab = aw // LANES
        for res in range(d):
            for j in range(nslab):
                scr[slot * nslab + j, pl.ds(res, tm // d, stride=d), :] = (
                    ref[:, res * aw + j * LANES:res * aw + (j + 1) * LANES])
        return jnp.concatenate([scr[slot * nslab + j] for j in range(nslab)], axis=1)

    lses = [token_major(r, d, 2 * i) for i, (r, d) in enumerate(zip(lse_refs, dils))]
    outs = [token_major(r, d, 2 * i + 1) for i, (r, d) in enumerate(zip(o_refs, dils))]
    m = functools.reduce(jnp.maximum, lses)
    es = [jnp.exp(z - m) for z in lses]
    den = functools.reduce(lambda a, b: a + b, es)
    attn = functools.reduce(lambda a, b: a + b, [(e / den) * o for e, o in zip(es, outs)])
    mix = _dot(attn.astype(BF16), wo_ref[0:aw, :]) + _dot(rw_ref[...].astype(BF16), wo_ref[aw:, :])
    y = _layer_norm(ALPHA * x_ref[...] + mix, ln_ref[0:1, :], ln_ref[1:2, :])
    y_ref[...] = y
    yb_ref[...] = y.astype(yb_ref.dtype)


def _outproj(os_, lses, dils, rw, x2, w_out, ln_g, ln_b, tm):
    t, d = x2.shape
    aw = rw.shape[1]
    ln = jnp.concatenate([ln_g[None], ln_b[None], jnp.zeros((SUBLANES - 2, d), F32)], axis=0)
    wo = w_out.astype(BF16)
    tok = lambda w: pl.BlockSpec((tm, w), lambda i: (i, 0))
    view = [pl.BlockSpec((tm // dl, dl * aw), lambda i: (i, 0)) for dl in dils]
    const = lambda a: pl.BlockSpec(a.shape, lambda i: (0, 0))
    return pl.pallas_call(
        functools.partial(_outproj_kernel, dils),
        grid=(t // tm,),
        in_specs=view + view + [tok(aw), tok(d), const(wo), const(ln)],
        out_specs=[tok(d), tok(d)],
        out_shape=[jax.ShapeDtypeStruct((t, d), F32), jax.ShapeDtypeStruct((t, d), BF16)],
        scratch_shapes=[pltpu.VMEM((2 * len(dils) * aw // LANES, tm, LANES), F32)],
        compiler_params=_cparams("arbitrary"),
        name="outproj",
    )(*os_, *lses, rw, x2, wo, ln)


def _lanes(col_rep, n):
    return jnp.concatenate([col_rep] * (n // LANES), axis=1)


MOE_TILE = 512
ROW_CHUNK = 8
SORT_ROWS = 256
COMBINE_ROWS = 512
EXPERT_ROWS = 256


def _tile_rows(ne):
    raw = MOE_TILE * TOP_K + ne * (ROW_CHUNK - 1)
    unit = max(SORT_ROWS, COMBINE_ROWS)
    return -(-raw // unit) * unit


def _route_kernel(x_ref, rwt_ref, bias_ref, tri_ref, ones_ref, low_ref, dloc_ref, gate_ref, tab_ref):
    ne = rwt_ref.shape[0]
    tm = x_ref.shape[0]
    gsz = ne // N_GROUPS
    ninf = -jnp.inf

    aff = _sigmoid(_dotf(rwt_ref[...], x_ref[...], nt=True))
    sel = aff + _lanes(bias_ref[...], tm)

    sel3 = sel.reshape(N_GROUPS, gsz, tm)
    rid = lax.broadcasted_iota(I32, sel3.shape, 1).astype(F32)
    m1 = jnp.max(sel3, axis=1, keepdims=True)
    i1 = jnp.min(jnp.where(sel3 == m1, rid, float(gsz)), axis=1, keepdims=True)
    m2 = jnp.max(jnp.where(rid == i1, ninf, sel3), axis=1, keepdims=True)
    gsc = (m1 + m2).reshape(N_GROUPS, tm)
    gid = lax.broadcasted_iota(I32, gsc.shape, 0).astype(F32)
    keep = jnp.zeros(gsc.shape, F32)
    for _ in range(TOPK_GROUPS):
        gm = jnp.max(gsc, axis=0, keepdims=True)
        gi = jnp.min(jnp.where(gsc == gm, gid, float(N_GROUPS)), axis=0, keepdims=True)
        hit = gid == gi
        keep = jnp.where(hit, 1.0, keep)
        gsc = jnp.where(hit, ninf, gsc)
    cand = jnp.where(keep.reshape(N_GROUPS, 1, tm) > 0.0, sel3, ninf).reshape(ne, tm)

    eid = lax.broadcasted_iota(I32, (ne, tm), 0).astype(F32)
    hits, graw = [], []
    for _ in range(TOP_K):
        m = jnp.max(cand, axis=0, keepdims=True)
        ij = jnp.min(jnp.where(cand == m, eid, float(ne)), axis=0, keepdims=True)
        hit = eid == ij
        hits.append(hit)
        graw.append(jnp.sum(jnp.where(hit, aff, 0.0), axis=0, keepdims=True))
        cand = jnp.where(hit, ninf, cand)
    gsum = functools.reduce(lambda a, b: a + b, graw)
    gate_ref[...] = jnp.concatenate([g / gsum * ROUTED_SCALE for g in graw], axis=0)

    onehot = functools.reduce(lambda a, b: a + b, [h.astype(F32) for h in hits]).astype(BF16)
    before = _dot(onehot, tri_ref[...])
    count = _dot(onehot, ones_ref[...])
    padded = jnp.ceil(count * (1.0 / ROW_CHUNK)) * ROW_CHUNK
    start = _dot3(low_ref[...], padded)
    row = _lanes(start, tm) + before
    dloc_ref[...] = jnp.concatenate(
        [jnp.sum(jnp.where(h, row, 0.0), axis=0, keepdims=True) for h in hits], axis=0).astype(I32)
    lane = lax.broadcasted_iota(I32, padded.shape, 1)
    both = jnp.where(lane == 0, padded, jnp.where(lane == 1, start, 0.0))
    tab_ref[...] = both.T[0:SUBLANES, :].astype(I32)


def _route(x1, router_w, router_bias):
    t, d = x1.shape
    ne = router_w.shape[1]
    tm = MOE_TILE
    rwt = router_w.T
    bias = jnp.broadcast_to(router_bias[:, None], (ne, LANES))
    pos = jnp.arange(tm)
    tri = (pos[:, None] < pos[None, :]).astype(BF16)
    ones = jnp.ones((tm, LANES), BF16)
    eid = jnp.arange(ne)
    low = (eid[:, None] > eid[None, :]).astype(BF16)
    const = lambda a: pl.BlockSpec(a.shape, lambda i: (0, 0))
    tokt = pl.BlockSpec((TOP_K, tm), lambda i: (0, i))
    return pl.pallas_call(
        _route_kernel,
        grid=(t // tm,),
        in_specs=[pl.BlockSpec((tm, d), lambda i: (i, 0)), const(rwt), const(bias), const(tri), const(ones),
                  const(low)],
        out_specs=[tokt, tokt, pl.BlockSpec((None, SUBLANES, ne), lambda i: (i, 0, 0))],
        out_shape=[jax.ShapeDtypeStruct((TOP_K, t), I32), jax.ShapeDtypeStruct((TOP_K, t), F32),
                   jax.ShapeDtypeStruct((t // tm, SUBLANES, ne), I32)],
        compiler_params=_cparams("arbitrary"),
        name="route",
    )(x1, rwt, bias, tri, ones, low)


def _pack_bf16_pairs(v, exact=False):
    half = v.shape[1] // 2
    u = pltpu.bitcast(v if exact else v.astype(BF16).astype(F32), U32)
    return (u[:, 0:half] >> 16) | (u[:, half:] & jnp.uint32(0xFFFF0000))


def _unpack_bf16_pairs(w):
    lo = pltpu.bitcast(w << 16, F32)
    hi = pltpu.bitcast(w & jnp.uint32(0xFFFF0000), F32)
    return jnp.concatenate([lo, hi], axis=1)


def _dot3l(a, b_bf16):
    a1 = a.astype(BF16)
    r1 = a - a1.astype(F32)
    a2 = r1.astype(BF16)
    a3 = (r1 - a2.astype(F32)).astype(BF16)
    return _dot(a1, b_bf16) + (_dot(a2, b_bf16) + _dot(a3, b_bf16))


def _plan_kernel(nblk, cnt_ref, low_ref, upper_ref, gstart_ref, meta_ref, emeta_ref):
    c = cnt_ref[...].astype(F32)
    nt, ne = c.shape
    earlier = _dot3(low_ref[...], c)
    total = jnp.sum(c, axis=0, keepdims=True)
    padded = jnp.ceil(total * (1.0 / EXPERT_ROWS)) * EXPERT_ROWS
    pad_end = _dot3l(jnp.broadcast_to(padded, (SUBLANES, ne)), upper_ref[...])
    pad_start = (pad_end - padded)[0:1, :]
    gstart_ref[...] = (pad_start + earlier).astype(I32)
    emeta_ref[...] = jnp.concatenate(
        [pad_start, padded * (1.0 / EXPERT_ROWS), jnp.zeros((SUBLANES - 2, ne), F32)], axis=0).astype(I32)
    end_col = jnp.broadcast_to(pad_end.T[:, 0:1], (ne, nblk))
    blk_start = (lax.broadcasted_iota(I32, (ne, nblk), 1) * EXPERT_ROWS).astype(F32)
    owner = jnp.minimum(jnp.sum((end_col <= blk_start).astype(F32), axis=0, keepdims=True), float(ne - 1))
    used = jnp.max(end_col, axis=0, keepdims=True) * (1.0 / EXPERT_ROWS)
    meta_ref[...] = jnp.concatenate([owner, used, jnp.zeros((SUBLANES - 2, nblk), F32)], axis=0).astype(I32)


def _plan(cnt, nblk_pad):
    nt, ne = cnt.shape
    ti = jnp.arange(nt)
    low = (ti[:, None] > ti[None, :]).astype(BF16)
    ei = jnp.arange(ne)
    upper = (ei[:, None] <= ei[None, :]).astype(BF16)
    return pl.pallas_call(
        functools.partial(_plan_kernel, nblk_pad),
        out_shape=[jax.ShapeDtypeStruct((nt, ne), I32), jax.ShapeDtypeStruct((SUBLANES, nblk_pad), I32),
                   jax.ShapeDtypeStruct((SUBLANES, ne), I32)],
        compiler_params=pltpu.CompilerParams(vmem_limit_bytes=VMEM_LIMIT_BYTES),
        name="plan",
    )(cnt, low, upper)


def _group_copies(cnt_ref, loc_ref, gstart_ref, tile, ne, local, remote, sem, to_remote):
    def one(e, carry):
        n = pl.multiple_of(cnt_ref[tile * ne + e], ROW_CHUNK)

        @pl.when(n > 0)
        def _():
            lo = pl.multiple_of(loc_ref[tile * ne + e], ROW_CHUNK)
            go = pl.multiple_of(gstart_ref[tile * ne + e], ROW_CHUNK)
            a, b = local.at[pl.ds(lo, n), :], remote.at[pl.ds(go, n), :]
            (pltpu.make_async_copy(a, b, sem) if to_remote else pltpu.make_async_copy(b, a, sem)).start()
        return carry

    lax.fori_loop(0, ne, one, 0, unroll=8)


def _group_wait(rows, local, remote, sem, to_remote):
    n = pl.multiple_of(rows, ROW_CHUNK)

    @pl.when(n > 0)
    def _():
        a, b = local.at[pl.ds(0, n), :], remote.at[pl.ds(0, n), :]
        (pltpu.make_async_copy(a, b, sem) if to_remote else pltpu.make_async_copy(b, a, sem)).wait()


def _dispatch_kernel(ne, cnt_ref, loc_ref, gstart_ref, owner_ref, used_ref, urows_ref, dloc_ref, x_ref, xs_ref,
                     buf_ref, zero_ref, sems, zsem):
    step = pl.program_id(0)
    nsteps = pl.num_programs(0)
    slot = step % 2
    tm = x_ref.shape[0]
    nblk = owner_ref.shape[0]
    used = used_ref[0]

    @pl.when(step == 0)
    def _():
        zero_ref[...] = jnp.zeros_like(zero_ref)

        def block_copy(i):
            return pltpu.make_async_copy(zero_ref, xs_ref.at[pl.ds(i * EXPERT_ROWS, EXPERT_ROWS), :], zsem)

        def is_last(i):
            return (i == used - 1) | (owner_ref[jnp.minimum(i + 1, nblk - 1)] != owner_ref[i])

        def start(i, carry):
            @pl.when(is_last(i))
            def _():
                block_copy(i).start()
            return carry

        def wait(i, carry):
            @pl.when(is_last(i))
            def _():
                block_copy(i).wait()
            return carry

        lax.fori_loop(0, used, start, 0)
        lax.fori_loop(0, used, wait, 0)

    x = x_ref[...]
    dl = dloc_ref[...].astype(I16)
    riota = lax.broadcasted_iota(I32, (SORT_ROWS, tm), 0).astype(I16)
    one = jnp.ones((SORT_ROWS, tm), BF16)

    def chunk(c, carry):
        r0 = pl.multiple_of(c * SORT_ROWS, SORT_ROWS)
        r = riota + r0.astype(I16)
        p = jnp.zeros((SORT_ROWS, tm), BF16)
        for j in range(TOP_K):
            p = jnp.where(r == dl[j:j + 1, :], one, p)
        buf_ref[slot, pl.ds(r0, SORT_ROWS), :] = _pack_bf16_pairs(_dot(p, x), exact=True)
        return carry

    lax.fori_loop(0, (urows_ref[step] + SORT_ROWS - 1) // SORT_ROWS, chunk, 0)

    copies = functools.partial(_group_copies, cnt_ref, loc_ref, gstart_ref)
    copies(step, ne, buf_ref.at[slot], xs_ref, sems.at[slot], True)

    @pl.when(step > 0)
    def _():
        _group_wait(urows_ref[jnp.maximum(step - 1, 0)], buf_ref.at[1 - slot], xs_ref, sems.at[1 - slot], True)

    @pl.when(step == nsteps - 1)
    def _():
        _group_wait(urows_ref[step], buf_ref.at[slot], xs_ref, sems.at[slot], True)


def _dispatch(x1b, dloc, cnt, loc, gstart, owner, used, used_rows, nrows):
    t, d = x1b.shape
    nt, ne = cnt.shape
    tm = MOE_TILE
    rt = _tile_rows(ne)
    grid_spec = pltpu.PrefetchScalarGridSpec(
        num_scalar_prefetch=6,
        grid=(nt,),
        in_specs=[pl.BlockSpec((TOP_K, tm), lambda i, *_: (0, i)), pl.BlockSpec((tm, d), lambda i, *_: (i, 0))],
        out_specs=pl.BlockSpec(memory_space=pl.ANY),
        scratch_shapes=[pltpu.VMEM((2, rt, d // 2), U32), pltpu.VMEM((EXPERT_ROWS, d // 2), U32),
                        pltpu.SemaphoreType.DMA((2,)), pltpu.SemaphoreType.DMA(())],
    )
    return pl.pallas_call(
        functools.partial(_dispatch_kernel, ne),
        grid_spec=grid_spec,
        out_shape=jax.ShapeDtypeStruct((nrows, d // 2), U32),
        compiler_params=_cparams("arbitrary"),
        name="dispatch",
    )(cnt.reshape(-1), loc.reshape(-1), gstart.reshape(-1), owner, used, used_rows, dloc, x1b)


EXPERT_PIECE = 1024


def _expert_kernel(first_ref, nblk_ref, xs_ref, wg_ref, wu_ref, wd_ref, ys_ref,
                   xbuf, ybuf, wgb_ref, wub_ref, wdb_ref, sem_in, sem_out, done_ref, pend_ref):
    e = pl.program_id(0)
    last = pl.num_programs(0) - 1
    per_piece = EXPERT_PIECE // EXPERT_ROWS

    def pieces(ex):
        return (nblk_ref[ex] + per_piece - 1) // per_piece

    def span(ex, s):
        n = jnp.minimum(EXPERT_PIECE, nblk_ref[ex] * EXPERT_ROWS - s * EXPERT_PIECE)
        n = pl.multiple_of(n, EXPERT_ROWS)
        return pl.ds(pl.multiple_of(first_ref[ex] + s * EXPERT_PIECE, EXPERT_ROWS), n), pl.ds(0, n)

    def in_copy(ex, s, slot):
        far, near = span(ex, s)
        return pltpu.make_async_copy(xs_ref.at[far, :], xbuf.at[slot, near, :], sem_in.at[slot])

    def out_copy(ex, s, slot):
        far, near = span(ex, s)
        return pltpu.make_async_copy(ybuf.at[slot, near, :], ys_ref.at[far, :], sem_out.at[slot])

    @pl.when(e == 0)
    def _():
        done_ref[0] = 0
        for slot in range(2):
            pend_ref[2 * slot] = -1

        @pl.when(nblk_ref[0] > 0)
        def _():
            in_copy(0, 0, 0).start()

    nb = nblk_ref[e]
    ns = pieces(e)
    g0 = done_ref[0]
    nxt = jnp.minimum(e + 1, last)
    has_next = (e < last) & (nblk_ref[nxt] > 0)

    def release(slot):
        pe = pend_ref[2 * slot]

        @pl.when(pe >= 0)
        def _():
            out_copy(pe, pend_ref[2 * slot + 1], slot).wait()

    @pl.when(nb > 0)
    def _():
        wgb_ref[...] = wg_ref[...].astype(BF16)
        wub_ref[...] = wu_ref[...].astype(BF16)
        wdb_ref[...] = wd_ref[...].astype(BF16)

        def piece(s, carry):
            slot = (g0 + s) % 2

            @pl.when(s + 1 < ns)
            def _():
                in_copy(e, s + 1, 1 - slot).start()

            @pl.when((s + 1 == ns) & has_next)
            def _():
                in_copy(nxt, 0, 1 - slot).start()

            in_copy(e, s, slot).wait()
            release(slot)

            def block(b, c2):
                rows = pl.ds(pl.multiple_of(b * EXPERT_ROWS, EXPERT_ROWS), EXPERT_ROWS)
                xb = _unpack_bf16_pairs(xbuf[slot, rows, :]).astype(BF16)
                hg = _dot(xb, wgb_ref[...])
                hb = hg * _sigmoid(hg) * _dot(xb, wub_ref[...])
                ybuf[slot, rows, :] = _pack_bf16_pairs(_dot(hb.astype(BF16), wdb_ref[...]))
                return c2

            lax.fori_loop(0, jnp.minimum(per_piece, nb - s * per_piece), block, 0)
            out_copy(e, s, slot).start()
            pend_ref[2 * slot] = e
            pend_ref[2 * slot + 1] = s
            return carry

        lax.fori_loop(0, ns, piece, 0)

    @pl.when((nb == 0) & has_next)
    def _():
        in_copy(nxt, 0, g0 % 2).start()

    @pl.when(e == last)
    def _():
        for slot in range(2):
            release(slot)

    done_ref[0] = g0 + ns


def _experts(xs, first_row, nblocks, e_gate, e_up, e_down):
    nrows, dh = xs.shape
    ne, d, ff = e_gate.shape
    wsel = lambda e, *_: (e, 0, 0)
    grid_spec = pltpu.PrefetchScalarGridSpec(
        num_scalar_prefetch=2,
        grid=(ne,),
        in_specs=[pl.BlockSpec(memory_space=pl.ANY),
                  pl.BlockSpec((None, d, ff), wsel), pl.BlockSpec((None, d, ff), wsel),
                  pl.BlockSpec((None, ff, d), wsel)],
        out_specs=pl.BlockSpec(memory_space=pl.ANY),
        scratch_shapes=[pltpu.VMEM((2, EXPERT_PIECE, dh), U32), pltpu.VMEM((2, EXPERT_PIECE, dh), U32),
                        pltpu.VMEM((d, ff), BF16), pltpu.VMEM((d, ff), BF16), pltpu.VMEM((ff, d), BF16),
                        pltpu.SemaphoreType.DMA((2,)), pltpu.SemaphoreType.DMA((2,)),
                        pltpu.SMEM((1,), I32), pltpu.SMEM((4,), I32)],
    )
    return pl.pallas_call(
        _expert_kernel,
        grid_spec=grid_spec,
        out_shape=jax.ShapeDtypeStruct((nrows, dh), U32),
        compiler_params=_cparams("arbitrary"),
        name="experts",
    )(first_row, nblocks, xs, e_gate, e_up, e_down)


def _final_kernel(ne, cnt_ref, loc_ref, gstart_ref, used_ref, dloc_ref, gate_ref, x_ref, xb_ref, ys_ref,
                  sg_ref, su_ref, sd_ref, ln_ref, o_ref, buf_ref, acc_ref, sems):
    step = pl.program_id(0)
    nsteps = pl.num_programs(0)
    slot = step % 2
    tm = x_ref.shape[0]
    used = used_ref[step]
    copies = functools.partial(_group_copies, cnt_ref, loc_ref, gstart_ref)

    @pl.when(step == 0)
    def _():
        copies(step, ne, buf_ref.at[0], ys_ref, sems.at[0], False)

    @pl.when(step + 1 < nsteps)
    def _():
        copies(step + 1, ne, buf_ref.at[1 - slot], ys_ref, sems.at[1 - slot], False)

    xb = xb_ref[...]
    hg = _dot(xb, sg_ref[...])
    hs = hg * _sigmoid(hg) * _dot(xb, su_ref[...])
    acc_ref[...] = _dot(hs.astype(BF16), sd_ref[...])

    _group_wait(used, buf_ref.at[slot], ys_ref, sems.at[slot], False)

    dl = dloc_ref[...].astype(I16)
    gt = gate_ref[...].astype(BF16)
    rows16 = lax.broadcasted_iota(I32, (COMBINE_ROWS, tm), 0).astype(I16)
    gts = [jnp.broadcast_to(gt[j:j + 1, :], (COMBINE_ROWS, tm)) for j in range(TOP_K)]
    riota = lax.broadcasted_iota(I32, (COMBINE_ROWS, buf_ref.shape[2]), 0)

    def chunk(c, carry):
        r0 = pl.multiple_of(c * COMBINE_ROWS, COMBINE_ROWS)
        r = rows16 + r0.astype(I16)
        g = jnp.zeros((COMBINE_ROWS, tm), BF16)
        for j in range(TOP_K):
            g = jnp.where(r == dl[j:j + 1, :], gts[j], g)
        w = buf_ref[slot, pl.ds(r0, COMBINE_ROWS), :]
        w = jnp.where(riota + r0 < used, w, jnp.zeros_like(w))
        y = _unpack_bf16_pairs(w).astype(BF16)
        acc_ref[...] += lax.dot_general(g, y, (((0,), (0,)), ((), ())), preferred_element_type=F32)
        return carry

    lax.fori_loop(0, (used + COMBINE_ROWS - 1) // COMBINE_ROWS, chunk, 0)
    o_ref[...] = _layer_norm(ALPHA * x_ref[...] + acc_ref[...], ln_ref[0:1, :], ln_ref[1:2, :])


def _final(x1, x1b, dloc, gate, cnt, loc, gstart, used_rows, ys, s_gate, s_up, s_down, ln_g, ln_b):
    t, d = x1.shape
    nt, ne = cnt.shape
    tm = MOE_TILE
    rt = _tile_rows(ne)
    ln = jnp.concatenate([ln_g[None], ln_b[None], jnp.zeros((SUBLANES - 2, d), F32)], axis=0)
    sg, su, sd = s_gate.astype(BF16), s_up.astype(BF16), s_down.astype(BF16)
    tok = pl.BlockSpec((tm, d), lambda i, *_: (i, 0))
    tokt = pl.BlockSpec((TOP_K, tm), lambda i, *_: (0, i))
    const = lambda a: pl.BlockSpec(a.shape, lambda i, *_: (0, 0))
    grid_spec = pltpu.PrefetchScalarGridSpec(
        num_scalar_prefetch=4,
        grid=(nt,),
        in_specs=[tokt, tokt, tok, tok, pl.BlockSpec(memory_space=pl.ANY), const(sg), const(su), const(sd), const(ln)],
        out_specs=tok,
        scratch_shapes=[pltpu.VMEM((2, rt, d // 2), U32), pltpu.VMEM((tm, d), F32), pltpu.SemaphoreType.DMA((2,))],
    )
    return pl.pallas_call(
        functools.partial(_final_kernel, ne),
        grid_spec=grid_spec,
        out_shape=jax.ShapeDtypeStruct((t, d), F32),
        compiler_params=_cparams("arbitrary"),
        name="final",
    )(cnt.reshape(-1), loc.reshape(-1), gstart.reshape(-1), used_rows, dloc, gate, x1, x1b, ys, sg, su, sd, ln)


def _moe(x1, x1b, router_w, router_bias, e_gate, e_up, e_down, s_gate, s_up, s_down, ln_g, ln_b):
    t = x1.shape[0]
    ne = router_w.shape[1]
    nt = t // MOE_TILE
    nblk = -(-(nt * _tile_rows(ne)) // EXPERT_ROWS) + ne
    nblk_pad = -(-nblk // LANES) * LANES
    dloc, gate, tab = _route(x1, router_w, router_bias)
    cnt, loc = tab[:, 0, :], tab[:, 1, :]
    used_rows = loc[:, ne - 1] + cnt[:, ne - 1]
    gstart, meta, emeta = _plan(cnt, nblk_pad)
    owner, used = meta[0], meta[1, 0:1]
    xs = _dispatch(x1b, dloc, cnt, loc, gstart, owner, used, used_rows, nblk * EXPERT_ROWS)
    ys = _experts(xs, emeta[0], emeta[1], e_gate, e_up, e_down)
    return _final(x1, x1b, dloc, gate, cnt, loc, gstart, used_rows, ys, s_gate, s_up, s_down, ln_g, ln_b)


def kernel(x, w_in, mu_shift, w0, w_decay_up, a0, w_aaa_up, w_gate_up, k_k, k_a, r_k, lnx_g, lnx_b, w_out,
           ln1_g, ln1_b, router_w, router_bias, e_gate, e_up, e_down, s_gate, s_up, s_down, ln2_g, ln2_b):
    bsz, seq, d = x.shape
    x2 = x.reshape(bsz * seq, d)
    qkvs, feats = _inproj(x2, seq, w_in[0], mu_shift[0], w0[0], w_decay_up[0], a0[0], w_aaa_up[0], w_gate_up[0],
                          k_k[0], k_a[0], r_k[0], tm=256)
    rw = _rwkv(feats, bsz, seq, lnx_g[0], lnx_b[0], nbat=4)
    x1, x1b = _mixer_tail(qkvs, rw, x2, bsz, seq, w_out[0], ln1_g[0], ln1_b[0])
    out = _moe(x1, x1b, router_w[0], router_bias[0], e_gate[0], e_up[0], e_down[0], s_gate[0], s_up[0], s_down[0],
               ln2_g[0], ln2_b[0])
    return out.reshape(bsz, seq, d)


def _mixer_tail(qkvs, rw, x2, bsz, seq, w_out, ln_g, ln_b):
    res = [_attention(q, k, v, bsz, seq, window, dilation)
           for (q, k, v), (window, dilation) in zip(qkvs, ATTN_PATTERNS)]
    dils = tuple(dl for _, dl in ATTN_PATTERNS)
    return _outproj([o for o, _ in res], [l for _, l in res], dils, rw, x2, w_out, ln_g, ln_b, tm=512)
```

```python
import functools

import jax
import jax.numpy as jnp
from jax import lax
from jax.experimental import pallas as pl
from jax.experimental.pallas import tpu as pltpu

F32 = jnp.float32
BF16 = jnp.bfloat16
I32 = jnp.int32
I16 = jnp.int16
U32 = jnp.uint32

LANES = 128
SUBLANES = 8
VMEM_LIMIT_BYTES = 56 * 1024 * 1024

HEAD_DIM = 64
ATTN_HEADS = 8
RWKV_HEADS = 8
ATTN_WIDTH = ATTN_HEADS * HEAD_DIM
RWKV_WIDTH = RWKV_HEADS * HEAD_DIM
ATTN_PATTERNS = ((128, 1), (512, 4), (2048, 16))
Q_BLOCK = 128
ROPE_THETA = 10000.0
DECAY_LORA = 64
AAA_LORA = 64
GATE_LORA = 160
GN_EPS = 64e-5
LN_EPS = 1e-5
TOP_K = 8
N_GROUPS = 8
TOPK_GROUPS = 4
ROUTED_SCALE = 2.5
DEPTH = 1
ALPHA = (2.0 * DEPTH) ** 0.25

RWKV_CHUNK = 64
PAIR = 2 * HEAD_DIM
LORA_PAD = 2 * LANES


def _cparams(*sem):
    return pltpu.CompilerParams(dimension_semantics=sem, vmem_limit_bytes=VMEM_LIMIT_BYTES)


def _split_bf16(a):
    hi = a.astype(BF16)
    lo = (a - hi.astype(F32)).astype(BF16)
    return hi, lo


def _dot(a, b):
    return jnp.dot(a, b, preferred_element_type=F32)


def _dot_nt(a, b):
    return lax.dot_general(a, b, (((1,), (1,)), ((), ())), preferred_element_type=F32)


def _dot_hl(a_f32, b_bf16):
    hi, lo = _split_bf16(a_f32)
    return _dot(hi, b_bf16) + _dot(lo, b_bf16)


def _softplus(z):
    return jnp.maximum(z, 0.0) + jnp.log(1.0 + jnp.exp(-jnp.abs(z)))


def _sigmoid(z):
    return 1.0 / (1.0 + jnp.exp(-z))


def _inproj_kernel(dils, x_ref, w_ref, cos_ref, sin_ref, mu_ref, wd_ref, wa_ref, wg_ref, vec_ref, bd_ref, *refs):
    nq = 3 * len(dils)
    qkv_refs = refs[0:nq]
    r_ref, ld_ref, kp_ref, vv_ref, kk_ref, b_ref, g_ref, bon_ref, carry_ref, qkv_scr = refs[nq:]
    s = pl.program_id(1)
    tm = x_ref.shape[0]
    aw = ATTN_WIDTH
    rw = RWKV_WIDTH

    @pl.when(s == 0)
    def _():
        carry_ref[...] = jnp.zeros_like(carry_ref)

    h = _dot(x_ref[...].astype(BF16), w_ref[...])
    reps = aw // cos_ref.shape[1]
    cos = jnp.concatenate([cos_ref[...]] * reps, axis=1)
    sin = jnp.concatenate([sin_ref[...]] * reps, axis=1)
    qkv = [(h[:, 0:aw] * cos + h[:, aw:2 * aw] * sin) * (HEAD_DIM ** -0.5),
           h[:, 2 * aw:3 * aw] * cos + h[:, 3 * aw:4 * aw] * sin,
           h[:, 4 * aw:5 * aw]]
    nslab = aw // LANES
    for i in range(3):
        for j in range(nslab):
            qkv_scr[i * nslab + j] = qkv[i][:, j * LANES:(j + 1) * LANES]
    for di, d in enumerate(dils):
        for i in range(3):
            o_ref = qkv_refs[3 * di + i]
            if d == 1:
                o_ref[...] = qkv[i].astype(o_ref.dtype)
                continue
            for res in range(d):
                for j in range(nslab):
                    rows = qkv_scr[i * nslab + j, pl.ds(res, tm // d, stride=d), :]
                    o_ref[:, res * aw + j * LANES:res * aw + (j + 1) * LANES] = rows.astype(o_ref.dtype)

    f = h[:, 5 * aw:]
    rows = lax.broadcasted_iota(I32, f.shape, 0)
    prev = jnp.where(rows == 0, carry_ref[SUBLANES - 1:SUBLANES, :], pltpu.roll(f, 1, axis=0))
    carry_ref[...] = f[tm - SUBLANES:tm, :]
    f = f + (prev - f) * mu_ref[...]

    r = f[:, 0:rw]
    k = f[:, rw:2 * rw]
    v = f[:, 2 * rw:3 * rw]
    la = f[:, 3 * rw:3 * rw + LANES]
    gl = f[:, 3 * rw + LANES:]
    w0, a0, k_k, k_a, r_k = (vec_ref[i:i + 1, :] for i in range(5))

    z = w0 + _dot(jnp.tanh(la).astype(BF16), wd_ref[...])
    w = -_softplus(-z) - 0.5
    ld_ref[...] = -jnp.exp(w)
    a = _sigmoid(a0 + _dot(la.astype(BF16), wa_ref[...]))
    g_ref[...] = _dot(_sigmoid(gl).astype(BF16), wg_ref[...])

    bd = bd_ref[...]
    kk = k * k_k
    nrm = jnp.sqrt(_dot_hl(kk * kk, bd))
    kk = kk / jnp.maximum(nrm, 1e-12)
    kp = k * (1.0 + (a - 1.0) * k_a)
    r_ref[...] = r
    kp_ref[...] = kp
    vv_ref[...] = v
    kk_ref[...] = kk
    b_ref[...] = kk * a
    bon_ref[...] = _dot_hl(r * kp * r_k, bd) * v


def _rot_half_cols(w):
    d, n = w.shape
    w4 = w.reshape(d, n // HEAD_DIM, 2, HEAD_DIM // 2)
    return jnp.stack([-w4[:, :, 1, :], w4[:, :, 0, :]], axis=2).reshape(d, n)


def _inproj(x2, seq, w_in, mu_shift, w0, w_decay_up, a0, w_aaa_up, w_gate_up, k_k, k_a, r_k, tm):
    t, d = x2.shape
    aw, rw = ATTN_WIDTH, RWKV_WIDTH
    wq, wk, wv = w_in[:, 0:aw], w_in[:, aw:2 * aw], w_in[:, 2 * aw:3 * aw]
    wf = w_in[:, 3 * aw:]
    gpad = LORA_PAD - GATE_LORA
    w_all = jnp.concatenate(
        [wq, _rot_half_cols(wq), wk, _rot_half_cols(wk), wv, wf, jnp.zeros((d, gpad), F32)], axis=1).astype(BF16)
    mu = jnp.concatenate([mu_shift, jnp.zeros((gpad,), F32)])[None, :]
    nf = mu.shape[1]
    wd = jnp.concatenate([w_decay_up, jnp.zeros((AAA_LORA, rw), F32)], axis=0).astype(BF16)
    wa = jnp.concatenate([jnp.zeros((DECAY_LORA, rw), F32), w_aaa_up], axis=0).astype(BF16)
    wg = jnp.concatenate([w_gate_up, jnp.zeros((gpad, rw), F32)], axis=0).astype(BF16)
    vec = jnp.stack([w0, a0, k_k, k_a, r_k.reshape(-1), w0 * 0, w0 * 0, w0 * 0])
    head = jnp.arange(rw) // HEAD_DIM
    bd = (head[:, None] == head[None, :]).astype(BF16)
    half = HEAD_DIM // 2
    inv_freq = ROPE_THETA ** (-jnp.arange(half, dtype=F32) * 2.0 / HEAD_DIM)
    ang = jnp.arange(seq, dtype=F32)[:, None] * inv_freq[None, :]
    cos = jnp.tile(jnp.cos(ang), (1, LANES // half))
    sin = jnp.tile(jnp.sin(ang), (1, LANES // half))

    nst = seq // tm
    tok = lambda b, s: (b * nst + s, 0)
    const = lambda b, s: (0, 0)
    full = lambda a: pl.BlockSpec(a.shape, const)
    dils = tuple(dl for _, dl in ATTN_PATTERNS)
    qkv_specs, qkv_shapes = [], []
    for dl in dils:
        qkv_specs += [pl.BlockSpec((tm // dl, dl * aw), tok)] * 3
        qkv_shapes += [jax.ShapeDtypeStruct((t // dl, dl * aw), BF16)] * 3
    out_f = jax.ShapeDtypeStruct((t, rw), F32)
    outs = pl.pallas_call(
        functools.partial(_inproj_kernel, dils),
        grid=(t // seq, nst),
        in_specs=[pl.BlockSpec((tm, d), tok), full(w_all),
                  pl.BlockSpec((tm, LANES), lambda b, s: (s, 0)), pl.BlockSpec((tm, LANES), lambda b, s: (s, 0)),
                  full(mu), full(wd), full(wa), full(wg), full(vec), full(bd)],
        out_specs=qkv_specs + [pl.BlockSpec((tm, rw), tok)] * 8,
        out_shape=qkv_shapes + [out_f] * 8,
        scratch_shapes=[pltpu.VMEM((SUBLANES, nf), F32), pltpu.VMEM((3 * aw // LANES, tm, LANES), F32)],
        compiler_params=_cparams("arbitrary", "arbitrary"),
        name="inproj",
    )(x2, w_all, cos, sin, mu, wd, wa, wg, vec, bd)
    nq = 3 * len(dils)
    return [outs[3 * i:3 * i + 3] for i in range(len(dils))], outs[nq:]


def _dotf(a, b, nt=False, passes=3):
    dot = _dot_nt if nt else _dot
    if passes == 1:
        return dot(a.astype(BF16), b.astype(BF16))
    ah, al = _split_bf16(a)
    if passes == 2:
        bh = b.astype(BF16)
        return dot(ah, bh) + dot(al, bh)
    bh, bl = _split_bf16(b)
    return dot(ah, bh) + (dot(ah, bl) + dot(al, bh))


RWKV_PASSES = dict(s8=1, inv=1, w1=1, au=1, ry=1, gh=1, yh=1)


def _dot3(a_bf16, b):
    b1 = b.astype(BF16)
    r1 = b - b1.astype(F32)
    b2 = r1.astype(BF16)
    b3 = (r1 - b2.astype(F32)).astype(BF16)
    return _dot(a_bf16, b1) + (_dot(a_bf16, b2) + _dot(a_bf16, b3))


def _rwkv_kernel(r_ref, ld_ref, kp_ref, v_ref, kk_ref, b_ref, g_ref, bon_ref, gn_ref, bd_ref, o_ref, h_ref):
    c = pl.program_id(1)
    nbat, ch, rw = r_ref.shape
    npairs = rw // PAIR

    @pl.when(c == 0)
    def _():
        h_ref[...] = jnp.zeros_like(h_ref)

    ri = lax.broadcasted_iota(I32, (ch, ch), 0)
    ci = lax.broadcasted_iota(I32, (ch, ch), 1)
    tril = (ri >= ci).astype(BF16)
    bd = bd_ref[...]
    row = lax.broadcasted_iota(I32, (ch, PAIR), 0)
    col = lax.broadcasted_iota(I32, (ch, PAIR), 1)
    first = col < HEAD_DIM
    jj = col & (HEAD_DIM - 1)
    strict = jj < row
    incl = jj <= row
    eye = (jj == row).astype(F32)
    row2 = lax.broadcasted_iota(I32, (PAIR, PAIR), 0)
    col2 = lax.broadcasted_iota(I32, (PAIR, PAIR), 1)
    same_head = (row2 < HEAD_DIM) == (col2 < HEAD_DIM)
    diag2 = row2 == col2
    zeros_cp = jnp.zeros((ch, PAIR), F32)

    def bdiag(y):
        return jnp.concatenate([jnp.where(first, y, 0.0), jnp.where(first, 0.0, y)], axis=0)

    ps = RWKV_PASSES
    units = []
    for bi in range(nbat):
        ld = ld_ref[bi]
        cum = _dot3(tril, ld)
        tot = cum[ch - 1:ch, :]
        a_t = -kk_ref[bi] * jnp.exp(cum - ld)
        pinv = jnp.exp(-cum)
        b_t = b_ref[bi] * pinv
        k_t = kp_ref[bi] * pinv
        r_t = r_ref[bi] * jnp.exp(cum)
        pend = jnp.exp(tot - cum)
        b_end = b_ref[bi] * pend
        k_end = kp_ref[bi] * pend
        p_tot = jnp.exp(tot)
        v_all = v_ref[bi]
        for p in range(npairs):
            sl = slice(p * PAIR, (p + 1) * PAIR)
            units.append(dict(h=bi * npairs + p, a=a_t[:, sl], r=r_t[:, sl], b=b_t[:, sl], k=k_t[:, sl],
                              v=v_all[:, sl], be=b_end[:, sl], ke=k_end[:, sl], pt=p_tot[:, sl]))

    for u in units:
        u["s8"] = _dotf(jnp.concatenate([u["a"], u["r"]], axis=0),
                        jnp.concatenate([bdiag(u["b"]), bdiag(u["k"])], axis=0), nt=True, passes=ps["s8"])
    for u in units:
        s8 = u.pop("s8")
        u["l"] = jnp.where(strict, s8[0:ch, 0:PAIR], 0.0)
        u["ak"] = jnp.where(strict, s8[0:ch, PAIR:], 0.0)
        u["mrb"] = jnp.where(incl, s8[ch:, 0:PAIR], 0.0)
        u["mrk"] = jnp.where(incl, s8[ch:, PAIR:], 0.0)
        u["t"] = eye + u["l"]
    for u in units:
        u["lk"] = _dotf(u["l"], bdiag(u["l"]), passes=ps["inv"])
        u["w1"] = _dotf(u["ak"], bdiag(u["v"]), passes=ps["w1"])
    for _ in range(max(ch.bit_length() - 3, 0)):
        for u in units:
            u["both"] = _dotf(jnp.concatenate([u["t"], u["lk"]], axis=0), bdiag(u["lk"]), passes=ps["inv"])
        for u in units:
            both = u.pop("both")
            u["t"] = u["t"] + both[0:ch]
            u["lk"] = both[ch:]
    for u in units:
        u["t"] = u["t"] + _dotf(u["t"], bdiag(u["lk"]), passes=ps["inv"])
    for u in units:
        u["au"] = _dotf(u["t"], jnp.concatenate([bdiag(u["a"]), bdiag(u["w1"])], axis=1), passes=ps["au"])
    for u in units:
        a_hat, u_loc = u["au"][:, 0:PAIR], u["au"][:, PAIR:]
        rhs = jnp.concatenate([
            jnp.concatenate([bdiag(a_hat), bdiag(u_loc)], axis=1),
            jnp.concatenate([jnp.zeros((2 * ch, PAIR), F32), bdiag(u["v"])], axis=1)], axis=0)
        u["ry"] = _dotf(jnp.concatenate([u["mrb"], u["mrk"]], axis=1), rhs, passes=ps["ry"])
        bkt = jnp.concatenate([u["be"], u["ke"]], axis=0).T
        u["gh"] = _dotf(bkt, jnp.concatenate([u["au"], jnp.concatenate([zeros_cp, u["v"]], axis=1)], axis=0),
                        passes=ps["gh"])
    for u in units:
        r_hat = u["r"] + u["ry"][:, 0:PAIR]
        g_mat = jnp.where(same_head, u["gh"][:, 0:PAIR], 0.0) + jnp.where(diag2, u["pt"], 0.0)
        u["yh"] = _dotf(jnp.concatenate([r_hat, g_mat], axis=0), h_ref[u["h"]], passes=ps["yh"])
    for u in units:
        h_ref[u["h"]] = u["yh"][ch:] + jnp.where(same_head, u["gh"][:, PAIR:], 0.0)
        u["y"] = u["yh"][0:ch] + u["ry"][:, PAIR:]

    inv_n = 1.0 / HEAD_DIM
    for bi in range(nbat):
        y = jnp.concatenate([u["y"] for u in units[bi * npairs:(bi + 1) * npairs]], axis=1)
        yc = y - _dot_hl(y, bd) * inv_n
        var = _dot_hl(yc * yc, bd) * inv_n
        yn = yc * lax.rsqrt(var + GN_EPS) * gn_ref[0:1, :] + gn_ref[1:2, :]
        o_ref[bi] = (yn + bon_ref[bi]) * g_ref[bi]


def _rwkv(feats, bsz, seq, lnx_g, lnx_b, nbat):
    t, rw = feats[0].shape
    ch = RWKV_CHUNK
    gn = jnp.concatenate([lnx_g[None], lnx_b[None], jnp.zeros((SUBLANES - 2, rw), F32)], axis=0)
    head = jnp.arange(rw) // HEAD_DIM
    bd = (head[:, None] == head[None, :]).astype(BF16)
    tok = pl.BlockSpec((nbat, ch, rw), lambda bb, c: (bb, c, 0))
    const = lambda a: pl.BlockSpec(a.shape, lambda bb, c: (0, 0))
    out = pl.pallas_call(
        _rwkv_kernel,
        grid=(bsz // nbat, seq // ch),
        in_specs=[tok] * 8 + [const(gn), const(bd)],
        out_specs=tok,
        out_shape=jax.ShapeDtypeStruct((bsz, seq, rw), F32),
        scratch_shapes=[pltpu.VMEM((nbat * (rw // PAIR), PAIR, PAIR), F32)],
        compiler_params=_cparams("arbitrary", "arbitrary"),
        name="rwkv",
    )(*[f.reshape(bsz, seq, rw) for f in feats], gn, bd)
    return out.reshape(t, rw)


NEG_BIG = -1e30


def _attn_kernel(n_back, q_ref, kc_ref, kp_ref, vc_ref, vp_ref, o_ref, lse_ref):
    n = pl.program_id(2)
    qb = q_ref.shape[0]
    qi = lax.broadcasted_iota(I32, (qb, 2 * qb), 0)
    ki = lax.broadcasted_iota(I32, (qb, 2 * qb), 1)
    dist = qb + qi - ki
    has_prev = jnp.where(n > 0, 0, qb)
    valid = (dist >= 0) & (dist <= n_back) & (ki >= has_prev)
    first = lax.broadcasted_iota(I32, (qb, PAIR), 1) < HEAD_DIM
    npairs = q_ref.shape[1] // PAIR
    heads = []
    for p in range(npairs):
        sl = slice(p * PAIR, (p + 1) * PAIR)
        q2 = q_ref[:, sl]
        k2 = jnp.concatenate([kp_ref[:, sl], kc_ref[:, sl]], axis=0)
        for hh in range(2):
            keep = first if hh == 0 else jnp.logical_not(first)
            s = _dot_nt(jnp.where(keep, q2, jnp.zeros_like(q2)), k2)
            heads.append(dict(sl=sl, s=s))
    for h in heads:
        s = jnp.where(valid, h.pop("s"), NEG_BIG)
        m = jnp.max(s, axis=1, keepdims=True)
        pe = jnp.where(valid, jnp.exp(s - m), 0.0)
        l = jnp.sum(pe, axis=1, keepdims=True)
        h["pe"] = pe.astype(BF16)
        h["l"] = l
        h["lse"] = m + jnp.log(l)
    for h in heads:
        sl = h["sl"]
        v2 = jnp.concatenate([vp_ref[:, sl], vc_ref[:, sl]], axis=0)
        h["o"] = _dot(h.pop("pe"), v2) / h["l"]
    for p in range(npairs):
        h0, h1 = heads[2 * p], heads[2 * p + 1]
        o_ref[:, h0["sl"]] = jnp.where(first, h0["o"], h1["o"])
        lse_ref[:, h0["sl"]] = jnp.where(first, h0["lse"], h1["lse"])


def _attention(q, k, v, bsz, seq, window, dilation):
    aw = q.shape[1] // dilation
    nb = seq // dilation // Q_BLOCK
    cur = pl.BlockSpec((Q_BLOCK, aw), lambda b, r, n: (b * nb + n, r))
    prv = pl.BlockSpec((Q_BLOCK, aw), lambda b, r, n: (b * nb + jnp.maximum(n - 1, 0), r))
    shp = jax.ShapeDtypeStruct(q.shape, F32)
    return pl.pallas_call(
        functools.partial(_attn_kernel, window // dilation),
        grid=(bsz, dilation, nb),
        in_specs=[cur, cur, prv, cur, prv],
        out_specs=[cur, cur],
        out_shape=[shp, shp],
        compiler_params=_cparams("arbitrary", "arbitrary", "arbitrary"),
        name=f"attn_d{dilation}",
    )(q, k, k, v, v)


def _layer_norm(y, g, b):
    mu = jnp.mean(y, axis=-1, keepdims=True)
    yc = y - mu
    var = jnp.mean(yc * yc, axis=-1, keepdims=True)
    return yc * lax.rsqrt(var + LN_EPS) * g + b


def _outproj_kernel(dils, *refs):
    npat = len(dils)
    o_refs = refs[0:npat]
    lse_refs = refs[npat:2 * npat]
    rw_ref, x_ref, wo_ref, ln_ref, y_ref, yb_ref, scr = refs[2 * npat:]
    tm, aw = rw_ref.shape

    def token_major(ref, d, slot):
        if d == 1:
            return ref[...]
        nslab = aw // LANES
        for res in range(d):
            for j in range(nslab):
                scr[slot * nslab + j, pl.ds(res, tm // d, stride=d), :] = (
                    ref[:, res * aw + j * LANES:res * aw + (j + 1) * LANES])
        return jnp.concatenate([scr[slot * nslab + j] for j in range(nslab)], axis=1)

    lses = [token_major(r, d, 2 * i) for i, (r, d) in enumerate(zip(lse_refs, dils))]
    outs = [token_major(r, d, 2 * i + 1) for i, (r, d) in enumerate(zip(o_refs, dils))]
    m = functools.reduce(jnp.maximum, lses)
    es = [jnp.exp(z - m) for z in lses]
    den = functools.reduce(lambda a, b: a + b, es)
    attn = functools.reduce(lambda a, b: a + b, [(e / den) * o for e, o in zip(es, outs)])
    mix = _dot(attn.astype(BF16), wo_ref[0:aw, :]) + _dot(rw_ref[...].astype(BF16), wo_ref[aw:, :])
    y = _layer_norm(ALPHA * x_ref[...] + mix, ln_ref[0:1, :], ln_ref[1:2, :])
    y_ref[...] = y
    yb_ref[...] = y.astype(yb_ref.dtype)


def _outproj(os_, lses, dils, rw, x2, w_out, ln_g, ln_b, tm):
    t, d = x2.shape
    aw = rw.shape[1]
    ln = jnp.concatenate([ln_g[None], ln_b[None], jnp.zeros((SUBLANES - 2, d), F32)], axis=0)
    wo = w_out.astype(BF16)
    tok = lambda w: pl.BlockSpec((tm, w), lambda i: (i, 0))
    view = [pl.BlockSpec((tm // dl, dl * aw), lambda i: (i, 0)) for dl in dils]
    const = lambda a: pl.BlockSpec(a.shape, lambda i: (0, 0))
    return pl.pallas_call(
        functools.partial(_outproj_kernel, dils),
        grid=(t // tm,),
        in_specs=view + view + [tok(aw), tok(d), const(wo), const(ln)],
        out_specs=[tok(d), tok(d)],
        out_shape=[jax.ShapeDtypeStruct((t, d), F32), jax.ShapeDtypeStruct((t, d), BF16)],
        scratch_shapes=[pltpu.VMEM((2 * len(dils) * aw // LANES, tm, LANES), F32)],
        compiler_params=_cparams("arbitrary"),
        name="outproj",
    )(*os_, *lses, rw, x2, wo, ln)


def _lanes(col_rep, n):
    return jnp.concatenate([col_rep] * (n // LANES), axis=1)


MOE_TILE = 512
ROW_CHUNK = 8
SORT_ROWS = 512
COMBINE_ROWS = 512
EXPERT_ROWS = 256


def _tile_rows(ne):
    raw = MOE_TILE * TOP_K + ne * (ROW_CHUNK - 1)
    unit = max(SORT_ROWS, COMBINE_ROWS)
    return -(-raw // unit) * unit


def _route_kernel(x_ref, rwt_ref, bias_ref, tri_ref, ones_ref, low_ref, dloc_ref, gate_ref, tab_ref):
    ne = rwt_ref.shape[0]
    tm = x_ref.shape[0]
    gsz = ne // N_GROUPS
    ninf = -jnp.inf

    aff = _sigmoid(_dotf(rwt_ref[...], x_ref[...], nt=True))
    sel = aff + _lanes(bias_ref[...], tm)

    sel3 = sel.reshape(N_GROUPS, gsz, tm)
    rid = lax.broadcasted_iota(I32, sel3.shape, 1).astype(F32)
    m1 = jnp.max(sel3, axis=1, keepdims=True)
    i1 = jnp.min(jnp.where(sel3 == m1, rid, float(gsz)), axis=1, keepdims=True)
    m2 = jnp.max(jnp.where(rid == i1, ninf, sel3), axis=1, keepdims=True)
    gsc = (m1 + m2).reshape(N_GROUPS, tm)
    gid = lax.broadcasted_iota(I32, gsc.shape, 0).astype(F32)
    keep = jnp.zeros(gsc.shape, F32)
    for _ in range(TOPK_GROUPS):
        gm = jnp.max(gsc, axis=0, keepdims=True)
        gi = jnp.min(jnp.where(gsc == gm, gid, float(N_GROUPS)), axis=0, keepdims=True)
        hit = gid == gi
        keep = jnp.where(hit, 1.0, keep)
        gsc = jnp.where(hit, ninf, gsc)
    cand = jnp.where(keep.reshape(N_GROUPS, 1, tm) > 0.0, sel3, ninf).reshape(ne, tm)

    eid = lax.broadcasted_iota(I32, (ne, tm), 0).astype(F32)
    hits, graw = [], []
    for _ in range(TOP_K):
        m = jnp.max(cand, axis=0, keepdims=True)
        ij = jnp.min(jnp.where(cand == m, eid, float(ne)), axis=0, keepdims=True)
        hit = eid == ij
        hits.append(hit)
        graw.append(jnp.sum(jnp.where(hit, aff, 0.0), axis=0, keepdims=True))
        cand = jnp.where(hit, ninf, cand)
    gsum = functools.reduce(lambda a, b: a + b, graw)
    gate_ref[...] = jnp.concatenate([g / gsum * ROUTED_SCALE for g in graw], axis=0)

    onehot = functools.reduce(lambda a, b: a + b, [h.astype(F32) for h in hits]).astype(BF16)
    before = _dot(onehot, tri_ref[...])
    count = _dot(onehot, ones_ref[...])
    padded = jnp.ceil(count * (1.0 / ROW_CHUNK)) * ROW_CHUNK
    start = _dot3(low_ref[...], padded)
    row = _lanes(start, tm) + before
    dloc_ref[...] = jnp.concatenate(
        [jnp.sum(jnp.where(h, row, 0.0), axis=0, keepdims=True) for h in hits], axis=0).astype(I32)
    lane = lax.broadcasted_iota(I32, padded.shape, 1)
    both = jnp.where(lane == 0, padded, jnp.where(lane == 1, start, 0.0))
    tab_ref[...] = both.T[0:SUBLANES, :].astype(I32)


def _route(x1, router_w, router_bias):
    t, d = x1.shape
    ne = router_w.shape[1]
    tm = MOE_TILE
    rwt = router_w.T
    bias = jnp.broadcast_to(router_bias[:, None], (ne, LANES))
    pos = jnp.arange(tm)
    tri = (pos[:, None] < pos[None, :]).astype(BF16)
    ones = jnp.ones((tm, LANES), BF16)
    eid = jnp.arange(ne)
    low = (eid[:, None] > eid[None, :]).astype(BF16)
    const = lambda a: pl.BlockSpec(a.shape, lambda i: (0, 0))
    tokt = pl.BlockSpec((TOP_K, tm), lambda i: (0, i))
    return pl.pallas_call(
        _route_kernel,
        grid=(t // tm,),
        in_specs=[pl.BlockSpec((tm, d), lambda i: (i, 0)), const(rwt), const(bias), const(tri), const(ones),
                  const(low)],
        out_specs=[tokt, tokt, pl.BlockSpec((None, SUBLANES, ne), lambda i: (i, 0, 0))],
        out_shape=[jax.ShapeDtypeStruct((TOP_K, t), I32), jax.ShapeDtypeStruct((TOP_K, t), F32),
                   jax.ShapeDtypeStruct((t // tm, SUBLANES, ne), I32)],
        compiler_params=_cparams("arbitrary"),
        name="route",
    )(x1, rwt, bias, tri, ones, low)


def _pack_bf16_pairs(v, exact=False):
    half = v.shape[1] // 2
    u = pltpu.bitcast(v if exact else v.astype(BF16).astype(F32), U32)
    return (u[:, 0:half] >> 16) | (u[:, half:] & jnp.uint32(0xFFFF0000))


def _unpack_bf16_pairs(w):
    lo = pltpu.bitcast(w << 16, F32)
    hi = pltpu.bitcast(w & jnp.uint32(0xFFFF0000), F32)
    return jnp.concatenate([lo, hi], axis=1)


def _dot3l(a, b_bf16):
    a1 = a.astype(BF16)
    r1 = a - a1.astype(F32)
    a2 = r1.astype(BF16)
    a3 = (r1 - a2.astype(F32)).astype(BF16)
    return _dot(a1, b_bf16) + (_dot(a2, b_bf16) + _dot(a3, b_bf16))


def _plan_kernel(nblk, cnt_ref, low_ref, upper_ref, gstart_ref, meta_ref, emeta_ref):
    c = cnt_ref[...].astype(F32)
    nt, ne = c.shape
    earlier = _dot3(low_ref[...], c)
    total = jnp.sum(c, axis=0, keepdims=True)
    padded = jnp.ceil(total * (1.0 / EXPERT_ROWS)) * EXPERT_ROWS
    pad_end = _dot3l(jnp.broadcast_to(padded, (SUBLANES, ne)), upper_ref[...])
    pad_start = (pad_end - padded)[0:1, :]
    gstart_ref[...] = (pad_start + earlier).astype(I32)
    emeta_ref[...] = jnp.concatenate(
        [pad_start, padded * (1.0 / EXPERT_ROWS), jnp.zeros((SUBLANES - 2, ne), F32)], axis=0).astype(I32)
    end_col = jnp.broadcast_to(pad_end.T[:, 0:1], (ne, nblk))
    blk_start = (lax.broadcasted_iota(I32, (ne, nblk), 1) * EXPERT_ROWS).astype(F32)
    owner = jnp.minimum(jnp.sum((end_col <= blk_start).astype(F32), axis=0, keepdims=True), float(ne - 1))
    used = jnp.max(end_col, axis=0, keepdims=True) * (1.0 / EXPERT_ROWS)
    meta_ref[...] = jnp.concatenate([owner, used, jnp.zeros((SUBLANES - 2, nblk), F32)], axis=0).astype(I32)


def _plan(cnt, nblk_pad):
    nt, ne = cnt.shape
    ti = jnp.arange(nt)
    low = (ti[:, None] > ti[None, :]).astype(BF16)
    ei = jnp.arange(ne)
    upper = (ei[:, None] <= ei[None, :]).astype(BF16)
    return pl.pallas_call(
        functools.partial(_plan_kernel, nblk_pad),
        out_shape=[jax.ShapeDtypeStruct((nt, ne), I32), jax.ShapeDtypeStruct((SUBLANES, nblk_pad), I32),
                   jax.ShapeDtypeStruct((SUBLANES, ne), I32)],
        compiler_params=pltpu.CompilerParams(vmem_limit_bytes=VMEM_LIMIT_BYTES),
        name="plan",
    )(cnt, low, upper)


def _group_copies(cnt_ref, loc_ref, gstart_ref, tile, ne, local, remote, sem, to_remote):
    def one(e, carry):
        n = pl.multiple_of(cnt_ref[tile * ne + e], ROW_CHUNK)

        @pl.when(n > 0)
        def _():
            lo = pl.multiple_of(loc_ref[tile * ne + e], ROW_CHUNK)
            go = pl.multiple_of(gstart_ref[tile * ne + e], ROW_CHUNK)
            a, b = local.at[pl.ds(lo, n), :], remote.at[pl.ds(go, n), :]
            (pltpu.make_async_copy(a, b, sem) if to_remote else pltpu.make_async_copy(b, a, sem)).start()
        return carry

    lax.fori_loop(0, ne, one, 0, unroll=8)


def _group_wait(rows, local, remote, sem, to_remote):
    n = pl.multiple_of(rows, ROW_CHUNK)

    @pl.when(n > 0)
    def _():
        a, b = local.at[pl.ds(0, n), :], remote.at[pl.ds(0, n), :]
        (pltpu.make_async_copy(a, b, sem) if to_remote else pltpu.make_async_copy(b, a, sem)).wait()


def _dispatch_kernel(ne, cnt_ref, loc_ref, gstart_ref, owner_ref, used_ref, urows_ref, dloc_ref, x_ref, xs_ref,
                     buf_ref, zero_ref, sems, zsem):
    step = pl.program_id(0)
    nsteps = pl.num_programs(0)
    slot = step % 2
    tm = x_ref.shape[0]
    nblk = owner_ref.shape[0]
    used = used_ref[0]

    @pl.when(step == 0)
    def _():
        zero_ref[...] = jnp.zeros_like(zero_ref)

        def block_copy(i):
            return pltpu.make_async_copy(zero_ref, xs_ref.at[pl.ds(i * EXPERT_ROWS, EXPERT_ROWS), :], zsem)

        def is_last(i):
            return (i == used - 1) | (owner_ref[jnp.minimum(i + 1, nblk - 1)] != owner_ref[i])

        def start(i, carry):
            @pl.when(is_last(i))
            def _():
                block_copy(i).start()
            return carry

        def wait(i, carry):
            @pl.when(is_last(i))
            def _():
                block_copy(i).wait()
            return carry

        lax.fori_loop(0, used, start, 0)
        lax.fori_loop(0, used, wait, 0)

    x = x_ref[...]
    dl = dloc_ref[...].astype(I16)
    riota = lax.broadcasted_iota(I32, (SORT_ROWS, tm), 0).astype(I16)
    one = jnp.ones((SORT_ROWS, tm), BF16)

    def chunk(c, carry):
        r0 = pl.multiple_of(c * SORT_ROWS, SORT_ROWS)
        r = riota + r0.astype(I16)
        p = jnp.zeros((SORT_ROWS, tm), BF16)
        for j in range(TOP_K):
            p = jnp.where(r == dl[j:j + 1, :], one, p)
        buf_ref[slot, pl.ds(r0, SORT_ROWS), :] = _pack_bf16_pairs(_dot(p, x), exact=True)
        return carry

    lax.fori_loop(0, (urows_ref[step] + SORT_ROWS - 1) // SORT_ROWS, chunk, 0)

    copies = functools.partial(_group_copies, cnt_ref, loc_ref, gstart_ref)
    copies(step, ne, buf_ref.at[slot], xs_ref, sems.at[slot], True)

    @pl.when(step > 0)
    def _():
        _group_wait(urows_ref[jnp.maximum(step - 1, 0)], buf_ref.at[1 - slot], xs_ref, sems.at[1 - slot], True)

    @pl.when(step == nsteps - 1)
    def _():
        _group_wait(urows_ref[step], buf_ref.at[slot], xs_ref, sems.at[slot], True)


def _dispatch(x1b, dloc, cnt, loc, gstart, owner, used, used_rows, nrows):
    t, d = x1b.shape
    nt, ne = cnt.shape
    tm = MOE_TILE
    rt = _tile_rows(ne)
    grid_spec = pltpu.PrefetchScalarGridSpec(
        num_scalar_prefetch=6,
        grid=(nt,),
        in_specs=[pl.BlockSpec((TOP_K, tm), lambda i, *_: (0, i)), pl.BlockSpec((tm, d), lambda i, *_: (i, 0))],
        out_specs=pl.BlockSpec(memory_space=pl.ANY),
        scratch_shapes=[pltpu.VMEM((2, rt, d // 2), U32), pltpu.VMEM((EXPERT_ROWS, d // 2), U32),
                        pltpu.SemaphoreType.DMA((2,)), pltpu.SemaphoreType.DMA(())],
    )
    return pl.pallas_call(
        functools.partial(_dispatch_kernel, ne),
        grid_spec=grid_spec,
        out_shape=jax.ShapeDtypeStruct((nrows, d // 2), U32),
        compiler_params=_cparams("arbitrary"),
        name="dispatch",
    )(cnt.reshape(-1), loc.reshape(-1), gstart.reshape(-1), owner, used, used_rows, dloc, x1b)


EXPERT_PIECE = 1024


def _expert_kernel(first_ref, nblk_ref, xs_ref, wg_ref, wu_ref, wd_ref, ys_ref,
                   xbuf, ybuf, wgb_ref, wub_ref, wdb_ref, sem_in, sem_out, done_ref, pend_ref):
    e = pl.program_id(0)
    last = pl.num_programs(0) - 1
    per_piece = EXPERT_PIECE // EXPERT_ROWS

    def pieces(ex):
        return (nblk_ref[ex] + per_piece - 1) // per_piece

    def span(ex, s):
        n = jnp.minimum(EXPERT_PIECE, nblk_ref[ex] * EXPERT_ROWS - s * EXPERT_PIECE)
        n = pl.multiple_of(n, EXPERT_ROWS)
        return pl.ds(pl.multiple_of(first_ref[ex] + s * EXPERT_PIECE, EXPERT_ROWS), n), pl.ds(0, n)

    def in_copy(ex, s, slot):
        far, near = span(ex, s)
        return pltpu.make_async_copy(xs_ref.at[far, :], xbuf.at[slot, near, :], sem_in.at[slot])

    def out_copy(ex, s, slot):
        far, near = span(ex, s)
        return pltpu.make_async_copy(ybuf.at[slot, near, :], ys_ref.at[far, :], sem_out.at[slot])

    @pl.when(e == 0)
    def _():
        done_ref[0] = 0
        for slot in range(2):
            pend_ref[2 * slot] = -1

        @pl.when(nblk_ref[0] > 0)
        def _():
            in_copy(0, 0, 0).start()

    nb = nblk_ref[e]
    ns = pieces(e)
    g0 = done_ref[0]
    nxt = jnp.minimum(e + 1, last)
    has_next = (e < last) & (nblk_ref[nxt] > 0)

    def release(slot):
        pe = pend_ref[2 * slot]

        @pl.when(pe >= 0)
        def _():
            out_copy(pe, pend_ref[2 * slot + 1], slot).wait()

    @pl.when(nb > 0)
    def _():
        wgb_ref[...] = wg_ref[...].astype(BF16)
        wub_ref[...] = wu_ref[...].astype(BF16)
        wdb_ref[...] = wd_ref[...].astype(BF16)

        def piece(s, carry):
            slot = (g0 + s) % 2

            @pl.when(s + 1 < ns)
            def _():
                in_copy(e, s + 1, 1 - slot).start()

            @pl.when((s + 1 == ns) & has_next)
            def _():
                in_copy(nxt, 0, 1 - slot).start()

            in_copy(e, s, slot).wait()
            release(slot)

            blocks_here = jnp.minimum(per_piece, nb - s * per_piece)
            for nblocks in range(1, per_piece + 1):
                @pl.when(blocks_here == nblocks)
                def _():
                    rows = pl.ds(0, nblocks * EXPERT_ROWS)
                    xb = _unpack_bf16_pairs(xbuf[slot, rows, :]).astype(BF16)
                    hg = _dot(xb, wgb_ref[...])
                    hb = hg * _sigmoid(hg) * _dot(xb, wub_ref[...])
                    ybuf[slot, rows, :] = _pack_bf16_pairs(_dot(hb.astype(BF16), wdb_ref[...]))
            out_copy(e, s, slot).start()
            pend_ref[2 * slot] = e
            pend_ref[2 * slot + 1] = s
            return carry

        lax.fori_loop(0, ns, piece, 0)

    @pl.when((nb == 0) & has_next)
    def _():
        in_copy(nxt, 0, g0 % 2).start()

    @pl.when(e == last)
    def _():
        for slot in range(2):
            release(slot)

    done_ref[0] = g0 + ns


def _experts(xs, first_row, nblocks, e_gate, e_up, e_down):
    nrows, dh = xs.shape
    ne, d, ff = e_gate.shape
    wsel = lambda e, *_: (e, 0, 0)
    grid_spec = pltpu.PrefetchScalarGridSpec(
        num_scalar_prefetch=2,
        grid=(ne,),
        in_specs=[pl.BlockSpec(memory_space=pl.ANY),
                  pl.BlockSpec((None, d, ff), wsel), pl.BlockSpec((None, d, ff), wsel),
                  pl.BlockSpec((None, ff, d), wsel)],
        out_specs=pl.BlockSpec(memory_space=pl.ANY),
        scratch_shapes=[pltpu.VMEM((2, EXPERT_PIECE, dh), U32), pltpu.VMEM((2, EXPERT_PIECE, dh), U32),
                        pltpu.VMEM((d, ff), BF16), pltpu.VMEM((d, ff), BF16), pltpu.VMEM((ff, d), BF16),
                        pltpu.SemaphoreType.DMA((2,)), pltpu.SemaphoreType.DMA((2,)),
                        pltpu.SMEM((1,), I32), pltpu.SMEM((4,), I32)],
    )
    return pl.pallas_call(
        _expert_kernel,
        grid_spec=grid_spec,
        out_shape=jax.ShapeDtypeStruct((nrows, dh), U32),
        compiler_params=_cparams("arbitrary"),
        name="experts",
    )(first_row, nblocks, xs, e_gate, e_up, e_down)


def _final_kernel(ne, cnt_ref, loc_ref, gstart_ref, used_ref, dloc_ref, gate_ref, x_ref, xb_ref, ys_ref,
                  sg_ref, su_ref, sd_ref, ln_ref, o_ref, buf_ref, acc_ref, sems):
    step = pl.program_id(0)
    nsteps = pl.num_programs(0)
    slot = step % 2
    tm = x_ref.shape[0]
    used = used_ref[step]
    copies = functools.partial(_group_copies, cnt_ref, loc_ref, gstart_ref)

    @pl.when(step == 0)
    def _():
        copies(step, ne, buf_ref.at[0], ys_ref, sems.at[0], False)

    @pl.when(step + 1 < nsteps)
    def _():
        copies(step + 1, ne, buf_ref.at[1 - slot], ys_ref, sems.at[1 - slot], False)

    xb = xb_ref[...]
    hg = _dot(xb, sg_ref[...])
    hs = hg * _sigmoid(hg) * _dot(xb, su_ref[...])
    acc_ref[...] = _dot(hs.astype(BF16), sd_ref[...])

    _group_wait(used, buf_ref.at[slot], ys_ref, sems.at[slot], False)

    dl = dloc_ref[...].astype(I16)
    gt = gate_ref[...].astype(BF16)
    rows16 = lax.broadcasted_iota(I32, (COMBINE_ROWS, tm), 0).astype(I16)
    gts = [jnp.broadcast_to(gt[j:j + 1, :], (COMBINE_ROWS, tm)) for j in range(TOP_K)]
    riota = lax.broadcasted_iota(I32, (COMBINE_ROWS, buf_ref.shape[2]), 0)

    def chunk(c, carry):
        r0 = pl.multiple_of(c * COMBINE_ROWS, COMBINE_ROWS)
        r = rows16 + r0.astype(I16)
        g = jnp.zeros((COMBINE_ROWS, tm), BF16)
        for j in range(TOP_K):
            g = jnp.where(r == dl[j:j + 1, :], gts[j], g)
        w = buf_ref[slot, pl.ds(r0, COMBINE_ROWS), :]
        w = jnp.where(riota + r0 < used, w, jnp.zeros_like(w))
        y = _unpack_bf16_pairs(w).astype(BF16)
        acc_ref[...] += lax.dot_general(g, y, (((0,), (0,)), ((), ())), preferred_element_type=F32)
        return carry

    lax.fori_loop(0, (used + COMBINE_ROWS - 1) // COMBINE_ROWS, chunk, 0)
    o_ref[...] = _layer_norm(ALPHA * x_ref[...] + acc_ref[...], ln_ref[0:1, :], ln_ref[1:2, :])


def _final(x1, x1b, dloc, gate, cnt, loc, gstart, used_rows, ys, s_gate, s_up, s_down, ln_g, ln_b):
    t, d = x1.shape
    nt, ne = cnt.shape
    tm = MOE_TILE
    rt = _tile_rows(ne)
    ln = jnp.concatenate([ln_g[None], ln_b[None], jnp.zeros((SUBLANES - 2, d), F32)], axis=0)
    sg, su, sd = s_gate.astype(BF16), s_up.astype(BF16), s_down.astype(BF16)
    tok = pl.BlockSpec((tm, d), lambda i, *_: (i, 0))
    tokt = pl.BlockSpec((TOP_K, tm), lambda i, *_: (0, i))
    const = lambda a: pl.BlockSpec(a.shape, lambda i, *_: (0, 0))
    grid_spec = pltpu.PrefetchScalarGridSpec(
        num_scalar_prefetch=4,
        grid=(nt,),
        in_specs=[tokt, tokt, tok, tok, pl.BlockSpec(memory_space=pl.ANY), const(sg), const(su), const(sd), const(ln)],
        out_specs=tok,
        scratch_shapes=[pltpu.VMEM((2, rt, d // 2), U32), pltpu.VMEM((tm, d), F32), pltpu.SemaphoreType.DMA((2,))],
    )
    return pl.pallas_call(
        functools.partial(_final_kernel, ne),
        grid_spec=grid_spec,
        out_shape=jax.ShapeDtypeStruct((t, d), F32),
        compiler_params=_cparams("arbitrary"),
        name="final",
    )(cnt.reshape(-1), loc.reshape(-1), gstart.reshape(-1), used_rows, dloc, gate, x1, x1b, ys, sg, su, sd, ln)


def _moe(x1, x1b, router_w, router_bias, e_gate, e_up, e_down, s_gate, s_up, s_down, ln_g, ln_b):
    t = x1.shape[0]
    ne = router_w.shape[1]
    nt = t // MOE_TILE
    nblk = -(-(nt * _tile_rows(ne)) // EXPERT_ROWS) + ne
    nblk_pad = -(-nblk // LANES) * LANES
    dloc, gate, tab = _route(x1, router_w, router_bias)
    cnt, loc = tab[:, 0, :], tab[:, 1, :]
    used_rows = loc[:, ne - 1] + cnt[:, ne - 1]
    gstart, meta, emeta = _plan(cnt, nblk_pad)
    owner, used = meta[0], meta[1, 0:1]
    xs = _dispatch(x1b, dloc, cnt, loc, gstart, owner, used, used_rows, nblk * EXPERT_ROWS)
    ys = _experts(xs, emeta[0], emeta[1], e_gate, e_up, e_down)
    return _final(x1, x1b, dloc, gate, cnt, loc, gstart, used_rows, ys, s_gate, s_up, s_down, ln_g, ln_b)


def kernel(x, w_in, mu_shift, w0, w_decay_up, a0, w_aaa_up, w_gate_up, k_k, k_a, r_k, lnx_g, lnx_b, w_out,
           ln1_g, ln1_b, router_w, router_bias, e_gate, e_up, e_down, s_gate, s_up, s_down, ln2_g, ln2_b):
    bsz, seq, d = x.shape
    x2 = x.reshape(bsz * seq, d)
    qkvs, feats = _inproj(x2, seq, w_in[0], mu_shift[0], w0[0], w_decay_up[0], a0[0], w_aaa_up[0], w_gate_up[0],
                          k_k[0], k_a[0], r_k[0], tm=256)
    rw = _rwkv(feats, bsz, seq, lnx_g[0], lnx_b[0], nbat=4)
    x1, x1b = _mixer_tail(qkvs, rw, x2, bsz, seq, w_out[0], ln1_g[0], ln1_b[0])
    out = _moe(x1, x1b, router_w[0], router_bias[0], e_gate[0], e_up[0], e_down[0], s_gate[0], s_up[0], s_down[0],
               ln2_g[0], ln2_b[0])
    return out.reshape(bsz, seq, d)


def _mixer_tail(qkvs, rw, x2, bsz, seq, w_out, ln_g, ln_b):
    res = [_attention(q, k, v, bsz, seq, window, dilation)
           for (q, k, v), (window, dilation) in zip(qkvs, ATTN_PATTERNS)]
    dils = tuple(dl for _, dl in ATTN_PATTERNS)
    return _outproj([o for o, _ in res], [l for _, l in res], dils, rw, x2, w_out, ln_g, ln_b, tm=512)
```

```python
import functools

import jax
import jax.numpy as jnp
from jax import lax
from jax.experimental import pallas as pl
from jax.experimental.pallas import tpu as pltpu

F32 = jnp.float32
BF16 = jnp.bfloat16
I32 = jnp.int32
I16 = jnp.int16
U32 = jnp.uint32

LANES = 128
SUBLANES = 8
VMEM_LIMIT_BYTES = 56 * 1024 * 1024

HEAD_DIM = 64
ATTN_HEADS = 8
RWKV_HEADS = 8
ATTN_WIDTH = ATTN_HEADS * HEAD_DIM
RWKV_WIDTH = RWKV_HEADS * HEAD_DIM
ATTN_PATTERNS = ((128, 1), (512, 4), (2048, 16))
Q_BLOCK = 128
ROPE_THETA = 10000.0
DECAY_LORA = 64
AAA_LORA = 64
GATE_LORA = 160
GN_EPS = 64e-5
LN_EPS = 1e-5
TOP_K = 8
N_GROUPS = 8
TOPK_GROUPS = 4
ROUTED_SCALE = 2.5
DEPTH = 1
ALPHA = (2.0 * DEPTH) ** 0.25

RWKV_CHUNK = 64
PAIR = 2 * HEAD_DIM
LORA_PAD = 2 * LANES


def _cparams(*sem):
    return pltpu.CompilerParams(dimension_semantics=sem, vmem_limit_bytes=VMEM_LIMIT_BYTES)


def _split_bf16(a):
    hi = a.astype(BF16)
    lo = (a - hi.astype(F32)).astype(BF16)
    return hi, lo


def _dot(a, b):
    return jnp.dot(a, b, preferred_element_type=F32)


def _dot_nt(a, b):
    return lax.dot_general(a, b, (((1,), (1,)), ((), ())), preferred_element_type=F32)


def _dot_hl(a_f32, b_bf16):
    hi, lo = _split_bf16(a_f32)
    return _dot(hi, b_bf16) + _dot(lo, b_bf16)


def _softplus(z):
    return jnp.maximum(z, 0.0) + jnp.log(1.0 + jnp.exp(-jnp.abs(z)))


def _sigmoid(z):
    return 1.0 / (1.0 + jnp.exp(-z))


def _inproj_kernel(dils, x_ref, w_ref, cos_ref, sin_ref, mu_ref, wd_ref, wa_ref, wg_ref, vec_ref, bd_ref, *refs):
    nq = 3 * len(dils)
    qkv_refs = refs[0:nq]
    r_ref, ld_ref, kp_ref, vv_ref, kk_ref, b_ref, g_ref, bon_ref, carry_ref, qkv_scr = refs[nq:]
    s = pl.program_id(1)
    tm = x_ref.shape[0]
    aw = ATTN_WIDTH
    rw = RWKV_WIDTH

    @pl.when(s == 0)
    def _():
        carry_ref[...] = jnp.zeros_like(carry_ref)

    h = _dot(x_ref[...].astype(BF16), w_ref[...])
    reps = aw // cos_ref.shape[1]
    cos = jnp.concatenate([cos_ref[...]] * reps, axis=1)
    sin = jnp.concatenate([sin_ref[...]] * reps, axis=1)
    qkv = [(h[:, 0:aw] * cos + h[:, aw:2 * aw] * sin) * (HEAD_DIM ** -0.5),
           h[:, 2 * aw:3 * aw] * cos + h[:, 3 * aw:4 * aw] * sin,
           h[:, 4 * aw:5 * aw]]
    nslab = aw // LANES
    for i in range(3):
        for j in range(nslab):
            qkv_scr[i * nslab + j] = qkv[i][:, j * LANES:(j + 1) * LANES]
    for di, d in enumerate(dils):
        for i in range(3):
            o_ref = qkv_refs[3 * di + i]
            if d == 1:
                o_ref[...] = qkv[i].astype(o_ref.dtype)
                continue
            for res in range(d):
                for j in range(nslab):
                    rows = qkv_scr[i * nslab + j, pl.ds(res, tm // d, stride=d), :]
                    o_ref[:, res * aw + j * LANES:res * aw + (j + 1) * LANES] = rows.astype(o_ref.dtype)

    f = h[:, 5 * aw:]
    rows = lax.broadcasted_iota(I32, f.shape, 0)
    prev = jnp.where(rows == 0, carry_ref[SUBLANES - 1:SUBLANES, :], pltpu.roll(f, 1, axis=0))
    carry_ref[...] = f[tm - SUBLANES:tm, :]
    f = f + (prev - f) * mu_ref[...]

    r = f[:, 0:rw]
    k = f[:, rw:2 * rw]
    v = f[:, 2 * rw:3 * rw]
    la = f[:, 3 * rw:3 * rw + LANES]
    gl = f[:, 3 * rw + LANES:]
    w0, a0, k_k, k_a, r_k = (vec_ref[i:i + 1, :] for i in range(5))

    z = w0 + _dot(jnp.tanh(la).astype(BF16), wd_ref[...])
    w = -_softplus(-z) - 0.5
    ld_ref[...] = -jnp.exp(w)
    a = _sigmoid(a0 + _dot(la.astype(BF16), wa_ref[...]))
    g_ref[...] = _dot(_sigmoid(gl).astype(BF16), wg_ref[...])

    bd = bd_ref[...]
    kk = k * k_k
    nrm = jnp.sqrt(_dot_hl(kk * kk, bd))
    kk = kk / jnp.maximum(nrm, 1e-12)
    kp = k * (1.0 + (a - 1.0) * k_a)
    r_ref[...] = r
    kp_ref[...] = kp
    vv_ref[...] = v
    kk_ref[...] = kk
    b_ref[...] = kk * a
    bon_ref[...] = _dot_hl(r * kp * r_k, bd) * v


def _rot_half_cols(w):
    d, n = w.shape
    w4 = w.reshape(d, n // HEAD_DIM, 2, HEAD_DIM // 2)
    return jnp.stack([-w4[:, :, 1, :], w4[:, :, 0, :]], axis=2).reshape(d, n)


def _inproj(x2, seq, w_in, mu_shift, w0, w_decay_up, a0, w_aaa_up, w_gate_up, k_k, k_a, r_k, tm):
    t, d = x2.shape
    aw, rw = ATTN_WIDTH, RWKV_WIDTH
    wq, wk, wv = w_in[:, 0:aw], w_in[:, aw:2 * aw], w_in[:, 2 * aw:3 * aw]
    wf = w_in[:, 3 * aw:]
    gpad = LORA_PAD - GATE_LORA
    w_all = jnp.concatenate(
        [wq, _rot_half_cols(wq), wk, _rot_half_cols(wk), wv, wf, jnp.zeros((d, gpad), F32)], axis=1).astype(BF16)
    mu = jnp.concatenate([mu_shift, jnp.zeros((gpad,), F32)])[None, :]
    nf = mu.shape[1]
    wd = jnp.concatenate([w_decay_up, jnp.zeros((AAA_LORA, rw), F32)], axis=0).astype(BF16)
    wa = jnp.concatenate([jnp.zeros((DECAY_LORA, rw), F32), w_aaa_up], axis=0).astype(BF16)
    wg = jnp.concatenate([w_gate_up, jnp.zeros((gpad, rw), F32)], axis=0).astype(BF16)
    vec = jnp.stack([w0, a0, k_k, k_a, r_k.reshape(-1), w0 * 0, w0 * 0, w0 * 0])
    head = jnp.arange(rw) // HEAD_DIM
    bd = (head[:, None] == head[None, :]).astype(BF16)
    half = HEAD_DIM // 2
    inv_freq = ROPE_THETA ** (-jnp.arange(half, dtype=F32) * 2.0 / HEAD_DIM)
    ang = jnp.arange(seq, dtype=F32)[:, None] * inv_freq[None, :]
    cos = jnp.tile(jnp.cos(ang), (1, LANES // half))
    sin = jnp.tile(jnp.sin(ang), (1, LANES // half))

    nst = seq // tm
    tok = lambda b, s: (b * nst + s, 0)
    const = lambda b, s: (0, 0)
    full = lambda a: pl.BlockSpec(a.shape, const)
    dils = tuple(dl for _, dl in ATTN_PATTERNS)
    qkv_specs, qkv_shapes = [], []
    for dl in dils:
        qkv_specs += [pl.BlockSpec((tm // dl, dl * aw), tok)] * 3
        qkv_shapes += [jax.ShapeDtypeStruct((t // dl, dl * aw), BF16)] * 3
    out_f = jax.ShapeDtypeStruct((t, rw), F32)
    outs = pl.pallas_call(
        functools.partial(_inproj_kernel, dils),
        grid=(t // seq, nst),
        in_specs=[pl.BlockSpec((tm, d), tok), full(w_all),
                  pl.BlockSpec((tm, LANES), lambda b, s: (s, 0)), pl.BlockSpec((tm, LANES), lambda b, s: (s, 0)),
                  full(mu), full(wd), full(wa), full(wg), full(vec), full(bd)],
        out_specs=qkv_specs + [pl.BlockSpec((tm, rw), tok)] * 8,
        out_shape=qkv_shapes + [out_f] * 8,
        scratch_shapes=[pltpu.VMEM((SUBLANES, nf), F32), pltpu.VMEM((3 * aw // LANES, tm, LANES), F32)],
        compiler_params=_cparams("arbitrary", "arbitrary"),
        name="inproj",
    )(x2, w_all, cos, sin, mu, wd, wa, wg, vec, bd)
    nq = 3 * len(dils)
    return [outs[3 * i:3 * i + 3] for i in range(len(dils))], outs[nq:]


def _dotf(a, b, nt=False, passes=3):
    dot = _dot_nt if nt else _dot
    if passes == 1:
        return dot(a.astype(BF16), b.astype(BF16))
    ah, al = _split_bf16(a)
    if passes == 2:
        bh = b.astype(BF16)
        return dot(ah, bh) + dot(al, bh)
    bh, bl = _split_bf16(b)
    return dot(ah, bh) + (dot(ah, bl) + dot(al, bh))


RWKV_PASSES = dict(s8=1, inv=1, w1=1, au=1, ry=1, gh=1, yh=1)


def _dot3(a_bf16, b):
    b1 = b.astype(BF16)
    r1 = b - b1.astype(F32)
    b2 = r1.astype(BF16)
    b3 = (r1 - b2.astype(F32)).astype(BF16)
    return _dot(a_bf16, b1) + (_dot(a_bf16, b2) + _dot(a_bf16, b3))


def _rwkv_kernel(r_ref, ld_ref, kp_ref, v_ref, kk_ref, b_ref, g_ref, bon_ref, gn_ref, bd_ref, o_ref, h_ref):
    c = pl.program_id(1)
    nbat, ch, rw = r_ref.shape
    npairs = rw // PAIR

    @pl.when(c == 0)
    def _():
        h_ref[...] = jnp.zeros_like(h_ref)

    ri = lax.broadcasted_iota(I32, (ch, ch), 0)
    ci = lax.broadcasted_iota(I32, (ch, ch), 1)
    tril = (ri >= ci).astype(BF16)
    bd = bd_ref[...]
    row = lax.broadcasted_iota(I32, (ch, PAIR), 0)
    col = lax.broadcasted_iota(I32, (ch, PAIR), 1)
    first = col < HEAD_DIM
    jj = col & (HEAD_DIM - 1)
    strict = jj < row
    incl = jj <= row
    eye = (jj == row).astype(F32)
    row2 = lax.broadcasted_iota(I32, (PAIR, PAIR), 0)
    col2 = lax.broadcasted_iota(I32, (PAIR, PAIR), 1)
    same_head = (row2 < HEAD_DIM) == (col2 < HEAD_DIM)
    diag2 = row2 == col2
    zeros_cp = jnp.zeros((ch, PAIR), F32)

    def bdiag(y):
        return jnp.concatenate([jnp.where(first, y, 0.0), jnp.where(first, 0.0, y)], axis=0)

    ps = RWKV_PASSES
    units = []
    for bi in range(nbat):
        ld = ld_ref[bi]
        cum = _dot3(tril, ld)
        tot = cum[ch - 1:ch, :]
        a_t = -kk_ref[bi] * jnp.exp(cum - ld)
        pinv = jnp.exp(-cum)
        b_t = b_ref[bi] * pinv
        k_t = kp_ref[bi] * pinv
        r_t = r_ref[bi] * jnp.exp(cum)
        pend = jnp.exp(tot - cum)
        b_end = b_ref[bi] * pend
        k_end = kp_ref[bi] * pend
        p_tot = jnp.exp(tot)
        v_all = v_ref[bi]
        for p in range(npairs):
            sl = slice(p * PAIR, (p + 1) * PAIR)
            units.append(dict(h=bi * npairs + p, a=a_t[:, sl], r=r_t[:, sl], b=b_t[:, sl], k=k_t[:, sl],
                              v=v_all[:, sl], be=b_end[:, sl], ke=k_end[:, sl], pt=p_tot[:, sl]))

    for u in units:
        u["s8"] = _dotf(jnp.concatenate([u["a"], u["r"]], axis=0),
                        jnp.concatenate([bdiag(u["b"]), bdiag(u["k"])], axis=0), nt=True, passes=ps["s8"])
    for u in units:
        s8 = u.pop("s8")
        u["l"] = jnp.where(strict, s8[0:ch, 0:PAIR], 0.0)
        u["ak"] = jnp.where(strict, s8[0:ch, PAIR:], 0.0)
        u["mrb"] = jnp.where(incl, s8[ch:, 0:PAIR], 0.0)
        u["mrk"] = jnp.where(incl, s8[ch:, PAIR:], 0.0)
        u["t"] = eye + u["l"]
    for u in units:
        u["lk"] = _dotf(u["l"], bdiag(u["l"]), passes=ps["inv"])
        u["w1"] = _dotf(u["ak"], bdiag(u["v"]), passes=ps["w1"])
    for _ in range(max(ch.bit_length() - 3, 0)):
        for u in units:
            u["both"] = _dotf(jnp.concatenate([u["t"], u["lk"]], axis=0), bdiag(u["lk"]), passes=ps["inv"])
        for u in units:
            both = u.pop("both")
            u["t"] = u["t"] + both[0:ch]
            u["lk"] = both[ch:]
    for u in units:
        u["t"] = u["t"] + _dotf(u["t"], bdiag(u["lk"]), passes=ps["inv"])
    for u in units:
        u["au"] = _dotf(u["t"], jnp.concatenate([bdiag(u["a"]), bdiag(u["w1"])], axis=1), passes=ps["au"])
    for u in units:
        a_hat, u_loc = u["au"][:, 0:PAIR], u["au"][:, PAIR:]
        rhs = jnp.concatenate([
            jnp.concatenate([bdiag(a_hat), bdiag(u_loc)], axis=1),
            jnp.concatenate([jnp.zeros((2 * ch, PAIR), F32), bdiag(u["v"])], axis=1)], axis=0)
        u["ry"] = _dotf(jnp.concatenate([u["mrb"], u["mrk"]], axis=1), rhs, passes=ps["ry"])
        bkt = jnp.concatenate([u["be"], u["ke"]], axis=0).T
        u["gh"] = _dotf(bkt, jnp.concatenate([u["au"], jnp.concatenate([zeros_cp, u["v"]], axis=1)], axis=0),
                        passes=ps["gh"])
    for u in units:
        r_hat = u["r"] + u["ry"][:, 0:PAIR]
        g_mat = jnp.where(same_head, u["gh"][:, 0:PAIR], 0.0) + jnp.where(diag2, u["pt"], 0.0)
        u["yh"] = _dotf(jnp.concatenate([r_hat, g_mat], axis=0), h_ref[u["h"]], passes=ps["yh"])
    for u in units:
        h_ref[u["h"]] = u["yh"][ch:] + jnp.where(same_head, u["gh"][:, PAIR:], 0.0)
        u["y"] = u["yh"][0:ch] + u["ry"][:, PAIR:]

    inv_n = 1.0 / HEAD_DIM
    for bi in range(nbat):
        y = jnp.concatenate([u["y"] for u in units[bi * npairs:(bi + 1) * npairs]], axis=1)
        yc = y - _dot_hl(y, bd) * inv_n
        var = _dot_hl(yc * yc, bd) * inv_n
        yn = yc * lax.rsqrt(var + GN_EPS) * gn_ref[0:1, :] + gn_ref[1:2, :]
        o_ref[bi] = (yn + bon_ref[bi]) * g_ref[bi]


def _rwkv(feats, bsz, seq, lnx_g, lnx_b, nbat):
    t, rw = feats[0].shape
    ch = RWKV_CHUNK
    gn = jnp.concatenate([lnx_g[None], lnx_b[None], jnp.zeros((SUBLANES - 2, rw), F32)], axis=0)
    head = jnp.arange(rw) // HEAD_DIM
    bd = (head[:, None] == head[None, :]).astype(BF16)
    tok = pl.BlockSpec((nbat, ch, rw), lambda bb, c: (bb, c, 0))
    const = lambda a: pl.BlockSpec(a.shape, lambda bb, c: (0, 0))
    out = pl.pallas_call(
        _rwkv_kernel,
        grid=(bsz // nbat, seq // ch),
        in_specs=[tok] * 8 + [const(gn), const(bd)],
        out_specs=tok,
        out_shape=jax.ShapeDtypeStruct((bsz, seq, rw), F32),
        scratch_shapes=[pltpu.VMEM((nbat * (rw // PAIR), PAIR, PAIR), F32)],
        compiler_params=_cparams("arbitrary", "arbitrary"),
        name="rwkv",
    )(*[f.reshape(bsz, seq, rw) for f in feats], gn, bd)
    return out.reshape(t, rw)


NEG_BIG = -1e30


def _attn_kernel(n_back, q_ref, kc_ref, kp_ref, vc_ref, vp_ref, o_ref, lse_ref):
    n = pl.program_id(2)
    qb = kp_ref.shape[0]
    nq = q_ref.shape[0] // qb
    qi = lax.broadcasted_iota(I32, (qb, 2 * qb), 0)
    ki = lax.broadcasted_iota(I32, (qb, 2 * qb), 1)
    dist = qb + qi - ki
    band = (dist >= 0) & (dist <= n_back)
    has_prev = jnp.where(n > 0, 0, qb)
    valid = [band & (ki >= has_prev)] + [band] * (nq - 1)
    first = lax.broadcasted_iota(I32, (qb, PAIR), 1) < HEAD_DIM
    npairs = q_ref.shape[1] // PAIR
    heads = []
    for u in range(nq):
        rows = slice(u * qb, (u + 1) * qb)
        for p in range(npairs):
            sl = slice(p * PAIR, (p + 1) * PAIR)
            q2 = q_ref[rows, sl]
            keys = jnp.concatenate([kp_ref[:, sl], kc_ref[:, sl]], axis=0)
            k2 = keys[u * qb:(u + 2) * qb]
            for hh in range(2):
                keep = first if hh == 0 else jnp.logical_not(first)
                s = _dot_nt(jnp.where(keep, q2, jnp.zeros_like(q2)), k2)
                heads.append(dict(u=u, rows=rows, sl=sl, s=s))
    for h in heads:
        ok = valid[h["u"]]
        s = jnp.where(ok, h.pop("s"), NEG_BIG)
        m = jnp.max(s, axis=1, keepdims=True)
        pe = jnp.where(ok, jnp.exp(s - m), 0.0)
        l = jnp.sum(pe, axis=1, keepdims=True)
        h["pe"] = pe.astype(BF16)
        h["l"] = l
        h["lse"] = m + jnp.log(l)
    for h in heads:
        sl, u = h["sl"], h["u"]
        vals = jnp.concatenate([vp_ref[:, sl], vc_ref[:, sl]], axis=0)
        h["o"] = _dot(h.pop("pe"), vals[u * qb:(u + 2) * qb]) / h["l"]
    for i in range(0, len(heads), 2):
        h0, h1 = heads[i], heads[i + 1]
        o_ref[h0["rows"], h0["sl"]] = jnp.where(first, h0["o"], h1["o"])
        lse_ref[h0["rows"], h0["sl"]] = jnp.where(first, h0["lse"], h1["lse"])


ATTN_QBLOCKS = 2


def _attention(q, k, v, bsz, seq, window, dilation):
    aw = q.shape[1] // dilation
    nq = ATTN_QBLOCKS
    nb = seq // dilation // (nq * Q_BLOCK)
    cur = pl.BlockSpec((nq * Q_BLOCK, aw), lambda b, r, n: (b * nb + n, r))
    prv = pl.BlockSpec((Q_BLOCK, aw), lambda b, r, n: (jnp.maximum((b * nb + n) * nq - 1, b * nb * nq), r))
    shp = jax.ShapeDtypeStruct(q.shape, F32)
    return pl.pallas_call(
        functools.partial(_attn_kernel, window // dilation),
        grid=(bsz, dilation, nb),
        in_specs=[cur, cur, prv, cur, prv],
        out_specs=[cur, cur],
        out_shape=[shp, shp],
        compiler_params=_cparams("arbitrary", "arbitrary", "arbitrary"),
        name=f"attn_d{dilation}",
    )(q, k, k, v, v)


def _layer_norm(y, g, b):
    mu = jnp.mean(y, axis=-1, keepdims=True)
    yc = y - mu
    var = jnp.mean(yc * yc, axis=-1, keepdims=True)
    return yc * lax.rsqrt(var + LN_EPS) * g + b


def _outproj_kernel(dils, *refs):
    npat = len(dils)
    o_refs = refs[0:npat]
    lse_refs = refs[npat:2 * npat]
    rw_ref, x_ref, wo_ref, ln_ref, y_ref, yb_ref, scr = refs[2 * npat:]
    tm, aw = rw_ref.shape

    def token_major(ref, d, slot):
        if d == 1:
            return ref[...]
        nslab = aw // LANES
        for res in range(d):
            for j in range(nslab):
                scr[slot * nslab + j, pl.ds(res, tm // d, stride=d), :] = (
                    ref[:, res * aw + j * LANES:res * aw + (j + 1) * LANES])
        return jnp.concatenate([scr[slot * nslab + j] for j in range(nslab)], axis=1)

    lses = [token_major(r, d, 2 * i) for i, (r, d) in enumerate(zip(lse_refs, dils))]
    outs = [token_major(r, d, 2 * i + 1) for i, (r, d) in enumerate(zip(o_refs, dils))]
    m = functools.reduce(jnp.maximum, lses)
    es = [jnp.exp(z - m) for z in lses]
    den = functools.reduce(lambda a, b: a + b, es)
    attn = functools.reduce(lambda a, b: a + b, [(e / den) * o for e, o in zip(es, outs)])
    mix = _dot(attn.astype(BF16), wo_ref[0:aw, :]) + _dot(rw_ref[...].astype(BF16), wo_ref[aw:, :])
    y = _layer_norm(ALPHA * x_ref[...] + mix, ln_ref[0:1, :], ln_ref[1:2, :])
    y_ref[...] = y
    yb_ref[...] = y.astype(yb_ref.dtype)


def _outproj(os_, lses, dils, rw, x2, w_out, ln_g, ln_b, tm):
    t, d = x2.shape
    aw = rw.shape[1]
    ln = jnp.concatenate([ln_g[None], ln_b[None], jnp.zeros((SUBLANES - 2, d), F32)], axis=0)
    wo = w_out.astype(BF16)
    tok = lambda w: pl.BlockSpec((tm, w), lambda i: (i, 0))
    view = [pl.BlockSpec((tm // dl, dl * aw), lambda i: (i, 0)) for dl in dils]
    const = lambda a: pl.BlockSpec(a.shape, lambda i: (0, 0))
    return pl.pallas_call(
        functools.partial(_outproj_kernel, dils),
        grid=(t // tm,),
        in_specs=view + view + [tok(aw), tok(d), const(wo), const(ln)],
        out_specs=[tok(d), tok(d)],
        out_shape=[jax.ShapeDtypeStruct((t, d), F32), jax.ShapeDtypeStruct((t, d), BF16)],
        scratch_shapes=[pltpu.VMEM((2 * len(dils) * aw // LANES, tm, LANES), F32)],
        compiler_params=_cparams("arbitrary"),
        name="outproj",
    )(*os_, *lses, rw, x2, wo, ln)


def _lanes(col_rep, n):
    return jnp.concatenate([col_rep] * (n // LANES), axis=1)


MOE_TILE = 512
ROW_CHUNK = 8
SORT_ROWS = 512
COMBINE_ROWS = 1024
EXPERT_ROWS = 256


def _tile_rows(ne):
    raw = MOE_TILE * TOP_K + ne * (ROW_CHUNK - 1)
    unit = max(SORT_ROWS, COMBINE_ROWS)
    return -(-raw // unit) * unit


def _route_kernel(x_ref, rwt_ref, bias_ref, tri_ref, ones_ref, low_ref, dloc_ref, gate_ref, tab_ref):
    ne = rwt_ref.shape[0]
    tm = x_ref.shape[0]
    gsz = ne // N_GROUPS
    ninf = -jnp.inf

    aff = _sigmoid(_dotf(rwt_ref[...], x_ref[...], nt=True))
    sel = aff + _lanes(bias_ref[...], tm)

    sel3 = sel.reshape(N_GROUPS, gsz, tm)
    rid = lax.broadcasted_iota(I32, sel3.shape, 1).astype(F32)
    m1 = jnp.max(sel3, axis=1, keepdims=True)
    i1 = jnp.min(jnp.where(sel3 == m1, rid, float(gsz)), axis=1, keepdims=True)
    m2 = jnp.max(jnp.where(rid == i1, ninf, sel3), axis=1, keepdims=True)
    gsc = (m1 + m2).reshape(N_GROUPS, tm)
    gid = lax.broadcasted_iota(I32, gsc.shape, 0).astype(F32)
    keep = jnp.zeros(gsc.shape, F32)
    for _ in range(TOPK_GROUPS):
        gm = jnp.max(gsc, axis=0, keepdims=True)
        gi = jnp.min(jnp.where(gsc == gm, gid, float(N_GROUPS)), axis=0, keepdims=True)
        hit = gid == gi
        keep = jnp.where(hit, 1.0, keep)
        gsc = jnp.where(hit, ninf, gsc)
    cand = jnp.where(keep.reshape(N_GROUPS, 1, tm) > 0.0, sel3, ninf).reshape(ne, tm)

    eid = lax.broadcasted_iota(I32, (ne, tm), 0).astype(F32)
    hits, graw = [], []
    for _ in range(TOP_K):
        m = jnp.max(cand, axis=0, keepdims=True)
        ij = jnp.min(jnp.where(cand == m, eid, float(ne)), axis=0, keepdims=True)
        hit = eid == ij
        hits.append(hit)
        graw.append(jnp.sum(jnp.where(hit, aff, 0.0), axis=0, keepdims=True))
        cand = jnp.where(hit, ninf, cand)
    gsum = functools.reduce(lambda a, b: a + b, graw)
    gate_ref[...] = jnp.concatenate([g / gsum * ROUTED_SCALE for g in graw], axis=0)

    onehot = functools.reduce(lambda a, b: a + b, [h.astype(F32) for h in hits]).astype(BF16)
    before = _dot(onehot, tri_ref[...])
    count = _dot(onehot, ones_ref[...])
    padded = jnp.ceil(count * (1.0 / ROW_CHUNK)) * ROW_CHUNK
    start = _dot3(low_ref[...], padded)
    row = _lanes(start, tm) + before
    dloc_ref[...] = jnp.concatenate(
        [jnp.sum(jnp.where(h, row, 0.0), axis=0, keepdims=True) for h in hits], axis=0).astype(I32)
    lane = lax.broadcasted_iota(I32, padded.shape, 1)
    both = jnp.where(lane == 0, padded, jnp.where(lane == 1, start, 0.0))
    tab_ref[...] = both.T[0:SUBLANES, :].astype(I32)


def _route(x1, router_w, router_bias):
    t, d = x1.shape
    ne = router_w.shape[1]
    tm = MOE_TILE
    rwt = router_w.T
    bias = jnp.broadcast_to(router_bias[:, None], (ne, LANES))
    pos = jnp.arange(tm)
    tri = (pos[:, None] < pos[None, :]).astype(BF16)
    ones = jnp.ones((tm, LANES), BF16)
    eid = jnp.arange(ne)
    low = (eid[:, None] > eid[None, :]).astype(BF16)
    const = lambda a: pl.BlockSpec(a.shape, lambda i: (0, 0))
    tokt = pl.BlockSpec((TOP_K, tm), lambda i: (0, i))
    return pl.pallas_call(
        _route_kernel,
        grid=(t // tm,),
        in_specs=[pl.BlockSpec((tm, d), lambda i: (i, 0)), const(rwt), const(bias), const(tri), const(ones),
                  const(low)],
        out_specs=[tokt, tokt, pl.BlockSpec((None, SUBLANES, ne), lambda i: (i, 0, 0))],
        out_shape=[jax.ShapeDtypeStruct((TOP_K, t), I32), jax.ShapeDtypeStruct((TOP_K, t), F32),
                   jax.ShapeDtypeStruct((t // tm, SUBLANES, ne), I32)],
        compiler_params=_cparams("arbitrary"),
        name="route",
    )(x1, rwt, bias, tri, ones, low)


def _pack_bf16_pairs(v, exact=False):
    half = v.shape[1] // 2
    u = pltpu.bitcast(v if exact else v.astype(BF16).astype(F32), U32)
    return (u[:, 0:half] >> 16) | (u[:, half:] & jnp.uint32(0xFFFF0000))


def _unpack_bf16_pairs(w):
    lo = pltpu.bitcast(w << 16, F32)
    hi = pltpu.bitcast(w & jnp.uint32(0xFFFF0000), F32)
    return jnp.concatenate([lo, hi], axis=1)


def _dot3l(a, b_bf16):
    a1 = a.astype(BF16)
    r1 = a - a1.astype(F32)
    a2 = r1.astype(BF16)
    a3 = (r1 - a2.astype(F32)).astype(BF16)
    return _dot(a1, b_bf16) + (_dot(a2, b_bf16) + _dot(a3, b_bf16))


def _plan_kernel(nblk, cnt_ref, low_ref, upper_ref, gstart_ref, meta_ref, emeta_ref):
    c = cnt_ref[...].astype(F32)
    nt, ne = c.shape
    earlier = _dot3(low_ref[...], c)
    total = jnp.sum(c, axis=0, keepdims=True)
    padded = jnp.ceil(total * (1.0 / EXPERT_ROWS)) * EXPERT_ROWS
    pad_end = _dot3l(jnp.broadcast_to(padded, (SUBLANES, ne)), upper_ref[...])
    pad_start = (pad_end - padded)[0:1, :]
    gstart_ref[...] = (pad_start + earlier).astype(I32)
    emeta_ref[...] = jnp.concatenate(
        [pad_start, padded * (1.0 / EXPERT_ROWS), jnp.zeros((SUBLANES - 2, ne), F32)], axis=0).astype(I32)
    end_col = jnp.broadcast_to(pad_end.T[:, 0:1], (ne, nblk))
    blk_start = (lax.broadcasted_iota(I32, (ne, nblk), 1) * EXPERT_ROWS).astype(F32)
    owner = jnp.minimum(jnp.sum((end_col <= blk_start).astype(F32), axis=0, keepdims=True), float(ne - 1))
    used = jnp.max(end_col, axis=0, keepdims=True) * (1.0 / EXPERT_ROWS)
    meta_ref[...] = jnp.concatenate([owner, used, jnp.zeros((SUBLANES - 2, nblk), F32)], axis=0).astype(I32)


def _plan(cnt, nblk_pad):
    nt, ne = cnt.shape
    ti = jnp.arange(nt)
    low = (ti[:, None] > ti[None, :]).astype(BF16)
    ei = jnp.arange(ne)
    upper = (ei[:, None] <= ei[None, :]).astype(BF16)
    return pl.pallas_call(
        functools.partial(_plan_kernel, nblk_pad),
        out_shape=[jax.ShapeDtypeStruct((nt, ne), I32), jax.ShapeDtypeStruct((SUBLANES, nblk_pad), I32),
                   jax.ShapeDtypeStruct((SUBLANES, ne), I32)],
        compiler_params=pltpu.CompilerParams(vmem_limit_bytes=VMEM_LIMIT_BYTES),
        name="plan",
    )(cnt, low, upper)


def _group_copies(cnt_ref, loc_ref, gstart_ref, tile, ne, local, remote, sem, to_remote):
    def one(e, carry):
        n = pl.multiple_of(cnt_ref[tile * ne + e], ROW_CHUNK)

        @pl.when(n > 0)
        def _():
            lo = pl.multiple_of(loc_ref[tile * ne + e], ROW_CHUNK)
            go = pl.multiple_of(gstart_ref[tile * ne + e], ROW_CHUNK)
            a, b = local.at[pl.ds(lo, n), :], remote.at[pl.ds(go, n), :]
            (pltpu.make_async_copy(a, b, sem) if to_remote else pltpu.make_async_copy(b, a, sem)).start()
        return carry

    lax.fori_loop(0, ne, one, 0, unroll=8)


def _group_wait(rows, local, remote, sem, to_remote):
    n = pl.multiple_of(rows, ROW_CHUNK)

    @pl.when(n > 0)
    def _():
        a, b = local.at[pl.ds(0, n), :], remote.at[pl.ds(0, n), :]
        (pltpu.make_async_copy(a, b, sem) if to_remote else pltpu.make_async_copy(b, a, sem)).wait()


def _dispatch_kernel(ne, cnt_ref, loc_ref, gstart_ref, owner_ref, used_ref, urows_ref, dloc_ref, x_ref, xs_ref,
                     buf_ref, zero_ref, sems, zsem):
    step = pl.program_id(0)
    nsteps = pl.num_programs(0)
    slot = step % 2
    tm = x_ref.shape[0]
    nblk = owner_ref.shape[0]
    used = used_ref[0]

    @pl.when(step == 0)
    def _():
        zero_ref[...] = jnp.zeros_like(zero_ref)

        def block_copy(i):
            return pltpu.make_async_copy(zero_ref, xs_ref.at[pl.ds(i * EXPERT_ROWS, EXPERT_ROWS), :], zsem)

        def is_last(i):
            return (i == used - 1) | (owner_ref[jnp.minimum(i + 1, nblk - 1)] != owner_ref[i])

        def start(i, carry):
            @pl.when(is_last(i))
            def _():
                block_copy(i).start()
            return carry

        def wait(i, carry):
            @pl.when(is_last(i))
            def _():
                block_copy(i).wait()
            return carry

        lax.fori_loop(0, used, start, 0)
        lax.fori_loop(0, used, wait, 0)

    x = x_ref[...]
    dl = dloc_ref[...].astype(I16)
    riota = lax.broadcasted_iota(I32, (SORT_ROWS, tm), 0).astype(I16)
    one = jnp.ones((SORT_ROWS, tm), BF16)

    def chunk(c, carry):
        r0 = pl.multiple_of(c * SORT_ROWS, SORT_ROWS)
        r = riota + r0.astype(I16)
        p = jnp.zeros((SORT_ROWS, tm), BF16)
        for j in range(TOP_K):
            p = jnp.where(r == dl[j:j + 1, :], one, p)
        buf_ref[slot, pl.ds(r0, SORT_ROWS), :] = _pack_bf16_pairs(_dot(p, x), exact=True)
        return carry

    lax.fori_loop(0, (urows_ref[step] + SORT_ROWS - 1) // SORT_ROWS, chunk, 0)

    copies = functools.partial(_group_copies, cnt_ref, loc_ref, gstart_ref)
    copies(step, ne, buf_ref.at[slot], xs_ref, sems.at[slot], True)

    @pl.when(step > 0)
    def _():
        _group_wait(urows_ref[jnp.maximum(step - 1, 0)], buf_ref.at[1 - slot], xs_ref, sems.at[1 - slot], True)

    @pl.when(step == nsteps - 1)
    def _():
        _group_wait(urows_ref[step], buf_ref.at[slot], xs_ref, sems.at[slot], True)


def _dispatch(x1b, dloc, cnt, loc, gstart, owner, used, used_rows, nrows):
    t, d = x1b.shape
    nt, ne = cnt.shape
    tm = MOE_TILE
    rt = _tile_rows(ne)
    grid_spec = pltpu.PrefetchScalarGridSpec(
        num_scalar_prefetch=6,
        grid=(nt,),
        in_specs=[pl.BlockSpec((TOP_K, tm), lambda i, *_: (0, i)), pl.BlockSpec((tm, d), lambda i, *_: (i, 0))],
        out_specs=pl.BlockSpec(memory_space=pl.ANY),
        scratch_shapes=[pltpu.VMEM((2, rt, d // 2), U32), pltpu.VMEM((EXPERT_ROWS, d // 2), U32),
                        pltpu.SemaphoreType.DMA((2,)), pltpu.SemaphoreType.DMA(())],
    )
    return pl.pallas_call(
        functools.partial(_dispatch_kernel, ne),
        grid_spec=grid_spec,
        out_shape=jax.ShapeDtypeStruct((nrows, d // 2), U32),
        compiler_params=_cparams("arbitrary"),
        name="dispatch",
    )(cnt.reshape(-1), loc.reshape(-1), gstart.reshape(-1), owner, used, used_rows, dloc, x1b)


EXPERT_PIECE = 1024


def _expert_kernel(first_ref, nblk_ref, xs_ref, wg_ref, wu_ref, wd_ref, ys_ref,
                   xbuf, ybuf, wgb_ref, wub_ref, wdb_ref, sem_in, sem_out, done_ref, pend_ref):
    e = pl.program_id(0)
    last = pl.num_programs(0) - 1
    per_piece = EXPERT_PIECE // EXPERT_ROWS

    def pieces(ex):
        return (nblk_ref[ex] + per_piece - 1) // per_piece

    def span(ex, s):
        n = jnp.minimum(EXPERT_PIECE, nblk_ref[ex] * EXPERT_ROWS - s * EXPERT_PIECE)
        n = pl.multiple_of(n, EXPERT_ROWS)
        return pl.ds(pl.multiple_of(first_ref[ex] + s * EXPERT_PIECE, EXPERT_ROWS), n), pl.ds(0, n)

    def in_copy(ex, s, slot):
        far, near = span(ex, s)
        return pltpu.make_async_copy(xs_ref.at[far, :], xbuf.at[slot, near, :], sem_in.at[slot])

    def out_copy(ex, s, slot):
        far, near = span(ex, s)
        return pltpu.make_async_copy(ybuf.at[slot, near, :], ys_ref.at[far, :], sem_out.at[slot])

    @pl.when(e == 0)
    def _():
        done_ref[0] = 0
        for slot in range(2):
            pend_ref[2 * slot] = -1

        @pl.when(nblk_ref[0] > 0)
        def _():
            in_copy(0, 0, 0).start()

    nb = nblk_ref[e]
    ns = pieces(e)
    g0 = done_ref[0]
    nxt = jnp.minimum(e + 1, last)
    has_next = (e < last) & (nblk_ref[nxt] > 0)

    def release(slot):
        pe = pend_ref[2 * slot]

        @pl.when(pe >= 0)
        def _():
            out_copy(pe, pend_ref[2 * slot + 1], slot).wait()

    @pl.when(nb > 0)
    def _():
        wgb_ref[...] = wg_ref[...].astype(BF16)
        wub_ref[...] = wu_ref[...].astype(BF16)
        wdb_ref[...] = wd_ref[...].astype(BF16)

        def piece(s, carry):
            slot = (g0 + s) % 2

            @pl.when(s + 1 < ns)
            def _():
                in_copy(e, s + 1, 1 - slot).start()

            @pl.when((s + 1 == ns) & has_next)
            def _():
                in_copy(nxt, 0, 1 - slot).start()

            in_copy(e, s, slot).wait()
            release(slot)

            blocks_here = jnp.minimum(per_piece, nb - s * per_piece)
            for nblocks in range(1, per_piece + 1):
                @pl.when(blocks_here == nblocks)
                def _():
                    rows = pl.ds(0, nblocks * EXPERT_ROWS)
                    xb = _unpack_bf16_pairs(xbuf[slot, rows, :]).astype(BF16)
                    hg = _dot(xb, wgb_ref[...])
                    hb = hg * _sigmoid(hg) * _dot(xb, wub_ref[...])
                    ybuf[slot, rows, :] = _pack_bf16_pairs(_dot(hb.astype(BF16), wdb_ref[...]))
            out_copy(e, s, slot).start()
            pend_ref[2 * slot] = e
            pend_ref[2 * slot + 1] = s
            return carry

        lax.fori_loop(0, ns, piece, 0)

    @pl.when((nb == 0) & has_next)
    def _():
        in_copy(nxt, 0, g0 % 2).start()

    @pl.when(e == last)
    def _():
        for slot in range(2):
            release(slot)

    done_ref[0] = g0 + ns


def _experts(xs, first_row, nblocks, e_gate, e_up, e_down):
    nrows, dh = xs.shape
    ne, d, ff = e_gate.shape
    wsel = lambda e, *_: (e, 0, 0)
    grid_spec = pltpu.PrefetchScalarGridSpec(
        num_scalar_prefetch=2,
        grid=(ne,),
        in_specs=[pl.BlockSpec(memory_space=pl.ANY),
                  pl.BlockSpec((None, d, ff), wsel), pl.BlockSpec((None, d, ff), wsel),
                  pl.BlockSpec((None, ff, d), wsel)],
        out_specs=pl.BlockSpec(memory_space=pl.ANY),
        scratch_shapes=[pltpu.VMEM((2, EXPERT_PIECE, dh), U32), pltpu.VMEM((2, EXPERT_PIECE, dh), U32),
                        pltpu.VMEM((d, ff), BF16), pltpu.VMEM((d, ff), BF16), pltpu.VMEM((ff, d), BF16),
                        pltpu.SemaphoreType.DMA((2,)), pltpu.SemaphoreType.DMA((2,)),
                        pltpu.SMEM((1,), I32), pltpu.SMEM((4,), I32)],
    )
    return pl.pallas_call(
        _expert_kernel,
        grid_spec=grid_spec,
        out_shape=jax.ShapeDtypeStruct((nrows, dh), U32),
        compiler_params=_cparams("arbitrary"),
        name="experts",
    )(first_row, nblocks, xs, e_gate, e_up, e_down)


def _final_kernel(ne, cnt_ref, loc_ref, gstart_ref, used_ref, dloc_ref, gate_ref, x_ref, xb_ref, ys_ref,
                  sg_ref, su_ref, sd_ref, ln_ref, o_ref, buf_ref, acc_ref, sems):
    step = pl.program_id(0)
    nsteps = pl.num_programs(0)
    slot = step % 2
    tm = x_ref.shape[0]
    used = used_ref[step]
    copies = functools.partial(_group_copies, cnt_ref, loc_ref, gstart_ref)

    @pl.when(step == 0)
    def _():
        copies(step, ne, buf_ref.at[0], ys_ref, sems.at[0], False)

    @pl.when(step + 1 < nsteps)
    def _():
        copies(step + 1, ne, buf_ref.at[1 - slot], ys_ref, sems.at[1 - slot], False)

    xb = xb_ref[...]
    hg = _dot(xb, sg_ref[...])
    hs = hg * _sigmoid(hg) * _dot(xb, su_ref[...])
    acc_ref[...] = _dot(hs.astype(BF16), sd_ref[...])

    _group_wait(used, buf_ref.at[slot], ys_ref, sems.at[slot], False)

    dl = dloc_ref[...].astype(I16)
    gt = gate_ref[...].astype(BF16)
    rows16 = lax.broadcasted_iota(I32, (COMBINE_ROWS, tm), 0).astype(I16)
    gts = [jnp.broadcast_to(gt[j:j + 1, :], (COMBINE_ROWS, tm)) for j in range(TOP_K)]
    riota = lax.broadcasted_iota(I32, (COMBINE_ROWS, buf_ref.shape[2]), 0)

    def chunk(c, carry):
        r0 = pl.multiple_of(c * COMBINE_ROWS, COMBINE_ROWS)
        r = rows16 + r0.astype(I16)
        g = jnp.zeros((COMBINE_ROWS, tm), BF16)
        for j in range(TOP_K):
            g = jnp.where(r == dl[j:j + 1, :], gts[j], g)
        w = buf_ref[slot, pl.ds(r0, COMBINE_ROWS), :]
        w = jnp.where(riota + r0 < used, w, jnp.zeros_like(w))
        y = _unpack_bf16_pairs(w).astype(BF16)
        acc_ref[...] += lax.dot_general(g, y, (((0,), (0,)), ((), ())), preferred_element_type=F32)
        return carry

    lax.fori_loop(0, (used + COMBINE_ROWS - 1) // COMBINE_ROWS, chunk, 0)
    o_ref[...] = _layer_norm(ALPHA * x_ref[...] + acc_ref[...], ln_ref[0:1, :], ln_ref[1:2, :])


def _final(x1, x1b, dloc, gate, cnt, loc, gstart, used_rows, ys, s_gate, s_up, s_down, ln_g, ln_b):
    t, d = x1.shape
    nt, ne = cnt.shape
    tm = MOE_TILE
    rt = _tile_rows(ne)
    ln = jnp.concatenate([ln_g[None], ln_b[None], jnp.zeros((SUBLANES - 2, d), F32)], axis=0)
    sg, su, sd = s_gate.astype(BF16), s_up.astype(BF16), s_down.astype(BF16)
    tok = pl.BlockSpec((tm, d), lambda i, *_: (i, 0))
    tokt = pl.BlockSpec((TOP_K, tm), lambda i, *_: (0, i))
    const = lambda a: pl.BlockSpec(a.shape, lambda i, *_: (0, 0))
    grid_spec = pltpu.PrefetchScalarGridSpec(
        num_scalar_prefetch=4,
        grid=(nt,),
        in_specs=[tokt, tokt, tok, tok, pl.BlockSpec(memory_space=pl.ANY), const(sg), const(su), const(sd), const(ln)],
        out_specs=tok,
        scratch_shapes=[pltpu.VMEM((2, rt, d // 2), U32), pltpu.VMEM((tm, d), F32), pltpu.SemaphoreType.DMA((2,))],
    )
    return pl.pallas_call(
        functools.partial(_final_kernel, ne),
        grid_spec=grid_spec,
        out_shape=jax.ShapeDtypeStruct((t, d), F32),
        compiler_params=_cparams("arbitrary"),
        name="final",
    )(cnt.reshape(-1), loc.reshape(-1), gstart.reshape(-1), used_rows, dloc, gate, x1, x1b, ys, sg, su, sd, ln)


def _moe(x1, x1b, router_w, router_bias, e_gate, e_up, e_down, s_gate, s_up, s_down, ln_g, ln_b):
    t = x1.shape[0]
    ne = router_w.shape[1]
    nt = t // MOE_TILE
    nblk = -(-(nt * _tile_rows(ne)) // EXPERT_ROWS) + ne
    nblk_pad = -(-nblk // LANES) * LANES
    dloc, gate, tab = _route(x1, router_w, router_bias)
    cnt, loc = tab[:, 0, :], tab[:, 1, :]
    used_rows = loc[:, ne - 1] + cnt[:, ne - 1]
    gstart, meta, emeta = _plan(cnt, nblk_pad)
    owner, used = meta[0], meta[1, 0:1]
    xs = _dispatch(x1b, dloc, cnt, loc, gstart, owner, used, used_rows, nblk * EXPERT_ROWS)
    ys = _experts(xs, emeta[0], emeta[1], e_gate, e_up, e_down)
    return _final(x1, x1b, dloc, gate, cnt, loc, gstart, used_rows, ys, s_gate, s_up, s_down, ln_g, ln_b)


def kernel(x, w_in, mu_shift, w0, w_decay_up, a0, w_aaa_up, w_gate_up, k_k, k_a, r_k, lnx_g, lnx_b, w_out,
           ln1_g, ln1_b, router_w, router_bias, e_gate, e_up, e_down, s_gate, s_up, s_down, ln2_g, ln2_b):
    bsz, seq, d = x.shape
    x2 = x.reshape(bsz * seq, d)
    qkvs, feats = _inproj(x2, seq, w_in[0], mu_shift[0], w0[0], w_decay_up[0], a0[0], w_aaa_up[0], w_gate_up[0],
                          k_k[0], k_a[0], r_k[0], tm=256)
    rw = _rwkv(feats, bsz, seq, lnx_g[0], lnx_b[0], nbat=4)
    x1, x1b = _mixer_tail(qkvs, rw, x2, bsz, seq, w_out[0], ln1_g[0], ln1_b[0])
    out = _moe(x1, x1b, router_w[0], router_bias[0], e_gate[0], e_up[0], e_down[0], s_gate[0], s_up[0], s_down[0],
               ln2_g[0], ln2_b[0])
    return out.reshape(bsz, seq, d)


def _mixer_tail(qkvs, rw, x2, bsz, seq, w_out, ln_g, ln_b):
    res = [_attention(q, k, v, bsz, seq, window, dilation)
           for (q, k, v), (window, dilation) in zip(qkvs, ATTN_PATTERNS)]
    dils = tuple(dl for _, dl in ATTN_PATTERNS)
    return _outproj([o for o, _ in res], [l for _, l in res], dils, rw, x2, w_out, ln_g, ln_b, tm=512)
```

```python
import functools

import jax
import jax.numpy as jnp
from jax import lax
from jax.experimental import pallas as pl
from jax.experimental.pallas import tpu as pltpu

F32 = jnp.float32
BF16 = jnp.bfloat16
I32 = jnp.int32
I16 = jnp.int16
U32 = jnp.uint32

LANES = 128
SUBLANES = 8
VMEM_LIMIT_BYTES = 56 * 1024 * 1024

HEAD_DIM = 64
ATTN_HEADS = 8
RWKV_HEADS = 8
ATTN_WIDTH = ATTN_HEADS * HEAD_DIM
RWKV_WIDTH = RWKV_HEADS * HEAD_DIM
ATTN_PATTERNS = ((128, 1), (512, 4), (2048, 16))
Q_BLOCK = 128
ROPE_THETA = 10000.0
DECAY_LORA = 64
AAA_LORA = 64
GATE_LORA = 160
GN_EPS = 64e-5
LN_EPS = 1e-5
TOP_K = 8
N_GROUPS = 8
TOPK_GROUPS = 4
ROUTED_SCALE = 2.5
DEPTH = 1
ALPHA = (2.0 * DEPTH) ** 0.25

RWKV_CHUNK = 64
PAIR = 2 * HEAD_DIM
LORA_PAD = 2 * LANES


def _cparams(*sem):
    return pltpu.CompilerParams(dimension_semantics=sem, vmem_limit_bytes=VMEM_LIMIT_BYTES)


def _split_bf16(a):
    hi = a.astype(BF16)
    lo = (a - hi.astype(F32)).astype(BF16)
    return hi, lo


def _dot(a, b):
    return jnp.dot(a, b, preferred_element_type=F32)


def _dot_nt(a, b):
    return lax.dot_general(a, b, (((1,), (1,)), ((), ())), preferred_element_type=F32)


def _softplus(z):
    return jnp.maximum(z, 0.0) + jnp.log(1.0 + jnp.exp(-jnp.abs(z)))


def _sigmoid(z):
    return 1.0 / (1.0 + jnp.exp(-z))


def _inproj_kernel(dils, x_ref, w_ref, cos_ref, sin_ref, mu_ref, wd_ref, wa_ref, wg_ref, vec_ref, *refs):
    nq = 3 * len(dils)
    qkv_refs = refs[0:nq]
    r_ref, ld_ref, kp_ref, vv_ref, kk_ref, b_ref, g_ref, bon_ref, carry_ref, qkv_scr = refs[nq:]
    s = pl.program_id(1)
    tm = x_ref.shape[0]
    aw = ATTN_WIDTH
    rw = RWKV_WIDTH

    @pl.when(s == 0)
    def _():
        carry_ref[...] = jnp.zeros_like(carry_ref)

    h = _dot(x_ref[...].astype(BF16), w_ref[...])
    reps = aw // cos_ref.shape[1]
    cos = jnp.concatenate([cos_ref[...]] * reps, axis=1)
    sin = jnp.concatenate([sin_ref[...]] * reps, axis=1)
    qkv = [(h[:, 0:aw] * cos + h[:, aw:2 * aw] * sin) * (HEAD_DIM ** -0.5),
           h[:, 2 * aw:3 * aw] * cos + h[:, 3 * aw:4 * aw] * sin,
           h[:, 4 * aw:5 * aw]]
    nslab = aw // LANES
    for i in range(3):
        for j in range(nslab):
            qkv_scr[i * nslab + j] = qkv[i][:, j * LANES:(j + 1) * LANES]
    for di, d in enumerate(dils):
        for i in range(3):
            o_ref = qkv_refs[3 * di + i]
            if d == 1:
                o_ref[...] = qkv[i].astype(o_ref.dtype)
                continue
            for res in range(d):
                for j in range(nslab):
                    rows = qkv_scr[i * nslab + j, pl.ds(res, tm // d, stride=d), :]
                    o_ref[:, res * aw + j * LANES:res * aw + (j + 1) * LANES] = rows.astype(o_ref.dtype)

    f = h[:, 5 * aw:]
    rows = lax.broadcasted_iota(I32, f.shape, 0)
    prev = jnp.where(rows == 0, carry_ref[SUBLANES - 1:SUBLANES, :], pltpu.roll(f, 1, axis=0))
    carry_ref[...] = f[tm - SUBLANES:tm, :]
    f = f + (prev - f) * mu_ref[...]

    r = f[:, 0:rw]
    k = f[:, rw:2 * rw]
    v = f[:, 2 * rw:3 * rw]
    la = f[:, 3 * rw:3 * rw + LANES]
    gl = f[:, 3 * rw + LANES:]
    w0, a0, k_k, k_a, r_k = (vec_ref[i:i + 1, :] for i in range(5))

    z = w0 + _dot(jnp.tanh(la).astype(BF16), wd_ref[...])
    w = -_softplus(-z) - 0.5
    ld_ref[...] = -jnp.exp(w)
    a = _sigmoid(a0 + _dot(la.astype(BF16), wa_ref[...]))
    g_ref[...] = _dot(_sigmoid(gl).astype(BF16), wg_ref[...])

    first = lax.broadcasted_iota(I32, (tm, PAIR), 1) < HEAD_DIM

    def head_sum(z):
        parts = []
        for p in range(rw // PAIR):
            zp = z[:, p * PAIR:(p + 1) * PAIR]
            s0 = jnp.sum(jnp.where(first, zp, 0.0), axis=1, keepdims=True)
            s1 = jnp.sum(jnp.where(first, 0.0, zp), axis=1, keepdims=True)
            parts.append(jnp.where(first, s0, s1))
        return jnp.concatenate(parts, axis=1)

    kk = k * k_k
    kk = kk / jnp.maximum(jnp.sqrt(head_sum(kk * kk)), 1e-12)
    kp = k * (1.0 + (a - 1.0) * k_a)
    r_ref[...] = r
    kp_ref[...] = kp
    vv_ref[...] = v
    kk_ref[...] = kk
    b_ref[...] = kk * a
    bon_ref[...] = head_sum(r * kp * r_k) * v


def _rot_half_cols(w):
    d, n = w.shape
    w4 = w.reshape(d, n // HEAD_DIM, 2, HEAD_DIM // 2)
    return jnp.stack([-w4[:, :, 1, :], w4[:, :, 0, :]], axis=2).reshape(d, n)


def _inproj(x2, seq, w_in, mu_shift, w0, w_decay_up, a0, w_aaa_up, w_gate_up, k_k, k_a, r_k, tm):
    t, d = x2.shape
    aw, rw = ATTN_WIDTH, RWKV_WIDTH
    wq, wk, wv = w_in[:, 0:aw], w_in[:, aw:2 * aw], w_in[:, 2 * aw:3 * aw]
    wf = w_in[:, 3 * aw:]
    gpad = LORA_PAD - GATE_LORA
    w_all = jnp.concatenate(
        [wq, _rot_half_cols(wq), wk, _rot_half_cols(wk), wv, wf, jnp.zeros((d, gpad), F32)], axis=1).astype(BF16)
    mu = jnp.concatenate([mu_shift, jnp.zeros((gpad,), F32)])[None, :]
    nf = mu.shape[1]
    wd = jnp.concatenate([w_decay_up, jnp.zeros((AAA_LORA, rw), F32)], axis=0).astype(BF16)
    wa = jnp.concatenate([jnp.zeros((DECAY_LORA, rw), F32), w_aaa_up], axis=0).astype(BF16)
    wg = jnp.concatenate([w_gate_up, jnp.zeros((gpad, rw), F32)], axis=0).astype(BF16)
    vec = jnp.stack([w0, a0, k_k, k_a, r_k.reshape(-1), w0 * 0, w0 * 0, w0 * 0])
    half = HEAD_DIM // 2
    inv_freq = ROPE_THETA ** (-jnp.arange(half, dtype=F32) * 2.0 / HEAD_DIM)
    ang = jnp.arange(seq, dtype=F32)[:, None] * inv_freq[None, :]
    cos = jnp.tile(jnp.cos(ang), (1, LANES // half))
    sin = jnp.tile(jnp.sin(ang), (1, LANES // half))

    nst = seq // tm
    tok = lambda b, s: (b * nst + s, 0)
    const = lambda b, s: (0, 0)
    full = lambda a: pl.BlockSpec(a.shape, const)
    dils = tuple(dl for _, dl in ATTN_PATTERNS)
    qkv_specs, qkv_shapes = [], []
    for dl in dils:
        qkv_specs += [pl.BlockSpec((tm // dl, dl * aw), tok)] * 3
        qkv_shapes += [jax.ShapeDtypeStruct((t // dl, dl * aw), BF16)] * 3
    out_f = jax.ShapeDtypeStruct((t, rw), F32)
    outs = pl.pallas_call(
        functools.partial(_inproj_kernel, dils),
        grid=(t // seq, nst),
        in_specs=[pl.BlockSpec((tm, d), tok), full(w_all),
                  pl.BlockSpec((tm, LANES), lambda b, s: (s, 0)), pl.BlockSpec((tm, LANES), lambda b, s: (s, 0)),
                  full(mu), full(wd), full(wa), full(wg), full(vec)],
        out_specs=qkv_specs + [pl.BlockSpec((tm, rw), tok)] * 8,
        out_shape=qkv_shapes + [out_f] * 8,
        scratch_shapes=[pltpu.VMEM((SUBLANES, nf), F32), pltpu.VMEM((3 * aw // LANES, tm, LANES), F32)],
        compiler_params=_cparams("arbitrary", "arbitrary"),
        name="inproj",
    )(x2, w_all, cos, sin, mu, wd, wa, wg, vec)
    nq = 3 * len(dils)
    return [outs[3 * i:3 * i + 3] for i in range(len(dils))], outs[nq:]


def _dotf(a, b, nt=False, passes=3):
    dot = _dot_nt if nt else _dot
    if passes == 1:
        return dot(a.astype(BF16), b.astype(BF16))
    ah, al = _split_bf16(a)
    if passes == 2:
        bh = b.astype(BF16)
        return dot(ah, bh) + dot(al, bh)
    bh, bl = _split_bf16(b)
    return dot(ah, bh) + (dot(ah, bl) + dot(al, bh))


RWKV_PASSES = dict(s8=1, inv=1, w1=1, au=1, ry=1, gh=1, yh=1)


def _dot3(a_bf16, b):
    b1 = b.astype(BF16)
    r1 = b - b1.astype(F32)
    b2 = r1.astype(BF16)
    b3 = (r1 - b2.astype(F32)).astype(BF16)
    return _dot(a_bf16, b1) + (_dot(a_bf16, b2) + _dot(a_bf16, b3))


def _rwkv_kernel(r_ref, ld_ref, kp_ref, v_ref, kk_ref, b_ref, g_ref, bon_ref, gn_ref, o_ref, h_ref):
    c = pl.program_id(1)
    nbat, ch, rw = r_ref.shape
    npairs = rw // PAIR

    @pl.when(c == 0)
    def _():
        h_ref[...] = jnp.zeros_like(h_ref)

    ri = lax.broadcasted_iota(I32, (ch, ch), 0)
    ci = lax.broadcasted_iota(I32, (ch, ch), 1)
    tril = (ri >= ci).astype(BF16)
    row = lax.broadcasted_iota(I32, (ch, PAIR), 0)
    col = lax.broadcasted_iota(I32, (ch, PAIR), 1)
    first = col < HEAD_DIM
    jj = col & (HEAD_DIM - 1)
    strict = jj < row
    incl = jj <= row
    eye = (jj == row).astype(F32)
    row2 = lax.broadcasted_iota(I32, (PAIR, PAIR), 0)
    col2 = lax.broadcasted_iota(I32, (PAIR, PAIR), 1)
    same_head = (row2 < HEAD_DIM) == (col2 < HEAD_DIM)
    diag2 = row2 == col2
    zeros_cp = jnp.zeros((ch, PAIR), F32)

    def bdiag(y):
        return jnp.concatenate([jnp.where(first, y, 0.0), jnp.where(first, 0.0, y)], axis=0)

    ps = RWKV_PASSES
    units = []
    for bi in range(nbat):
        ld = ld_ref[bi]
        cum = _dot3(tril, ld)
        tot = cum[ch - 1:ch, :]
        a_t = -kk_ref[bi] * jnp.exp(cum - ld)
        pinv = jnp.exp(-cum)
        b_t = b_ref[bi] * pinv
        k_t = kp_ref[bi] * pinv
        r_t = r_ref[bi] * jnp.exp(cum)
        pend = jnp.exp(tot - cum)
        b_end = b_ref[bi] * pend
        k_end = kp_ref[bi] * pend
        p_tot = jnp.exp(tot)
        v_all = v_ref[bi]
        for p in range(npairs):
            sl = slice(p * PAIR, (p + 1) * PAIR)
            units.append(dict(h=bi * npairs + p, a=a_t[:, sl], r=r_t[:, sl], b=b_t[:, sl], k=k_t[:, sl],
                              v=v_all[:, sl], be=b_end[:, sl], ke=k_end[:, sl], pt=p_tot[:, sl]))

    for u in units:
        u["s8"] = _dotf(jnp.concatenate([u["a"], u["r"]], axis=0),
                        jnp.concatenate([bdiag(u["b"]), bdiag(u["k"])], axis=0), nt=True, passes=ps["s8"])
    for u in units:
        s8 = u.pop("s8")
        u["l"] = jnp.where(strict, s8[0:ch, 0:PAIR], 0.0)
        u["ak"] = jnp.where(strict, s8[0:ch, PAIR:], 0.0)
        u["mrb"] = jnp.where(incl, s8[ch:, 0:PAIR], 0.0)
        u["mrk"] = jnp.where(incl, s8[ch:, PAIR:], 0.0)
        u["t"] = eye + u["l"]
    for u in units:
        u["lk"] = _dotf(u["l"], bdiag(u["l"]), passes=ps["inv"])
        u["w1"] = _dotf(u["ak"], bdiag(u["v"]), passes=ps["w1"])
    for _ in range(max(ch.bit_length() - 3, 0)):
        for u in units:
            u["both"] = _dotf(jnp.concatenate([u["t"], u["lk"]], axis=0), bdiag(u["lk"]), passes=ps["inv"])
        for u in units:
            both = u.pop("both")
            u["t"] = u["t"] + both[0:ch]
            u["lk"] = both[ch:]
    for u in units:
        u["t"] = u["t"] + _dotf(u["t"], bdiag(u["lk"]), passes=ps["inv"])
    for u in units:
        u["au"] = _dotf(u["t"], jnp.concatenate([bdiag(u["a"]), bdiag(u["w1"])], axis=1), passes=ps["au"])
    for u in units:
        a_hat, u_loc = u["au"][:, 0:PAIR], u["au"][:, PAIR:]
        rhs = jnp.concatenate([
            jnp.concatenate([bdiag(a_hat), bdiag(u_loc)], axis=1),
            jnp.concatenate([jnp.zeros((2 * ch, PAIR), F32), bdiag(u["v"])], axis=1)], axis=0)
        u["ry"] = _dotf(jnp.concatenate([u["mrb"], u["mrk"]], axis=1), rhs, passes=ps["ry"])
        bkt = jnp.concatenate([u["be"], u["ke"]], axis=0).T
        u["gh"] = _dotf(bkt, jnp.concatenate([u["au"], jnp.concatenate([zeros_cp, u["v"]], axis=1)], axis=0),
                        passes=ps["gh"])
    for u in units:
        r_hat = u["r"] + u["ry"][:, 0:PAIR]
        g_mat = jnp.where(same_head, u["gh"][:, 0:PAIR], 0.0) + jnp.where(diag2, u["pt"], 0.0)
        u["yh"] = _dotf(jnp.concatenate([r_hat, g_mat], axis=0), h_ref[u["h"]], passes=ps["yh"])
    for u in units:
        h_ref[u["h"]] = u["yh"][ch:] + jnp.where(same_head, u["gh"][:, PAIR:], 0.0)
        u["y"] = u["yh"][0:ch] + u["ry"][:, PAIR:]

    inv_n = 1.0 / HEAD_DIM

    def head_mean(z):
        s0 = jnp.sum(jnp.where(first, z, 0.0), axis=1, keepdims=True)
        s1 = jnp.sum(jnp.where(first, 0.0, z), axis=1, keepdims=True)
        return jnp.where(first, s0, s1) * inv_n

    for u in units:
        yc = u["y"] - head_mean(u["y"])
        u["yn"] = yc * lax.rsqrt(head_mean(yc * yc) + GN_EPS)
    for bi in range(nbat):
        yn = jnp.concatenate([u["yn"] for u in units[bi * npairs:(bi + 1) * npairs]], axis=1)
        o_ref[bi] = (yn * gn_ref[0:1, :] + gn_ref[1:2, :] + bon_ref[bi]) * g_ref[bi]


def _rwkv(feats, bsz, seq, lnx_g, lnx_b, nbat):
    t, rw = feats[0].shape
    ch = RWKV_CHUNK
    gn = jnp.concatenate([lnx_g[None], lnx_b[None], jnp.zeros((SUBLANES - 2, rw), F32)], axis=0)
    tok = pl.BlockSpec((nbat, ch, rw), lambda bb, c: (bb, c, 0))
    const = lambda a: pl.BlockSpec(a.shape, lambda bb, c: (0, 0))
    out = pl.pallas_call(
        _rwkv_kernel,
        grid=(bsz // nbat, seq // ch),
        in_specs=[tok] * 8 + [const(gn)],
        out_specs=tok,
        out_shape=jax.ShapeDtypeStruct((bsz, seq, rw), F32),
        scratch_shapes=[pltpu.VMEM((nbat * (rw // PAIR), PAIR, PAIR), F32)],
        compiler_params=_cparams("arbitrary", "arbitrary"),
        name="rwkv",
    )(*[f.reshape(bsz, seq, rw) for f in feats], gn)
    return out.reshape(t, rw)


NEG_BIG = -1e30


def _attn_kernel(n_back, q_ref, kc_ref, kp_ref, vc_ref, vp_ref, o_ref, lse_ref):
    n = pl.program_id(2)
    qb = kp_ref.shape[0]
    nq = q_ref.shape[0] // qb
    qi = lax.broadcasted_iota(I32, (qb, 2 * qb), 0)
    ki = lax.broadcasted_iota(I32, (qb, 2 * qb), 1)
    dist = qb + qi - ki
    band = (dist >= 0) & (dist <= n_back)
    has_prev = jnp.where(n > 0, 0, qb)
    valid = [band & (ki >= has_prev)] + [band] * (nq - 1)
    first = lax.broadcasted_iota(I32, (qb, PAIR), 1) < HEAD_DIM
    npairs = q_ref.shape[1] // PAIR
    heads = []
    for u in range(nq):
        rows = slice(u * qb, (u + 1) * qb)
        for p in range(npairs):
            sl = slice(p * PAIR, (p + 1) * PAIR)
            q2 = q_ref[rows, sl]
            keys = jnp.concatenate([kp_ref[:, sl], kc_ref[:, sl]], axis=0)
            k2 = keys[u * qb:(u + 2) * qb]
            for hh in range(2):
                keep = first if hh == 0 else jnp.logical_not(first)
                s = _dot_nt(jnp.where(keep, q2, jnp.zeros_like(q2)), k2)
                heads.append(dict(u=u, rows=rows, sl=sl, s=s))
    for h in heads:
        ok = valid[h["u"]]
        s = jnp.where(ok, h.pop("s"), NEG_BIG)
        m = jnp.max(s, axis=1, keepdims=True)
        pe = jnp.where(ok, jnp.exp(s - m), 0.0)
        l = jnp.sum(pe, axis=1, keepdims=True)
        h["pe"] = pe.astype(BF16)
        h["l"] = l
        h["lse"] = m + jnp.log(l)
    for h in heads:
        sl, u = h["sl"], h["u"]
        vals = jnp.concatenate([vp_ref[:, sl], vc_ref[:, sl]], axis=0)
        h["o"] = _dot(h.pop("pe"), vals[u * qb:(u + 2) * qb]) / h["l"]
    for i in range(0, len(heads), 2):
        h0, h1 = heads[i], heads[i + 1]
        o_ref[h0["rows"], h0["sl"]] = jnp.where(first, h0["o"], h1["o"])
        lse_ref[h0["rows"], h0["sl"]] = jnp.where(first, h0["lse"], h1["lse"])


ATTN_QBLOCKS = 4


def _attention(q, k, v, bsz, seq, window, dilation):
    aw = q.shape[1] // dilation
    nq = min(ATTN_QBLOCKS, seq // dilation // Q_BLOCK)
    nb = seq // dilation // (nq * Q_BLOCK)
    cur = pl.BlockSpec((nq * Q_BLOCK, aw), lambda b, r, n: (b * nb + n, r))
    prv = pl.BlockSpec((Q_BLOCK, aw), lambda b, r, n: (jnp.maximum((b * nb + n) * nq - 1, b * nb * nq), r))
    shp = jax.ShapeDtypeStruct(q.shape, F32)
    return pl.pallas_call(
        functools.partial(_attn_kernel, window // dilation),
        grid=(bsz, dilation, nb),
        in_specs=[cur, cur, prv, cur, prv],
        out_specs=[cur, cur],
        out_shape=[shp, shp],
        compiler_params=_cparams("arbitrary", "arbitrary", "arbitrary"),
        name=f"attn_d{dilation}",
    )(q, k, k, v, v)


def _layer_norm(y, g, b):
    mu = jnp.mean(y, axis=-1, keepdims=True)
    yc = y - mu
    var = jnp.mean(yc * yc, axis=-1, keepdims=True)
    return yc * lax.rsqrt(var + LN_EPS) * g + b


def _outproj_kernel(dils, *refs):
    npat = len(dils)
    o_refs = refs[0:npat]
    lse_refs = refs[npat:2 * npat]
    rw_ref, x_ref, wo_ref, ln_ref, y_ref, yb_ref, scr = refs[2 * npat:]
    tm, aw = rw_ref.shape

    def token_major(ref, d, slot):
        if d == 1:
            return ref[...]
        nslab = aw // LANES
        for res in range(d):
            for j in range(nslab):
                scr[slot * nslab + j, pl.ds(res, tm // d, stride=d), :] = (
                    ref[:, res * aw + j * LANES:res * aw + (j + 1) * LANES])
        return jnp.concatenate([scr[slot * nslab + j] for j in range(nslab)], axis=1)

    lses = [token_major(r, d, 2 * i) for i, (r, d) in enumerate(zip(lse_refs, dils))]
    outs = [token_major(r, d, 2 * i + 1) for i, (r, d) in enumerate(zip(o_refs, dils))]
    m = functools.reduce(jnp.maximum, lses)
    es = [jnp.exp(z - m) for z in lses]
    den = functools.reduce(lambda a, b: a + b, es)
    attn = functools.reduce(lambda a, b: a + b, [(e / den) * o for e, o in zip(es, outs)])
    mix = _dot(attn.astype(BF16), wo_ref[0:aw, :]) + _dot(rw_ref[...].astype(BF16), wo_ref[aw:, :])
    y = _layer_norm(ALPHA * x_ref[...] + mix, ln_ref[0:1, :], ln_ref[1:2, :])
    y_ref[...] = y
    yb_ref[...] = y.astype(yb_ref.dtype)


def _outproj(os_, lses, dils, rw, x2, w_out, ln_g, ln_b, tm):
    t, d = x2.shape
    aw = rw.shape[1]
    ln = jnp.concatenate([ln_g[None], ln_b[None], jnp.zeros((SUBLANES - 2, d), F32)], axis=0)
    wo = w_out.astype(BF16)
    tok = lambda w: pl.BlockSpec((tm, w), lambda i: (i, 0))
    view = [pl.BlockSpec((tm // dl, dl * aw), lambda i: (i, 0)) for dl in dils]
    const = lambda a: pl.BlockSpec(a.shape, lambda i: (0, 0))
    return pl.pallas_call(
        functools.partial(_outproj_kernel, dils),
        grid=(t // tm,),
        in_specs=view + view + [tok(aw), tok(d), const(wo), const(ln)],
        out_specs=[tok(d), tok(d)],
        out_shape=[jax.ShapeDtypeStruct((t, d), F32), jax.ShapeDtypeStruct((t, d), BF16)],
        scratch_shapes=[pltpu.VMEM((2 * len(dils) * aw // LANES, tm, LANES), F32)],
        compiler_params=_cparams("arbitrary"),
        name="outproj",
    )(*os_, *lses, rw, x2, wo, ln)


def _lanes(col_rep, n):
    return jnp.concatenate([col_rep] * (n // LANES), axis=1)


MOE_TILE = 512
ROW_CHUNK = 8
SORT_ROWS = 512
COMBINE_ROWS = 1024
EXPERT_ROWS = 256


def _tile_rows(ne):
    raw = MOE_TILE * TOP_K + ne * (ROW_CHUNK - 1)
    unit = max(SORT_ROWS, COMBINE_ROWS)
    return -(-raw // unit) * unit


def _route_kernel(x_ref, rwt_ref, bias_ref, tri_ref, ones_ref, low_ref, dloc_ref, gate_ref, tab_ref):
    ne = rwt_ref.shape[0]
    tm = x_ref.shape[0]
    gsz = ne // N_GROUPS
    ninf = -jnp.inf

    aff = _sigmoid(_dotf(rwt_ref[...], x_ref[...], nt=True))
    sel = aff + _lanes(bias_ref[...], tm)

    sel3 = sel.reshape(N_GROUPS, gsz, tm)
    rid = lax.broadcasted_iota(I32, sel3.shape, 1).astype(F32)
    m1 = jnp.max(sel3, axis=1, keepdims=True)
    i1 = jnp.min(jnp.where(sel3 == m1, rid, float(gsz)), axis=1, keepdims=True)
    m2 = jnp.max(jnp.where(rid == i1, ninf, sel3), axis=1, keepdims=True)
    gsc = (m1 + m2).reshape(N_GROUPS, tm)
    gid = lax.broadcasted_iota(I32, gsc.shape, 0).astype(F32)
    keep = jnp.zeros(gsc.shape, F32)
    for _ in range(TOPK_GROUPS):
        gm = jnp.max(gsc, axis=0, keepdims=True)
        gi = jnp.min(jnp.where(gsc == gm, gid, float(N_GROUPS)), axis=0, keepdims=True)
        hit = gid == gi
        keep = jnp.where(hit, 1.0, keep)
        gsc = jnp.where(hit, ninf, gsc)
    cand = jnp.where(keep.reshape(N_GROUPS, 1, tm) > 0.0, sel3, ninf).reshape(ne, tm)

    eid = lax.broadcasted_iota(I32, (ne, tm), 0).astype(F32)
    hits, graw = [], []
    for _ in range(TOP_K):
        m = jnp.max(cand, axis=0, keepdims=True)
        ij = jnp.min(jnp.where(cand == m, eid, float(ne)), axis=0, keepdims=True)
        hit = eid == ij
        hits.append(hit)
        graw.append(jnp.sum(jnp.where(hit, aff, 0.0), axis=0, keepdims=True))
        cand = jnp.where(hit, ninf, cand)
    gsum = functools.reduce(lambda a, b: a + b, graw)
    gate_ref[...] = jnp.concatenate([g / gsum * ROUTED_SCALE for g in graw], axis=0)

    onehot = functools.reduce(lambda a, b: a + b, [h.astype(F32) for h in hits]).astype(BF16)
    before = _dot(onehot, tri_ref[...])
    count = _dot(onehot, ones_ref[...])
    padded = jnp.ceil(count * (1.0 / ROW_CHUNK)) * ROW_CHUNK
    start = _dot3(low_ref[...], padded)
    row = _lanes(start, tm) + before
    dloc_ref[...] = jnp.concatenate(
        [jnp.sum(jnp.where(h, row, 0.0), axis=0, keepdims=True) for h in hits], axis=0).astype(I32)
    lane = lax.broadcasted_iota(I32, padded.shape, 1)
    both = jnp.where(lane == 0, padded, jnp.where(lane == 1, start, 0.0))
    tab_ref[...] = both.T[0:SUBLANES, :].astype(I32)


def _route(x1, router_w, router_bias):
    t, d = x1.shape
    ne = router_w.shape[1]
    tm = MOE_TILE
    rwt = router_w.T
    bias = jnp.broadcast_to(router_bias[:, None], (ne, LANES))
    pos = jnp.arange(tm)
    tri = (pos[:, None] < pos[None, :]).astype(BF16)
    ones = jnp.ones((tm, LANES), BF16)
    eid = jnp.arange(ne)
    low = (eid[:, None] > eid[None, :]).astype(BF16)
    const = lambda a: pl.BlockSpec(a.shape, lambda i: (0, 0))
    tokt = pl.BlockSpec((TOP_K, tm), lambda i: (0, i))
    return pl.pallas_call(
        _route_kernel,
        grid=(t // tm,),
        in_specs=[pl.BlockSpec((tm, d), lambda i: (i, 0)), const(rwt), const(bias), const(tri), const(ones),
                  const(low)],
        out_specs=[tokt, tokt, pl.BlockSpec((None, SUBLANES, ne), lambda i: (i, 0, 0))],
        out_shape=[jax.ShapeDtypeStruct((TOP_K, t), I32), jax.ShapeDtypeStruct((TOP_K, t), F32),
                   jax.ShapeDtypeStruct((t // tm, SUBLANES, ne), I32)],
        compiler_params=_cparams("arbitrary"),
        name="route",
    )(x1, rwt, bias, tri, ones, low)


def _pack_bf16_pairs(v, exact=False):
    half = v.shape[1] // 2
    u = pltpu.bitcast(v if exact else v.astype(BF16).astype(F32), U32)
    return (u[:, 0:half] >> 16) | (u[:, half:] & jnp.uint32(0xFFFF0000))


def _unpack_bf16_pairs(w):
    lo = pltpu.bitcast(w << 16, F32)
    hi = pltpu.bitcast(w & jnp.uint32(0xFFFF0000), F32)
    return jnp.concatenate([lo, hi], axis=1)


def _dot3l(a, b_bf16):
    a1 = a.astype(BF16)
    r1 = a - a1.astype(F32)
    a2 = r1.astype(BF16)
    a3 = (r1 - a2.astype(F32)).astype(BF16)
    return _dot(a1, b_bf16) + (_dot(a2, b_bf16) + _dot(a3, b_bf16))


def _plan_kernel(nblk, cnt_ref, low_ref, upper_ref, gstart_ref, meta_ref, emeta_ref):
    c = cnt_ref[...].astype(F32)
    nt, ne = c.shape
    earlier = _dot3(low_ref[...], c)
    total = jnp.sum(c, axis=0, keepdims=True)
    padded = jnp.ceil(total * (1.0 / EXPERT_ROWS)) * EXPERT_ROWS
    pad_end = _dot3l(jnp.broadcast_to(padded, (SUBLANES, ne)), upper_ref[...])
    pad_start = (pad_end - padded)[0:1, :]
    gstart_ref[...] = (pad_start + earlier).astype(I32)
    emeta_ref[...] = jnp.concatenate(
        [pad_start, padded * (1.0 / EXPERT_ROWS), jnp.zeros((SUBLANES - 2, ne), F32)], axis=0).astype(I32)
    end_col = jnp.broadcast_to(pad_end.T[:, 0:1], (ne, nblk))
    blk_start = (lax.broadcasted_iota(I32, (ne, nblk), 1) * EXPERT_ROWS).astype(F32)
    owner = jnp.minimum(jnp.sum((end_col <= blk_start).astype(F32), axis=0, keepdims=True), float(ne - 1))
    used = jnp.max(end_col, axis=0, keepdims=True) * (1.0 / EXPERT_ROWS)
    meta_ref[...] = jnp.concatenate([owner, used, jnp.zeros((SUBLANES - 2, nblk), F32)], axis=0).astype(I32)


def _plan(cnt, nblk_pad):
    nt, ne = cnt.shape
    ti = jnp.arange(nt)
    low = (ti[:, None] > ti[None, :]).astype(BF16)
    ei = jnp.arange(ne)
    upper = (ei[:, None] <= ei[None, :]).astype(BF16)
    return pl.pallas_call(
        functools.partial(_plan_kernel, nblk_pad),
        out_shape=[jax.ShapeDtypeStruct((nt, ne), I32), jax.ShapeDtypeStruct((SUBLANES, nblk_pad), I32),
                   jax.ShapeDtypeStruct((SUBLANES, ne), I32)],
        compiler_params=pltpu.CompilerParams(vmem_limit_bytes=VMEM_LIMIT_BYTES),
        name="plan",
    )(cnt, low, upper)


def _group_copies(cnt_ref, loc_ref, gstart_ref, tile, ne, local, remote, sem, to_remote):
    def one(e, carry):
        n = pl.multiple_of(cnt_ref[tile * ne + e], ROW_CHUNK)

        @pl.when(n > 0)
        def _():
            lo = pl.multiple_of(loc_ref[tile * ne + e], ROW_CHUNK)
            go = pl.multiple_of(gstart_ref[tile * ne + e], ROW_CHUNK)
            a, b = local.at[pl.ds(lo, n), :], remote.at[pl.ds(go, n), :]
            (pltpu.make_async_copy(a, b, sem) if to_remote else pltpu.make_async_copy(b, a, sem)).start()
        return carry

    lax.fori_loop(0, ne, one, 0, unroll=8)


def _group_wait(rows, local, remote, sem, to_remote):
    n = pl.multiple_of(rows, ROW_CHUNK)

    @pl.when(n > 0)
    def _():
        a, b = local.at[pl.ds(0, n), :], remote.at[pl.ds(0, n), :]
        (pltpu.make_async_copy(a, b, sem) if to_remote else pltpu.make_async_copy(b, a, sem)).wait()


def _dispatch_kernel(ne, cnt_ref, loc_ref, gstart_ref, owner_ref, used_ref, urows_ref, dloc_ref, x_ref, xs_ref,
                     buf_ref, zero_ref, sems, zsem):
    step = pl.program_id(0)
    nsteps = pl.num_programs(0)
    slot = step % 2
    tm = x_ref.shape[0]
    nblk = owner_ref.shape[0]
    used = used_ref[0]

    @pl.when(step == 0)
    def _():
        zero_ref[...] = jnp.zeros_like(zero_ref)

        def block_copy(i):
            return pltpu.make_async_copy(zero_ref, xs_ref.at[pl.ds(i * EXPERT_ROWS, EXPERT_ROWS), :], zsem)

        def is_last(i):
            return (i == used - 1) | (owner_ref[jnp.minimum(i + 1, nblk - 1)] != owner_ref[i])

        def start(i, carry):
            @pl.when(is_last(i))
            def _():
                block_copy(i).start()
            return carry

        def wait(i, carry):
            @pl.when(is_last(i))
            def _():
                block_copy(i).wait()
            return carry

        lax.fori_loop(0, used, start, 0)
        lax.fori_loop(0, used, wait, 0)

    x = x_ref[...]
    dl = dloc_ref[...].astype(I16)
    riota = lax.broadcasted_iota(I32, (SORT_ROWS, tm), 0).astype(I16)
    one = jnp.ones((SORT_ROWS, tm), BF16)

    def chunk(c, carry):
        r0 = pl.multiple_of(c * SORT_ROWS, SORT_ROWS)
        r = riota + r0.astype(I16)
        p = jnp.zeros((SORT_ROWS, tm), BF16)
        for j in range(TOP_K):
            p = jnp.where(r == dl[j:j + 1, :], one, p)
        buf_ref[slot, pl.ds(r0, SORT_ROWS), :] = _pack_bf16_pairs(_dot(p, x), exact=True)
        return carry

    lax.fori_loop(0, (urows_ref[step] + SORT_ROWS - 1) // SORT_ROWS, chunk, 0)

    copies = functools.partial(_group_copies, cnt_ref, loc_ref, gstart_ref)
    copies(step, ne, buf_ref.at[slot], xs_ref, sems.at[slot], True)

    @pl.when(step > 0)
    def _():
        _group_wait(urows_ref[jnp.maximum(step - 1, 0)], buf_ref.at[1 - slot], xs_ref, sems.at[1 - slot], True)

    @pl.when(step == nsteps - 1)
    def _():
        _group_wait(urows_ref[step], buf_ref.at[slot], xs_ref, sems.at[slot], True)


def _dispatch(x1b, dloc, cnt, loc, gstart, owner, used, used_rows, nrows):
    t, d = x1b.shape
    nt, ne = cnt.shape
    tm = MOE_TILE
    rt = _tile_rows(ne)
    grid_spec = pltpu.PrefetchScalarGridSpec(
        num_scalar_prefetch=6,
        grid=(nt,),
        in_specs=[pl.BlockSpec((TOP_K, tm), lambda i, *_: (0, i)), pl.BlockSpec((tm, d), lambda i, *_: (i, 0))],
        out_specs=pl.BlockSpec(memory_space=pl.ANY),
        scratch_shapes=[pltpu.VMEM((2, rt, d // 2), U32), pltpu.VMEM((EXPERT_ROWS, d // 2), U32),
                        pltpu.SemaphoreType.DMA((2,)), pltpu.SemaphoreType.DMA(())],
    )
    return pl.pallas_call(
        functools.partial(_dispatch_kernel, ne),
        grid_spec=grid_spec,
        out_shape=jax.ShapeDtypeStruct((nrows, d // 2), U32),
        compiler_params=_cparams("arbitrary"),
        name="dispatch",
    )(cnt.reshape(-1), loc.reshape(-1), gstart.reshape(-1), owner, used, used_rows, dloc, x1b)


EXPERT_PIECE = 1024


def _expert_kernel(first_ref, nblk_ref, xs_ref, wg_ref, wu_ref, wd_ref, ys_ref,
                   xbuf, ybuf, wgb_ref, wub_ref, wdb_ref, sem_in, sem_out, done_ref, pend_ref):
    e = pl.program_id(0)
    last = pl.num_programs(0) - 1
    per_piece = EXPERT_PIECE // EXPERT_ROWS

    def pieces(ex):
        return (nblk_ref[ex] + per_piece - 1) // per_piece

    def span(ex, s):
        n = jnp.minimum(EXPERT_PIECE, nblk_ref[ex] * EXPERT_ROWS - s * EXPERT_PIECE)
        n = pl.multiple_of(n, EXPERT_ROWS)
        return pl.ds(pl.multiple_of(first_ref[ex] + s * EXPERT_PIECE, EXPERT_ROWS), n), pl.ds(0, n)

    def in_copy(ex, s, slot):
        far, near = span(ex, s)
        return pltpu.make_async_copy(xs_ref.at[far, :], xbuf.at[slot, near, :], sem_in.at[slot])

    def out_copy(ex, s, slot):
        far, near = span(ex, s)
        return pltpu.make_async_copy(ybuf.at[slot, near, :], ys_ref.at[far, :], sem_out.at[slot])

    @pl.when(e == 0)
    def _():
        done_ref[0] = 0
        for slot in range(2):
            pend_ref[2 * slot] = -1

        @pl.when(nblk_ref[0] > 0)
        def _():
            in_copy(0, 0, 0).start()

    nb = nblk_ref[e]
    ns = pieces(e)
    g0 = done_ref[0]
    nxt = jnp.minimum(e + 1, last)
    has_next = (e < last) & (nblk_ref[nxt] > 0)

    def release(slot):
        pe = pend_ref[2 * slot]

        @pl.when(pe >= 0)
        def _():
            out_copy(pe, pend_ref[2 * slot + 1], slot).wait()

    @pl.when(nb > 0)
    def _():
        wgb_ref[...] = wg_ref[...].astype(BF16)
        wub_ref[...] = wu_ref[...].astype(BF16)
        wdb_ref[...] = wd_ref[...].astype(BF16)

        def piece(s, carry):
            slot = (g0 + s) % 2

            @pl.when(s + 1 < ns)
            def _():
                in_copy(e, s + 1, 1 - slot).start()

            @pl.when((s + 1 == ns) & has_next)
            def _():
                in_copy(nxt, 0, 1 - slot).start()

            in_copy(e, s, slot).wait()
            release(slot)

            blocks_here = jnp.minimum(per_piece, nb - s * per_piece)
            for nblocks in range(1, per_piece + 1):
                @pl.when(blocks_here == nblocks)
                def _():
                    rows = pl.ds(0, nblocks * EXPERT_ROWS)
                    xb = _unpack_bf16_pairs(xbuf[slot, rows, :]).astype(BF16)
                    hg = _dot(xb, wgb_ref[...])
                    hb = hg * _sigmoid(hg) * _dot(xb, wub_ref[...])
                    ybuf[slot, rows, :] = _pack_bf16_pairs(_dot(hb.astype(BF16), wdb_ref[...]))
            out_copy(e, s, slot).start()
            pend_ref[2 * slot] = e
            pend_ref[2 * slot + 1] = s
            return carry

        lax.fori_loop(0, ns, piece, 0)

    @pl.when((nb == 0) & has_next)
    def _():
        in_copy(nxt, 0, g0 % 2).start()

    @pl.when(e == last)
    def _():
        for slot in range(2):
            release(slot)

    done_ref[0] = g0 + ns


def _experts(xs, first_row, nblocks, e_gate, e_up, e_down):
    nrows, dh = xs.shape
    ne, d, ff = e_gate.shape
    wsel = lambda e, *_: (e, 0, 0)
    grid_spec = pltpu.PrefetchScalarGridSpec(
        num_scalar_prefetch=2,
        grid=(ne,),
        in_specs=[pl.BlockSpec(memory_space=pl.ANY),
                  pl.BlockSpec((None, d, ff), wsel), pl.BlockSpec((None, d, ff), wsel),
                  pl.BlockSpec((None, ff, d), wsel)],
        out_specs=pl.BlockSpec(memory_space=pl.ANY),
        scratch_shapes=[pltpu.VMEM((2, EXPERT_PIECE, dh), U32), pltpu.VMEM((2, EXPERT_PIECE, dh), U32),
                        pltpu.VMEM((d, ff), BF16), pltpu.VMEM((d, ff), BF16), pltpu.VMEM((ff, d), BF16),
                        pltpu.SemaphoreType.DMA((2,)), pltpu.SemaphoreType.DMA((2,)),
                        pltpu.SMEM((1,), I32), pltpu.SMEM((4,), I32)],
    )
    return pl.pallas_call(
        _expert_kernel,
        grid_spec=grid_spec,
        out_shape=jax.ShapeDtypeStruct((nrows, dh), U32),
        compiler_params=_cparams("arbitrary"),
        name="experts",
    )(first_row, nblocks, xs, e_gate, e_up, e_down)


def _final_kernel(ne, cnt_ref, loc_ref, gstart_ref, used_ref, dloc_ref, gate_ref, x_ref, xb_ref, ys_ref,
                  sg_ref, su_ref, sd_ref, ln_ref, o_ref, buf_ref, acc_ref, sems):
    step = pl.program_id(0)
    nsteps = pl.num_programs(0)
    slot = step % 2
    tm = x_ref.shape[0]
    used = used_ref[step]
    copies = functools.partial(_group_copies, cnt_ref, loc_ref, gstart_ref)

    @pl.when(step == 0)
    def _():
        copies(step, ne, buf_ref.at[0], ys_ref, sems.at[0], False)

    @pl.when(step + 1 < nsteps)
    def _():
        copies(step + 1, ne, buf_ref.at[1 - slot], ys_ref, sems.at[1 - slot], False)

    xb = xb_ref[...]
    hg = _dot(xb, sg_ref[...])
    hs = hg * _sigmoid(hg) * _dot(xb, su_ref[...])
    acc_ref[...] = _dot(hs.astype(BF16), sd_ref[...])

    _group_wait(used, buf_ref.at[slot], ys_ref, sems.at[slot], False)

    dl = dloc_ref[...].astype(I16)
    gt = gate_ref[...].astype(BF16)
    rows16 = lax.broadcasted_iota(I32, (COMBINE_ROWS, tm), 0).astype(I16)
    gts = [jnp.broadcast_to(gt[j:j + 1, :], (COMBINE_ROWS, tm)) for j in range(TOP_K)]
    riota = lax.broadcasted_iota(I32, (COMBINE_ROWS, buf_ref.shape[2]), 0)

    def chunk(c, carry):
        r0 = pl.multiple_of(c * COMBINE_ROWS, COMBINE_ROWS)
        r = rows16 + r0.astype(I16)
        g = jnp.zeros((COMBINE_ROWS, tm), BF16)
        for j in range(TOP_K):
            g = jnp.where(r == dl[j:j + 1, :], gts[j], g)
        w = buf_ref[slot, pl.ds(r0, COMBINE_ROWS), :]
        w = jnp.where(riota + r0 < used, w, jnp.zeros_like(w))
        y = _unpack_bf16_pairs(w).astype(BF16)
        acc_ref[...] += lax.dot_general(g, y, (((0,), (0,)), ((), ())), preferred_element_type=F32)
        return carry

    lax.fori_loop(0, (used + COMBINE_ROWS - 1) // COMBINE_ROWS, chunk, 0)
    o_ref[...] = _layer_norm(ALPHA * x_ref[...] + acc_ref[...], ln_ref[0:1, :], ln_ref[1:2, :])


def _final(x1, x1b, dloc, gate, cnt, loc, gstart, used_rows, ys, s_gate, s_up, s_down, ln_g, ln_b):
    t, d = x1.shape
    nt, ne = cnt.shape
    tm = MOE_TILE
    rt = _tile_rows(ne)
    ln = jnp.concatenate([ln_g[None], ln_b[None], jnp.zeros((SUBLANES - 2, d), F32)], axis=0)
    sg, su, sd = s_gate.astype(BF16), s_up.astype(BF16), s_down.astype(BF16)
    tok = pl.BlockSpec((tm, d), lambda i, *_: (i, 0))
    tokt = pl.BlockSpec((TOP_K, tm), lambda i, *_: (0, i))
    const = lambda a: pl.BlockSpec(a.shape, lambda i, *_: (0, 0))
    grid_spec = pltpu.PrefetchScalarGridSpec(
        num_scalar_prefetch=4,
        grid=(nt,),
        in_specs=[tokt, tokt, tok, tok, pl.BlockSpec(memory_space=pl.ANY), const(sg), const(su), const(sd), const(ln)],
        out_specs=tok,
        scratch_shapes=[pltpu.VMEM((2, rt, d // 2), U32), pltpu.VMEM((tm, d), F32), pltpu.SemaphoreType.DMA((2,))],
    )
    return pl.pallas_call(
        functools.partial(_final_kernel, ne),
        grid_spec=grid_spec,
        out_shape=jax.ShapeDtypeStruct((t, d), F32),
        compiler_params=_cparams("arbitrary"),
        name="final",
    )(cnt.reshape(-1), loc.reshape(-1), gstart.reshape(-1), used_rows, dloc, gate, x1, x1b, ys, sg, su, sd, ln)


def _moe(x1, x1b, router_w, router_bias, e_gate, e_up, e_down, s_gate, s_up, s_down, ln_g, ln_b):
    t = x1.shape[0]
    ne = router_w.shape[1]
    nt = t // MOE_TILE
    nblk = -(-(nt * _tile_rows(ne)) // EXPERT_ROWS) + ne
    nblk_pad = -(-nblk // LANES) * LANES
    dloc, gate, tab = _route(x1, router_w, router_bias)
    cnt, loc = tab[:, 0, :], tab[:, 1, :]
    used_rows = loc[:, ne - 1] + cnt[:, ne - 1]
    gstart, meta, emeta = _plan(cnt, nblk_pad)
    owner, used = meta[0], meta[1, 0:1]
    xs = _dispatch(x1b, dloc, cnt, loc, gstart, owner, used, used_rows, nblk * EXPERT_ROWS)
    ys = _experts(xs, emeta[0], emeta[1], e_gate, e_up, e_down)
    return _final(x1, x1b, dloc, gate, cnt, loc, gstart, used_rows, ys, s_gate, s_up, s_down, ln_g, ln_b)


def kernel(x, w_in, mu_shift, w0, w_decay_up, a0, w_aaa_up, w_gate_up, k_k, k_a, r_k, lnx_g, lnx_b, w_out,
           ln1_g, ln1_b, router_w, router_bias, e_gate, e_up, e_down, s_gate, s_up, s_down, ln2_g, ln2_b):
    bsz, seq, d = x.shape
    x2 = x.reshape(bsz * seq, d)
    qkvs, feats = _inproj(x2, seq, w_in[0], mu_shift[0], w0[0], w_decay_up[0], a0[0], w_aaa_up[0], w_gate_up[0],
                          k_k[0], k_a[0], r_k[0], tm=256)
    rw = _rwkv(feats, bsz, seq, lnx_g[0], lnx_b[0], nbat=4)
    x1, x1b = _mixer_tail(qkvs, rw, x2, bsz, seq, w_out[0], ln1_g[0], ln1_b[0])
    out = _moe(x1, x1b, router_w[0], router_bias[0], e_gate[0], e_up[0], e_down[0], s_gate[0], s_up[0], s_down[0],
               ln2_g[0], ln2_b[0])
    return out.reshape(bsz, seq, d)


def _mixer_tail(qkvs, rw, x2, bsz, seq, w_out, ln_g, ln_b):
    res = [_attention(q, k, v, bsz, seq, window, dilation)
           for (q, k, v), (window, dilation) in zip(qkvs, ATTN_PATTERNS)]
    dils = tuple(dl for _, dl in ATTN_PATTERNS)
    return _outproj([o for o, _ in res], [l for _, l in res], dils, rw, x2, w_out, ln_g, ln_b, tm=512)
```

```python
import functools

import jax
import jax.numpy as jnp
from jax import lax
from jax.experimental import pallas as pl
from jax.experimental.pallas import tpu as pltpu

F32 = jnp.float32
BF16 = jnp.bfloat16
I32 = jnp.int32
I16 = jnp.int16
U32 = jnp.uint32

LANES = 128
SUBLANES = 8
VMEM_LIMIT_BYTES = 56 * 1024 * 1024

HEAD_DIM = 64
ATTN_HEADS = 8
RWKV_HEADS = 8
ATTN_WIDTH = ATTN_HEADS * HEAD_DIM
RWKV_WIDTH = RWKV_HEADS * HEAD_DIM
ATTN_PATTERNS = ((128, 1), (512, 4), (2048, 16))
Q_BLOCK = 128
ROPE_THETA = 10000.0
DECAY_LORA = 64
AAA_LORA = 64
GATE_LORA = 160
GN_EPS = 64e-5
LN_EPS = 1e-5
TOP_K = 8
N_GROUPS = 8
TOPK_GROUPS = 4
ROUTED_SCALE = 2.5
DEPTH = 1
ALPHA = (2.0 * DEPTH) ** 0.25

RWKV_CHUNK = 64
PAIR = 2 * HEAD_DIM
LORA_PAD = 2 * LANES


def _cparams(*sem):
    return pltpu.CompilerParams(dimension_semantics=sem, vmem_limit_bytes=VMEM_LIMIT_BYTES)


def _split_bf16(a):
    hi = a.astype(BF16)
    lo = (a - hi.astype(F32)).astype(BF16)
    return hi, lo


def _dot(a, b):
    return jnp.dot(a, b, preferred_element_type=F32)


def _dot_nt(a, b):
    return lax.dot_general(a, b, (((1,), (1,)), ((), ())), preferred_element_type=F32)


def _softplus(z):
    return jnp.maximum(z, 0.0) + jnp.log(1.0 + jnp.exp(-jnp.abs(z)))


def _sigmoid(z):
    return 1.0 / (1.0 + jnp.exp(-z))


def _inproj_kernel(dils, x_ref, w_ref, cos_ref, sin_ref, mu_ref, wd_ref, wa_ref, wg_ref, vec_ref, *refs):
    nq = 3 * len(dils)
    qkv_refs = refs[0:nq]
    r_ref, ld_ref, kp_ref, vv_ref, kk_ref, b_ref, g_ref, bon_ref, carry_ref, qkv_scr = refs[nq:]
    s = pl.program_id(1)
    tm = x_ref.shape[0]
    aw = ATTN_WIDTH
    rw = RWKV_WIDTH

    @pl.when(s == 0)
    def _():
        carry_ref[...] = jnp.zeros_like(carry_ref)

    h = _dot(x_ref[...].astype(BF16), w_ref[...])
    reps = aw // cos_ref.shape[1]
    cos = jnp.concatenate([cos_ref[...]] * reps, axis=1)
    sin = jnp.concatenate([sin_ref[...]] * reps, axis=1)
    qkv = [(h[:, 0:aw] * cos + h[:, aw:2 * aw] * sin) * (HEAD_DIM ** -0.5),
           h[:, 2 * aw:3 * aw] * cos + h[:, 3 * aw:4 * aw] * sin,
           h[:, 4 * aw:5 * aw]]
    nslab = aw // LANES
    for i in range(3):
        for j in range(nslab):
            qkv_scr[i * nslab + j] = qkv[i][:, j * LANES:(j + 1) * LANES]
    for di, d in enumerate(dils):
        for i in range(3):
            o_ref = qkv_refs[3 * di + i]
            if d == 1:
                o_ref[...] = qkv[i].astype(o_ref.dtype)
                continue
            for res in range(d):
                for j in range(nslab):
                    rows = qkv_scr[i * nslab + j, pl.ds(res, tm // d, stride=d), :]
                    o_ref[:, res * aw + j * LANES:res * aw + (j + 1) * LANES] = rows.astype(o_ref.dtype)

    f = h[:, 5 * aw:]
    rows = lax.broadcasted_iota(I32, f.shape, 0)
    prev = jnp.where(rows == 0, carry_ref[SUBLANES - 1:SUBLANES, :], pltpu.roll(f, 1, axis=0))
    carry_ref[...] = f[tm - SUBLANES:tm, :]
    f = f + (prev - f) * mu_ref[...]

    r = f[:, 0:rw]
    k = f[:, rw:2 * rw]
    v = f[:, 2 * rw:3 * rw]
    la = f[:, 3 * rw:3 * rw + LANES]
    gl = f[:, 3 * rw + LANES:]
    w0, a0, k_k, k_a, r_k = (vec_ref[i:i + 1, :] for i in range(5))

    z = w0 + _dot(jnp.tanh(la).astype(BF16), wd_ref[...])
    w = -_softplus(-z) - 0.5
    ld_ref[...] = -jnp.exp(w)
    a = _sigmoid(a0 + _dot(la.astype(BF16), wa_ref[...]))
    g_ref[...] = _dot(_sigmoid(gl).astype(BF16), wg_ref[...])

    first = lax.broadcasted_iota(I32, (tm, PAIR), 1) < HEAD_DIM

    def head_sum(z):
        parts = []
        for p in range(rw // PAIR):
            zp = z[:, p * PAIR:(p + 1) * PAIR]
            s0 = jnp.sum(jnp.where(first, zp, 0.0), axis=1, keepdims=True)
            s1 = jnp.sum(jnp.where(first, 0.0, zp), axis=1, keepdims=True)
            parts.append(jnp.where(first, s0, s1))
        return jnp.concatenate(parts, axis=1)

    kk = k * k_k
    kk = kk / jnp.maximum(jnp.sqrt(head_sum(kk * kk)), 1e-12)
    kp = k * (1.0 + (a - 1.0) * k_a)
    r_ref[...] = r
    kp_ref[...] = kp
    vv_ref[...] = v
    kk_ref[...] = kk
    b_ref[...] = kk * a
    bon_ref[...] = head_sum(r * kp * r_k) * v


def _rot_half_cols(w):
    d, n = w.shape
    w4 = w.reshape(d, n // HEAD_DIM, 2, HEAD_DIM // 2)
    return jnp.stack([-w4[:, :, 1, :], w4[:, :, 0, :]], axis=2).reshape(d, n)


def _inproj(x2, seq, w_in, mu_shift, w0, w_decay_up, a0, w_aaa_up, w_gate_up, k_k, k_a, r_k, tm):
    t, d = x2.shape
    aw, rw = ATTN_WIDTH, RWKV_WIDTH
    wq, wk, wv = w_in[:, 0:aw], w_in[:, aw:2 * aw], w_in[:, 2 * aw:3 * aw]
    wf = w_in[:, 3 * aw:]
    gpad = LORA_PAD - GATE_LORA
    w_all = jnp.concatenate(
        [wq, _rot_half_cols(wq), wk, _rot_half_cols(wk), wv, wf, jnp.zeros((d, gpad), F32)], axis=1).astype(BF16)
    mu = jnp.concatenate([mu_shift, jnp.zeros((gpad,), F32)])[None, :]
    nf = mu.shape[1]
    wd = jnp.concatenate([w_decay_up, jnp.zeros((AAA_LORA, rw), F32)], axis=0).astype(BF16)
    wa = jnp.concatenate([jnp.zeros((DECAY_LORA, rw), F32), w_aaa_up], axis=0).astype(BF16)
    wg = jnp.concatenate([w_gate_up, jnp.zeros((gpad, rw), F32)], axis=0).astype(BF16)
    vec = jnp.stack([w0, a0, k_k, k_a, r_k.reshape(-1), w0 * 0, w0 * 0, w0 * 0])
    half = HEAD_DIM // 2
    inv_freq = ROPE_THETA ** (-jnp.arange(half, dtype=F32) * 2.0 / HEAD_DIM)
    ang = jnp.arange(seq, dtype=F32)[:, None] * inv_freq[None, :]
    cos = jnp.tile(jnp.cos(ang), (1, LANES // half))
    sin = jnp.tile(jnp.sin(ang), (1, LANES // half))

    nst = seq // tm
    tok = lambda b, s: (b * nst + s, 0)
    const = lambda b, s: (0, 0)
    full = lambda a: pl.BlockSpec(a.shape, const)
    dils = tuple(dl for _, dl in ATTN_PATTERNS)
    qkv_specs, qkv_shapes = [], []
    for dl in dils:
        qkv_specs += [pl.BlockSpec((tm // dl, dl * aw), tok)] * 3
        qkv_shapes += [jax.ShapeDtypeStruct((t // dl, dl * aw), BF16)] * 3
    out_f = jax.ShapeDtypeStruct((t, rw), F32)
    outs = pl.pallas_call(
        functools.partial(_inproj_kernel, dils),
        grid=(t // seq, nst),
        in_specs=[pl.BlockSpec((tm, d), tok), full(w_all),
                  pl.BlockSpec((tm, LANES), lambda b, s: (s, 0)), pl.BlockSpec((tm, LANES), lambda b, s: (s, 0)),
                  full(mu), full(wd), full(wa), full(wg), full(vec)],
        out_specs=qkv_specs + [pl.BlockSpec((tm, rw), tok)] * 8,
        out_shape=qkv_shapes + [out_f] * 8,
        scratch_shapes=[pltpu.VMEM((SUBLANES, nf), F32), pltpu.VMEM((3 * aw // LANES, tm, LANES), F32)],
        compiler_params=_cparams("arbitrary", "arbitrary"),
        name="inproj",
    )(x2, w_all, cos, sin, mu, wd, wa, wg, vec)
    nq = 3 * len(dils)
    return [outs[3 * i:3 * i + 3] for i in range(len(dils))], outs[nq:]


def _dotf(a, b, nt=False, passes=3):
    dot = _dot_nt if nt else _dot
    if passes == 1:
        return dot(a.astype(BF16), b.astype(BF16))
    ah, al = _split_bf16(a)
    if passes == 2:
        bh = b.astype(BF16)
        return dot(ah, bh) + dot(al, bh)
    bh, bl = _split_bf16(b)
    return dot(ah, bh) + (dot(ah, bl) + dot(al, bh))


RWKV_PASSES = dict(s8=1, inv=1, w1=1, au=1, ry=1, gh=1, yh=1)


def _dot3(a_bf16, b):
    b1 = b.astype(BF16)
    r1 = b - b1.astype(F32)
    b2 = r1.astype(BF16)
    b3 = (r1 - b2.astype(F32)).astype(BF16)
    return _dot(a_bf16, b1) + (_dot(a_bf16, b2) + _dot(a_bf16, b3))


def _rwkv_kernel(r_ref, ld_ref, kp_ref, v_ref, kk_ref, b_ref, g_ref, bon_ref, gn_ref, o_ref, h_ref):
    c = pl.program_id(1)
    nbat, ch, rw = r_ref.shape
    npairs = rw // PAIR

    @pl.when(c == 0)
    def _():
        h_ref[...] = jnp.zeros_like(h_ref)

    ri = lax.broadcasted_iota(I32, (ch, ch), 0)
    ci = lax.broadcasted_iota(I32, (ch, ch), 1)
    tril = (ri >= ci).astype(BF16)
    row = lax.broadcasted_iota(I32, (ch, PAIR), 0)
    col = lax.broadcasted_iota(I32, (ch, PAIR), 1)
    first = col < HEAD_DIM
    jj = col & (HEAD_DIM - 1)
    strict = jj < row
    incl = jj <= row
    eye = (jj == row).astype(F32)
    row2 = lax.broadcasted_iota(I32, (PAIR, PAIR), 0)
    col2 = lax.broadcasted_iota(I32, (PAIR, PAIR), 1)
    same_head = (row2 < HEAD_DIM) == (col2 < HEAD_DIM)
    diag2 = row2 == col2
    zeros_cp = jnp.zeros((ch, PAIR), F32)

    def bdiag(y):
        return jnp.concatenate([jnp.where(first, y, 0.0), jnp.where(first, 0.0, y)], axis=0)

    ps = RWKV_PASSES
    units = []
    for bi in range(nbat):
        ld = ld_ref[bi]
        cum = _dot3(tril, ld)
        tot = cum[ch - 1:ch, :]
        a_t = -kk_ref[bi] * jnp.exp(cum - ld)
        pinv = jnp.exp(-cum)
        b_t = b_ref[bi] * pinv
        k_t = kp_ref[bi] * pinv
        r_t = r_ref[bi] * jnp.exp(cum)
        pend = jnp.exp(tot - cum)
        b_end = b_ref[bi] * pend
        k_end = kp_ref[bi] * pend
        p_tot = jnp.exp(tot)
        v_all = v_ref[bi]
        for p in range(npairs):
            sl = slice(p * PAIR, (p + 1) * PAIR)
            units.append(dict(h=bi * npairs + p, a=a_t[:, sl], r=r_t[:, sl], b=b_t[:, sl], k=k_t[:, sl],
                              v=v_all[:, sl], be=b_end[:, sl], ke=k_end[:, sl], pt=p_tot[:, sl]))

    for u in units:
        u["s8"] = _dotf(jnp.concatenate([u["a"], u["r"]], axis=0),
                        jnp.concatenate([bdiag(u["b"]), bdiag(u["k"])], axis=0), nt=True, passes=ps["s8"])
    for u in units:
        s8 = u.pop("s8")
        u["l"] = jnp.where(strict, s8[0:ch, 0:PAIR], 0.0)
        u["ak"] = jnp.where(strict, s8[0:ch, PAIR:], 0.0)
        u["mrb"] = jnp.where(incl, s8[ch:, 0:PAIR], 0.0)
        u["mrk"] = jnp.where(incl, s8[ch:, PAIR:], 0.0)
        u["t"] = eye + u["l"]
    for u in units:
        u["lk"] = _dotf(u["l"], bdiag(u["l"]), passes=ps["inv"])
        u["w1"] = _dotf(u["ak"], bdiag(u["v"]), passes=ps["w1"])
    for _ in range(max(ch.bit_length() - 3, 0)):
        for u in units:
            u["both"] = _dotf(jnp.concatenate([u["t"], u["lk"]], axis=0), bdiag(u["lk"]), passes=ps["inv"])
        for u in units:
            both = u.pop("both")
            u["t"] = u["t"] + both[0:ch]
            u["lk"] = both[ch:]
    for u in units:
        u["t"] = u["t"] + _dotf(u["t"], bdiag(u["lk"]), passes=ps["inv"])
    for u in units:
        u["au"] = _dotf(u["t"], jnp.concatenate([bdiag(u["a"]), bdiag(u["w1"])], axis=1), passes=ps["au"])
    for u in units:
        a_hat, u_loc = u["au"][:, 0:PAIR], u["au"][:, PAIR:]
        rhs = jnp.concatenate([
            jnp.concatenate([bdiag(a_hat), bdiag(u_loc)], axis=1),
            jnp.concatenate([jnp.zeros((2 * ch, PAIR), F32), bdiag(u["v"])], axis=1)], axis=0)
        u["ry"] = _dotf(jnp.concatenate([u["mrb"], u["mrk"]], axis=1), rhs, passes=ps["ry"])
        bkt = jnp.concatenate([u["be"], u["ke"]], axis=0).T
        u["gh"] = _dotf(bkt, jnp.concatenate([u["au"], jnp.concatenate([zeros_cp, u["v"]], axis=1)], axis=0),
                        passes=ps["gh"])
    for u in units:
        r_hat = u["r"] + u["ry"][:, 0:PAIR]
        g_mat = jnp.where(same_head, u["gh"][:, 0:PAIR], 0.0) + jnp.where(diag2, u["pt"], 0.0)
        u["yh"] = _dotf(jnp.concatenate([r_hat, g_mat], axis=0), h_ref[u["h"]], passes=ps["yh"])
    for u in units:
        h_ref[u["h"]] = u["yh"][ch:] + jnp.where(same_head, u["gh"][:, PAIR:], 0.0)
        u["y"] = u["yh"][0:ch] + u["ry"][:, PAIR:]

    inv_n = 1.0 / HEAD_DIM

    def head_mean(z):
        s0 = jnp.sum(jnp.where(first, z, 0.0), axis=1, keepdims=True)
        s1 = jnp.sum(jnp.where(first, 0.0, z), axis=1, keepdims=True)
        return jnp.where(first, s0, s1) * inv_n

    for u in units:
        yc = u["y"] - head_mean(u["y"])
        u["yn"] = yc * lax.rsqrt(head_mean(yc * yc) + GN_EPS)
    for bi in range(nbat):
        yn = jnp.concatenate([u["yn"] for u in units[bi * npairs:(bi + 1) * npairs]], axis=1)
        o_ref[bi] = (yn * gn_ref[0:1, :] + gn_ref[1:2, :] + bon_ref[bi]) * g_ref[bi]


def _rwkv(feats, bsz, seq, lnx_g, lnx_b, nbat):
    t, rw = feats[0].shape
    ch = RWKV_CHUNK
    gn = jnp.concatenate([lnx_g[None], lnx_b[None], jnp.zeros((SUBLANES - 2, rw), F32)], axis=0)
    tok = pl.BlockSpec((nbat, ch, rw), lambda bb, c: (bb, c, 0))
    const = lambda a: pl.BlockSpec(a.shape, lambda bb, c: (0, 0))
    out = pl.pallas_call(
        _rwkv_kernel,
        grid=(bsz // nbat, seq // ch),
        in_specs=[tok] * 8 + [const(gn)],
        out_specs=tok,
        out_shape=jax.ShapeDtypeStruct((bsz, seq, rw), F32),
        scratch_shapes=[pltpu.VMEM((nbat * (rw // PAIR), PAIR, PAIR), F32)],
        compiler_params=_cparams("arbitrary", "arbitrary"),
        name="rwkv",
    )(*[f.reshape(bsz, seq, rw) for f in feats], gn)
    return out.reshape(t, rw)


NEG_BIG = -1e30


def _attn_kernel(n_back, q_ref, kc_ref, kp_ref, vc_ref, vp_ref, o_ref, lse_ref):
    n = pl.program_id(2)
    qb = kp_ref.shape[0]
    nq = q_ref.shape[0] // qb
    qi = lax.broadcasted_iota(I32, (qb, 2 * qb), 0)
    ki = lax.broadcasted_iota(I32, (qb, 2 * qb), 1)
    dist = qb + qi - ki
    band = (dist >= 0) & (dist <= n_back)
    has_prev = jnp.where(n > 0, 0, qb)
    valid = [band & (ki >= has_prev)] + [band] * (nq - 1)
    first = lax.broadcasted_iota(I32, (qb, PAIR), 1) < HEAD_DIM
    npairs = q_ref.shape[1] // PAIR
    heads = []
    for u in range(nq):
        rows = slice(u * qb, (u + 1) * qb)
        for p in range(npairs):
            sl = slice(p * PAIR, (p + 1) * PAIR)
            q2 = q_ref[rows, sl]
            keys = jnp.concatenate([kp_ref[:, sl], kc_ref[:, sl]], axis=0)
            k2 = keys[u * qb:(u + 2) * qb]
            for hh in range(2):
                keep = first if hh == 0 else jnp.logical_not(first)
                s = _dot_nt(jnp.where(keep, q2, jnp.zeros_like(q2)), k2)
                heads.append(dict(u=u, rows=rows, sl=sl, s=s))
    for h in heads:
        ok = valid[h["u"]]
        s = jnp.where(ok, h.pop("s"), NEG_BIG)
        m = jnp.max(s, axis=1, keepdims=True)
        pe = jnp.where(ok, jnp.exp(s - m), 0.0)
        l = jnp.sum(pe, axis=1, keepdims=True)
        h["pe"] = pe.astype(BF16)
        h["l"] = l
        h["lse"] = m + jnp.log(l)
    for h in heads:
        sl, u = h["sl"], h["u"]
        vals = jnp.concatenate([vp_ref[:, sl], vc_ref[:, sl]], axis=0)
        h["o"] = _dot(h.pop("pe"), vals[u * qb:(u + 2) * qb]) / h["l"]
    for i in range(0, len(heads), 2):
        h0, h1 = heads[i], heads[i + 1]
        o_ref[h0["rows"], h0["sl"]] = jnp.where(first, h0["o"], h1["o"])
        lse_ref[h0["rows"], h0["sl"]] = jnp.where(first, h0["lse"], h1["lse"])


ATTN_QBLOCKS = 4


def _attention(q, k, v, bsz, seq, window, dilation):
    aw = q.shape[1] // dilation
    nq = min(ATTN_QBLOCKS, seq // dilation // Q_BLOCK)
    nb = seq // dilation // (nq * Q_BLOCK)
    cur = pl.BlockSpec((nq * Q_BLOCK, aw), lambda b, r, n: (b * nb + n, r))
    prv = pl.BlockSpec((Q_BLOCK, aw), lambda b, r, n: (jnp.maximum((b * nb + n) * nq - 1, b * nb * nq), r))
    shp = jax.ShapeDtypeStruct(q.shape, F32)
    return pl.pallas_call(
        functools.partial(_attn_kernel, window // dilation),
        grid=(bsz, dilation, nb),
        in_specs=[cur, cur, prv, cur, prv],
        out_specs=[cur, cur],
        out_shape=[shp, shp],
        compiler_params=_cparams("arbitrary", "arbitrary", "arbitrary"),
        name=f"attn_d{dilation}",
    )(q, k, k, v, v)


def _layer_norm(y, g, b):
    mu = jnp.mean(y, axis=-1, keepdims=True)
    yc = y - mu
    var = jnp.mean(yc * yc, axis=-1, keepdims=True)
    return yc * lax.rsqrt(var + LN_EPS) * g + b


def _outproj_kernel(dils, *refs):
    npat = len(dils)
    o_refs = refs[0:npat]
    lse_refs = refs[npat:2 * npat]
    rw_ref, x_ref, wo_ref, ln_ref, y_ref, yb_ref, scr = refs[2 * npat:]
    tm, aw = rw_ref.shape

    def token_major(ref, d, slot):
        if d == 1:
            return ref[...]
        nslab = aw // LANES
        for res in range(d):
            for j in range(nslab):
                scr[slot * nslab + j, pl.ds(res, tm // d, stride=d), :] = (
                    ref[:, res * aw + j * LANES:res * aw + (j + 1) * LANES])
        return jnp.concatenate([scr[slot * nslab + j] for j in range(nslab)], axis=1)

    lses = [token_major(r, d, 2 * i) for i, (r, d) in enumerate(zip(lse_refs, dils))]
    outs = [token_major(r, d, 2 * i + 1) for i, (r, d) in enumerate(zip(o_refs, dils))]
    m = functools.reduce(jnp.maximum, lses)
    es = [jnp.exp(z - m) for z in lses]
    den = functools.reduce(lambda a, b: a + b, es)
    attn = functools.reduce(lambda a, b: a + b, [(e / den) * o for e, o in zip(es, outs)])
    mix = _dot(attn.astype(BF16), wo_ref[0:aw, :]) + _dot(rw_ref[...].astype(BF16), wo_ref[aw:, :])
    y = _layer_norm(ALPHA * x_ref[...] + mix, ln_ref[0:1, :], ln_ref[1:2, :])
    y_ref[...] = y
    yb_ref[...] = y.astype(yb_ref.dtype)


def _outproj(os_, lses, dils, rw, x2, w_out, ln_g, ln_b, tm):
    t, d = x2.shape
    aw = rw.shape[1]
    ln = jnp.concatenate([ln_g[None], ln_b[None], jnp.zeros((SUBLANES - 2, d), F32)], axis=0)
    wo = w_out.astype(BF16)
    tok = lambda w: pl.BlockSpec((tm, w), lambda i: (i, 0))
    view = [pl.BlockSpec((tm // dl, dl * aw), lambda i: (i, 0)) for dl in dils]
    const = lambda a: pl.BlockSpec(a.shape, lambda i: (0, 0))
    return pl.pallas_call(
        functools.partial(_outproj_kernel, dils),
        grid=(t // tm,),
        in_specs=view + view + [tok(aw), tok(d), const(wo), const(ln)],
        out_specs=[tok(d), tok(d)],
        out_shape=[jax.ShapeDtypeStruct((t, d), F32), jax.ShapeDtypeStruct((t, d), BF16)],
        scratch_shapes=[pltpu.VMEM((2 * len(dils) * aw // LANES, tm, LANES), F32)],
        compiler_params=_cparams("arbitrary"),
        name="outproj",
    )(*os_, *lses, rw, x2, wo, ln)


def _lanes(col_rep, n):
    return jnp.concatenate([col_rep] * (n // LANES), axis=1)


MOE_TILE = 512
ROW_CHUNK = 8
SORT_ROWS = 512
COMBINE_ROWS = 1024
EXPERT_ROWS = 128


def _tile_rows(ne):
    raw = MOE_TILE * TOP_K + ne * (ROW_CHUNK - 1)
    unit = max(SORT_ROWS, COMBINE_ROWS)
    return -(-raw // unit) * unit


def _route_kernel(x_ref, rwt_ref, bias_ref, tri_ref, ones_ref, low_ref, dloc_ref, gate_ref, tab_ref):
    ne = rwt_ref.shape[0]
    tm = x_ref.shape[0]
    gsz = ne // N_GROUPS
    ninf = -jnp.inf

    aff = _sigmoid(_dotf(rwt_ref[...], x_ref[...], nt=True))
    sel = aff + _lanes(bias_ref[...], tm)

    sel3 = sel.reshape(N_GROUPS, gsz, tm)
    rid = lax.broadcasted_iota(I32, sel3.shape, 1).astype(F32)
    m1 = jnp.max(sel3, axis=1, keepdims=True)
    i1 = jnp.min(jnp.where(sel3 == m1, rid, float(gsz)), axis=1, keepdims=True)
    m2 = jnp.max(jnp.where(rid == i1, ninf, sel3), axis=1, keepdims=True)
    gsc = (m1 + m2).reshape(N_GROUPS, tm)
    gid = lax.broadcasted_iota(I32, gsc.shape, 0).astype(F32)
    keep = jnp.zeros(gsc.shape, F32)
    for _ in range(TOPK_GROUPS):
        gm = jnp.max(gsc, axis=0, keepdims=True)
        gi = jnp.min(jnp.where(gsc == gm, gid, float(N_GROUPS)), axis=0, keepdims=True)
        hit = gid == gi
        keep = jnp.where(hit, 1.0, keep)
        gsc = jnp.where(hit, ninf, gsc)
    cand = jnp.where(keep.reshape(N_GROUPS, 1, tm) > 0.0, sel3, ninf).reshape(ne, tm)

    eid = lax.broadcasted_iota(I32, (ne, tm), 0).astype(F32)
    hits, graw = [], []
    for _ in range(TOP_K):
        m = jnp.max(cand, axis=0, keepdims=True)
        ij = jnp.min(jnp.where(cand == m, eid, float(ne)), axis=0, keepdims=True)
        hit = eid == ij
        hits.append(hit)
        graw.append(jnp.sum(jnp.where(hit, aff, 0.0), axis=0, keepdims=True))
        cand = jnp.where(hit, ninf, cand)
    gsum = functools.reduce(lambda a, b: a + b, graw)
    gate_ref[...] = jnp.concatenate([g / gsum * ROUTED_SCALE for g in graw], axis=0)

    onehot = functools.reduce(lambda a, b: a + b, [h.astype(F32) for h in hits]).astype(BF16)
    before = _dot(onehot, tri_ref[...])
    count = _dot(onehot, ones_ref[...])
    padded = jnp.ceil(count * (1.0 / ROW_CHUNK)) * ROW_CHUNK
    start = _dot3(low_ref[...], padded)
    row = _lanes(start, tm) + before
    dloc_ref[...] = jnp.concatenate(
        [jnp.sum(jnp.where(h, row, 0.0), axis=0, keepdims=True) for h in hits], axis=0).astype(I32)
    lane = lax.broadcasted_iota(I32, padded.shape, 1)
    both = jnp.where(lane == 0, padded, jnp.where(lane == 1, start, 0.0))
    tab_ref[...] = both.T[0:SUBLANES, :].astype(I32)


def _route(x1, router_w, router_bias):
    t, d = x1.shape
    ne = router_w.shape[1]
    tm = MOE_TILE
    rwt = router_w.T
    bias = jnp.broadcast_to(router_bias[:, None], (ne, LANES))
    pos = jnp.arange(tm)
    tri = (pos[:, None] < pos[None, :]).astype(BF16)
    ones = jnp.ones((tm, LANES), BF16)
    eid = jnp.arange(ne)
    low = (eid[:, None] > eid[None, :]).astype(BF16)
    const = lambda a: pl.BlockSpec(a.shape, lambda i: (0, 0))
    tokt = pl.BlockSpec((TOP_K, tm), lambda i: (0, i))
    return pl.pallas_call(
        _route_kernel,
        grid=(t // tm,),
        in_specs=[pl.BlockSpec((tm, d), lambda i: (i, 0)), const(rwt), const(bias), const(tri), const(ones),
                  const(low)],
        out_specs=[tokt, tokt, pl.BlockSpec((None, SUBLANES, ne), lambda i: (i, 0, 0))],
        out_shape=[jax.ShapeDtypeStruct((TOP_K, t), I32), jax.ShapeDtypeStruct((TOP_K, t), F32),
                   jax.ShapeDtypeStruct((t // tm, SUBLANES, ne), I32)],
        compiler_params=_cparams("arbitrary"),
        name="route",
    )(x1, rwt, bias, tri, ones, low)


def _pack_bf16_pairs(v, exact=False):
    half = v.shape[1] // 2
    u = pltpu.bitcast(v if exact else v.astype(BF16).astype(F32), U32)
    return (u[:, 0:half] >> 16) | (u[:, half:] & jnp.uint32(0xFFFF0000))


def _unpack_bf16_pairs(w):
    lo = pltpu.bitcast(w << 16, F32)
    hi = pltpu.bitcast(w & jnp.uint32(0xFFFF0000), F32)
    return jnp.concatenate([lo, hi], axis=1)


def _dot3l(a, b_bf16):
    a1 = a.astype(BF16)
    r1 = a - a1.astype(F32)
    a2 = r1.astype(BF16)
    a3 = (r1 - a2.astype(F32)).astype(BF16)
    return _dot(a1, b_bf16) + (_dot(a2, b_bf16) + _dot(a3, b_bf16))


def _plan_kernel(nblk, cnt_ref, low_ref, upper_ref, gstart_ref, meta_ref, emeta_ref):
    c = cnt_ref[...].astype(F32)
    nt, ne = c.shape
    earlier = _dot3(low_ref[...], c)
    total = jnp.sum(c, axis=0, keepdims=True)
    padded = jnp.ceil(total * (1.0 / EXPERT_ROWS)) * EXPERT_ROWS
    pad_end = _dot3l(jnp.broadcast_to(padded, (SUBLANES, ne)), upper_ref[...])
    pad_start = (pad_end - padded)[0:1, :]
    gstart_ref[...] = (pad_start + earlier).astype(I32)
    emeta_ref[...] = jnp.concatenate(
        [pad_start, padded * (1.0 / EXPERT_ROWS), jnp.zeros((SUBLANES - 2, ne), F32)], axis=0).astype(I32)
    end_col = jnp.broadcast_to(pad_end.T[:, 0:1], (ne, nblk))
    blk_start = (lax.broadcasted_iota(I32, (ne, nblk), 1) * EXPERT_ROWS).astype(F32)
    owner = jnp.minimum(jnp.sum((end_col <= blk_start).astype(F32), axis=0, keepdims=True), float(ne - 1))
    used = jnp.max(end_col, axis=0, keepdims=True) * (1.0 / EXPERT_ROWS)
    meta_ref[...] = jnp.concatenate([owner, used, jnp.zeros((SUBLANES - 2, nblk), F32)], axis=0).astype(I32)


def _plan(cnt, nblk_pad):
    nt, ne = cnt.shape
    ti = jnp.arange(nt)
    low = (ti[:, None] > ti[None, :]).astype(BF16)
    ei = jnp.arange(ne)
    upper = (ei[:, None] <= ei[None, :]).astype(BF16)
    return pl.pallas_call(
        functools.partial(_plan_kernel, nblk_pad),
        out_shape=[jax.ShapeDtypeStruct((nt, ne), I32), jax.ShapeDtypeStruct((SUBLANES, nblk_pad), I32),
                   jax.ShapeDtypeStruct((SUBLANES, ne), I32)],
        compiler_params=pltpu.CompilerParams(vmem_limit_bytes=VMEM_LIMIT_BYTES),
        name="plan",
    )(cnt, low, upper)


def _group_copies(cnt_ref, loc_ref, gstart_ref, tile, ne, local, remote, sem, to_remote):
    def one(e, carry):
        n = pl.multiple_of(cnt_ref[tile * ne + e], ROW_CHUNK)

        @pl.when(n > 0)
        def _():
            lo = pl.multiple_of(loc_ref[tile * ne + e], ROW_CHUNK)
            go = pl.multiple_of(gstart_ref[tile * ne + e], ROW_CHUNK)
            a, b = local.at[pl.ds(lo, n), :], remote.at[pl.ds(go, n), :]
            (pltpu.make_async_copy(a, b, sem) if to_remote else pltpu.make_async_copy(b, a, sem)).start()
        return carry

    lax.fori_loop(0, ne, one, 0, unroll=8)


def _group_wait(rows, local, remote, sem, to_remote):
    n = pl.multiple_of(rows, ROW_CHUNK)

    @pl.when(n > 0)
    def _():
        a, b = local.at[pl.ds(0, n), :], remote.at[pl.ds(0, n), :]
        (pltpu.make_async_copy(a, b, sem) if to_remote else pltpu.make_async_copy(b, a, sem)).wait()


def _dispatch_kernel(ne, cnt_ref, loc_ref, gstart_ref, owner_ref, used_ref, urows_ref, dloc_ref, x_ref, xs_ref,
                     buf_ref, zero_ref, sems, zsem):
    step = pl.program_id(0)
    nsteps = pl.num_programs(0)
    slot = step % 2
    tm = x_ref.shape[0]
    nblk = owner_ref.shape[0]
    used = used_ref[0]

    @pl.when(step == 0)
    def _():
        zero_ref[...] = jnp.zeros_like(zero_ref)

        def block_copy(i):
            return pltpu.make_async_copy(zero_ref, xs_ref.at[pl.ds(i * EXPERT_ROWS, EXPERT_ROWS), :], zsem)

        def is_last(i):
            return (i == used - 1) | (owner_ref[jnp.minimum(i + 1, nblk - 1)] != owner_ref[i])

        def start(i, carry):
            @pl.when(is_last(i))
            def _():
                block_copy(i).start()
            return carry

        def wait(i, carry):
            @pl.when(is_last(i))
            def _():
                block_copy(i).wait()
            return carry

        lax.fori_loop(0, used, start, 0)
        lax.fori_loop(0, used, wait, 0)

    x = x_ref[...]
    dl = dloc_ref[...].astype(I16)
    riota = lax.broadcasted_iota(I32, (SORT_ROWS, tm), 0).astype(I16)
    one = jnp.ones((SORT_ROWS, tm), BF16)

    def chunk(c, carry):
        r0 = pl.multiple_of(c * SORT_ROWS, SORT_ROWS)
        r = riota + r0.astype(I16)
        p = jnp.zeros((SORT_ROWS, tm), BF16)
        for j in range(TOP_K):
            p = jnp.where(r == dl[j:j + 1, :], one, p)
        buf_ref[slot, pl.ds(r0, SORT_ROWS), :] = _pack_bf16_pairs(_dot(p, x), exact=True)
        return carry

    lax.fori_loop(0, (urows_ref[step] + SORT_ROWS - 1) // SORT_ROWS, chunk, 0)

    copies = functools.partial(_group_copies, cnt_ref, loc_ref, gstart_ref)
    copies(step, ne, buf_ref.at[slot], xs_ref, sems.at[slot], True)

    @pl.when(step > 0)
    def _():
        _group_wait(urows_ref[jnp.maximum(step - 1, 0)], buf_ref.at[1 - slot], xs_ref, sems.at[1 - slot], True)

    @pl.when(step == nsteps - 1)
    def _():
        _group_wait(urows_ref[step], buf_ref.at[slot], xs_ref, sems.at[slot], True)


def _dispatch(x1b, dloc, cnt, loc, gstart, owner, used, used_rows, nrows):
    t, d = x1b.shape
    nt, ne = cnt.shape
    tm = MOE_TILE
    rt = _tile_rows(ne)
    grid_spec = pltpu.PrefetchScalarGridSpec(
        num_scalar_prefetch=6,
        grid=(nt,),
        in_specs=[pl.BlockSpec((TOP_K, tm), lambda i, *_: (0, i)), pl.BlockSpec((tm, d), lambda i, *_: (i, 0))],
        out_specs=pl.BlockSpec(memory_space=pl.ANY),
        scratch_shapes=[pltpu.VMEM((2, rt, d // 2), U32), pltpu.VMEM((EXPERT_ROWS, d // 2), U32),
                        pltpu.SemaphoreType.DMA((2,)), pltpu.SemaphoreType.DMA(())],
    )
    return pl.pallas_call(
        functools.partial(_dispatch_kernel, ne),
        grid_spec=grid_spec,
        out_shape=jax.ShapeDtypeStruct((nrows, d // 2), U32),
        compiler_params=_cparams("arbitrary"),
        name="dispatch",
    )(cnt.reshape(-1), loc.reshape(-1), gstart.reshape(-1), owner, used, used_rows, dloc, x1b)


EXPERT_PIECE = 1024


def _expert_kernel(first_ref, nblk_ref, xs_ref, wg_ref, wu_ref, wd_ref, ys_ref,
                   xbuf, ybuf, wgb_ref, wub_ref, wdb_ref, sem_in, sem_out, done_ref, pend_ref):
    e = pl.program_id(0)
    last = pl.num_programs(0) - 1
    per_piece = EXPERT_PIECE // EXPERT_ROWS

    def pieces(ex):
        return (nblk_ref[ex] + per_piece - 1) // per_piece

    def span(ex, s):
        n = jnp.minimum(EXPERT_PIECE, nblk_ref[ex] * EXPERT_ROWS - s * EXPERT_PIECE)
        n = pl.multiple_of(n, EXPERT_ROWS)
        return pl.ds(pl.multiple_of(first_ref[ex] + s * EXPERT_PIECE, EXPERT_ROWS), n), pl.ds(0, n)

    def in_copy(ex, s, slot):
        far, near = span(ex, s)
        return pltpu.make_async_copy(xs_ref.at[far, :], xbuf.at[slot, near, :], sem_in.at[slot])

    def out_copy(ex, s, slot):
        far, near = span(ex, s)
        return pltpu.make_async_copy(ybuf.at[slot, near, :], ys_ref.at[far, :], sem_out.at[slot])

    @pl.when(e == 0)
    def _():
        done_ref[0] = 0
        for slot in range(2):
            pend_ref[2 * slot] = -1

        @pl.when(nblk_ref[0] > 0)
        def _():
            in_copy(0, 0, 0).start()

    nb = nblk_ref[e]
    ns = pieces(e)
    g0 = done_ref[0]
    nxt = jnp.minimum(e + 1, last)
    has_next = (e < last) & (nblk_ref[nxt] > 0)

    def release(slot):
        pe = pend_ref[2 * slot]

        @pl.when(pe >= 0)
        def _():
            out_copy(pe, pend_ref[2 * slot + 1], slot).wait()

    @pl.when(nb > 0)
    def _():
        wgb_ref[...] = wg_ref[...].astype(BF16)
        wub_ref[...] = wu_ref[...].astype(BF16)
        wdb_ref[...] = wd_ref[...].astype(BF16)

        def piece(s, carry):
            slot = (g0 + s) % 2

            @pl.when(s + 1 < ns)
            def _():
                in_copy(e, s + 1, 1 - slot).start()

            @pl.when((s + 1 == ns) & has_next)
            def _():
                in_copy(nxt, 0, 1 - slot).start()

            in_copy(e, s, slot).wait()
            release(slot)

            blocks_here = jnp.minimum(per_piece, nb - s * per_piece)
            for nblocks in range(1, per_piece + 1):
                @pl.when(blocks_here == nblocks)
                def _():
                    rows = pl.ds(0, nblocks * EXPERT_ROWS)
                    xb = _unpack_bf16_pairs(xbuf[slot, rows, :]).astype(BF16)
                    hg = _dot(xb, wgb_ref[...])
                    hb = hg * _sigmoid(hg) * _dot(xb, wub_ref[...])
                    ybuf[slot, rows, :] = _pack_bf16_pairs(_dot(hb.astype(BF16), wdb_ref[...]))
            out_copy(e, s, slot).start()
            pend_ref[2 * slot] = e
            pend_ref[2 * slot + 1] = s
            return carry

        lax.fori_loop(0, ns, piece, 0)

    @pl.when((nb == 0) & has_next)
    def _():
        in_copy(nxt, 0, g0 % 2).start()

    @pl.when(e == last)
    def _():
        for slot in range(2):
            release(slot)

    done_ref[0] = g0 + ns


def _experts(xs, first_row, nblocks, e_gate, e_up, e_down):
    nrows, dh = xs.shape
    ne, d, ff = e_gate.shape
    wsel = lambda e, *_: (e, 0, 0)
    grid_spec = pltpu.PrefetchScalarGridSpec(
        num_scalar_prefetch=2,
        grid=(ne,),
        in_specs=[pl.BlockSpec(memory_space=pl.ANY),
                  pl.BlockSpec((None, d, ff), wsel), pl.BlockSpec((None, d, ff), wsel),
                  pl.BlockSpec((None, ff, d), wsel)],
        out_specs=pl.BlockSpec(memory_space=pl.ANY),
        scratch_shapes=[pltpu.VMEM((2, EXPERT_PIECE, dh), U32), pltpu.VMEM((2, EXPERT_PIECE, dh), U32),
                        pltpu.VMEM((d, ff), BF16), pltpu.VMEM((d, ff), BF16), pltpu.VMEM((ff, d), BF16),
                        pltpu.SemaphoreType.DMA((2,)), pltpu.SemaphoreType.DMA((2,)),
                        pltpu.SMEM((1,), I32), pltpu.SMEM((4,), I32)],
    )
    return pl.pallas_call(
        _expert_kernel,
        grid_spec=grid_spec,
        out_shape=jax.ShapeDtypeStruct((nrows, dh), U32),
        compiler_params=_cparams("arbitrary"),
        name="experts",
    )(first_row, nblocks, xs, e_gate, e_up, e_down)


def _final_kernel(ne, cnt_ref, loc_ref, gstart_ref, used_ref, dloc_ref, gate_ref, x_ref, xb_ref, ys_ref,
                  sg_ref, su_ref, sd_ref, ln_ref, o_ref, buf_ref, acc_ref, sems):
    step = pl.program_id(0)
    nsteps = pl.num_programs(0)
    slot = step % 2
    tm = x_ref.shape[0]
    used = used_ref[step]
    copies = functools.partial(_group_copies, cnt_ref, loc_ref, gstart_ref)

    @pl.when(step == 0)
    def _():
        copies(step, ne, buf_ref.at[0], ys_ref, sems.at[0], False)

    @pl.when(step + 1 < nsteps)
    def _():
        copies(step + 1, ne, buf_ref.at[1 - slot], ys_ref, sems.at[1 - slot], False)

    xb = xb_ref[...]
    hg = _dot(xb, sg_ref[...])
    hs = hg * _sigmoid(hg) * _dot(xb, su_ref[...])
    acc_ref[...] = _dot(hs.astype(BF16), sd_ref[...])

    _group_wait(used, buf_ref.at[slot], ys_ref, sems.at[slot], False)

    dl = dloc_ref[...].astype(I16)
    gt = gate_ref[...].astype(BF16)
    rows16 = lax.broadcasted_iota(I32, (COMBINE_ROWS, tm), 0).astype(I16)
    gts = [jnp.broadcast_to(gt[j:j + 1, :], (COMBINE_ROWS, tm)) for j in range(TOP_K)]
    riota = lax.broadcasted_iota(I32, (COMBINE_ROWS, buf_ref.shape[2]), 0)

    def chunk(c, carry):
        r0 = pl.multiple_of(c * COMBINE_ROWS, COMBINE_ROWS)
        r = rows16 + r0.astype(I16)
        g = jnp.zeros((COMBINE_ROWS, tm), BF16)
        for j in range(TOP_K):
            g = jnp.where(r == dl[j:j + 1, :], gts[j], g)
        w = buf_ref[slot, pl.ds(r0, COMBINE_ROWS), :]
        w = jnp.where(riota + r0 < used, w, jnp.zeros_like(w))
        y = _unpack_bf16_pairs(w).astype(BF16)
        acc_ref[...] += lax.dot_general(g, y, (((0,), (0,)), ((), ())), preferred_element_type=F32)
        return carry

    lax.fori_loop(0, (used + COMBINE_ROWS - 1) // COMBINE_ROWS, chunk, 0)
    o_ref[...] = _layer_norm(ALPHA * x_ref[...] + acc_ref[...], ln_ref[0:1, :], ln_ref[1:2, :])


def _final(x1, x1b, dloc, gate, cnt, loc, gstart, used_rows, ys, s_gate, s_up, s_down, ln_g, ln_b):
    t, d = x1.shape
    nt, ne = cnt.shape
    tm = MOE_TILE
    rt = _tile_rows(ne)
    ln = jnp.concatenate([ln_g[None], ln_b[None], jnp.zeros((SUBLANES - 2, d), F32)], axis=0)
    sg, su, sd = s_gate.astype(BF16), s_up.astype(BF16), s_down.astype(BF16)
    tok = pl.BlockSpec((tm, d), lambda i, *_: (i, 0))
    tokt = pl.BlockSpec((TOP_K, tm), lambda i, *_: (0, i))
    const = lambda a: pl.BlockSpec(a.shape, lambda i, *_: (0, 0))
    grid_spec = pltpu.PrefetchScalarGridSpec(
        num_scalar_prefetch=4,
        grid=(nt,),
        in_specs=[tokt, tokt, tok, tok, pl.BlockSpec(memory_space=pl.ANY), const(sg), const(su), const(sd), const(ln)],
        out_specs=tok,
        scratch_shapes=[pltpu.VMEM((2, rt, d // 2), U32), pltpu.VMEM((tm, d), F32), pltpu.SemaphoreType.DMA((2,))],
    )
    return pl.pallas_call(
        functools.partial(_final_kernel, ne),
        grid_spec=grid_spec,
        out_shape=jax.ShapeDtypeStruct((t, d), F32),
        compiler_params=_cparams("arbitrary"),
        name="final",
    )(cnt.reshape(-1), loc.reshape(-1), gstart.reshape(-1), used_rows, dloc, gate, x1, x1b, ys, sg, su, sd, ln)


def _moe(x1, x1b, router_w, router_bias, e_gate, e_up, e_down, s_gate, s_up, s_down, ln_g, ln_b):
    t = x1.shape[0]
    ne = router_w.shape[1]
    nt = t // MOE_TILE
    nblk = -(-(nt * _tile_rows(ne)) // EXPERT_ROWS) + ne
    nblk_pad = -(-nblk // LANES) * LANES
    dloc, gate, tab = _route(x1, router_w, router_bias)
    cnt, loc = tab[:, 0, :], tab[:, 1, :]
    used_rows = loc[:, ne - 1] + cnt[:, ne - 1]
    gstart, meta, emeta = _plan(cnt, nblk_pad)
    owner, used = meta[0], meta[1, 0:1]
    xs = _dispatch(x1b, dloc, cnt, loc, gstart, owner, used, used_rows, nblk * EXPERT_ROWS)
    ys = _experts(xs, emeta[0], emeta[1], e_gate, e_up, e_down)
    return _final(x1, x1b, dloc, gate, cnt, loc, gstart, used_rows, ys, s_gate, s_up, s_down, ln_g, ln_b)


def kernel(x, w_in, mu_shift, w0, w_decay_up, a0, w_aaa_up, w_gate_up, k_k, k_a, r_k, lnx_g, lnx_b, w_out,
           ln1_g, ln1_b, router_w, router_bias, e_gate, e_up, e_down, s_gate, s_up, s_down, ln2_g, ln2_b):
    bsz, seq, d = x.shape
    x2 = x.reshape(bsz * seq, d)
    qkvs, feats = _inproj(x2, seq, w_in[0], mu_shift[0], w0[0], w_decay_up[0], a0[0], w_aaa_up[0], w_gate_up[0],
                          k_k[0], k_a[0], r_k[0], tm=256)
    rw = _rwkv(feats, bsz, seq, lnx_g[0], lnx_b[0], nbat=8)
    x1, x1b = _mixer_tail(qkvs, rw, x2, bsz, seq, w_out[0], ln1_g[0], ln1_b[0])
    out = _moe(x1, x1b, router_w[0], router_bias[0], e_gate[0], e_up[0], e_down[0], s_gate[0], s_up[0], s_down[0],
               ln2_g[0], ln2_b[0])
    return out.reshape(bsz, seq, d)


def _mixer_tail(qkvs, rw, x2, bsz, seq, w_out, ln_g, ln_b):
    res = [_attention(q, k, v, bsz, seq, window, dilation)
           for (q, k, v), (window, dilation) in zip(qkvs, ATTN_PATTERNS)]
    dils = tuple(dl for _, dl in ATTN_PATTERNS)
    return _outproj([o for o, _ in res], [l for _, l in res], dils, rw, x2, w_out, ln_g, ln_b, tm=512)
```

```python
import functools

import jax
import jax.numpy as jnp
from jax import lax
from jax.experimental import pallas as pl
from jax.experimental.pallas import tpu as pltpu

F32 = jnp.float32
BF16 = jnp.bfloat16
I32 = jnp.int32
I16 = jnp.int16
U32 = jnp.uint32

LANES = 128
SUBLANES = 8
VMEM_LIMIT_BYTES = 56 * 1024 * 1024

HEAD_DIM = 64
ATTN_HEADS = 8
RWKV_HEADS = 8
ATTN_WIDTH = ATTN_HEADS * HEAD_DIM
RWKV_WIDTH = RWKV_HEADS * HEAD_DIM
ATTN_PATTERNS = ((128, 1), (512, 4), (2048, 16))
Q_BLOCK = 128
ROPE_THETA = 10000.0
DECAY_LORA = 64
AAA_LORA = 64
GATE_LORA = 160
GN_EPS = 64e-5
LN_EPS = 1e-5
TOP_K = 8
N_GROUPS = 8
TOPK_GROUPS = 4
ROUTED_SCALE = 2.5
DEPTH = 1
ALPHA = (2.0 * DEPTH) ** 0.25

RWKV_CHUNK = 64
PAIR = 2 * HEAD_DIM
LORA_PAD = 2 * LANES


def _cparams(*sem):
    return pltpu.CompilerParams(dimension_semantics=sem, vmem_limit_bytes=VMEM_LIMIT_BYTES)


def _split_bf16(a):
    hi = a.astype(BF16)
    lo = (a - hi.astype(F32)).astype(BF16)
    return hi, lo


def _dot(a, b):
    return jnp.dot(a, b, preferred_element_type=F32)


def _dot_nt(a, b):
    return lax.dot_general(a, b, (((1,), (1,)), ((), ())), preferred_element_type=F32)


def _softplus(z):
    return jnp.maximum(z, 0.0) + jnp.log(1.0 + jnp.exp(-jnp.abs(z)))


def _sigmoid(z):
    return 1.0 / (1.0 + jnp.exp(-z))


def _inproj_kernel(dils, x_ref, w_ref, cos_ref, sin_ref, mu_ref, wd_ref, wa_ref, wg_ref, vec_ref, *refs):
    nq = 3 * len(dils)
    qkv_refs = refs[0:nq]
    r_ref, ld_ref, kp_ref, vv_ref, kk_ref, b_ref, g_ref, bon_ref, carry_ref, qkv_scr = refs[nq:]
    s = pl.program_id(1)
    tm = x_ref.shape[0]
    aw = ATTN_WIDTH
    rw = RWKV_WIDTH

    @pl.when(s == 0)
    def _():
        carry_ref[...] = jnp.zeros_like(carry_ref)

    h = _dot(x_ref[...].astype(BF16), w_ref[...])
    reps = aw // cos_ref.shape[1]
    cos = jnp.concatenate([cos_ref[...]] * reps, axis=1)
    sin = jnp.concatenate([sin_ref[...]] * reps, axis=1)
    qkv = [(h[:, 0:aw] * cos + h[:, aw:2 * aw] * sin) * (HEAD_DIM ** -0.5),
           h[:, 2 * aw:3 * aw] * cos + h[:, 3 * aw:4 * aw] * sin,
           h[:, 4 * aw:5 * aw]]
    nslab = aw // LANES
    for i in range(3):
        for j in range(nslab):
            qkv_scr[i * nslab + j] = qkv[i][:, j * LANES:(j + 1) * LANES]
    for di, d in enumerate(dils):
        for i in range(3):
            o_ref = qkv_refs[3 * di + i]
            if d == 1:
                o_ref[...] = qkv[i].astype(o_ref.dtype)
                continue
            for res in range(d):
                for j in range(nslab):
                    rows = qkv_scr[i * nslab + j, pl.ds(res, tm // d, stride=d), :]
                    o_ref[:, res * aw + j * LANES:res * aw + (j + 1) * LANES] = rows.astype(o_ref.dtype)

    f = h[:, 5 * aw:]
    rows = lax.broadcasted_iota(I32, f.shape, 0)
    prev = jnp.where(rows == 0, carry_ref[SUBLANES - 1:SUBLANES, :], pltpu.roll(f, 1, axis=0))
    carry_ref[...] = f[tm - SUBLANES:tm, :]
    f = f + (prev - f) * mu_ref[...]

    r = f[:, 0:rw]
    k = f[:, rw:2 * rw]
    v = f[:, 2 * rw:3 * rw]
    la = f[:, 3 * rw:3 * rw + LANES]
    gl = f[:, 3 * rw + LANES:]
    w0, a0, k_k, k_a, r_k = (vec_ref[i:i + 1, :] for i in range(5))

    z = w0 + _dot(jnp.tanh(la).astype(BF16), wd_ref[...])
    w = -_softplus(-z) - 0.5
    ld_ref[...] = -jnp.exp(w)
    a = _sigmoid(a0 + _dot(la.astype(BF16), wa_ref[...]))
    g_ref[...] = _dot(_sigmoid(gl).astype(BF16), wg_ref[...])

    first = lax.broadcasted_iota(I32, (tm, PAIR), 1) < HEAD_DIM

    def head_sum(z):
        parts = []
        for p in range(rw // PAIR):
            zp = z[:, p * PAIR:(p + 1) * PAIR]
            s0 = jnp.sum(jnp.where(first, zp, 0.0), axis=1, keepdims=True)
            s1 = jnp.sum(jnp.where(first, 0.0, zp), axis=1, keepdims=True)
            parts.append(jnp.where(first, s0, s1))
        return jnp.concatenate(parts, axis=1)

    kk = k * k_k
    kk = kk / jnp.maximum(jnp.sqrt(head_sum(kk * kk)), 1e-12)
    kp = k * (1.0 + (a - 1.0) * k_a)
    r_ref[...] = r
    kp_ref[...] = kp
    vv_ref[...] = v
    kk_ref[...] = kk
    b_ref[...] = kk * a
    bon_ref[...] = head_sum(r * kp * r_k) * v


def _rot_half_cols(w):
    d, n = w.shape
    w4 = w.reshape(d, n // HEAD_DIM, 2, HEAD_DIM // 2)
    return jnp.stack([-w4[:, :, 1, :], w4[:, :, 0, :]], axis=2).reshape(d, n)


def _inproj(x2, seq, w_in, mu_shift, w0, w_decay_up, a0, w_aaa_up, w_gate_up, k_k, k_a, r_k, tm):
    t, d = x2.shape
    aw, rw = ATTN_WIDTH, RWKV_WIDTH
    wq, wk, wv = w_in[:, 0:aw], w_in[:, aw:2 * aw], w_in[:, 2 * aw:3 * aw]
    wf = w_in[:, 3 * aw:]
    gpad = LORA_PAD - GATE_LORA
    w_all = jnp.concatenate(
        [wq, _rot_half_cols(wq), wk, _rot_half_cols(wk), wv, wf, jnp.zeros((d, gpad), F32)], axis=1).astype(BF16)
    mu = jnp.concatenate([mu_shift, jnp.zeros((gpad,), F32)])[None, :]
    nf = mu.shape[1]
    wd = jnp.concatenate([w_decay_up, jnp.zeros((AAA_LORA, rw), F32)], axis=0).astype(BF16)
    wa = jnp.concatenate([jnp.zeros((DECAY_LORA, rw), F32), w_aaa_up], axis=0).astype(BF16)
    wg = jnp.concatenate([w_gate_up, jnp.zeros((gpad, rw), F32)], axis=0).astype(BF16)
    vec = jnp.stack([w0, a0, k_k, k_a, r_k.reshape(-1), w0 * 0, w0 * 0, w0 * 0])
    half = HEAD_DIM // 2
    inv_freq = ROPE_THETA ** (-jnp.arange(half, dtype=F32) * 2.0 / HEAD_DIM)
    ang = jnp.arange(seq, dtype=F32)[:, None] * inv_freq[None, :]
    cos = jnp.tile(jnp.cos(ang), (1, LANES // half))
    sin = jnp.tile(jnp.sin(ang), (1, LANES // half))

    nst = seq // tm
    tok = lambda b, s: (b * nst + s, 0)
    const = lambda b, s: (0, 0)
    full = lambda a: pl.BlockSpec(a.shape, const)
    dils = tuple(dl for _, dl in ATTN_PATTERNS)
    qkv_specs, qkv_shapes = [], []
    for dl in dils:
        qkv_specs += [pl.BlockSpec((tm // dl, dl * aw), tok)] * 3
        qkv_shapes += [jax.ShapeDtypeStruct((t // dl, dl * aw), BF16)] * 3
    out_f = jax.ShapeDtypeStruct((t, rw), F32)
    outs = pl.pallas_call(
        functools.partial(_inproj_kernel, dils),
        grid=(t // seq, nst),
        in_specs=[pl.BlockSpec((tm, d), tok), full(w_all),
                  pl.BlockSpec((tm, LANES), lambda b, s: (s, 0)), pl.BlockSpec((tm, LANES), lambda b, s: (s, 0)),
                  full(mu), full(wd), full(wa), full(wg), full(vec)],
        out_specs=qkv_specs + [pl.BlockSpec((tm, rw), tok)] * 8,
        out_shape=qkv_shapes + [out_f] * 8,
        scratch_shapes=[pltpu.VMEM((SUBLANES, nf), F32), pltpu.VMEM((3 * aw // LANES, tm, LANES), F32)],
        compiler_params=_cparams("arbitrary", "arbitrary"),
        name="inproj",
    )(x2, w_all, cos, sin, mu, wd, wa, wg, vec)
    nq = 3 * len(dils)
    return [outs[3 * i:3 * i + 3] for i in range(len(dils))], outs[nq:]


def _dotf(a, b, nt=False, passes=3):
    dot = _dot_nt if nt else _dot
    if passes == 1:
        return dot(a.astype(BF16), b.astype(BF16))
    ah, al = _split_bf16(a)
    if passes == 2:
        bh = b.astype(BF16)
        return dot(ah, bh) + dot(al, bh)
    bh, bl = _split_bf16(b)
    return dot(ah, bh) + (dot(ah, bl) + dot(al, bh))


RWKV_PASSES = dict(s8=1, inv=1, w1=1, au=1, ry=1, gh=1, yh=1)


def _dot3(a_bf16, b):
    b1 = b.astype(BF16)
    r1 = b - b1.astype(F32)
    b2 = r1.astype(BF16)
    b3 = (r1 - b2.astype(F32)).astype(BF16)
    return _dot(a_bf16, b1) + (_dot(a_bf16, b2) + _dot(a_bf16, b3))


def _rwkv_kernel(r_ref, ld_ref, kp_ref, v_ref, kk_ref, b_ref, g_ref, bon_ref, gn_ref, o_ref, h_ref):
    c = pl.program_id(1)
    nbat, ch, rw = r_ref.shape
    npairs = rw // PAIR

    @pl.when(c == 0)
    def _():
        h_ref[...] = jnp.zeros_like(h_ref)

    ri = lax.broadcasted_iota(I32, (ch, ch), 0)
    ci = lax.broadcasted_iota(I32, (ch, ch), 1)
    tril = (ri >= ci).astype(BF16)
    row = lax.broadcasted_iota(I32, (ch, PAIR), 0)
    col = lax.broadcasted_iota(I32, (ch, PAIR), 1)
    first = col < HEAD_DIM
    jj = col & (HEAD_DIM - 1)
    strict = jj < row
    incl = jj <= row
    eye = (jj == row).astype(F32)
    row2 = lax.broadcasted_iota(I32, (PAIR, PAIR), 0)
    col2 = lax.broadcasted_iota(I32, (PAIR, PAIR), 1)
    same_head = (row2 < HEAD_DIM) == (col2 < HEAD_DIM)
    diag2 = row2 == col2
    zeros_cp = jnp.zeros((ch, PAIR), F32)

    def bdiag(y):
        return jnp.concatenate([jnp.where(first, y, 0.0), jnp.where(first, 0.0, y)], axis=0)

    ps = RWKV_PASSES
    units = []
    for bi in range(nbat):
        ld = ld_ref[bi]
        cum = _dot3(tril, ld)
        tot = cum[ch - 1:ch, :]
        a_t = -kk_ref[bi] * jnp.exp(cum - ld)
        pinv = jnp.exp(-cum)
        b_t = b_ref[bi] * pinv
        k_t = kp_ref[bi] * pinv
        r_t = r_ref[bi] * jnp.exp(cum)
        pend = jnp.exp(tot - cum)
        b_end = b_ref[bi] * pend
        k_end = kp_ref[bi] * pend
        p_tot = jnp.exp(tot)
        v_all = v_ref[bi]
        for p in range(npairs):
            sl = slice(p * PAIR, (p + 1) * PAIR)
            units.append(dict(h=bi * npairs + p, a=a_t[:, sl], r=r_t[:, sl], b=b_t[:, sl], k=k_t[:, sl],
                              v=v_all[:, sl], be=b_end[:, sl], ke=k_end[:, sl], pt=p_tot[:, sl]))

    for u in units:
        u["s8"] = _dotf(jnp.concatenate([u["a"], u["r"]], axis=0),
                        jnp.concatenate([bdiag(u["b"]), bdiag(u["k"])], axis=0), nt=True, passes=ps["s8"])
    for u in units:
        s8 = u.pop("s8")
        u["l"] = jnp.where(strict, s8[0:ch, 0:PAIR], 0.0)
        u["ak"] = jnp.where(strict, s8[0:ch, PAIR:], 0.0)
        u["mrb"] = jnp.where(incl, s8[ch:, 0:PAIR], 0.0)
        u["mrk"] = jnp.where(incl, s8[ch:, PAIR:], 0.0)
        u["t"] = eye + u["l"]
    for u in units:
        u["lk"] = _dotf(u["l"], bdiag(u["l"]), passes=ps["inv"])
        u["w1"] = _dotf(u["ak"], bdiag(u["v"]), passes=ps["w1"])
    for _ in range(max(ch.bit_length() - 3, 0)):
        for u in units:
            u["both"] = _dotf(jnp.concatenate([u["t"], u["lk"]], axis=0), bdiag(u["lk"]), passes=ps["inv"])
        for u in units:
            both = u.pop("both")
            u["t"] = u["t"] + both[0:ch]
            u["lk"] = both[ch:]
    for u in units:
        u["t"] = u["t"] + _dotf(u["t"], bdiag(u["lk"]), passes=ps["inv"])
    for u in units:
        u["au"] = _dotf(u["t"], jnp.concatenate([bdiag(u["a"]), bdiag(u["w1"])], axis=1), passes=ps["au"])
    for u in units:
        a_hat, u_loc = u["au"][:, 0:PAIR], u["au"][:, PAIR:]
        rhs = jnp.concatenate([
            jnp.concatenate([bdiag(a_hat), bdiag(u_loc)], axis=1),
            jnp.concatenate([jnp.zeros((2 * ch, PAIR), F32), bdiag(u["v"])], axis=1)], axis=0)
        u["ry"] = _dotf(jnp.concatenate([u["mrb"], u["mrk"]], axis=1), rhs, passes=ps["ry"])
        bkt = jnp.concatenate([u["be"], u["ke"]], axis=0).T
        u["gh"] = _dotf(bkt, jnp.concatenate([u["au"], jnp.concatenate([zeros_cp, u["v"]], axis=1)], axis=0),
                        passes=ps["gh"])
    for u in units:
        r_hat = u["r"] + u["ry"][:, 0:PAIR]
        g_mat = jnp.where(same_head, u["gh"][:, 0:PAIR], 0.0) + jnp.where(diag2, u["pt"], 0.0)
        u["yh"] = _dotf(jnp.concatenate([r_hat, g_mat], axis=0), h_ref[u["h"]], passes=ps["yh"])
    for u in units:
        h_ref[u["h"]] = u["yh"][ch:] + jnp.where(same_head, u["gh"][:, PAIR:], 0.0)
        u["y"] = u["yh"][0:ch] + u["ry"][:, PAIR:]

    inv_n = 1.0 / HEAD_DIM

    def head_mean(z):
        s0 = jnp.sum(jnp.where(first, z, 0.0), axis=1, keepdims=True)
        s1 = jnp.sum(jnp.where(first, 0.0, z), axis=1, keepdims=True)
        return jnp.where(first, s0, s1) * inv_n

    for u in units:
        yc = u["y"] - head_mean(u["y"])
        u["yn"] = yc * lax.rsqrt(head_mean(yc * yc) + GN_EPS)
    for bi in range(nbat):
        yn = jnp.concatenate([u["yn"] for u in units[bi * npairs:(bi + 1) * npairs]], axis=1)
        o_ref[bi] = (yn * gn_ref[0:1, :] + gn_ref[1:2, :] + bon_ref[bi]) * g_ref[bi]


def _rwkv(feats, bsz, seq, lnx_g, lnx_b, nbat):
    t, rw = feats[0].shape
    ch = RWKV_CHUNK
    gn = jnp.concatenate([lnx_g[None], lnx_b[None], jnp.zeros((SUBLANES - 2, rw), F32)], axis=0)
    tok = pl.BlockSpec((nbat, ch, rw), lambda bb, c: (bb, c, 0))
    const = lambda a: pl.BlockSpec(a.shape, lambda bb, c: (0, 0))
    out = pl.pallas_call(
        _rwkv_kernel,
        grid=(bsz // nbat, seq // ch),
        in_specs=[tok] * 8 + [const(gn)],
        out_specs=tok,
        out_shape=jax.ShapeDtypeStruct((bsz, seq, rw), F32),
        scratch_shapes=[pltpu.VMEM((nbat * (rw // PAIR), PAIR, PAIR), F32)],
        compiler_params=_cparams("arbitrary", "arbitrary"),
        name="rwkv",
    )(*[f.reshape(bsz, seq, rw) for f in feats], gn)
    return out.reshape(t, rw)


NEG_BIG = -1e30


def _attn_kernel(n_back, q_ref, kc_ref, kp_ref, vc_ref, vp_ref, o_ref, lse_ref):
    n = pl.program_id(2)
    qb = kp_ref.shape[0]
    nq = q_ref.shape[0] // qb
    qi = lax.broadcasted_iota(I32, (qb, 2 * qb), 0)
    ki = lax.broadcasted_iota(I32, (qb, 2 * qb), 1)
    dist = qb + qi - ki
    band = (dist >= 0) & (dist <= n_back)
    has_prev = jnp.where(n > 0, 0, qb)
    valid = [band & (ki >= has_prev)] + [band] * (nq - 1)
    first = lax.broadcasted_iota(I32, (qb, PAIR), 1) < HEAD_DIM
    npairs = q_ref.shape[1] // PAIR
    heads = []
    for u in range(nq):
        rows = slice(u * qb, (u + 1) * qb)
        for p in range(npairs):
            sl = slice(p * PAIR, (p + 1) * PAIR)
            q2 = q_ref[rows, sl]
            keys = jnp.concatenate([kp_ref[:, sl], kc_ref[:, sl]], axis=0)
            k2 = keys[u * qb:(u + 2) * qb]
            for hh in range(2):
                keep = first if hh == 0 else jnp.logical_not(first)
                s = _dot_nt(jnp.where(keep, q2, jnp.zeros_like(q2)), k2)
                heads.append(dict(u=u, rows=rows, sl=sl, s=s))
    for h in heads:
        ok = valid[h["u"]]
        s = jnp.where(ok, h.pop("s"), NEG_BIG)
        m = jnp.max(s, axis=1, keepdims=True)
        pe = jnp.where(ok, jnp.exp(s - m), 0.0)
        l = jnp.sum(pe, axis=1, keepdims=True)
        h["pe"] = pe.astype(BF16)
        h["l"] = l
        h["lse"] = m + jnp.log(l)
    for h in heads:
        sl, u = h["sl"], h["u"]
        vals = jnp.concatenate([vp_ref[:, sl], vc_ref[:, sl]], axis=0)
        h["o"] = _dot(h.pop("pe"), vals[u * qb:(u + 2) * qb]) / h["l"]
    for i in range(0, len(heads), 2):
        h0, h1 = heads[i], heads[i + 1]
        o_ref[h0["rows"], h0["sl"]] = jnp.where(first, h0["o"], h1["o"])
        lse_ref[h0["rows"], h0["sl"]] = jnp.where(first, h0["lse"], h1["lse"])


ATTN_QBLOCKS = 4


def _attention(q, k, v, bsz, seq, window, dilation):
    aw = q.shape[1] // dilation
    nq = min(ATTN_QBLOCKS, seq // dilation // Q_BLOCK)
    nb = seq // dilation // (nq * Q_BLOCK)
    cur = pl.BlockSpec((nq * Q_BLOCK, aw), lambda b, r, n: (b * nb + n, r))
    prv = pl.BlockSpec((Q_BLOCK, aw), lambda b, r, n: (jnp.maximum((b * nb + n) * nq - 1, b * nb * nq), r))
    shp = jax.ShapeDtypeStruct(q.shape, F32)
    return pl.pallas_call(
        functools.partial(_attn_kernel, window // dilation),
        grid=(bsz, dilation, nb),
        in_specs=[cur, cur, prv, cur, prv],
        out_specs=[cur, cur],
        out_shape=[shp, shp],
        compiler_params=_cparams("arbitrary", "arbitrary", "arbitrary"),
        name=f"attn_d{dilation}",
    )(q, k, k, v, v)


def _layer_norm(y, g, b):
    mu = jnp.mean(y, axis=-1, keepdims=True)
    yc = y - mu
    var = jnp.mean(yc * yc, axis=-1, keepdims=True)
    return yc * lax.rsqrt(var + LN_EPS) * g + b


def _outproj_kernel(dils, *refs):
    npat = len(dils)
    o_refs = refs[0:npat]
    lse_refs = refs[npat:2 * npat]
    rw_ref, x_ref, wo_ref, ln_ref, y_ref, yb_ref, scr = refs[2 * npat:]
    tm, aw = rw_ref.shape

    def token_major(ref, d, slot):
        if d == 1:
            return ref[...]
        nslab = aw // LANES
        for res in range(d):
            for j in range(nslab):
                scr[slot * nslab + j, pl.ds(res, tm // d, stride=d), :] = (
                    ref[:, res * aw + j * LANES:res * aw + (j + 1) * LANES])
        return jnp.concatenate([scr[slot * nslab + j] for j in range(nslab)], axis=1)

    lses = [token_major(r, d, 2 * i) for i, (r, d) in enumerate(zip(lse_refs, dils))]
    outs = [token_major(r, d, 2 * i + 1) for i, (r, d) in enumerate(zip(o_refs, dils))]
    m = functools.reduce(jnp.maximum, lses)
    es = [jnp.exp(z - m) for z in lses]
    den = functools.reduce(lambda a, b: a + b, es)
    attn = functools.reduce(lambda a, b: a + b, [(e / den) * o for e, o in zip(es, outs)])
    mix = _dot(attn.astype(BF16), wo_ref[0:aw, :]) + _dot(rw_ref[...].astype(BF16), wo_ref[aw:, :])
    y = _layer_norm(ALPHA * x_ref[...] + mix, ln_ref[0:1, :], ln_ref[1:2, :])
    y_ref[...] = y
    yb_ref[...] = y.astype(yb_ref.dtype)


def _outproj(os_, lses, dils, rw, x2, w_out, ln_g, ln_b, tm):
    t, d = x2.shape
    aw = rw.shape[1]
    ln = jnp.concatenate([ln_g[None], ln_b[None], jnp.zeros((SUBLANES - 2, d), F32)], axis=0)
    wo = w_out.astype(BF16)
    tok = lambda w: pl.BlockSpec((tm, w), lambda i: (i, 0))
    view = [pl.BlockSpec((tm // dl, dl * aw), lambda i: (i, 0)) for dl in dils]
    const = lambda a: pl.BlockSpec(a.shape, lambda i: (0, 0))
    return pl.pallas_call(
        functools.partial(_outproj_kernel, dils),
        grid=(t // tm,),
        in_specs=view + view + [tok(aw), tok(d), const(wo), const(ln)],
        out_specs=[tok(d), tok(d)],
        out_shape=[jax.ShapeDtypeStruct((t, d), F32), jax.ShapeDtypeStruct((t, d), BF16)],
        scratch_shapes=[pltpu.VMEM((2 * len(dils) * aw // LANES, tm, LANES), F32)],
        compiler_params=_cparams("arbitrary"),
        name="outproj",
    )(*os_, *lses, rw, x2, wo, ln)


def _lanes(col_rep, n):
    return jnp.concatenate([col_rep] * (n // LANES), axis=1)


MOE_TILE = 512
ROW_CHUNK = 8
SORT_ROWS = 512
COMBINE_ROWS = 1024
EXPERT_ROWS = 256


def _tile_rows(ne):
    raw = MOE_TILE * TOP_K + ne * ROW_CHUNK
    unit = max(SORT_ROWS, COMBINE_ROWS)
    return -(-raw // unit) * unit


def _route_kernel(x_ref, rwt_ref, bias_ref, tri_ref, ones_ref, low_ref, dloc_ref, gate_ref, tab_ref):
    ne = rwt_ref.shape[0]
    tm = x_ref.shape[0]
    gsz = ne // N_GROUPS
    ninf = -jnp.inf

    aff = _sigmoid(_dotf(rwt_ref[...], x_ref[...], nt=True))
    sel = aff + _lanes(bias_ref[...], tm)

    sel3 = sel.reshape(N_GROUPS, gsz, tm)
    rid = lax.broadcasted_iota(I32, sel3.shape, 1).astype(F32)
    m1 = jnp.max(sel3, axis=1, keepdims=True)
    i1 = jnp.min(jnp.where(sel3 == m1, rid, float(gsz)), axis=1, keepdims=True)
    m2 = jnp.max(jnp.where(rid == i1, ninf, sel3), axis=1, keepdims=True)
    gsc = (m1 + m2).reshape(N_GROUPS, tm)
    gid = lax.broadcasted_iota(I32, gsc.shape, 0).astype(F32)
    keep = jnp.zeros(gsc.shape, F32)
    for _ in range(TOPK_GROUPS):
        gm = jnp.max(gsc, axis=0, keepdims=True)
        gi = jnp.min(jnp.where(gsc == gm, gid, float(N_GROUPS)), axis=0, keepdims=True)
        hit = gid == gi
        keep = jnp.where(hit, 1.0, keep)
        gsc = jnp.where(hit, ninf, gsc)
    cand = jnp.where(keep.reshape(N_GROUPS, 1, tm) > 0.0, sel3, ninf).reshape(ne, tm)

    eid = lax.broadcasted_iota(I32, (ne, tm), 0).astype(F32)
    hits, graw = [], []
    for _ in range(TOP_K):
        m = jnp.max(cand, axis=0, keepdims=True)
        ij = jnp.min(jnp.where(cand == m, eid, float(ne)), axis=0, keepdims=True)
        hit = eid == ij
        hits.append(hit)
        graw.append(jnp.sum(jnp.where(hit, aff, 0.0), axis=0, keepdims=True))
        cand = jnp.where(hit, ninf, cand)
    gsum = functools.reduce(lambda a, b: a + b, graw)
    gate_ref[...] = jnp.concatenate([g / gsum * ROUTED_SCALE for g in graw], axis=0)

    onehot = functools.reduce(lambda a, b: a + b, [h.astype(F32) for h in hits]).astype(BF16)
    before = _dot(onehot, tri_ref[...])
    count = _dot(onehot, ones_ref[...])
    padded = jnp.maximum(jnp.ceil(count * (1.0 / ROW_CHUNK)) * ROW_CHUNK, ROW_CHUNK)
    start = _dot3(low_ref[...], padded)
    row = _lanes(start, tm) + before
    dloc_ref[...] = jnp.concatenate(
        [jnp.sum(jnp.where(h, row, 0.0), axis=0, keepdims=True) for h in hits], axis=0).astype(I32)
    lane = lax.broadcasted_iota(I32, padded.shape, 1)
    both = jnp.where(lane == 0, padded, jnp.where(lane == 1, start, 0.0))
    tab_ref[...] = both.T[0:SUBLANES, :].astype(I32)


def _route(x1, router_w, router_bias):
    t, d = x1.shape
    ne = router_w.shape[1]
    tm = MOE_TILE
    rwt = router_w.T
    bias = jnp.broadcast_to(router_bias[:, None], (ne, LANES))
    pos = jnp.arange(tm)
    tri = (pos[:, None] < pos[None, :]).astype(BF16)
    ones = jnp.ones((tm, LANES), BF16)
    eid = jnp.arange(ne)
    low = (eid[:, None] > eid[None, :]).astype(BF16)
    const = lambda a: pl.BlockSpec(a.shape, lambda i: (0, 0))
    tokt = pl.BlockSpec((TOP_K, tm), lambda i: (0, i))
    return pl.pallas_call(
        _route_kernel,
        grid=(t // tm,),
        in_specs=[pl.BlockSpec((tm, d), lambda i: (i, 0)), const(rwt), const(bias), const(tri), const(ones),
                  const(low)],
        out_specs=[tokt, tokt, pl.BlockSpec((None, SUBLANES, ne), lambda i: (i, 0, 0))],
        out_shape=[jax.ShapeDtypeStruct((TOP_K, t), I32), jax.ShapeDtypeStruct((TOP_K, t), F32),
                   jax.ShapeDtypeStruct((t // tm, SUBLANES, ne), I32)],
        compiler_params=_cparams("arbitrary"),
        name="route",
    )(x1, rwt, bias, tri, ones, low)


def _pack_bf16_pairs(v, exact=False):
    half = v.shape[1] // 2
    u = pltpu.bitcast(v if exact else v.astype(BF16).astype(F32), U32)
    return (u[:, 0:half] >> 16) | (u[:, half:] & jnp.uint32(0xFFFF0000))


def _unpack_bf16_pairs(w):
    lo = pltpu.bitcast(w << 16, F32)
    hi = pltpu.bitcast(w & jnp.uint32(0xFFFF0000), F32)
    return jnp.concatenate([lo, hi], axis=1)


def _dot3l(a, b_bf16):
    a1 = a.astype(BF16)
    r1 = a - a1.astype(F32)
    a2 = r1.astype(BF16)
    a3 = (r1 - a2.astype(F32)).astype(BF16)
    return _dot(a1, b_bf16) + (_dot(a2, b_bf16) + _dot(a3, b_bf16))


def _plan_kernel(nblk, cnt_ref, low_ref, upper_ref, gstart_ref, meta_ref, emeta_ref):
    c = cnt_ref[...].astype(F32)
    nt, ne = c.shape
    earlier = _dot3(low_ref[...], c)
    total = jnp.sum(c, axis=0, keepdims=True)
    padded = jnp.ceil(total * (1.0 / EXPERT_ROWS)) * EXPERT_ROWS
    pad_end = _dot3l(jnp.broadcast_to(padded, (SUBLANES, ne)), upper_ref[...])
    pad_start = (pad_end - padded)[0:1, :]
    gstart_ref[...] = (pad_start + earlier).astype(I32)
    emeta_ref[...] = jnp.concatenate(
        [pad_start, padded * (1.0 / EXPERT_ROWS), jnp.zeros((SUBLANES - 2, ne), F32)], axis=0).astype(I32)
    end_col = jnp.broadcast_to(pad_end.T[:, 0:1], (ne, nblk))
    blk_start = (lax.broadcasted_iota(I32, (ne, nblk), 1) * EXPERT_ROWS).astype(F32)
    owner = jnp.minimum(jnp.sum((end_col <= blk_start).astype(F32), axis=0, keepdims=True), float(ne - 1))
    used = jnp.max(end_col, axis=0, keepdims=True) * (1.0 / EXPERT_ROWS)
    meta_ref[...] = jnp.concatenate([owner, used, jnp.zeros((SUBLANES - 2, nblk), F32)], axis=0).astype(I32)


def _plan(cnt, nblk_pad):
    nt, ne = cnt.shape
    ti = jnp.arange(nt)
    low = (ti[:, None] > ti[None, :]).astype(BF16)
    ei = jnp.arange(ne)
    upper = (ei[:, None] <= ei[None, :]).astype(BF16)
    return pl.pallas_call(
        functools.partial(_plan_kernel, nblk_pad),
        out_shape=[jax.ShapeDtypeStruct((nt, ne), I32), jax.ShapeDtypeStruct((SUBLANES, nblk_pad), I32),
                   jax.ShapeDtypeStruct((SUBLANES, ne), I32)],
        compiler_params=pltpu.CompilerParams(vmem_limit_bytes=VMEM_LIMIT_BYTES),
        name="plan",
    )(cnt, low, upper)


def _group_copies(cnt_ref, loc_ref, gstart_ref, tile, ne, local, remote, sem, to_remote):
    def one(e, carry):
        n = pl.multiple_of(cnt_ref[tile * ne + e], ROW_CHUNK)
        lo = pl.multiple_of(loc_ref[tile * ne + e], ROW_CHUNK)
        go = pl.multiple_of(gstart_ref[tile * ne + e], ROW_CHUNK)
        a, b = local.at[pl.ds(lo, n), :], remote.at[pl.ds(go, n), :]
        (pltpu.make_async_copy(a, b, sem) if to_remote else pltpu.make_async_copy(b, a, sem)).start()
        return carry

    lax.fori_loop(0, ne, one, 0, unroll=8)


def _group_wait(rows, local, remote, sem, to_remote):
    n = pl.multiple_of(rows, ROW_CHUNK)

    @pl.when(n > 0)
    def _():
        a, b = local.at[pl.ds(0, n), :], remote.at[pl.ds(0, n), :]
        (pltpu.make_async_copy(a, b, sem) if to_remote else pltpu.make_async_copy(b, a, sem)).wait()


def _dispatch_kernel(ne, cnt_ref, loc_ref, gstart_ref, owner_ref, used_ref, urows_ref, dloc_ref, x_ref, xs_ref,
                     buf_ref, zero_ref, sems, zsem):
    step = pl.program_id(0)
    nsteps = pl.num_programs(0)
    slot = step % 2
    tm = x_ref.shape[0]
    nblk = owner_ref.shape[0]
    used = used_ref[0]

    @pl.when(step == 0)
    def _():
        zero_ref[...] = jnp.zeros_like(zero_ref)

        def block_copy(i):
            return pltpu.make_async_copy(zero_ref, xs_ref.at[pl.ds(i * EXPERT_ROWS, EXPERT_ROWS), :], zsem)

        def is_last(i):
            return (i == used - 1) | (owner_ref[jnp.minimum(i + 1, nblk - 1)] != owner_ref[i])

        def start(i, carry):
            @pl.when(is_last(i))
            def _():
                block_copy(i).start()
            return carry

        def wait(i, carry):
            @pl.when(is_last(i))
            def _():
                block_copy(i).wait()
            return carry

        lax.fori_loop(0, used, start, 0)
        lax.fori_loop(0, used, wait, 0)

    x = x_ref[...]
    dl = dloc_ref[...].astype(I16)
    riota = lax.broadcasted_iota(I32, (SORT_ROWS, tm), 0).astype(I16)
    one = jnp.ones((SORT_ROWS, tm), BF16)

    def chunk(c, carry):
        r0 = pl.multiple_of(c * SORT_ROWS, SORT_ROWS)
        r = riota + r0.astype(I16)
        p = jnp.zeros((SORT_ROWS, tm), BF16)
        for j in range(TOP_K):
            p = jnp.where(r == dl[j:j + 1, :], one, p)
        buf_ref[slot, pl.ds(r0, SORT_ROWS), :] = _pack_bf16_pairs(_dot(p, x), exact=True)
        return carry

    lax.fori_loop(0, (urows_ref[step] + SORT_ROWS - 1) // SORT_ROWS, chunk, 0)

    copies = functools.partial(_group_copies, cnt_ref, loc_ref, gstart_ref)
    copies(step, ne, buf_ref.at[slot], xs_ref, sems.at[slot], True)

    @pl.when(step > 0)
    def _():
        _group_wait(urows_ref[jnp.maximum(step - 1, 0)], buf_ref.at[1 - slot], xs_ref, sems.at[1 - slot], True)

    @pl.when(step == nsteps - 1)
    def _():
        _group_wait(urows_ref[step], buf_ref.at[slot], xs_ref, sems.at[slot], True)


def _dispatch(x1b, dloc, cnt, loc, gstart, owner, used, used_rows, nrows):
    t, d = x1b.shape
    nt, ne = cnt.shape
    tm = MOE_TILE
    rt = _tile_rows(ne)
    grid_spec = pltpu.PrefetchScalarGridSpec(
        num_scalar_prefetch=6,
        grid=(nt,),
        in_specs=[pl.BlockSpec((TOP_K, tm), lambda i, *_: (0, i)), pl.BlockSpec((tm, d), lambda i, *_: (i, 0))],
        out_specs=pl.BlockSpec(memory_space=pl.ANY),
        scratch_shapes=[pltpu.VMEM((2, rt, d // 2), U32), pltpu.VMEM((EXPERT_ROWS, d // 2), U32),
                        pltpu.SemaphoreType.DMA((2,)), pltpu.SemaphoreType.DMA(())],
    )
    return pl.pallas_call(
        functools.partial(_dispatch_kernel, ne),
        grid_spec=grid_spec,
        out_shape=jax.ShapeDtypeStruct((nrows, d // 2), U32),
        compiler_params=_cparams("arbitrary"),
        name="dispatch",
    )(cnt.reshape(-1), loc.reshape(-1), gstart.reshape(-1), owner, used, used_rows, dloc, x1b)


EXPERT_PIECE = 1024


def _expert_kernel(first_ref, nblk_ref, xs_ref, wg_ref, wu_ref, wd_ref, ys_ref,
                   xbuf, ybuf, wgb_ref, wub_ref, wdb_ref, sem_in, sem_out, done_ref, pend_ref):
    e = pl.program_id(0)
    last = pl.num_programs(0) - 1
    per_piece = EXPERT_PIECE // EXPERT_ROWS

    def pieces(ex):
        return (nblk_ref[ex] + per_piece - 1) // per_piece

    def span(ex, s):
        n = jnp.minimum(EXPERT_PIECE, nblk_ref[ex] * EXPERT_ROWS - s * EXPERT_PIECE)
        n = pl.multiple_of(n, EXPERT_ROWS)
        return pl.ds(pl.multiple_of(first_ref[ex] + s * EXPERT_PIECE, EXPERT_ROWS), n), pl.ds(0, n)

    def in_copy(ex, s, slot):
        far, near = span(ex, s)
        return pltpu.make_async_copy(xs_ref.at[far, :], xbuf.at[slot, near, :], sem_in.at[slot])

    def out_copy(ex, s, slot):
        far, near = span(ex, s)
        return pltpu.make_async_copy(ybuf.at[slot, near, :], ys_ref.at[far, :], sem_out.at[slot])

    @pl.when(e == 0)
    def _():
        done_ref[0] = 0
        for slot in range(2):
            pend_ref[2 * slot] = -1

        @pl.when(nblk_ref[0] > 0)
        def _():
            in_copy(0, 0, 0).start()

    nb = nblk_ref[e]
    ns = pieces(e)
    g0 = done_ref[0]
    nxt = jnp.minimum(e + 1, last)
    has_next = (e < last) & (nblk_ref[nxt] > 0)

    def release(slot):
        pe = pend_ref[2 * slot]

        @pl.when(pe >= 0)
        def _():
            out_copy(pe, pend_ref[2 * slot + 1], slot).wait()

    @pl.when(nb > 0)
    def _():
        wgb_ref[...] = wg_ref[...].astype(BF16)
        wub_ref[...] = wu_ref[...].astype(BF16)
        wdb_ref[...] = wd_ref[...].astype(BF16)

        def piece(s, carry):
            slot = (g0 + s) % 2

            @pl.when(s + 1 < ns)
            def _():
                in_copy(e, s + 1, 1 - slot).start()

            @pl.when((s + 1 == ns) & has_next)
            def _():
                in_copy(nxt, 0, 1 - slot).start()

            in_copy(e, s, slot).wait()
            release(slot)

            blocks_here = jnp.minimum(per_piece, nb - s * per_piece)
            for nblocks in range(1, per_piece + 1):
                @pl.when(blocks_here == nblocks)
                def _():
                    rows = pl.ds(0, nblocks * EXPERT_ROWS)
                    xb = _unpack_bf16_pairs(xbuf[slot, rows, :]).astype(BF16)
                    hg = _dot(xb, wgb_ref[...])
                    hb = hg * _sigmoid(hg) * _dot(xb, wub_ref[...])
                    ybuf[slot, rows, :] = _pack_bf16_pairs(_dot(hb.astype(BF16), wdb_ref[...]))
            out_copy(e, s, slot).start()
            pend_ref[2 * slot] = e
            pend_ref[2 * slot + 1] = s
            return carry

        lax.fori_loop(0, ns, piece, 0)

    @pl.when((nb == 0) & has_next)
    def _():
        in_copy(nxt, 0, g0 % 2).start()

    @pl.when(e == last)
    def _():
        for slot in range(2):
            release(slot)

    done_ref[0] = g0 + ns


def _experts(xs, first_row, nblocks, e_gate, e_up, e_down):
    nrows, dh = xs.shape
    ne, d, ff = e_gate.shape
    wsel = lambda e, *_: (e, 0, 0)
    grid_spec = pltpu.PrefetchScalarGridSpec(
        num_scalar_prefetch=2,
        grid=(ne,),
        in_specs=[pl.BlockSpec(memory_space=pl.ANY),
                  pl.BlockSpec((None, d, ff), wsel), pl.BlockSpec((None, d, ff), wsel),
                  pl.BlockSpec((None, ff, d), wsel)],
        out_specs=pl.BlockSpec(memory_space=pl.ANY),
        scratch_shapes=[pltpu.VMEM((2, EXPERT_PIECE, dh), U32), pltpu.VMEM((2, EXPERT_PIECE, dh), U32),
                        pltpu.VMEM((d, ff), BF16), pltpu.VMEM((d, ff), BF16), pltpu.VMEM((ff, d), BF16),
                        pltpu.SemaphoreType.DMA((2,)), pltpu.SemaphoreType.DMA((2,)),
                        pltpu.SMEM((1,), I32), pltpu.SMEM((4,), I32)],
    )
    return pl.pallas_call(
        _expert_kernel,
        grid_spec=grid_spec,
        out_shape=jax.ShapeDtypeStruct((nrows, dh), U32),
        compiler_params=_cparams("arbitrary"),
        name="experts",
    )(first_row, nblocks, xs, e_gate, e_up, e_down)


def _final_kernel(ne, cnt_ref, loc_ref, gstart_ref, used_ref, dloc_ref, gate_ref, x_ref, xb_ref, ys_ref,
                  sg_ref, su_ref, sd_ref, ln_ref, o_ref, buf_ref, acc_ref, sems):
    step = pl.program_id(0)
    nsteps = pl.num_programs(0)
    slot = step % 2
    tm = x_ref.shape[0]
    used = used_ref[step]
    copies = functools.partial(_group_copies, cnt_ref, loc_ref, gstart_ref)

    @pl.when(step == 0)
    def _():
        copies(step, ne, buf_ref.at[0], ys_ref, sems.at[0], False)

    @pl.when(step + 1 < nsteps)
    def _():
        copies(step + 1, ne, buf_ref.at[1 - slot], ys_ref, sems.at[1 - slot], False)

    xb = xb_ref[...]
    hg = _dot(xb, sg_ref[...])
    hs = hg * _sigmoid(hg) * _dot(xb, su_ref[...])
    acc_ref[...] = _dot(hs.astype(BF16), sd_ref[...])

    _group_wait(used, buf_ref.at[slot], ys_ref, sems.at[slot], False)

    dl = dloc_ref[...].astype(I16)
    gt = gate_ref[...].astype(BF16)
    rows16 = lax.broadcasted_iota(I32, (COMBINE_ROWS, tm), 0).astype(I16)
    gts = [jnp.broadcast_to(gt[j:j + 1, :], (COMBINE_ROWS, tm)) for j in range(TOP_K)]
    riota = lax.broadcasted_iota(I32, (COMBINE_ROWS, buf_ref.shape[2]), 0)

    def chunk(c, carry):
        r0 = pl.multiple_of(c * COMBINE_ROWS, COMBINE_ROWS)
        r = rows16 + r0.astype(I16)
        g = jnp.zeros((COMBINE_ROWS, tm), BF16)
        for j in range(TOP_K):
            g = jnp.where(r == dl[j:j + 1, :], gts[j], g)
        w = buf_ref[slot, pl.ds(r0, COMBINE_ROWS), :]
        w = jnp.where(riota + r0 < used, w, jnp.zeros_like(w))
        y = _unpack_bf16_pairs(w).astype(BF16)
        acc_ref[...] += lax.dot_general(g, y, (((0,), (0,)), ((), ())), preferred_element_type=F32)
        return carry

    lax.fori_loop(0, (used + COMBINE_ROWS - 1) // COMBINE_ROWS, chunk, 0)
    o_ref[...] = _layer_norm(ALPHA * x_ref[...] + acc_ref[...], ln_ref[0:1, :], ln_ref[1:2, :])


def _final(x1, x1b, dloc, gate, cnt, loc, gstart, used_rows, ys, s_gate, s_up, s_down, ln_g, ln_b):
    t, d = x1.shape
    nt, ne = cnt.shape
    tm = MOE_TILE
    rt = _tile_rows(ne)
    ln = jnp.concatenate([ln_g[None], ln_b[None], jnp.zeros((SUBLANES - 2, d), F32)], axis=0)
    sg, su, sd = s_gate.astype(BF16), s_up.astype(BF16), s_down.astype(BF16)
    tok = pl.BlockSpec((tm, d), lambda i, *_: (i, 0))
    tokt = pl.BlockSpec((TOP_K, tm), lambda i, *_: (0, i))
    const = lambda a: pl.BlockSpec(a.shape, lambda i, *_: (0, 0))
    grid_spec = pltpu.PrefetchScalarGridSpec(
        num_scalar_prefetch=4,
        grid=(nt,),
        in_specs=[tokt, tokt, tok, tok, pl.BlockSpec(memory_space=pl.ANY), const(sg), const(su), const(sd), const(ln)],
        out_specs=tok,
        scratch_shapes=[pltpu.VMEM((2, rt, d // 2), U32), pltpu.VMEM((tm, d), F32), pltpu.SemaphoreType.DMA((2,))],
    )
    return pl.pallas_call(
        functools.partial(_final_kernel, ne),
        grid_spec=grid_spec,
        out_shape=jax.ShapeDtypeStruct((t, d), F32),
        compiler_params=_cparams("arbitrary"),
        name="final",
    )(cnt.reshape(-1), loc.reshape(-1), gstart.reshape(-1), used_rows, dloc, gate, x1, x1b, ys, sg, su, sd, ln)


def _moe(x1, x1b, router_w, router_bias, e_gate, e_up, e_down, s_gate, s_up, s_down, ln_g, ln_b):
    t = x1.shape[0]
    ne = router_w.shape[1]
    nt = t // MOE_TILE
    nblk = -(-(nt * _tile_rows(ne)) // EXPERT_ROWS) + ne
    nblk_pad = -(-nblk // LANES) * LANES
    dloc, gate, tab = _route(x1, router_w, router_bias)
    cnt, loc = tab[:, 0, :], tab[:, 1, :]
    used_rows = loc[:, ne - 1] + cnt[:, ne - 1]
    gstart, meta, emeta = _plan(cnt, nblk_pad)
    owner, used = meta[0], meta[1, 0:1]
    xs = _dispatch(x1b, dloc, cnt, loc, gstart, owner, used, used_rows, nblk * EXPERT_ROWS)
    ys = _experts(xs, emeta[0], emeta[1], e_gate, e_up, e_down)
    return _final(x1, x1b, dloc, gate, cnt, loc, gstart, used_rows, ys, s_gate, s_up, s_down, ln_g, ln_b)


def kernel(x, w_in, mu_shift, w0, w_decay_up, a0, w_aaa_up, w_gate_up, k_k, k_a, r_k, lnx_g, lnx_b, w_out,
           ln1_g, ln1_b, router_w, router_bias, e_gate, e_up, e_down, s_gate, s_up, s_down, ln2_g, ln2_b):
    bsz, seq, d = x.shape
    x2 = x.reshape(bsz * seq, d)
    qkvs, feats = _inproj(x2, seq, w_in[0], mu_shift[0], w0[0], w_decay_up[0], a0[0], w_aaa_up[0], w_gate_up[0],
                          k_k[0], k_a[0], r_k[0], tm=256)
    rw = _rwkv(feats, bsz, seq, lnx_g[0], lnx_b[0], nbat=8)
    x1, x1b = _mixer_tail(qkvs, rw, x2, bsz, seq, w_out[0], ln1_g[0], ln1_b[0])
    out = _moe(x1, x1b, router_w[0], router_bias[0], e_gate[0], e_up[0], e_down[0], s_gate[0], s_up[0], s_down[0],
               ln2_g[0], ln2_b[0])
    return out.reshape(bsz, seq, d)


def _mixer_tail(qkvs, rw, x2, bsz, seq, w_out, ln_g, ln_b):
    res = [_attention(q, k, v, bsz, seq, window, dilation)
           for (q, k, v), (window, dilation) in zip(qkvs, ATTN_PATTERNS)]
    dils = tuple(dl for _, dl in ATTN_PATTERNS)
    return _outproj([o for o, _ in res], [l for _, l in res], dils, rw, x2, w_out, ln_g, ln_b, tm=512)
```

```python
import functools

import jax
import jax.numpy as jnp
from jax import lax
from jax.experimental import pallas as pl
from jax.experimental.pallas import tpu as pltpu

F32 = jnp.float32
BF16 = jnp.bfloat16
I32 = jnp.int32
I16 = jnp.int16
U32 = jnp.uint32

LANES = 128
SUBLANES = 8
VMEM_LIMIT_BYTES = 56 * 1024 * 1024

HEAD_DIM = 64
ATTN_HEADS = 8
RWKV_HEADS = 8
ATTN_WIDTH = ATTN_HEADS * HEAD_DIM
RWKV_WIDTH = RWKV_HEADS * HEAD_DIM
ATTN_PATTERNS = ((128, 1), (512, 4), (2048, 16))
Q_BLOCK = 128
ROPE_THETA = 10000.0
DECAY_LORA = 64
AAA_LORA = 64
GATE_LORA = 160
GN_EPS = 64e-5
LN_EPS = 1e-5
TOP_K = 8
N_GROUPS = 8
TOPK_GROUPS = 4
ROUTED_SCALE = 2.5
DEPTH = 1
ALPHA = (2.0 * DEPTH) ** 0.25

RWKV_CHUNK = 64
PAIR = 2 * HEAD_DIM
LORA_PAD = 2 * LANES


def _cparams(*sem):
    return pltpu.CompilerParams(dimension_semantics=sem, vmem_limit_bytes=VMEM_LIMIT_BYTES)


def _split_bf16(a):
    hi = a.astype(BF16)
    lo = (a - hi.astype(F32)).astype(BF16)
    return hi, lo


def _dot(a, b):
    return jnp.dot(a, b, preferred_element_type=F32)


def _dot_nt(a, b):
    return lax.dot_general(a, b, (((1,), (1,)), ((), ())), preferred_element_type=F32)


def _softplus(z):
    return jnp.maximum(z, 0.0) + jnp.log(1.0 + jnp.exp(-jnp.abs(z)))


def _sigmoid(z):
    return 1.0 / (1.0 + jnp.exp(-z))


def _inproj_kernel(dils, x_ref, w_ref, cos_ref, sin_ref, mu_ref, wd_ref, wa_ref, wg_ref, vec_ref, *refs):
    nq = 3 * len(dils)
    qkv_refs = refs[0:nq]
    r_ref, ld_ref, kp_ref, vv_ref, kk_ref, b_ref, g_ref, bon_ref, carry_ref, qkv_scr, qkv_scr2 = refs[nq:]
    s = pl.program_id(1)
    tm = x_ref.shape[0]
    aw = ATTN_WIDTH
    rw = RWKV_WIDTH

    @pl.when(s == 0)
    def _():
        carry_ref[...] = jnp.zeros_like(carry_ref)

    h = _dot(x_ref[...].astype(BF16), w_ref[...])
    reps = aw // cos_ref.shape[1]
    cos = jnp.concatenate([cos_ref[...]] * reps, axis=1)
    sin = jnp.concatenate([sin_ref[...]] * reps, axis=1)
    qkv = [(h[:, 0:aw] * cos + h[:, aw:2 * aw] * sin) * (HEAD_DIM ** -0.5),
           h[:, 2 * aw:3 * aw] * cos + h[:, 3 * aw:4 * aw] * sin,
           h[:, 4 * aw:5 * aw]]
    nslab = aw // LANES
    for i in range(3):
        for j in range(nslab):
            qkv_scr[i * nslab + j] = qkv[i][:, j * LANES:(j + 1) * LANES]
    src, src_d = qkv_scr, 1
    for di, d in enumerate(dils):
        if d == 1:
            for i in range(3):
                qkv_refs[3 * di + i][...] = qkv[i].astype(qkv_refs[3 * di + i].dtype)
            continue
        assert d % src_d == 0
        step, per = d // src_d, tm // src_d
        dst = qkv_scr2 if src is qkv_scr else qkv_scr
        for i in range(3):
            o_ref = qkv_refs[3 * di + i]
            for res in range(d):
                lo, hi = res % src_d, res // src_d
                for j in range(nslab):
                    rows = src[i * nslab + j, pl.ds(lo * per + hi, tm // d, stride=step), :]
                    if di + 1 < len(dils):
                        dst[i * nslab + j, res * (tm // d):(res + 1) * (tm // d), :] = rows
                    o_ref[:, res * aw + j * LANES:res * aw + (j + 1) * LANES] = rows.astype(o_ref.dtype)
        src, src_d = dst, d

    f = h[:, 5 * aw:]
    rows = lax.broadcasted_iota(I32, f.shape, 0)
    prev = jnp.where(rows == 0, carry_ref[SUBLANES - 1:SUBLANES, :], pltpu.roll(f, 1, axis=0))
    carry_ref[...] = f[tm - SUBLANES:tm, :]
    f = f + (prev - f) * mu_ref[...]

    r = f[:, 0:rw]
    k = f[:, rw:2 * rw]
    v = f[:, 2 * rw:3 * rw]
    la = f[:, 3 * rw:3 * rw + LANES]
    gl = f[:, 3 * rw + LANES:]
    w0, a0, k_k, k_a, r_k = (vec_ref[i:i + 1, :] for i in range(5))

    z = w0 + _dot(jnp.tanh(la).astype(BF16), wd_ref[...])
    w = -_softplus(-z) - 0.5
    ld_ref[...] = -jnp.exp(w)
    a = _sigmoid(a0 + _dot(la.astype(BF16), wa_ref[...]))
    g_ref[...] = _dot(_sigmoid(gl).astype(BF16), wg_ref[...])

    first = lax.broadcasted_iota(I32, (tm, PAIR), 1) < HEAD_DIM

    def head_sum(z):
        parts = []
        for p in range(rw // PAIR):
            zp = z[:, p * PAIR:(p + 1) * PAIR]
            s0 = jnp.sum(jnp.where(first, zp, 0.0), axis=1, keepdims=True)
            s1 = jnp.sum(jnp.where(first, 0.0, zp), axis=1, keepdims=True)
            parts.append(jnp.where(first, s0, s1))
        return jnp.concatenate(parts, axis=1)

    kk = k * k_k
    kk = kk / jnp.maximum(jnp.sqrt(head_sum(kk * kk)), 1e-12)
    kp = k * (1.0 + (a - 1.0) * k_a)
    r_ref[...] = r
    kp_ref[...] = kp
    vv_ref[...] = v
    kk_ref[...] = kk
    b_ref[...] = kk * a
    bon_ref[...] = head_sum(r * kp * r_k) * v


def _rot_half_cols(w):
    d, n = w.shape
    w4 = w.reshape(d, n // HEAD_DIM, 2, HEAD_DIM // 2)
    return jnp.stack([-w4[:, :, 1, :], w4[:, :, 0, :]], axis=2).reshape(d, n)


def _inproj(x2, seq, w_in, mu_shift, w0, w_decay_up, a0, w_aaa_up, w_gate_up, k_k, k_a, r_k, tm):
    t, d = x2.shape
    aw, rw = ATTN_WIDTH, RWKV_WIDTH
    wq, wk, wv = w_in[:, 0:aw], w_in[:, aw:2 * aw], w_in[:, 2 * aw:3 * aw]
    wf = w_in[:, 3 * aw:]
    gpad = LORA_PAD - GATE_LORA
    w_all = jnp.concatenate(
        [wq, _rot_half_cols(wq), wk, _rot_half_cols(wk), wv, wf, jnp.zeros((d, gpad), F32)], axis=1).astype(BF16)
    mu = jnp.concatenate([mu_shift, jnp.zeros((gpad,), F32)])[None, :]
    nf = mu.shape[1]
    wd = jnp.concatenate([w_decay_up, jnp.zeros((AAA_LORA, rw), F32)], axis=0).astype(BF16)
    wa = jnp.concatenate([jnp.zeros((DECAY_LORA, rw), F32), w_aaa_up], axis=0).astype(BF16)
    wg = jnp.concatenate([w_gate_up, jnp.zeros((gpad, rw), F32)], axis=0).astype(BF16)
    vec = jnp.stack([w0, a0, k_k, k_a, r_k.reshape(-1), w0 * 0, w0 * 0, w0 * 0])
    half = HEAD_DIM // 2
    inv_freq = ROPE_THETA ** (-jnp.arange(half, dtype=F32) * 2.0 / HEAD_DIM)
    ang = jnp.arange(seq, dtype=F32)[:, None] * inv_freq[None, :]
    cos = jnp.tile(jnp.cos(ang), (1, LANES // half))
    sin = jnp.tile(jnp.sin(ang), (1, LANES // half))

    nst = seq // tm
    tok = lambda b, s: (b * nst + s, 0)
    const = lambda b, s: (0, 0)
    full = lambda a: pl.BlockSpec(a.shape, const)
    dils = tuple(dl for _, dl in ATTN_PATTERNS)
    qkv_specs, qkv_shapes = [], []
    for dl in dils:
        qkv_specs += [pl.BlockSpec((tm // dl, dl * aw), tok)] * 3
        qkv_shapes += [jax.ShapeDtypeStruct((t // dl, dl * aw), BF16)] * 3
    out_f = jax.ShapeDtypeStruct((t, rw), F32)
    outs = pl.pallas_call(
        functools.partial(_inproj_kernel, dils),
        grid=(t // seq, nst),
        in_specs=[pl.BlockSpec((tm, d), tok), full(w_all),
                  pl.BlockSpec((tm, LANES), lambda b, s: (s, 0)), pl.BlockSpec((tm, LANES), lambda b, s: (s, 0)),
                  full(mu), full(wd), full(wa), full(wg), full(vec)],
        out_specs=qkv_specs + [pl.BlockSpec((tm, rw), tok)] * 8,
        out_shape=qkv_shapes + [out_f] * 8,
        scratch_shapes=[pltpu.VMEM((SUBLANES, nf), F32), pltpu.VMEM((3 * aw // LANES, tm, LANES), F32),
                        pltpu.VMEM((3 * aw // LANES, tm, LANES), F32)],
        compiler_params=_cparams("arbitrary", "arbitrary"),
        name="inproj",
    )(x2, w_all, cos, sin, mu, wd, wa, wg, vec)
    nq = 3 * len(dils)
    return [outs[3 * i:3 * i + 3] for i in range(len(dils))], outs[nq:]


def _dotf(a, b, nt=False, passes=3):
    dot = _dot_nt if nt else _dot
    if passes == 1:
        return dot(a.astype(BF16), b.astype(BF16))
    ah, al = _split_bf16(a)
    if passes == 2:
        bh = b.astype(BF16)
        return dot(ah, bh) + dot(al, bh)
    bh, bl = _split_bf16(b)
    return dot(ah, bh) + (dot(ah, bl) + dot(al, bh))


RWKV_PASSES = dict(s8=1, inv=1, w1=1, au=1, ry=1, gh=1, yh=1)


def _dot3(a_bf16, b):
    b1 = b.astype(BF16)
    r1 = b - b1.astype(F32)
    b2 = r1.astype(BF16)
    b3 = (r1 - b2.astype(F32)).astype(BF16)
    return _dot(a_bf16, b1) + (_dot(a_bf16, b2) + _dot(a_bf16, b3))


def _rwkv_kernel(r_ref, ld_ref, kp_ref, v_ref, kk_ref, b_ref, g_ref, bon_ref, gn_ref, o_ref, h_ref):
    c = pl.program_id(1)
    nbat, ch, rw = r_ref.shape
    npairs = rw // PAIR

    @pl.when(c == 0)
    def _():
        h_ref[...] = jnp.zeros_like(h_ref)

    ri = lax.broadcasted_iota(I32, (ch, ch), 0)
    ci = lax.broadcasted_iota(I32, (ch, ch), 1)
    tril = (ri >= ci).astype(BF16)
    row = lax.broadcasted_iota(I32, (ch, PAIR), 0)
    col = lax.broadcasted_iota(I32, (ch, PAIR), 1)
    first = col < HEAD_DIM
    jj = col & (HEAD_DIM - 1)
    strict = jj < row
    incl = jj <= row
    eye = (jj == row).astype(F32)
    row2 = lax.broadcasted_iota(I32, (PAIR, PAIR), 0)
    col2 = lax.broadcasted_iota(I32, (PAIR, PAIR), 1)
    same_head = (row2 < HEAD_DIM) == (col2 < HEAD_DIM)
    diag2 = row2 == col2
    zeros_cp = jnp.zeros((ch, PAIR), F32)

    def bdiag(y):
        return jnp.concatenate([jnp.where(first, y, 0.0), jnp.where(first, 0.0, y)], axis=0)

    ps = RWKV_PASSES
    units = []
    for bi in range(nbat):
        ld = ld_ref[bi]
        cum = _dot3(tril, ld)
        tot = cum[ch - 1:ch, :]
        a_t = -kk_ref[bi] * jnp.exp(cum - ld)
        pinv = jnp.exp(-cum)
        b_t = b_ref[bi] * pinv
        k_t = kp_ref[bi] * pinv
        r_t = r_ref[bi] * jnp.exp(cum)
        pend = jnp.exp(tot - cum)
        b_end = b_ref[bi] * pend
        k_end = kp_ref[bi] * pend
        p_tot = jnp.exp(tot)
        v_all = v_ref[bi]
        for p in range(npairs):
            sl = slice(p * PAIR, (p + 1) * PAIR)
            units.append(dict(h=bi * npairs + p, a=a_t[:, sl], r=r_t[:, sl], b=b_t[:, sl], k=k_t[:, sl],
                              v=v_all[:, sl], be=b_end[:, sl], ke=k_end[:, sl], pt=p_tot[:, sl]))

    for u in units:
        u["s8"] = _dotf(jnp.concatenate([u["a"], u["r"]], axis=0),
                        jnp.concatenate([bdiag(u["b"]), bdiag(u["k"])], axis=0), nt=True, passes=ps["s8"])
    for u in units:
        s8 = u.pop("s8")
        u["l"] = jnp.where(strict, s8[0:ch, 0:PAIR], 0.0)
        u["ak"] = jnp.where(strict, s8[0:ch, PAIR:], 0.0)
        u["mrb"] = jnp.where(incl, s8[ch:, 0:PAIR], 0.0)
        u["mrk"] = jnp.where(incl, s8[ch:, PAIR:], 0.0)
        u["t"] = eye + u["l"]
    for u in units:
        u["lk"] = _dotf(u["l"], bdiag(u["l"]), passes=ps["inv"])
        u["w1"] = _dotf(u["ak"], bdiag(u["v"]), passes=ps["w1"])
    for _ in range(max(ch.bit_length() - 3, 0)):
        for u in units:
            u["both"] = _dotf(jnp.concatenate([u["t"], u["lk"]], axis=0), bdiag(u["lk"]), passes=ps["inv"])
        for u in units:
            both = u.pop("both")
            u["t"] = u["t"] + both[0:ch]
            u["lk"] = both[ch:]
    for u in units:
        u["t"] = u["t"] + _dotf(u["t"], bdiag(u["lk"]), passes=ps["inv"])
    for u in units:
        u["au"] = _dotf(u["t"], jnp.concatenate([bdiag(u["a"]), bdiag(u["w1"])], axis=1), passes=ps["au"])
    for u in units:
        a_hat, u_loc = u["au"][:, 0:PAIR], u["au"][:, PAIR:]
        rhs = jnp.concatenate([
            jnp.concatenate([bdiag(a_hat), bdiag(u_loc)], axis=1),
            jnp.concatenate([jnp.zeros((2 * ch, PAIR), F32), bdiag(u["v"])], axis=1)], axis=0)
        u["ry"] = _dotf(jnp.concatenate([u["mrb"], u["mrk"]], axis=1), rhs, passes=ps["ry"])
        bkt = jnp.concatenate([u["be"], u["ke"]], axis=0).T
        u["gh"] = _dotf(bkt, jnp.concatenate([u["au"], jnp.concatenate([zeros_cp, u["v"]], axis=1)], axis=0),
                        passes=ps["gh"])
    for u in units:
        r_hat = u["r"] + u["ry"][:, 0:PAIR]
        g_mat = jnp.where(same_head, u["gh"][:, 0:PAIR], 0.0) + jnp.where(diag2, u["pt"], 0.0)
        u["yh"] = _dotf(jnp.concatenate([r_hat, g_mat], axis=0), h_ref[u["h"]], passes=ps["yh"])
    for u in units:
        h_ref[u["h"]] = u["yh"][ch:] + jnp.where(same_head, u["gh"][:, PAIR:], 0.0)
        u["y"] = u["yh"][0:ch] + u["ry"][:, PAIR:]

    inv_n = 1.0 / HEAD_DIM

    def head_mean(z):
        s0 = jnp.sum(jnp.where(first, z, 0.0), axis=1, keepdims=True)
        s1 = jnp.sum(jnp.where(first, 0.0, z), axis=1, keepdims=True)
        return jnp.where(first, s0, s1) * inv_n

    for u in units:
        yc = u["y"] - head_mean(u["y"])
        u["yn"] = yc * lax.rsqrt(head_mean(yc * yc) + GN_EPS)
    for bi in range(nbat):
        yn = jnp.concatenate([u["yn"] for u in units[bi * npairs:(bi + 1) * npairs]], axis=1)
        o_ref[bi] = (yn * gn_ref[0:1, :] + gn_ref[1:2, :] + bon_ref[bi]) * g_ref[bi]


def _rwkv(feats, bsz, seq, lnx_g, lnx_b, nbat):
    t, rw = feats[0].shape
    ch = RWKV_CHUNK
    gn = jnp.concatenate([lnx_g[None], lnx_b[None], jnp.zeros((SUBLANES - 2, rw), F32)], axis=0)
    tok = pl.BlockSpec((nbat, ch, rw), lambda bb, c: (bb, c, 0))
    const = lambda a: pl.BlockSpec(a.shape, lambda bb, c: (0, 0))
    out = pl.pallas_call(
        _rwkv_kernel,
        grid=(bsz // nbat, seq // ch),
        in_specs=[tok] * 8 + [const(gn)],
        out_specs=tok,
        out_shape=jax.ShapeDtypeStruct((bsz, seq, rw), F32),
        scratch_shapes=[pltpu.VMEM((nbat * (rw // PAIR), PAIR, PAIR), F32)],
        compiler_params=_cparams("arbitrary", "arbitrary"),
        name="rwkv",
    )(*[f.reshape(bsz, seq, rw) for f in feats], gn)
    return out.reshape(t, rw)


NEG_BIG = -1e30


def _attn_kernel(n_back, q_ref, kc_ref, kp_ref, vc_ref, vp_ref, o_ref, lse_ref):
    n = pl.program_id(2)
    qb = kp_ref.shape[0]
    nq = q_ref.shape[0] // qb
    qi = lax.broadcasted_iota(I32, (qb, 2 * qb), 0)
    ki = lax.broadcasted_iota(I32, (qb, 2 * qb), 1)
    dist = qb + qi - ki
    band = (dist >= 0) & (dist <= n_back)
    has_prev = jnp.where(n > 0, 0, qb)
    valid = [band & (ki >= has_prev)] + [band] * (nq - 1)
    first = lax.broadcasted_iota(I32, (qb, PAIR), 1) < HEAD_DIM
    npairs = q_ref.shape[1] // PAIR
    heads = []
    for u in range(nq):
        rows = slice(u * qb, (u + 1) * qb)
        for p in range(npairs):
            sl = slice(p * PAIR, (p + 1) * PAIR)
            q2 = q_ref[rows, sl]
            keys = jnp.concatenate([kp_ref[:, sl], kc_ref[:, sl]], axis=0)
            k2 = keys[u * qb:(u + 2) * qb]
            for hh in range(2):
                keep = first if hh == 0 else jnp.logical_not(first)
                s = _dot_nt(jnp.where(keep, q2, jnp.zeros_like(q2)), k2)
                heads.append(dict(u=u, rows=rows, sl=sl, s=s))
    for h in heads:
        ok = valid[h["u"]]
        s = jnp.where(ok, h.pop("s"), NEG_BIG)
        m = jnp.max(s, axis=1, keepdims=True)
        pe = jnp.where(ok, jnp.exp(s - m), 0.0)
        l = jnp.sum(pe, axis=1, keepdims=True)
        h["pe"] = pe.astype(BF16)
        h["l"] = l
        h["lse"] = m + jnp.log(l)
    for h in heads:
        sl, u = h["sl"], h["u"]
        vals = jnp.concatenate([vp_ref[:, sl], vc_ref[:, sl]], axis=0)
        h["o"] = _dot(h.pop("pe"), vals[u * qb:(u + 2) * qb]) / h["l"]
    for i in range(0, len(heads), 2):
        h0, h1 = heads[i], heads[i + 1]
        o_ref[h0["rows"], h0["sl"]] = jnp.where(first, h0["o"], h1["o"])
        lse_ref[h0["rows"], h0["sl"]] = jnp.where(first, h0["lse"], h1["lse"])


ATTN_QBLOCKS = 4


def _attention(q, k, v, bsz, seq, window, dilation):
    aw = q.shape[1] // dilation
    nq = min(ATTN_QBLOCKS, seq // dilation // Q_BLOCK)
    nb = seq // dilation // (nq * Q_BLOCK)
    cur = pl.BlockSpec((nq * Q_BLOCK, aw), lambda b, r, n: (b * nb + n, r))
    prv = pl.BlockSpec((Q_BLOCK, aw), lambda b, r, n: (jnp.maximum((b * nb + n) * nq - 1, b * nb * nq), r))
    shp = jax.ShapeDtypeStruct(q.shape, F32)
    return pl.pallas_call(
        functools.partial(_attn_kernel, window // dilation),
        grid=(bsz, dilation, nb),
        in_specs=[cur, cur, prv, cur, prv],
        out_specs=[cur, cur],
        out_shape=[shp, shp],
        compiler_params=_cparams("arbitrary", "arbitrary", "arbitrary"),
        name=f"attn_d{dilation}",
    )(q, k, k, v, v)


def _layer_norm(y, g, b):
    mu = jnp.mean(y, axis=-1, keepdims=True)
    yc = y - mu
    var = jnp.mean(yc * yc, axis=-1, keepdims=True)
    return yc * lax.rsqrt(var + LN_EPS) * g + b


def _outproj_kernel(dils, *refs):
    npat = len(dils)
    o_refs = refs[0:npat]
    lse_refs = refs[npat:2 * npat]
    rw_ref, x_ref, wo_ref, ln_ref, y_ref, yb_ref, scr = refs[2 * npat:]
    tm, aw = rw_ref.shape

    def token_major(ref, d, slot):
        if d == 1:
            return ref[...]
        nslab = aw // LANES
        for res in range(d):
            for j in range(nslab):
                scr[slot * nslab + j, pl.ds(res, tm // d, stride=d), :] = (
                    ref[:, res * aw + j * LANES:res * aw + (j + 1) * LANES])
        return jnp.concatenate([scr[slot * nslab + j] for j in range(nslab)], axis=1)

    lses = [token_major(r, d, 2 * i) for i, (r, d) in enumerate(zip(lse_refs, dils))]
    outs = [token_major(r, d, 2 * i + 1) for i, (r, d) in enumerate(zip(o_refs, dils))]
    m = functools.reduce(jnp.maximum, lses)
    es = [jnp.exp(z - m) for z in lses]
    den = functools.reduce(lambda a, b: a + b, es)
    attn = functools.reduce(lambda a, b: a + b, [(e / den) * o for e, o in zip(es, outs)])
    mix = _dot(attn.astype(BF16), wo_ref[0:aw, :]) + _dot(rw_ref[...].astype(BF16), wo_ref[aw:, :])
    y = _layer_norm(ALPHA * x_ref[...] + mix, ln_ref[0:1, :], ln_ref[1:2, :])
    y_ref[...] = y
    yb_ref[...] = y.astype(yb_ref.dtype)


def _outproj(os_, lses, dils, rw, x2, w_out, ln_g, ln_b, tm):
    t, d = x2.shape
    aw = rw.shape[1]
    ln = jnp.concatenate([ln_g[None], ln_b[None], jnp.zeros((SUBLANES - 2, d), F32)], axis=0)
    wo = w_out.astype(BF16)
    tok = lambda w: pl.BlockSpec((tm, w), lambda i: (i, 0))
    view = [pl.BlockSpec((tm // dl, dl * aw), lambda i: (i, 0)) for dl in dils]
    const = lambda a: pl.BlockSpec(a.shape, lambda i: (0, 0))
    return pl.pallas_call(
        functools.partial(_outproj_kernel, dils),
        grid=(t // tm,),
        in_specs=view + view + [tok(aw), tok(d), const(wo), const(ln)],
        out_specs=[tok(d), tok(d)],
        out_shape=[jax.ShapeDtypeStruct((t, d), F32), jax.ShapeDtypeStruct((t, d), BF16)],
        scratch_shapes=[pltpu.VMEM((2 * len(dils) * aw // LANES, tm, LANES), F32)],
        compiler_params=_cparams("arbitrary"),
        name="outproj",
    )(*os_, *lses, rw, x2, wo, ln)


def _lanes(col_rep, n):
    return jnp.concatenate([col_rep] * (n // LANES), axis=1)


MOE_TILE = 512
ROW_CHUNK = 8
SORT_ROWS = 512
COMBINE_ROWS = 1024
EXPERT_ROWS = 256


def _tile_rows(ne):
    raw = MOE_TILE * TOP_K + ne * ROW_CHUNK
    unit = max(SORT_ROWS, COMBINE_ROWS)
    return -(-raw // unit) * unit


def _route_kernel(x_ref, rwt_ref, bias_ref, tri_ref, ones_ref, low_ref, dloc_ref, gate_ref, tab_ref):
    ne = rwt_ref.shape[0]
    tm = x_ref.shape[0]
    gsz = ne // N_GROUPS
    ninf = -jnp.inf

    aff = _sigmoid(_dotf(rwt_ref[...], x_ref[...], nt=True))
    sel = aff + _lanes(bias_ref[...], tm)

    sel3 = sel.reshape(N_GROUPS, gsz, tm)
    rid = lax.broadcasted_iota(I32, sel3.shape, 1).astype(F32)
    m1 = jnp.max(sel3, axis=1, keepdims=True)
    i1 = jnp.min(jnp.where(sel3 == m1, rid, float(gsz)), axis=1, keepdims=True)
    m2 = jnp.max(jnp.where(rid == i1, ninf, sel3), axis=1, keepdims=True)
    gsc = (m1 + m2).reshape(N_GROUPS, tm)
    gid = lax.broadcasted_iota(I32, gsc.shape, 0).astype(F32)
    keep = jnp.zeros(gsc.shape, F32)
    for _ in range(TOPK_GROUPS):
        gm = jnp.max(gsc, axis=0, keepdims=True)
        gi = jnp.min(jnp.where(gsc == gm, gid, float(N_GROUPS)), axis=0, keepdims=True)
        hit = gid == gi
        keep = jnp.where(hit, 1.0, keep)
        gsc = jnp.where(hit, ninf, gsc)
    cand = jnp.where(keep.reshape(N_GROUPS, 1, tm) > 0.0, sel3, ninf).reshape(ne, tm)

    eid = lax.broadcasted_iota(I32, (ne, tm), 0).astype(F32)
    hits, graw = [], []
    for _ in range(TOP_K):
        m = jnp.max(cand, axis=0, keepdims=True)
        ij = jnp.min(jnp.where(cand == m, eid, float(ne)), axis=0, keepdims=True)
        hit = eid == ij
        hits.append(hit)
        graw.append(jnp.sum(jnp.where(hit, aff, 0.0), axis=0, keepdims=True))
        cand = jnp.where(hit, ninf, cand)
    gsum = functools.reduce(lambda a, b: a + b, graw)
    gate_ref[...] = jnp.concatenate([g / gsum * ROUTED_SCALE for g in graw], axis=0)

    onehot = functools.reduce(lambda a, b: a + b, [h.astype(F32) for h in hits]).astype(BF16)
    before = _dot(onehot, tri_ref[...])
    count = _dot(onehot, ones_ref[...])
    padded = jnp.maximum(jnp.ceil(count * (1.0 / ROW_CHUNK)) * ROW_CHUNK, ROW_CHUNK)
    start = _dot3(low_ref[...], padded)
    row = _lanes(start, tm) + before
    dloc_ref[...] = jnp.concatenate(
        [jnp.sum(jnp.where(h, row, 0.0), axis=0, keepdims=True) for h in hits], axis=0).astype(I32)
    lane = lax.broadcasted_iota(I32, padded.shape, 1)
    both = jnp.where(lane == 0, padded, jnp.where(lane == 1, start, 0.0))
    tab_ref[...] = both.T[0:SUBLANES, :].astype(I32)


def _route(x1, router_w, router_bias):
    t, d = x1.shape
    ne = router_w.shape[1]
    tm = MOE_TILE
    rwt = router_w.T
    bias = jnp.broadcast_to(router_bias[:, None], (ne, LANES))
    pos = jnp.arange(tm)
    tri = (pos[:, None] < pos[None, :]).astype(BF16)
    ones = jnp.ones((tm, LANES), BF16)
    eid = jnp.arange(ne)
    low = (eid[:, None] > eid[None, :]).astype(BF16)
    const = lambda a: pl.BlockSpec(a.shape, lambda i: (0, 0))
    tokt = pl.BlockSpec((TOP_K, tm), lambda i: (0, i))
    return pl.pallas_call(
        _route_kernel,
        grid=(t // tm,),
        in_specs=[pl.BlockSpec((tm, d), lambda i: (i, 0)), const(rwt), const(bias), const(tri), const(ones),
                  const(low)],
        out_specs=[tokt, tokt, pl.BlockSpec((None, SUBLANES, ne), lambda i: (i, 0, 0))],
        out_shape=[jax.ShapeDtypeStruct((TOP_K, t), I32), jax.ShapeDtypeStruct((TOP_K, t), F32),
                   jax.ShapeDtypeStruct((t // tm, SUBLANES, ne), I32)],
        compiler_params=_cparams("arbitrary"),
        name="route",
    )(x1, rwt, bias, tri, ones, low)


def _pack_bf16_pairs(v, exact=False):
    half = v.shape[1] // 2
    u = pltpu.bitcast(v if exact else v.astype(BF16).astype(F32), U32)
    return (u[:, 0:half] >> 16) | (u[:, half:] & jnp.uint32(0xFFFF0000))


def _unpack_bf16_pairs(w):
    lo = pltpu.bitcast(w << 16, F32)
    hi = pltpu.bitcast(w & jnp.uint32(0xFFFF0000), F32)
    return jnp.concatenate([lo, hi], axis=1)


def _dot3l(a, b_bf16):
    a1 = a.astype(BF16)
    r1 = a - a1.astype(F32)
    a2 = r1.astype(BF16)
    a3 = (r1 - a2.astype(F32)).astype(BF16)
    return _dot(a1, b_bf16) + (_dot(a2, b_bf16) + _dot(a3, b_bf16))


def _plan_kernel(nblk, cnt_ref, low_ref, upper_ref, gstart_ref, meta_ref, emeta_ref):
    c = cnt_ref[...].astype(F32)
    nt, ne = c.shape
    earlier = _dot3(low_ref[...], c)
    total = jnp.sum(c, axis=0, keepdims=True)
    padded = jnp.ceil(total * (1.0 / EXPERT_ROWS)) * EXPERT_ROWS
    pad_end = _dot3l(jnp.broadcast_to(padded, (SUBLANES, ne)), upper_ref[...])
    pad_start = (pad_end - padded)[0:1, :]
    gstart_ref[...] = (pad_start + earlier).astype(I32)
    emeta_ref[...] = jnp.concatenate(
        [pad_start, padded * (1.0 / EXPERT_ROWS), jnp.zeros((SUBLANES - 2, ne), F32)], axis=0).astype(I32)
    end_col = jnp.broadcast_to(pad_end.T[:, 0:1], (ne, nblk))
    blk_start = (lax.broadcasted_iota(I32, (ne, nblk), 1) * EXPERT_ROWS).astype(F32)
    owner = jnp.minimum(jnp.sum((end_col <= blk_start).astype(F32), axis=0, keepdims=True), float(ne - 1))
    used = jnp.max(end_col, axis=0, keepdims=True) * (1.0 / EXPERT_ROWS)
    meta_ref[...] = jnp.concatenate([owner, used, jnp.zeros((SUBLANES - 2, nblk), F32)], axis=0).astype(I32)


def _plan(cnt, nblk_pad):
    nt, ne = cnt.shape
    ti = jnp.arange(nt)
    low = (ti[:, None] > ti[None, :]).astype(BF16)
    ei = jnp.arange(ne)
    upper = (ei[:, None] <= ei[None, :]).astype(BF16)
    return pl.pallas_call(
        functools.partial(_plan_kernel, nblk_pad),
        out_shape=[jax.ShapeDtypeStruct((nt, ne), I32), jax.ShapeDtypeStruct((SUBLANES, nblk_pad), I32),
                   jax.ShapeDtypeStruct((SUBLANES, ne), I32)],
        compiler_params=pltpu.CompilerParams(vmem_limit_bytes=VMEM_LIMIT_BYTES),
        name="plan",
    )(cnt, low, upper)


def _group_copies(cnt_ref, loc_ref, gstart_ref, tile, ne, local, remote, sem, to_remote):
    def one(e, carry):
        n = pl.multiple_of(cnt_ref[tile * ne + e], ROW_CHUNK)
        lo = pl.multiple_of(loc_ref[tile * ne + e], ROW_CHUNK)
        go = pl.multiple_of(gstart_ref[tile * ne + e], ROW_CHUNK)
        a, b = local.at[pl.ds(lo, n), :], remote.at[pl.ds(go, n), :]
        (pltpu.make_async_copy(a, b, sem) if to_remote else pltpu.make_async_copy(b, a, sem)).start()
        return carry

    lax.fori_loop(0, ne, one, 0, unroll=8)


def _group_wait(rows, local, remote, sem, to_remote):
    n = pl.multiple_of(rows, ROW_CHUNK)

    @pl.when(n > 0)
    def _():
        a, b = local.at[pl.ds(0, n), :], remote.at[pl.ds(0, n), :]
        (pltpu.make_async_copy(a, b, sem) if to_remote else pltpu.make_async_copy(b, a, sem)).wait()


def _dispatch_kernel(ne, cnt_ref, loc_ref, gstart_ref, owner_ref, used_ref, urows_ref, dloc_ref, x_ref, xs_ref,
                     buf_ref, zero_ref, sems, zsem):
    step = pl.program_id(0)
    nsteps = pl.num_programs(0)
    slot = step % 2
    tm = x_ref.shape[0]
    nblk = owner_ref.shape[0]
    used = used_ref[0]

    @pl.when(step == 0)
    def _():
        zero_ref[...] = jnp.zeros_like(zero_ref)

        def block_copy(i):
            return pltpu.make_async_copy(zero_ref, xs_ref.at[pl.ds(i * EXPERT_ROWS, EXPERT_ROWS), :], zsem)

        def is_last(i):
            return (i == used - 1) | (owner_ref[jnp.minimum(i + 1, nblk - 1)] != owner_ref[i])

        def start(i, carry):
            @pl.when(is_last(i))
            def _():
                block_copy(i).start()
            return carry

        def wait(i, carry):
            @pl.when(is_last(i))
            def _():
                block_copy(i).wait()
            return carry

        lax.fori_loop(0, used, start, 0)
        lax.fori_loop(0, used, wait, 0)

    x = x_ref[...]
    dl = dloc_ref[...].astype(I16)
    riota = lax.broadcasted_iota(I32, (SORT_ROWS, tm), 0).astype(I16)
    one = jnp.ones((SORT_ROWS, tm), BF16)

    def chunk(c, carry):
        r0 = pl.multiple_of(c * SORT_ROWS, SORT_ROWS)
        r = riota + r0.astype(I16)
        p = jnp.zeros((SORT_ROWS, tm), BF16)
        for j in range(TOP_K):
            p = jnp.where(r == dl[j:j + 1, :], one, p)
        buf_ref[slot, pl.ds(r0, SORT_ROWS), :] = _pack_bf16_pairs(_dot(p, x), exact=True)
        return carry

    lax.fori_loop(0, (urows_ref[step] + SORT_ROWS - 1) // SORT_ROWS, chunk, 0)

    copies = functools.partial(_group_copies, cnt_ref, loc_ref, gstart_ref)
    copies(step, ne, buf_ref.at[slot], xs_ref, sems.at[slot], True)

    @pl.when(step > 0)
    def _():
        _group_wait(urows_ref[jnp.maximum(step - 1, 0)], buf_ref.at[1 - slot], xs_ref, sems.at[1 - slot], True)

    @pl.when(step == nsteps - 1)
    def _():
        _group_wait(urows_ref[step], buf_ref.at[slot], xs_ref, sems.at[slot], True)


def _dispatch(x1b, dloc, cnt, loc, gstart, owner, used, used_rows, nrows):
    t, d = x1b.shape
    nt, ne = cnt.shape
    tm = MOE_TILE
    rt = _tile_rows(ne)
    grid_spec = pltpu.PrefetchScalarGridSpec(
        num_scalar_prefetch=6,
        grid=(nt,),
        in_specs=[pl.BlockSpec((TOP_K, tm), lambda i, *_: (0, i)), pl.BlockSpec((tm, d), lambda i, *_: (i, 0))],
        out_specs=pl.BlockSpec(memory_space=pl.ANY),
        scratch_shapes=[pltpu.VMEM((2, rt, d // 2), U32), pltpu.VMEM((EXPERT_ROWS, d // 2), U32),
                        pltpu.SemaphoreType.DMA((2,)), pltpu.SemaphoreType.DMA(())],
    )
    return pl.pallas_call(
        functools.partial(_dispatch_kernel, ne),
        grid_spec=grid_spec,
        out_shape=jax.ShapeDtypeStruct((nrows, d // 2), U32),
        compiler_params=_cparams("arbitrary"),
        name="dispatch",
    )(cnt.reshape(-1), loc.reshape(-1), gstart.reshape(-1), owner, used, used_rows, dloc, x1b)


EXPERT_PIECE = 1024


def _expert_kernel(first_ref, nblk_ref, xs_ref, wg_ref, wu_ref, wd_ref, ys_ref,
                   xbuf, ybuf, wgb_ref, wub_ref, wdb_ref, sem_in, sem_out, done_ref, pend_ref):
    e = pl.program_id(0)
    last = pl.num_programs(0) - 1
    per_piece = EXPERT_PIECE // EXPERT_ROWS

    def pieces(ex):
        return (nblk_ref[ex] + per_piece - 1) // per_piece

    def span(ex, s):
        n = jnp.minimum(EXPERT_PIECE, nblk_ref[ex] * EXPERT_ROWS - s * EXPERT_PIECE)
        n = pl.multiple_of(n, EXPERT_ROWS)
        return pl.ds(pl.multiple_of(first_ref[ex] + s * EXPERT_PIECE, EXPERT_ROWS), n), pl.ds(0, n)

    def in_copy(ex, s, slot):
        far, near = span(ex, s)
        return pltpu.make_async_copy(xs_ref.at[far, :], xbuf.at[slot, near, :], sem_in.at[slot])

    def out_copy(ex, s, slot):
        far, near = span(ex, s)
        return pltpu.make_async_copy(ybuf.at[slot, near, :], ys_ref.at[far, :], sem_out.at[slot])

    @pl.when(e == 0)
    def _():
        done_ref[0] = 0
        for slot in range(2):
            pend_ref[2 * slot] = -1

        @pl.when(nblk_ref[0] > 0)
        def _():
            in_copy(0, 0, 0).start()

    nb = nblk_ref[e]
    ns = pieces(e)
    g0 = done_ref[0]
    nxt = jnp.minimum(e + 1, last)
    has_next = (e < last) & (nblk_ref[nxt] > 0)

    def release(slot):
        pe = pend_ref[2 * slot]

        @pl.when(pe >= 0)
        def _():
            out_copy(pe, pend_ref[2 * slot + 1], slot).wait()

    @pl.when(nb > 0)
    def _():
        wgb_ref[...] = wg_ref[...].astype(BF16)
        wub_ref[...] = wu_ref[...].astype(BF16)
        wdb_ref[...] = wd_ref[...].astype(BF16)

        def piece(s, carry):
            slot = (g0 + s) % 2

            @pl.when(s + 1 < ns)
            def _():
                in_copy(e, s + 1, 1 - slot).start()

            @pl.when((s + 1 == ns) & has_next)
            def _():
                in_copy(nxt, 0, 1 - slot).start()

            in_copy(e, s, slot).wait()
            release(slot)

            blocks_here = jnp.minimum(per_piece, nb - s * per_piece)
            for nblocks in range(1, per_piece + 1):
                @pl.when(blocks_here == nblocks)
                def _():
                    rows = pl.ds(0, nblocks * EXPERT_ROWS)
                    xb = _unpack_bf16_pairs(xbuf[slot, rows, :]).astype(BF16)
                    hg = _dot(xb, wgb_ref[...])
                    hb = hg * _sigmoid(hg) * _dot(xb, wub_ref[...])
                    ybuf[slot, rows, :] = _pack_bf16_pairs(_dot(hb.astype(BF16), wdb_ref[...]))
            out_copy(e, s, slot).start()
            pend_ref[2 * slot] = e
            pend_ref[2 * slot + 1] = s
            return carry

        lax.fori_loop(0, ns, piece, 0)

    @pl.when((nb == 0) & has_next)
    def _():
        in_copy(nxt, 0, g0 % 2).start()

    @pl.when(e == last)
    def _():
        for slot in range(2):
            release(slot)

    done_ref[0] = g0 + ns


def _experts(xs, first_row, nblocks, e_gate, e_up, e_down):
    nrows, dh = xs.shape
    ne, d, ff = e_gate.shape
    wsel = lambda e, *_: (e, 0, 0)
    grid_spec = pltpu.PrefetchScalarGridSpec(
        num_scalar_prefetch=2,
        grid=(ne,),
        in_specs=[pl.BlockSpec(memory_space=pl.ANY),
                  pl.BlockSpec((None, d, ff), wsel), pl.BlockSpec((None, d, ff), wsel),
                  pl.BlockSpec((None, ff, d), wsel)],
        out_specs=pl.BlockSpec(memory_space=pl.ANY),
        scratch_shapes=[pltpu.VMEM((2, EXPERT_PIECE, dh), U32), pltpu.VMEM((2, EXPERT_PIECE, dh), U32),
                        pltpu.VMEM((d, ff), BF16), pltpu.VMEM((d, ff), BF16), pltpu.VMEM((ff, d), BF16),
                        pltpu.SemaphoreType.DMA((2,)), pltpu.SemaphoreType.DMA((2,)),
                        pltpu.SMEM((1,), I32), pltpu.SMEM((4,), I32)],
    )
    return pl.pallas_call(
        _expert_kernel,
        grid_spec=grid_spec,
        out_shape=jax.ShapeDtypeStruct((nrows, dh), U32),
        compiler_params=_cparams("arbitrary"),
        name="experts",
    )(first_row, nblocks, xs, e_gate, e_up, e_down)


def _final_kernel(ne, cnt_ref, loc_ref, gstart_ref, used_ref, dloc_ref, gate_ref, x_ref, xb_ref, ys_ref,
                  sg_ref, su_ref, sd_ref, ln_ref, o_ref, buf_ref, acc_ref, sems):
    step = pl.program_id(0)
    nsteps = pl.num_programs(0)
    slot = step % 2
    tm = x_ref.shape[0]
    used = used_ref[step]
    copies = functools.partial(_group_copies, cnt_ref, loc_ref, gstart_ref)

    @pl.when(step == 0)
    def _():
        copies(step, ne, buf_ref.at[0], ys_ref, sems.at[0], False)

    @pl.when(step + 1 < nsteps)
    def _():
        copies(step + 1, ne, buf_ref.at[1 - slot], ys_ref, sems.at[1 - slot], False)

    xb = xb_ref[...]
    hg = _dot(xb, sg_ref[...])
    hs = hg * _sigmoid(hg) * _dot(xb, su_ref[...])
    acc_ref[...] = _dot(hs.astype(BF16), sd_ref[...])

    _group_wait(used, buf_ref.at[slot], ys_ref, sems.at[slot], False)

    dl = dloc_ref[...].astype(I16)
    gt = gate_ref[...].astype(BF16)
    rows16 = lax.broadcasted_iota(I32, (COMBINE_ROWS, tm), 0).astype(I16)
    gts = [jnp.broadcast_to(gt[j:j + 1, :], (COMBINE_ROWS, tm)) for j in range(TOP_K)]
    riota = lax.broadcasted_iota(I32, (COMBINE_ROWS, buf_ref.shape[2]), 0)

    def chunk(c, carry):
        r0 = pl.multiple_of(c * COMBINE_ROWS, COMBINE_ROWS)
        r = rows16 + r0.astype(I16)
        g = jnp.zeros((COMBINE_ROWS, tm), BF16)
        for j in range(TOP_K):
            g = jnp.where(r == dl[j:j + 1, :], gts[j], g)
        w = buf_ref[slot, pl.ds(r0, COMBINE_ROWS), :]
        w = jnp.where(riota + r0 < used, w, jnp.zeros_like(w))
        y = _unpack_bf16_pairs(w).astype(BF16)
        acc_ref[...] += lax.dot_general(g, y, (((0,), (0,)), ((), ())), preferred_element_type=F32)
        return carry

    lax.fori_loop(0, (used + COMBINE_ROWS - 1) // COMBINE_ROWS, chunk, 0)
    o_ref[...] = _layer_norm(ALPHA * x_ref[...] + acc_ref[...], ln_ref[0:1, :], ln_ref[1:2, :])


def _final(x1, x1b, dloc, gate, cnt, loc, gstart, used_rows, ys, s_gate, s_up, s_down, ln_g, ln_b):
    t, d = x1.shape
    nt, ne = cnt.shape
    tm = MOE_TILE
    rt = _tile_rows(ne)
    ln = jnp.concatenate([ln_g[None], ln_b[None], jnp.zeros((SUBLANES - 2, d), F32)], axis=0)
    sg, su, sd = s_gate.astype(BF16), s_up.astype(BF16), s_down.astype(BF16)
    tok = pl.BlockSpec((tm, d), lambda i, *_: (i, 0))
    tokt = pl.BlockSpec((TOP_K, tm), lambda i, *_: (0, i))
    const = lambda a: pl.BlockSpec(a.shape, lambda i, *_: (0, 0))
    grid_spec = pltpu.PrefetchScalarGridSpec(
        num_scalar_prefetch=4,
        grid=(nt,),
        in_specs=[tokt, tokt, tok, tok, pl.BlockSpec(memory_space=pl.ANY), const(sg), const(su), const(sd), const(ln)],
        out_specs=tok,
        scratch_shapes=[pltpu.VMEM((2, rt, d // 2), U32), pltpu.VMEM((tm, d), F32), pltpu.SemaphoreType.DMA((2,))],
    )
    return pl.pallas_call(
        functools.partial(_final_kernel, ne),
        grid_spec=grid_spec,
        out_shape=jax.ShapeDtypeStruct((t, d), F32),
        compiler_params=_cparams("arbitrary"),
        name="final",
    )(cnt.reshape(-1), loc.reshape(-1), gstart.reshape(-1), used_rows, dloc, gate, x1, x1b, ys, sg, su, sd, ln)


def _moe(x1, x1b, router_w, router_bias, e_gate, e_up, e_down, s_gate, s_up, s_down, ln_g, ln_b):
    t = x1.shape[0]
    ne = router_w.shape[1]
    nt = t // MOE_TILE
    nblk = -(-(nt * _tile_rows(ne)) // EXPERT_ROWS) + ne
    nblk_pad = -(-nblk // LANES) * LANES
    dloc, gate, tab = _route(x1, router_w, router_bias)
    cnt, loc = tab[:, 0, :], tab[:, 1, :]
    used_rows = loc[:, ne - 1] + cnt[:, ne - 1]
    gstart, meta, emeta = _plan(cnt, nblk_pad)
    owner, used = meta[0], meta[1, 0:1]
    xs = _dispatch(x1b, dloc, cnt, loc, gstart, owner, used, used_rows, nblk * EXPERT_ROWS)
    ys = _experts(xs, emeta[0], emeta[1], e_gate, e_up, e_down)
    return _final(x1, x1b, dloc, gate, cnt, loc, gstart, used_rows, ys, s_gate, s_up, s_down, ln_g, ln_b)


def kernel(x, w_in, mu_shift, w0, w_decay_up, a0, w_aaa_up, w_gate_up, k_k, k_a, r_k, lnx_g, lnx_b, w_out,
           ln1_g, ln1_b, router_w, router_bias, e_gate, e_up, e_down, s_gate, s_up, s_down, ln2_g, ln2_b):
    bsz, seq, d = x.shape
    x2 = x.reshape(bsz * seq, d)
    qkvs, feats = _inproj(x2, seq, w_in[0], mu_shift[0], w0[0], w_decay_up[0], a0[0], w_aaa_up[0], w_gate_up[0],
                          k_k[0], k_a[0], r_k[0], tm=256)
    rw = _rwkv(feats, bsz, seq, lnx_g[0], lnx_b[0], nbat=8)
    x1, x1b = _mixer_tail(qkvs, rw, x2, bsz, seq, w_out[0], ln1_g[0], ln1_b[0])
    out = _moe(x1, x1b, router_w[0], router_bias[0], e_gate[0], e_up[0], e_down[0], s_gate[0], s_up[0], s_down[0],
               ln2_g[0], ln2_b[0])
    return out.reshape(bsz, seq, d)


def _mixer_tail(qkvs, rw, x2, bsz, seq, w_out, ln_g, ln_b):
    res = [_attention(q, k, v, bsz, seq, window, dilation)
           for (q, k, v), (window, dilation) in zip(qkvs, ATTN_PATTERNS)]
    dils = tuple(dl for _, dl in ATTN_PATTERNS)
    return _outproj([o for o, _ in res], [l for _, l in res], dils, rw, x2, w_out, ln_g, ln_b, tm=512)
```

```python
import functools

import jax
import jax.numpy as jnp
from jax import lax
from jax.experimental import pallas as pl
from jax.experimental.pallas import tpu as pltpu

F32 = jnp.float32
BF16 = jnp.bfloat16
I32 = jnp.int32
I16 = jnp.int16
U32 = jnp.uint32

LANES = 128
SUBLANES = 8
VMEM_LIMIT_BYTES = 56 * 1024 * 1024

HEAD_DIM = 64
ATTN_HEADS = 8
RWKV_HEADS = 8
ATTN_WIDTH = ATTN_HEADS * HEAD_DIM
RWKV_WIDTH = RWKV_HEADS * HEAD_DIM
ATTN_PATTERNS = ((128, 1), (512, 4), (2048, 16))
Q_BLOCK = 128
ROPE_THETA = 10000.0
DECAY_LORA = 64
AAA_LORA = 64
GATE_LORA = 160
GN_EPS = 64e-5
LN_EPS = 1e-5
TOP_K = 8
N_GROUPS = 8
TOPK_GROUPS = 4
ROUTED_SCALE = 2.5
DEPTH = 1
ALPHA = (2.0 * DEPTH) ** 0.25

RWKV_CHUNK = 64
PAIR = 2 * HEAD_DIM
LORA_PAD = 2 * LANES


def _cparams(*sem):
    return pltpu.CompilerParams(dimension_semantics=sem, vmem_limit_bytes=VMEM_LIMIT_BYTES)


def _split_bf16(a):
    hi = a.astype(BF16)
    lo = (a - hi.astype(F32)).astype(BF16)
    return hi, lo


def _dot(a, b):
    return jnp.dot(a, b, preferred_element_type=F32)


def _dot_nt(a, b):
    return lax.dot_general(a, b, (((1,), (1,)), ((), ())), preferred_element_type=F32)


def _softplus(z):
    return jnp.maximum(z, 0.0) + jnp.log(1.0 + jnp.exp(-jnp.abs(z)))


def _sigmoid(z):
    return 1.0 / (1.0 + jnp.exp(-z))


def _inproj_kernel(dils, x_ref, w_ref, cos_ref, sin_ref, mu_ref, wd_ref, wa_ref, wg_ref, vec_ref, *refs):
    nq = 3 * len(dils)
    qkv_refs = refs[0:nq]
    r_ref, ld_ref, kp_ref, vv_ref, kk_ref, b_ref, g_ref, bon_ref, carry_ref, qkv_scr, qkv_scr2 = refs[nq:]
    s = pl.program_id(1)
    tm = x_ref.shape[0]
    aw = ATTN_WIDTH
    rw = RWKV_WIDTH

    @pl.when(s == 0)
    def _():
        carry_ref[...] = jnp.zeros_like(carry_ref)

    h = _dot(x_ref[...].astype(BF16), w_ref[...])
    reps = aw // cos_ref.shape[1]
    cos = jnp.concatenate([cos_ref[...]] * reps, axis=1)
    sin = jnp.concatenate([sin_ref[...]] * reps, axis=1)
    qkv = [(h[:, 0:aw] * cos + h[:, aw:2 * aw] * sin) * (HEAD_DIM ** -0.5),
           h[:, 2 * aw:3 * aw] * cos + h[:, 3 * aw:4 * aw] * sin,
           h[:, 4 * aw:5 * aw]]
    nslab = aw // LANES
    for i in range(3):
        for j in range(nslab):
            qkv_scr[i * nslab + j] = qkv[i][:, j * LANES:(j + 1) * LANES]
    src, src_d = qkv_scr, 1
    for di, d in enumerate(dils):
        if d == 1:
            for i in range(3):
                qkv_refs[3 * di + i][...] = qkv[i].astype(qkv_refs[3 * di + i].dtype)
            continue
        assert d % src_d == 0
        step, per = d // src_d, tm // src_d
        dst = qkv_scr2 if src is qkv_scr else qkv_scr
        for i in range(3):
            o_ref = qkv_refs[3 * di + i]
            for res in range(d):
                lo, hi = res % src_d, res // src_d
                for j in range(nslab):
                    rows = src[i * nslab + j, pl.ds(lo * per + hi, tm // d, stride=step), :]
                    if di + 1 < len(dils):
                        dst[i * nslab + j, res * (tm // d):(res + 1) * (tm // d), :] = rows
                    o_ref[:, res * aw + j * LANES:res * aw + (j + 1) * LANES] = rows.astype(o_ref.dtype)
        src, src_d = dst, d

    f = h[:, 5 * aw:]
    rows = lax.broadcasted_iota(I32, f.shape, 0)
    prev = jnp.where(rows == 0, carry_ref[SUBLANES - 1:SUBLANES, :], pltpu.roll(f, 1, axis=0))
    carry_ref[...] = f[tm - SUBLANES:tm, :]
    f = f + (prev - f) * mu_ref[...]

    r = f[:, 0:rw]
    k = f[:, rw:2 * rw]
    v = f[:, 2 * rw:3 * rw]
    la = f[:, 3 * rw:3 * rw + LANES]
    gl = f[:, 3 * rw + LANES:]
    w0, a0, k_k, k_a, r_k = (vec_ref[i:i + 1, :] for i in range(5))

    z = w0 + _dot(jnp.tanh(la).astype(BF16), wd_ref[...])
    w = -_softplus(-z) - 0.5
    ld_ref[...] = -jnp.exp(w)
    a = _sigmoid(a0 + _dot(la.astype(BF16), wa_ref[...]))
    g_ref[...] = _dot(_sigmoid(gl).astype(BF16), wg_ref[...])

    first = lax.broadcasted_iota(I32, (tm, PAIR), 1) < HEAD_DIM

    def head_sum(z):
        parts = []
        for p in range(rw // PAIR):
            zp = z[:, p * PAIR:(p + 1) * PAIR]
            s0 = jnp.sum(jnp.where(first, zp, 0.0), axis=1, keepdims=True)
            s1 = jnp.sum(jnp.where(first, 0.0, zp), axis=1, keepdims=True)
            parts.append(jnp.where(first, s0, s1))
        return jnp.concatenate(parts, axis=1)

    kk = k * k_k
    kk = kk / jnp.maximum(jnp.sqrt(head_sum(kk * kk)), 1e-12)
    kp = k * (1.0 + (a - 1.0) * k_a)
    r_ref[...] = r
    kp_ref[...] = kp
    vv_ref[...] = v
    kk_ref[...] = kk
    b_ref[...] = kk * a
    bon_ref[...] = head_sum(r * kp * r_k) * v


def _rot_half_cols(w):
    d, n = w.shape
    w4 = w.reshape(d, n // HEAD_DIM, 2, HEAD_DIM // 2)
    return jnp.stack([-w4[:, :, 1, :], w4[:, :, 0, :]], axis=2).reshape(d, n)


def _inproj(x2, seq, w_in, mu_shift, w0, w_decay_up, a0, w_aaa_up, w_gate_up, k_k, k_a, r_k, tm):
    t, d = x2.shape
    aw, rw = ATTN_WIDTH, RWKV_WIDTH
    wq, wk, wv = w_in[:, 0:aw], w_in[:, aw:2 * aw], w_in[:, 2 * aw:3 * aw]
    wf = w_in[:, 3 * aw:]
    gpad = LORA_PAD - GATE_LORA
    w_all = jnp.concatenate(
        [wq, _rot_half_cols(wq), wk, _rot_half_cols(wk), wv, wf, jnp.zeros((d, gpad), F32)], axis=1).astype(BF16)
    mu = jnp.concatenate([mu_shift, jnp.zeros((gpad,), F32)])[None, :]
    nf = mu.shape[1]
    wd = jnp.concatenate([w_decay_up, jnp.zeros((AAA_LORA, rw), F32)], axis=0).astype(BF16)
    wa = jnp.concatenate([jnp.zeros((DECAY_LORA, rw), F32), w_aaa_up], axis=0).astype(BF16)
    wg = jnp.concatenate([w_gate_up, jnp.zeros((gpad, rw), F32)], axis=0).astype(BF16)
    vec = jnp.stack([w0, a0, k_k, k_a, r_k.reshape(-1), w0 * 0, w0 * 0, w0 * 0])
    half = HEAD_DIM // 2
    inv_freq = ROPE_THETA ** (-jnp.arange(half, dtype=F32) * 2.0 / HEAD_DIM)
    ang = jnp.arange(seq, dtype=F32)[:, None] * inv_freq[None, :]
    cos = jnp.tile(jnp.cos(ang), (1, LANES // half))
    sin = jnp.tile(jnp.sin(ang), (1, LANES // half))

    nst = seq // tm
    tok = lambda b, s: (b * nst + s, 0)
    const = lambda b, s: (0, 0)
    full = lambda a: pl.BlockSpec(a.shape, const)
    dils = tuple(dl for _, dl in ATTN_PATTERNS)
    qkv_specs, qkv_shapes = [], []
    for dl in dils:
        qkv_specs += [pl.BlockSpec((tm // dl, dl * aw), tok)] * 3
        qkv_shapes += [jax.ShapeDtypeStruct((t // dl, dl * aw), BF16)] * 3
    out_f = jax.ShapeDtypeStruct((t, rw), F32)
    outs = pl.pallas_call(
        functools.partial(_inproj_kernel, dils),
        grid=(t // seq, nst),
        in_specs=[pl.BlockSpec((tm, d), tok), full(w_all),
                  pl.BlockSpec((tm, LANES), lambda b, s: (s, 0)), pl.BlockSpec((tm, LANES), lambda b, s: (s, 0)),
                  full(mu), full(wd), full(wa), full(wg), full(vec)],
        out_specs=qkv_specs + [pl.BlockSpec((tm, rw), tok)] * 8,
        out_shape=qkv_shapes + [out_f] * 8,
        scratch_shapes=[pltpu.VMEM((SUBLANES, nf), F32), pltpu.VMEM((3 * aw // LANES, tm, LANES), F32),
                        pltpu.VMEM((3 * aw // LANES, tm, LANES), F32)],
        compiler_params=_cparams("arbitrary", "arbitrary"),
        name="inproj",
    )(x2, w_all, cos, sin, mu, wd, wa, wg, vec)
    nq = 3 * len(dils)
    return [outs[3 * i:3 * i + 3] for i in range(len(dils))], outs[nq:]


def _dotf(a, b, nt=False, passes=3):
    dot = _dot_nt if nt else _dot
    if passes == 1:
        return dot(a.astype(BF16), b.astype(BF16))
    ah, al = _split_bf16(a)
    if passes == 2:
        bh = b.astype(BF16)
        return dot(ah, bh) + dot(al, bh)
    bh, bl = _split_bf16(b)
    return dot(ah, bh) + (dot(ah, bl) + dot(al, bh))


RWKV_PASSES = dict(s8=1, inv=1, w1=1, au=1, ry=1, gh=1, yh=1)


def _dot3(a_bf16, b):
    b1 = b.astype(BF16)
    r1 = b - b1.astype(F32)
    b2 = r1.astype(BF16)
    b3 = (r1 - b2.astype(F32)).astype(BF16)
    return _dot(a_bf16, b1) + (_dot(a_bf16, b2) + _dot(a_bf16, b3))


def _rwkv_kernel(r_ref, ld_ref, kp_ref, v_ref, kk_ref, b_ref, g_ref, bon_ref, gn_ref, o_ref, h_ref):
    c = pl.program_id(1)
    nbat, ch, rw = r_ref.shape
    npairs = rw // PAIR

    @pl.when(c == 0)
    def _():
        h_ref[...] = jnp.zeros_like(h_ref)

    ri = lax.broadcasted_iota(I32, (ch, ch), 0)
    ci = lax.broadcasted_iota(I32, (ch, ch), 1)
    tril = (ri >= ci).astype(BF16)
    row = lax.broadcasted_iota(I32, (ch, PAIR), 0)
    col = lax.broadcasted_iota(I32, (ch, PAIR), 1)
    first = col < HEAD_DIM
    jj = col & (HEAD_DIM - 1)
    strict = jj < row
    incl = jj <= row
    eye = (jj == row).astype(F32)
    row2 = lax.broadcasted_iota(I32, (PAIR, PAIR), 0)
    col2 = lax.broadcasted_iota(I32, (PAIR, PAIR), 1)
    same_head = (row2 < HEAD_DIM) == (col2 < HEAD_DIM)
    diag2 = row2 == col2
    zeros_cp = jnp.zeros((ch, PAIR), F32)

    def bdiag(y):
        return jnp.concatenate([jnp.where(first, y, 0.0), jnp.where(first, 0.0, y)], axis=0)

    ps = RWKV_PASSES
    units = []
    for bi in range(nbat):
        ld = ld_ref[bi]
        cum = _dot3(tril, ld)
        tot = cum[ch - 1:ch, :]
        a_t = -kk_ref[bi] * jnp.exp(cum - ld)
        pinv = jnp.exp(-cum)
        b_t = b_ref[bi] * pinv
        k_t = kp_ref[bi] * pinv
        r_t = r_ref[bi] * jnp.exp(cum)
        pend = jnp.exp(tot - cum)
        b_end = b_ref[bi] * pend
        k_end = kp_ref[bi] * pend
        p_tot = jnp.exp(tot)
        v_all = v_ref[bi]
        for p in range(npairs):
            sl = slice(p * PAIR, (p + 1) * PAIR)
            units.append(dict(h=bi * npairs + p, a=a_t[:, sl], r=r_t[:, sl], b=b_t[:, sl], k=k_t[:, sl],
                              v=v_all[:, sl], be=b_end[:, sl], ke=k_end[:, sl], pt=p_tot[:, sl]))

    for u in units:
        u["s8"] = _dotf(jnp.concatenate([u["a"], u["r"]], axis=0),
                        jnp.concatenate([bdiag(u["b"]), bdiag(u["k"])], axis=0), nt=True, passes=ps["s8"])
    for u in units:
        s8 = u.pop("s8")
        u["l"] = jnp.where(strict, s8[0:ch, 0:PAIR], 0.0)
        u["ak"] = jnp.where(strict, s8[0:ch, PAIR:], 0.0)
        u["mrb"] = jnp.where(incl, s8[ch:, 0:PAIR], 0.0)
        u["mrk"] = jnp.where(incl, s8[ch:, PAIR:], 0.0)
        u["t"] = eye + u["l"]
    for u in units:
        u["lk"] = _dotf(u["l"], bdiag(u["l"]), passes=ps["inv"])
        u["w1"] = _dotf(u["ak"], bdiag(u["v"]), passes=ps["w1"])
    for _ in range(max(ch.bit_length() - 3, 0)):
        for u in units:
            u["both"] = _dotf(jnp.concatenate([u["t"], u["lk"]], axis=0), bdiag(u["lk"]), passes=ps["inv"])
        for u in units:
            both = u.pop("both")
            u["t"] = u["t"] + both[0:ch]
            u["lk"] = both[ch:]
    for u in units:
        u["t"] = u["t"] + _dotf(u["t"], bdiag(u["lk"]), passes=ps["inv"])
    for u in units:
        u["au"] = _dotf(u["t"], jnp.concatenate([bdiag(u["a"]), bdiag(u["w1"])], axis=1), passes=ps["au"])
    for u in units:
        a_hat, u_loc = u["au"][:, 0:PAIR], u["au"][:, PAIR:]
        rhs = jnp.concatenate([
            jnp.concatenate([bdiag(a_hat), bdiag(u_loc)], axis=1),
            jnp.concatenate([jnp.zeros((2 * ch, PAIR), F32), bdiag(u["v"])], axis=1)], axis=0)
        u["ry"] = _dotf(jnp.concatenate([u["mrb"], u["mrk"]], axis=1), rhs, passes=ps["ry"])
        bkt = jnp.concatenate([u["be"], u["ke"]], axis=0).T
        u["gh"] = _dotf(bkt, jnp.concatenate([u["au"], jnp.concatenate([zeros_cp, u["v"]], axis=1)], axis=0),
                        passes=ps["gh"])
    for u in units:
        r_hat = u["r"] + u["ry"][:, 0:PAIR]
        g_mat = jnp.where(same_head, u["gh"][:, 0:PAIR], 0.0) + jnp.where(diag2, u["pt"], 0.0)
        u["yh"] = _dotf(jnp.concatenate([r_hat, g_mat], axis=0), h_ref[u["h"]], passes=ps["yh"])
    for u in units:
        h_ref[u["h"]] = u["yh"][ch:] + jnp.where(same_head, u["gh"][:, PAIR:], 0.0)
        u["y"] = u["yh"][0:ch] + u["ry"][:, PAIR:]

    inv_n = 1.0 / HEAD_DIM

    def head_mean(z):
        s0 = jnp.sum(jnp.where(first, z, 0.0), axis=1, keepdims=True)
        s1 = jnp.sum(jnp.where(first, 0.0, z), axis=1, keepdims=True)
        return jnp.where(first, s0, s1) * inv_n

    for u in units:
        yc = u["y"] - head_mean(u["y"])
        u["yn"] = yc * lax.rsqrt(head_mean(yc * yc) + GN_EPS)
    for bi in range(nbat):
        yn = jnp.concatenate([u["yn"] for u in units[bi * npairs:(bi + 1) * npairs]], axis=1)
        o_ref[bi] = (yn * gn_ref[0:1, :] + gn_ref[1:2, :] + bon_ref[bi]) * g_ref[bi]


def _rwkv(feats, bsz, seq, lnx_g, lnx_b, nbat):
    t, rw = feats[0].shape
    ch = RWKV_CHUNK
    gn = jnp.concatenate([lnx_g[None], lnx_b[None], jnp.zeros((SUBLANES - 2, rw), F32)], axis=0)
    tok = pl.BlockSpec((nbat, ch, rw), lambda bb, c: (bb, c, 0))
    const = lambda a: pl.BlockSpec(a.shape, lambda bb, c: (0, 0))
    out = pl.pallas_call(
        _rwkv_kernel,
        grid=(bsz // nbat, seq // ch),
        in_specs=[tok] * 8 + [const(gn)],
        out_specs=tok,
        out_shape=jax.ShapeDtypeStruct((bsz, seq, rw), F32),
        scratch_shapes=[pltpu.VMEM((nbat * (rw // PAIR), PAIR, PAIR), F32)],
        compiler_params=_cparams("arbitrary", "arbitrary"),
        name="rwkv",
    )(*[f.reshape(bsz, seq, rw) for f in feats], gn)
    return out.reshape(t, rw)


NEG_BIG = -1e30


def _attn_kernel(n_back, q_ref, kc_ref, kp_ref, vc_ref, vp_ref, o_ref, lse_ref):
    n = pl.program_id(2)
    qb = kp_ref.shape[0]
    nq = q_ref.shape[0] // qb
    qi = lax.broadcasted_iota(I32, (qb, 2 * qb), 0)
    ki = lax.broadcasted_iota(I32, (qb, 2 * qb), 1)
    dist = qb + qi - ki
    band = (dist >= 0) & (dist <= n_back)
    has_prev = jnp.where(n > 0, 0, qb)
    valid = [band & (ki >= has_prev)] + [band] * (nq - 1)
    first = lax.broadcasted_iota(I32, (qb, PAIR), 1) < HEAD_DIM
    npairs = q_ref.shape[1] // PAIR
    heads = []
    for u in range(nq):
        rows = slice(u * qb, (u + 1) * qb)
        for p in range(npairs):
            sl = slice(p * PAIR, (p + 1) * PAIR)
            q2 = q_ref[rows, sl]
            keys = jnp.concatenate([kp_ref[:, sl], kc_ref[:, sl]], axis=0)
            k2 = keys[u * qb:(u + 2) * qb]
            for hh in range(2):
                keep = first if hh == 0 else jnp.logical_not(first)
                s = _dot_nt(jnp.where(keep, q2, jnp.zeros_like(q2)), k2)
                heads.append(dict(u=u, rows=rows, sl=sl, s=s))
    for h in heads:
        ok = valid[h["u"]]
        s = jnp.where(ok, h.pop("s"), NEG_BIG)
        m = jnp.max(s, axis=1, keepdims=True)
        pe = jnp.where(ok, jnp.exp(s - m), 0.0)
        l = jnp.sum(pe, axis=1, keepdims=True)
        h["pe"] = pe.astype(BF16)
        h["l"] = l
        h["lse"] = m + jnp.log(l)
    for h in heads:
        sl, u = h["sl"], h["u"]
        vals = jnp.concatenate([vp_ref[:, sl], vc_ref[:, sl]], axis=0)
        h["o"] = _dot(h.pop("pe"), vals[u * qb:(u + 2) * qb]) / h["l"]
    for i in range(0, len(heads), 2):
        h0, h1 = heads[i], heads[i + 1]
        o_ref[h0["rows"], h0["sl"]] = jnp.where(first, h0["o"], h1["o"])
        lse_ref[h0["rows"], h0["sl"]] = jnp.where(first, h0["lse"], h1["lse"])


ATTN_QBLOCKS = 4


def _attention(q, k, v, bsz, seq, window, dilation):
    aw = q.shape[1] // dilation
    nq = min(ATTN_QBLOCKS, seq // dilation // Q_BLOCK)
    nb = seq // dilation // (nq * Q_BLOCK)
    cur = pl.BlockSpec((nq * Q_BLOCK, aw), lambda b, r, n: (b * nb + n, r))
    prv = pl.BlockSpec((Q_BLOCK, aw), lambda b, r, n: (jnp.maximum((b * nb + n) * nq - 1, b * nb * nq), r))
    shp = jax.ShapeDtypeStruct(q.shape, F32)
    return pl.pallas_call(
        functools.partial(_attn_kernel, window // dilation),
        grid=(bsz, dilation, nb),
        in_specs=[cur, cur, prv, cur, prv],
        out_specs=[cur, cur],
        out_shape=[shp, shp],
        compiler_params=_cparams("arbitrary", "arbitrary", "arbitrary"),
        name=f"attn_d{dilation}",
    )(q, k, k, v, v)


def _layer_norm(y, g, b):
    mu = jnp.mean(y, axis=-1, keepdims=True)
    yc = y - mu
    var = jnp.mean(yc * yc, axis=-1, keepdims=True)
    return yc * lax.rsqrt(var + LN_EPS) * g + b


def _outproj_kernel(dils, *refs):
    npat = len(dils)
    o_refs = refs[0:npat]
    lse_refs = refs[npat:2 * npat]
    rw_ref, x_ref, wo_ref, ln_ref, y_ref, yb_ref, scr = refs[2 * npat:]
    tm, aw = rw_ref.shape

    def token_major(ref, d, slot):
        if d == 1:
            return ref[...]
        nslab = aw // LANES
        for res in range(d):
            for j in range(nslab):
                scr[slot * nslab + j, pl.ds(res, tm // d, stride=d), :] = (
                    ref[:, res * aw + j * LANES:res * aw + (j + 1) * LANES])
        return jnp.concatenate([scr[slot * nslab + j] for j in range(nslab)], axis=1)

    lses = [token_major(r, d, 2 * i) for i, (r, d) in enumerate(zip(lse_refs, dils))]
    outs = [token_major(r, d, 2 * i + 1) for i, (r, d) in enumerate(zip(o_refs, dils))]
    m = functools.reduce(jnp.maximum, lses)
    es = [jnp.exp(z - m) for z in lses]
    den = functools.reduce(lambda a, b: a + b, es)
    attn = functools.reduce(lambda a, b: a + b, [(e / den) * o for e, o in zip(es, outs)])
    mix = _dot(attn.astype(BF16), wo_ref[0:aw, :]) + _dot(rw_ref[...].astype(BF16), wo_ref[aw:, :])
    y = _layer_norm(ALPHA * x_ref[...] + mix, ln_ref[0:1, :], ln_ref[1:2, :])
    y_ref[...] = y
    yb_ref[...] = y.astype(yb_ref.dtype)


def _outproj(os_, lses, dils, rw, x2, w_out, ln_g, ln_b, tm):
    t, d = x2.shape
    aw = rw.shape[1]
    ln = jnp.concatenate([ln_g[None], ln_b[None], jnp.zeros((SUBLANES - 2, d), F32)], axis=0)
    wo = w_out.astype(BF16)
    tok = lambda w: pl.BlockSpec((tm, w), lambda i: (i, 0))
    view = [pl.BlockSpec((tm // dl, dl * aw), lambda i: (i, 0)) for dl in dils]
    const = lambda a: pl.BlockSpec(a.shape, lambda i: (0, 0))
    return pl.pallas_call(
        functools.partial(_outproj_kernel, dils),
        grid=(t // tm,),
        in_specs=view + view + [tok(aw), tok(d), const(wo), const(ln)],
        out_specs=[tok(d), tok(d)],
        out_shape=[jax.ShapeDtypeStruct((t, d), F32), jax.ShapeDtypeStruct((t, d), BF16)],
        scratch_shapes=[pltpu.VMEM((2 * len(dils) * aw // LANES, tm, LANES), F32)],
        compiler_params=_cparams("arbitrary"),
        name="outproj",
    )(*os_, *lses, rw, x2, wo, ln)


def _lanes(col_rep, n):
    return jnp.concatenate([col_rep] * (n // LANES), axis=1)


MOE_TILE = 512
ROW_CHUNK = 8
SORT_ROWS = 512
COMBINE_ROWS = 1024
EXPERT_ROWS = 256


def _tile_rows(ne):
    raw = MOE_TILE * TOP_K + ne * ROW_CHUNK
    unit = max(SORT_ROWS, COMBINE_ROWS)
    return -(-raw // unit) * unit


def _route_kernel(x_ref, rwt_ref, bias_ref, tri_ref, ones_ref, low_ref, dloc_ref, gate_ref, tab_ref):
    ne = rwt_ref.shape[0]
    tm = x_ref.shape[0]
    gsz = ne // N_GROUPS
    ninf = -jnp.inf

    aff = _sigmoid(_dotf(rwt_ref[...], x_ref[...], nt=True))
    sel = aff + _lanes(bias_ref[...], tm)

    sel3 = sel.reshape(N_GROUPS, gsz, tm)
    rid = lax.broadcasted_iota(I32, sel3.shape, 1).astype(F32)
    m1 = jnp.max(sel3, axis=1, keepdims=True)
    i1 = jnp.min(jnp.where(sel3 == m1, rid, float(gsz)), axis=1, keepdims=True)
    m2 = jnp.max(jnp.where(rid == i1, ninf, sel3), axis=1, keepdims=True)
    gsc = (m1 + m2).reshape(N_GROUPS, tm)
    gid = lax.broadcasted_iota(I32, gsc.shape, 0).astype(F32)
    keep = jnp.zeros(gsc.shape, F32)
    for _ in range(TOPK_GROUPS):
        gm = jnp.max(gsc, axis=0, keepdims=True)
        gi = jnp.min(jnp.where(gsc == gm, gid, float(N_GROUPS)), axis=0, keepdims=True)
        hit = gid == gi
        keep = jnp.where(hit, 1.0, keep)
        gsc = jnp.where(hit, ninf, gsc)
    cand = jnp.where(keep.reshape(N_GROUPS, 1, tm) > 0.0, sel3, ninf).reshape(ne, tm)

    eid = lax.broadcasted_iota(I32, (ne, tm), 0).astype(F32)
    hits, graw = [], []
    for _ in range(TOP_K):
        m = jnp.max(cand, axis=0, keepdims=True)
        ij = jnp.min(jnp.where(cand == m, eid, float(ne)), axis=0, keepdims=True)
        hit = eid == ij
        hits.append(hit)
        graw.append(jnp.sum(jnp.where(hit, aff, 0.0), axis=0, keepdims=True))
        cand = jnp.where(hit, ninf, cand)
    gsum = functools.reduce(lambda a, b: a + b, graw)
    gate_ref[...] = jnp.concatenate([g / gsum * ROUTED_SCALE for g in graw], axis=0)

    onehot = functools.reduce(lambda a, b: a + b, [h.astype(F32) for h in hits]).astype(BF16)
    before = _dot(onehot, tri_ref[...])
    count = _dot(onehot, ones_ref[...])
    padded = jnp.maximum(jnp.ceil(count * (1.0 / ROW_CHUNK)) * ROW_CHUNK, ROW_CHUNK)
    start = _dot3(low_ref[...], padded)
    row = _lanes(start, tm) + before
    dloc_ref[...] = jnp.concatenate(
        [jnp.sum(jnp.where(h, row, 0.0), axis=0, keepdims=True) for h in hits], axis=0).astype(I32)
    lane = lax.broadcasted_iota(I32, padded.shape, 1)
    both = jnp.where(lane == 0, padded, jnp.where(lane == 1, start, 0.0))
    tab_ref[...] = both.T[0:SUBLANES, :].astype(I32)


def _route(x1, router_w, router_bias):
    t, d = x1.shape
    ne = router_w.shape[1]
    tm = MOE_TILE
    rwt = router_w.T
    bias = jnp.broadcast_to(router_bias[:, None], (ne, LANES))
    pos = jnp.arange(tm)
    tri = (pos[:, None] < pos[None, :]).astype(BF16)
    ones = jnp.ones((tm, LANES), BF16)
    eid = jnp.arange(ne)
    low = (eid[:, None] > eid[None, :]).astype(BF16)
    const = lambda a: pl.BlockSpec(a.shape, lambda i: (0, 0))
    tokt = pl.BlockSpec((TOP_K, tm), lambda i: (0, i))
    return pl.pallas_call(
        _route_kernel,
        grid=(t // tm,),
        in_specs=[pl.BlockSpec((tm, d), lambda i: (i, 0)), const(rwt), const(bias), const(tri), const(ones),
                  const(low)],
        out_specs=[tokt, tokt, pl.BlockSpec((None, SUBLANES, ne), lambda i: (i, 0, 0))],
        out_shape=[jax.ShapeDtypeStruct((TOP_K, t), I32), jax.ShapeDtypeStruct((TOP_K, t), F32),
                   jax.ShapeDtypeStruct((t // tm, SUBLANES, ne), I32)],
        compiler_params=_cparams("arbitrary"),
        name="route",
    )(x1, rwt, bias, tri, ones, low)


def _pack_bf16_pairs(v, exact=False):
    half = v.shape[1] // 2
    u = pltpu.bitcast(v if exact else v.astype(BF16).astype(F32), U32)
    return (u[:, 0:half] >> 16) | (u[:, half:] & jnp.uint32(0xFFFF0000))


def _unpack_bf16_pairs(w):
    lo = pltpu.bitcast(w << 16, F32)
    hi = pltpu.bitcast(w & jnp.uint32(0xFFFF0000), F32)
    return jnp.concatenate([lo, hi], axis=1)


def _dot3l(a, b_bf16):
    a1 = a.astype(BF16)
    r1 = a - a1.astype(F32)
    a2 = r1.astype(BF16)
    a3 = (r1 - a2.astype(F32)).astype(BF16)
    return _dot(a1, b_bf16) + (_dot(a2, b_bf16) + _dot(a3, b_bf16))


def _plan_kernel(nblk, cnt_ref, low_ref, upper_ref, gstart_ref, meta_ref, emeta_ref):
    c = cnt_ref[...].astype(F32)
    nt, ne = c.shape
    earlier = _dot3(low_ref[...], c)
    total = jnp.sum(c, axis=0, keepdims=True)
    padded = jnp.ceil(total * (1.0 / EXPERT_ROWS)) * EXPERT_ROWS
    pad_end = _dot3l(jnp.broadcast_to(padded, (SUBLANES, ne)), upper_ref[...])
    pad_start = (pad_end - padded)[0:1, :]
    gstart_ref[...] = (pad_start + earlier).astype(I32)
    emeta_ref[...] = jnp.concatenate(
        [pad_start, padded * (1.0 / EXPERT_ROWS), jnp.zeros((SUBLANES - 2, ne), F32)], axis=0).astype(I32)
    end_col = jnp.broadcast_to(pad_end.T[:, 0:1], (ne, nblk))
    blk_start = (lax.broadcasted_iota(I32, (ne, nblk), 1) * EXPERT_ROWS).astype(F32)
    owner = jnp.minimum(jnp.sum((end_col <= blk_start).astype(F32), axis=0, keepdims=True), float(ne - 1))
    used = jnp.max(end_col, axis=0, keepdims=True) * (1.0 / EXPERT_ROWS)
    meta_ref[...] = jnp.concatenate([owner, used, jnp.zeros((SUBLANES - 2, nblk), F32)], axis=0).astype(I32)


def _plan(cnt, nblk_pad):
    nt, ne = cnt.shape
    ti = jnp.arange(nt)
    low = (ti[:, None] > ti[None, :]).astype(BF16)
    ei = jnp.arange(ne)
    upper = (ei[:, None] <= ei[None, :]).astype(BF16)
    return pl.pallas_call(
        functools.partial(_plan_kernel, nblk_pad),
        out_shape=[jax.ShapeDtypeStruct((nt, ne), I32), jax.ShapeDtypeStruct((SUBLANES, nblk_pad), I32),
                   jax.ShapeDtypeStruct((SUBLANES, ne), I32)],
        compiler_params=pltpu.CompilerParams(vmem_limit_bytes=VMEM_LIMIT_BYTES),
        name="plan",
    )(cnt, low, upper)


def _group_copies(cnt_ref, loc_ref, gstart_ref, tile, ne, local, remote, sem, to_remote):
    def one(e, carry):
        n = pl.multiple_of(cnt_ref[tile * ne + e], ROW_CHUNK)
        lo = pl.multiple_of(loc_ref[tile * ne + e], ROW_CHUNK)
        go = pl.multiple_of(gstart_ref[tile * ne + e], ROW_CHUNK)
        a, b = local.at[pl.ds(lo, n), :], remote.at[pl.ds(go, n), :]
        (pltpu.make_async_copy(a, b, sem) if to_remote else pltpu.make_async_copy(b, a, sem)).start()
        return carry

    lax.fori_loop(0, ne, one, 0, unroll=8)


def _group_wait(rows, local, remote, sem, to_remote):
    n = pl.multiple_of(rows, ROW_CHUNK)

    @pl.when(n > 0)
    def _():
        a, b = local.at[pl.ds(0, n), :], remote.at[pl.ds(0, n), :]
        (pltpu.make_async_copy(a, b, sem) if to_remote else pltpu.make_async_copy(b, a, sem)).wait()


def _dispatch_kernel(ne, cnt_ref, loc_ref, gstart_ref, owner_ref, used_ref, urows_ref, dloc_ref, x_ref, xs_ref,
                     buf_ref, zero_ref, sems, zsem):
    step = pl.program_id(0)
    nsteps = pl.num_programs(0)
    slot = step % 2
    tm = x_ref.shape[0]
    nblk = owner_ref.shape[0]
    used = used_ref[0]

    @pl.when(step == 0)
    def _():
        zero_ref[...] = jnp.zeros_like(zero_ref)

        def block_copy(i):
            return pltpu.make_async_copy(zero_ref, xs_ref.at[pl.ds(i * EXPERT_ROWS, EXPERT_ROWS), :], zsem)

        def is_last(i):
            return (i == used - 1) | (owner_ref[jnp.minimum(i + 1, nblk - 1)] != owner_ref[i])

        def start(i, carry):
            @pl.when(is_last(i))
            def _():
                block_copy(i).start()
            return carry

        def wait(i, carry):
            @pl.when(is_last(i))
            def _():
                block_copy(i).wait()
            return carry

        lax.fori_loop(0, used, start, 0)
        lax.fori_loop(0, used, wait, 0)

    x = x_ref[...]
    dl = dloc_ref[...].astype(I16)
    riota = lax.broadcasted_iota(I32, (SORT_ROWS, tm), 0).astype(I16)
    one = jnp.ones((SORT_ROWS, tm), BF16)

    def chunk(c, carry):
        r0 = pl.multiple_of(c * SORT_ROWS, SORT_ROWS)
        r = riota + r0.astype(I16)
        p = jnp.zeros((SORT_ROWS, tm), BF16)
        for j in range(TOP_K):
            p = jnp.where(r == dl[j:j + 1, :], one, p)
        buf_ref[slot, pl.ds(r0, SORT_ROWS), :] = _pack_bf16_pairs(_dot(p, x), exact=True)
        return carry

    lax.fori_loop(0, (urows_ref[step] + SORT_ROWS - 1) // SORT_ROWS, chunk, 0)

    copies = functools.partial(_group_copies, cnt_ref, loc_ref, gstart_ref)
    copies(step, ne, buf_ref.at[slot], xs_ref, sems.at[slot], True)

    @pl.when(step > 0)
    def _():
        _group_wait(urows_ref[jnp.maximum(step - 1, 0)], buf_ref.at[1 - slot], xs_ref, sems.at[1 - slot], True)

    @pl.when(step == nsteps - 1)
    def _():
        _group_wait(urows_ref[step], buf_ref.at[slot], xs_ref, sems.at[slot], True)


def _dispatch(x1b, dloc, cnt, loc, gstart, owner, used, used_rows, nrows):
    t, d = x1b.shape
    nt, ne = cnt.shape
    tm = MOE_TILE
    rt = _tile_rows(ne)
    grid_spec = pltpu.PrefetchScalarGridSpec(
        num_scalar_prefetch=6,
        grid=(nt,),
        in_specs=[pl.BlockSpec((TOP_K, tm), lambda i, *_: (0, i)), pl.BlockSpec((tm, d), lambda i, *_: (i, 0))],
        out_specs=pl.BlockSpec(memory_space=pl.ANY),
        scratch_shapes=[pltpu.VMEM((2, rt, d // 2), U32), pltpu.VMEM((EXPERT_ROWS, d // 2), U32),
                        pltpu.SemaphoreType.DMA((2,)), pltpu.SemaphoreType.DMA(())],
    )
    return pl.pallas_call(
        functools.partial(_dispatch_kernel, ne),
        grid_spec=grid_spec,
        out_shape=jax.ShapeDtypeStruct((nrows, d // 2), U32),
        compiler_params=_cparams("arbitrary"),
        name="dispatch",
    )(cnt.reshape(-1), loc.reshape(-1), gstart.reshape(-1), owner, used, used_rows, dloc, x1b)


EXPERT_PIECE = 1024


WEIGHT_SLOTS = 3


def _expert_kernel(first_ref, nblk_ref, xs_ref, wg_ref, wu_ref, wd_ref, ys_ref,
                   xbuf, ybuf, wgf_ref, wuf_ref, wdf_ref, wgb_ref, wub_ref, wdb_ref,
                   sem_in, sem_out, wsem, done_ref, pend_ref):
    e = pl.program_id(0)
    last = pl.num_programs(0) - 1
    per_piece = EXPERT_PIECE // EXPERT_ROWS

    def pieces(ex):
        return (nblk_ref[ex] + per_piece - 1) // per_piece

    def span(ex, s):
        n = jnp.minimum(EXPERT_PIECE, nblk_ref[ex] * EXPERT_ROWS - s * EXPERT_PIECE)
        n = pl.multiple_of(n, EXPERT_ROWS)
        return pl.ds(pl.multiple_of(first_ref[ex] + s * EXPERT_PIECE, EXPERT_ROWS), n), pl.ds(0, n)

    def in_copy(ex, s, slot):
        far, near = span(ex, s)
        return pltpu.make_async_copy(xs_ref.at[far, :], xbuf.at[slot, near, :], sem_in.at[slot])

    def out_copy(ex, s, slot):
        far, near = span(ex, s)
        return pltpu.make_async_copy(ybuf.at[slot, near, :], ys_ref.at[far, :], sem_out.at[slot])

    @pl.when(e == 0)
    def _():
        done_ref[0] = 0
        for slot in range(2):
            pend_ref[2 * slot] = -1

        @pl.when(nblk_ref[0] > 0)
        def _():
            in_copy(0, 0, 0).start()

    nb = nblk_ref[e]
    ns = pieces(e)
    g0 = done_ref[0]
    nxt = jnp.minimum(e + 1, last)
    has_next = (e < last) & (nblk_ref[nxt] > 0)

    def release(slot):
        pe = pend_ref[2 * slot]

        @pl.when(pe >= 0)
        def _():
            out_copy(pe, pend_ref[2 * slot + 1], slot).wait()

    def weight_copies(ex):
        slot = ex % WEIGHT_SLOTS
        return [pltpu.make_async_copy(src.at[ex], buf.at[slot], wsem.at[slot])
                for src, buf in ((wg_ref, wgf_ref), (wu_ref, wuf_ref), (wd_ref, wdf_ref))]

    @pl.when(e == 0)
    def _():
        for ex in range(WEIGHT_SLOTS - 1):
            for cp in weight_copies(ex):
                cp.start()

    @pl.when(e + WEIGHT_SLOTS - 1 <= last)
    def _():
        for cp in weight_copies(e + WEIGHT_SLOTS - 1):
            cp.start()

    for cp in weight_copies(e):
        cp.wait()

    @pl.when(nb > 0)
    def _():
        wslot = e % WEIGHT_SLOTS
        wgb_ref[...] = wgf_ref[wslot].astype(BF16)
        wub_ref[...] = wuf_ref[wslot].astype(BF16)
        wdb_ref[...] = wdf_ref[wslot].astype(BF16)

        def piece(s, carry):
            slot = (g0 + s) % 2

            @pl.when(s + 1 < ns)
            def _():
                in_copy(e, s + 1, 1 - slot).start()

            @pl.when((s + 1 == ns) & has_next)
            def _():
                in_copy(nxt, 0, 1 - slot).start()

            in_copy(e, s, slot).wait()
            release(slot)

            blocks_here = jnp.minimum(per_piece, nb - s * per_piece)
            for nblocks in range(1, per_piece + 1):
                @pl.when(blocks_here == nblocks)
                def _():
                    rows = pl.ds(0, nblocks * EXPERT_ROWS)
                    xb = _unpack_bf16_pairs(xbuf[slot, rows, :]).astype(BF16)
                    hg = _dot(xb, wgb_ref[...])
                    hb = hg * _sigmoid(hg) * _dot(xb, wub_ref[...])
                    ybuf[slot, rows, :] = _pack_bf16_pairs(_dot(hb.astype(BF16), wdb_ref[...]))
            out_copy(e, s, slot).start()
            pend_ref[2 * slot] = e
            pend_ref[2 * slot + 1] = s
            return carry

        lax.fori_loop(0, ns, piece, 0)

    @pl.when((nb == 0) & has_next)
    def _():
        in_copy(nxt, 0, g0 % 2).start()

    @pl.when(e == last)
    def _():
        for slot in range(2):
            release(slot)

    done_ref[0] = g0 + ns


def _experts(xs, first_row, nblocks, e_gate, e_up, e_down):
    nrows, dh = xs.shape
    ne, d, ff = e_gate.shape
    anywhere = pl.BlockSpec(memory_space=pl.ANY)
    grid_spec = pltpu.PrefetchScalarGridSpec(
        num_scalar_prefetch=2,
        grid=(ne,),
        in_specs=[anywhere, anywhere, anywhere, anywhere],
        out_specs=anywhere,
        scratch_shapes=[pltpu.VMEM((2, EXPERT_PIECE, dh), U32), pltpu.VMEM((2, EXPERT_PIECE, dh), U32),
                        pltpu.VMEM((WEIGHT_SLOTS, d, ff), F32), pltpu.VMEM((WEIGHT_SLOTS, d, ff), F32),
                        pltpu.VMEM((WEIGHT_SLOTS, ff, d), F32),
                        pltpu.VMEM((d, ff), BF16), pltpu.VMEM((d, ff), BF16), pltpu.VMEM((ff, d), BF16),
                        pltpu.SemaphoreType.DMA((2,)), pltpu.SemaphoreType.DMA((2,)),
                        pltpu.SemaphoreType.DMA((WEIGHT_SLOTS,)),
                        pltpu.SMEM((1,), I32), pltpu.SMEM((4,), I32)],
    )
    return pl.pallas_call(
        _expert_kernel,
        grid_spec=grid_spec,
        out_shape=jax.ShapeDtypeStruct((nrows, dh), U32),
        compiler_params=_cparams("arbitrary"),
        name="experts",
    )(first_row, nblocks, xs, e_gate, e_up, e_down)


def _final_kernel(ne, cnt_ref, loc_ref, gstart_ref, used_ref, dloc_ref, gate_ref, x_ref, xb_ref, ys_ref,
                  sg_ref, su_ref, sd_ref, ln_ref, o_ref, buf_ref, acc_ref, sems):
    step = pl.program_id(0)
    nsteps = pl.num_programs(0)
    slot = step % 2
    tm = x_ref.shape[0]
    used = used_ref[step]
    copies = functools.partial(_group_copies, cnt_ref, loc_ref, gstart_ref)

    @pl.when(step == 0)
    def _():
        copies(step, ne, buf_ref.at[0], ys_ref, sems.at[0], False)

    @pl.when(step + 1 < nsteps)
    def _():
        copies(step + 1, ne, buf_ref.at[1 - slot], ys_ref, sems.at[1 - slot], False)

    xb = xb_ref[...]
    hg = _dot(xb, sg_ref[...])
    hs = hg * _sigmoid(hg) * _dot(xb, su_ref[...])
    acc_ref[...] = _dot(hs.astype(BF16), sd_ref[...])

    _group_wait(used, buf_ref.at[slot], ys_ref, sems.at[slot], False)

    dl = dloc_ref[...].astype(I16)
    gt = gate_ref[...].astype(BF16)
    rows16 = lax.broadcasted_iota(I32, (COMBINE_ROWS, tm), 0).astype(I16)
    gts = [jnp.broadcast_to(gt[j:j + 1, :], (COMBINE_ROWS, tm)) for j in range(TOP_K)]
    riota = lax.broadcasted_iota(I32, (COMBINE_ROWS, buf_ref.shape[2]), 0)

    def chunk(c, carry):
        r0 = pl.multiple_of(c * COMBINE_ROWS, COMBINE_ROWS)
        r = rows16 + r0.astype(I16)
        g = jnp.zeros((COMBINE_ROWS, tm), BF16)
        for j in range(TOP_K):
            g = jnp.where(r == dl[j:j + 1, :], gts[j], g)
        w = buf_ref[slot, pl.ds(r0, COMBINE_ROWS), :]
        w = jnp.where(riota + r0 < used, w, jnp.zeros_like(w))
        y = _unpack_bf16_pairs(w).astype(BF16)
        acc_ref[...] += lax.dot_general(g, y, (((0,), (0,)), ((), ())), preferred_element_type=F32)
        return carry

    lax.fori_loop(0, (used + COMBINE_ROWS - 1) // COMBINE_ROWS, chunk, 0)
    o_ref[...] = _layer_norm(ALPHA * x_ref[...] + acc_ref[...], ln_ref[0:1, :], ln_ref[1:2, :])


def _final(x1, x1b, dloc, gate, cnt, loc, gstart, used_rows, ys, s_gate, s_up, s_down, ln_g, ln_b):
    t, d = x1.shape
    nt, ne = cnt.shape
    tm = MOE_TILE
    rt = _tile_rows(ne)
    ln = jnp.concatenate([ln_g[None], ln_b[None], jnp.zeros((SUBLANES - 2, d), F32)], axis=0)
    sg, su, sd = s_gate.astype(BF16), s_up.astype(BF16), s_down.astype(BF16)
    tok = pl.BlockSpec((tm, d), lambda i, *_: (i, 0))
    tokt = pl.BlockSpec((TOP_K, tm), lambda i, *_: (0, i))
    const = lambda a: pl.BlockSpec(a.shape, lambda i, *_: (0, 0))
    grid_spec = pltpu.PrefetchScalarGridSpec(
        num_scalar_prefetch=4,
        grid=(nt,),
        in_specs=[tokt, tokt, tok, tok, pl.BlockSpec(memory_space=pl.ANY), const(sg), const(su), const(sd), const(ln)],
        out_specs=tok,
        scratch_shapes=[pltpu.VMEM((2, rt, d // 2), U32), pltpu.VMEM((tm, d), F32), pltpu.SemaphoreType.DMA((2,))],
    )
    return pl.pallas_call(
        functools.partial(_final_kernel, ne),
        grid_spec=grid_spec,
        out_shape=jax.ShapeDtypeStruct((t, d), F32),
        compiler_params=_cparams("arbitrary"),
        name="final",
    )(cnt.reshape(-1), loc.reshape(-1), gstart.reshape(-1), used_rows, dloc, gate, x1, x1b, ys, sg, su, sd, ln)


def _moe(x1, x1b, router_w, router_bias, e_gate, e_up, e_down, s_gate, s_up, s_down, ln_g, ln_b):
    t = x1.shape[0]
    ne = router_w.shape[1]
    nt = t // MOE_TILE
    nblk = -(-(nt * _tile_rows(ne)) // EXPERT_ROWS) + ne
    nblk_pad = -(-nblk // LANES) * LANES
    dloc, gate, tab = _route(x1, router_w, router_bias)
    cnt, loc = tab[:, 0, :], tab[:, 1, :]
    used_rows = loc[:, ne - 1] + cnt[:, ne - 1]
    gstart, meta, emeta = _plan(cnt, nblk_pad)
    owner, used = meta[0], meta[1, 0:1]
    xs = _dispatch(x1b, dloc, cnt, loc, gstart, owner, used, used_rows, nblk * EXPERT_ROWS)
    ys = _experts(xs, emeta[0], emeta[1], e_gate, e_up, e_down)
    return _final(x1, x1b, dloc, gate, cnt, loc, gstart, used_rows, ys, s_gate, s_up, s_down, ln_g, ln_b)


def kernel(x, w_in, mu_shift, w0, w_decay_up, a0, w_aaa_up, w_gate_up, k_k, k_a, r_k, lnx_g, lnx_b, w_out,
           ln1_g, ln1_b, router_w, router_bias, e_gate, e_up, e_down, s_gate, s_up, s_down, ln2_g, ln2_b):
    bsz, seq, d = x.shape
    x2 = x.reshape(bsz * seq, d)
    qkvs, feats = _inproj(x2, seq, w_in[0], mu_shift[0], w0[0], w_decay_up[0], a0[0], w_aaa_up[0], w_gate_up[0],
                          k_k[0], k_a[0], r_k[0], tm=256)
    rw = _rwkv(feats, bsz, seq, lnx_g[0], lnx_b[0], nbat=8)
    x1, x1b = _mixer_tail(qkvs, rw, x2, bsz, seq, w_out[0], ln1_g[0], ln1_b[0])
    out = _moe(x1, x1b, router_w[0], router_bias[0], e_gate[0], e_up[0], e_down[0], s_gate[0], s_up[0], s_down[0],
               ln2_g[0], ln2_b[0])
    return out.reshape(bsz, seq, d)


def _mixer_tail(qkvs, rw, x2, bsz, seq, w_out, ln_g, ln_b):
    res = [_attention(q, k, v, bsz, seq, window, dilation)
           for (q, k, v), (window, dilation) in zip(qkvs, ATTN_PATTERNS)]
    dils = tuple(dl for _, dl in ATTN_PATTERNS)
    return _outproj([o for o, _ in res], [l for _, l in res], dils, rw, x2, w_out, ln_g, ln_b, tm=512)
```

```python
import functools

import jax
import jax.numpy as jnp
from jax import lax
from jax.experimental import pallas as pl
from jax.experimental.pallas import tpu as pltpu

F32 = jnp.float32
BF16 = jnp.bfloat16
I32 = jnp.int32
I16 = jnp.int16
U32 = jnp.uint32

LANES = 128
SUBLANES = 8
VMEM_LIMIT_BYTES = 56 * 1024 * 1024

HEAD_DIM = 64
ATTN_HEADS = 8
RWKV_HEADS = 8
ATTN_WIDTH = ATTN_HEADS * HEAD_DIM
RWKV_WIDTH = RWKV_HEADS * HEAD_DIM
ATTN_PATTERNS = ((128, 1), (512, 4), (2048, 16))
Q_BLOCK = 128
ROPE_THETA = 10000.0
DECAY_LORA = 64
AAA_LORA = 64
GATE_LORA = 160
GN_EPS = 64e-5
LN_EPS = 1e-5
TOP_K = 8
N_GROUPS = 8
TOPK_GROUPS = 4
ROUTED_SCALE = 2.5
DEPTH = 1
ALPHA = (2.0 * DEPTH) ** 0.25

RWKV_CHUNK = 64
PAIR = 2 * HEAD_DIM
LORA_PAD = 2 * LANES


def _cparams(*sem):
    return pltpu.CompilerParams(dimension_semantics=sem, vmem_limit_bytes=VMEM_LIMIT_BYTES)


def _split_bf16(a):
    hi = a.astype(BF16)
    lo = (a - hi.astype(F32)).astype(BF16)
    return hi, lo


def _dot(a, b):
    return jnp.dot(a, b, preferred_element_type=F32)


def _dot_nt(a, b):
    return lax.dot_general(a, b, (((1,), (1,)), ((), ())), preferred_element_type=F32)


def _softplus(z):
    return jnp.maximum(z, 0.0) + jnp.log(1.0 + jnp.exp(-jnp.abs(z)))


def _sigmoid(z):
    return 1.0 / (1.0 + jnp.exp(-z))


def _inproj_kernel(dils, x_ref, w_ref, cos_ref, sin_ref, mu_ref, wd_ref, wa_ref, wg_ref, vec_ref, *refs):
    nq = 3 * len(dils)
    qkv_refs = refs[0:nq]
    r_ref, ld_ref, kp_ref, vv_ref, kk_ref, b_ref, g_ref, bon_ref, carry_ref, qkv_scr, qkv_scr2 = refs[nq:]
    s = pl.program_id(1)
    tm = x_ref.shape[0]
    aw = ATTN_WIDTH
    rw = RWKV_WIDTH

    @pl.when(s == 0)
    def _():
        carry_ref[...] = jnp.zeros_like(carry_ref)

    h = _dot(x_ref[...].astype(BF16), w_ref[...])
    reps = aw // cos_ref.shape[1]
    cos = jnp.concatenate([cos_ref[...]] * reps, axis=1)
    sin = jnp.concatenate([sin_ref[...]] * reps, axis=1)
    qkv = [(h[:, 0:aw] * cos + h[:, aw:2 * aw] * sin) * (HEAD_DIM ** -0.5),
           h[:, 2 * aw:3 * aw] * cos + h[:, 3 * aw:4 * aw] * sin,
           h[:, 4 * aw:5 * aw]]
    nslab = aw // LANES
    for i in range(3):
        for j in range(nslab):
            qkv_scr[i * nslab + j] = qkv[i][:, j * LANES:(j + 1) * LANES]
    src, src_d = qkv_scr, 1
    for di, d in enumerate(dils):
        if d == 1:
            for i in range(3):
                qkv_refs[3 * di + i][...] = qkv[i].astype(qkv_refs[3 * di + i].dtype)
            continue
        assert d % src_d == 0
        step, per = d // src_d, tm // src_d
        dst = qkv_scr2 if src is qkv_scr else qkv_scr
        for i in range(3):
            o_ref = qkv_refs[3 * di + i]
            for res in range(d):
                lo, hi = res % src_d, res // src_d
                for j in range(nslab):
                    rows = src[i * nslab + j, pl.ds(lo * per + hi, tm // d, stride=step), :]
                    if di + 1 < len(dils):
                        dst[i * nslab + j, res * (tm // d):(res + 1) * (tm // d), :] = rows
                    o_ref[:, res * aw + j * LANES:res * aw + (j + 1) * LANES] = rows.astype(o_ref.dtype)
        src, src_d = dst, d

    f = h[:, 5 * aw:]
    rows = lax.broadcasted_iota(I32, f.shape, 0)
    prev = jnp.where(rows == 0, carry_ref[SUBLANES - 1:SUBLANES, :], pltpu.roll(f, 1, axis=0))
    carry_ref[...] = f[tm - SUBLANES:tm, :]
    f = f + (prev - f) * mu_ref[...]

    r = f[:, 0:rw]
    k = f[:, rw:2 * rw]
    v = f[:, 2 * rw:3 * rw]
    la = f[:, 3 * rw:3 * rw + LANES]
    gl = f[:, 3 * rw + LANES:]
    w0, a0, k_k, k_a, r_k = (vec_ref[i:i + 1, :] for i in range(5))

    z = w0 + _dot(jnp.tanh(la).astype(BF16), wd_ref[...])
    w = -_softplus(-z) - 0.5
    ld_ref[...] = -jnp.exp(w)
    a = _sigmoid(a0 + _dot(la.astype(BF16), wa_ref[...]))
    g_ref[...] = _dot(_sigmoid(gl).astype(BF16), wg_ref[...])

    first = lax.broadcasted_iota(I32, (tm, PAIR), 1) < HEAD_DIM

    def head_sum(z):
        parts = []
        for p in range(rw // PAIR):
            zp = z[:, p * PAIR:(p + 1) * PAIR]
            s0 = jnp.sum(jnp.where(first, zp, 0.0), axis=1, keepdims=True)
            s1 = jnp.sum(jnp.where(first, 0.0, zp), axis=1, keepdims=True)
            parts.append(jnp.where(first, s0, s1))
        return jnp.concatenate(parts, axis=1)

    kk = k * k_k
    kk = kk / jnp.maximum(jnp.sqrt(head_sum(kk * kk)), 1e-12)
    kp = k * (1.0 + (a - 1.0) * k_a)
    r_ref[...] = r
    kp_ref[...] = kp
    vv_ref[...] = v
    kk_ref[...] = kk
    b_ref[...] = kk * a
    bon_ref[...] = head_sum(r * kp * r_k) * v


def _rot_half_cols(w):
    d, n = w.shape
    w4 = w.reshape(d, n // HEAD_DIM, 2, HEAD_DIM // 2)
    return jnp.stack([-w4[:, :, 1, :], w4[:, :, 0, :]], axis=2).reshape(d, n)


def _inproj(x2, seq, w_in, mu_shift, w0, w_decay_up, a0, w_aaa_up, w_gate_up, k_k, k_a, r_k, tm):
    t, d = x2.shape
    aw, rw = ATTN_WIDTH, RWKV_WIDTH
    wq, wk, wv = w_in[:, 0:aw], w_in[:, aw:2 * aw], w_in[:, 2 * aw:3 * aw]
    wf = w_in[:, 3 * aw:]
    gpad = LORA_PAD - GATE_LORA
    w_all = jnp.concatenate(
        [wq, _rot_half_cols(wq), wk, _rot_half_cols(wk), wv, wf, jnp.zeros((d, gpad), F32)], axis=1).astype(BF16)
    mu = jnp.concatenate([mu_shift, jnp.zeros((gpad,), F32)])[None, :]
    nf = mu.shape[1]
    wd = jnp.concatenate([w_decay_up, jnp.zeros((AAA_LORA, rw), F32)], axis=0).astype(BF16)
    wa = jnp.concatenate([jnp.zeros((DECAY_LORA, rw), F32), w_aaa_up], axis=0).astype(BF16)
    wg = jnp.concatenate([w_gate_up, jnp.zeros((gpad, rw), F32)], axis=0).astype(BF16)
    vec = jnp.stack([w0, a0, k_k, k_a, r_k.reshape(-1), w0 * 0, w0 * 0, w0 * 0])
    half = HEAD_DIM // 2
    inv_freq = ROPE_THETA ** (-jnp.arange(half, dtype=F32) * 2.0 / HEAD_DIM)
    ang = jnp.arange(seq, dtype=F32)[:, None] * inv_freq[None, :]
    cos = jnp.tile(jnp.cos(ang), (1, LANES // half))
    sin = jnp.tile(jnp.sin(ang), (1, LANES // half))

    nst = seq // tm
    tok = lambda b, s: (b * nst + s, 0)
    const = lambda b, s: (0, 0)
    full = lambda a: pl.BlockSpec(a.shape, const)
    dils = tuple(dl for _, dl in ATTN_PATTERNS)
    qkv_specs, qkv_shapes = [], []
    for dl in dils:
        qkv_specs += [pl.BlockSpec((tm // dl, dl * aw), tok)] * 3
        qkv_shapes += [jax.ShapeDtypeStruct((t // dl, dl * aw), BF16)] * 3
    out_f = jax.ShapeDtypeStruct((t, rw), F32)
    outs = pl.pallas_call(
        functools.partial(_inproj_kernel, dils),
        grid=(t // seq, nst),
        in_specs=[pl.BlockSpec((tm, d), tok), full(w_all),
                  pl.BlockSpec((tm, LANES), lambda b, s: (s, 0)), pl.BlockSpec((tm, LANES), lambda b, s: (s, 0)),
                  full(mu), full(wd), full(wa), full(wg), full(vec)],
        out_specs=qkv_specs + [pl.BlockSpec((tm, rw), tok)] * 8,
        out_shape=qkv_shapes + [out_f] * 8,
        scratch_shapes=[pltpu.VMEM((SUBLANES, nf), F32), pltpu.VMEM((3 * aw // LANES, tm, LANES), F32),
                        pltpu.VMEM((3 * aw // LANES, tm, LANES), F32)],
        compiler_params=_cparams("arbitrary", "arbitrary"),
        name="inproj",
    )(x2, w_all, cos, sin, mu, wd, wa, wg, vec)
    nq = 3 * len(dils)
    return [outs[3 * i:3 * i + 3] for i in range(len(dils))], outs[nq:]


def _dotf(a, b, nt=False, passes=3):
    dot = _dot_nt if nt else _dot
    if passes == 1:
        return dot(a.astype(BF16), b.astype(BF16))
    ah, al = _split_bf16(a)
    if passes == 2:
        bh = b.astype(BF16)
        return dot(ah, bh) + dot(al, bh)
    bh, bl = _split_bf16(b)
    return dot(ah, bh) + (dot(ah, bl) + dot(al, bh))


RWKV_PASSES = dict(s8=1, inv=1, w1=1, au=1, ry=1, gh=1, yh=1)


def _dot3(a_bf16, b):
    b1 = b.astype(BF16)
    r1 = b - b1.astype(F32)
    b2 = r1.astype(BF16)
    b3 = (r1 - b2.astype(F32)).astype(BF16)
    return _dot(a_bf16, b1) + (_dot(a_bf16, b2) + _dot(a_bf16, b3))


def _rwkv_kernel(r_ref, ld_ref, kp_ref, v_ref, kk_ref, b_ref, g_ref, bon_ref, gn_ref, o_ref, h_ref):
    c = pl.program_id(1)
    nbat, ch, rw = r_ref.shape
    npairs = rw // PAIR

    @pl.when(c == 0)
    def _():
        h_ref[...] = jnp.zeros_like(h_ref)

    ri = lax.broadcasted_iota(I32, (ch, ch), 0)
    ci = lax.broadcasted_iota(I32, (ch, ch), 1)
    tril = (ri >= ci).astype(BF16)
    row = lax.broadcasted_iota(I32, (ch, PAIR), 0)
    col = lax.broadcasted_iota(I32, (ch, PAIR), 1)
    first = col < HEAD_DIM
    jj = col & (HEAD_DIM - 1)
    strict = jj < row
    incl = jj <= row
    eye = (jj == row).astype(F32)
    row2 = lax.broadcasted_iota(I32, (PAIR, PAIR), 0)
    col2 = lax.broadcasted_iota(I32, (PAIR, PAIR), 1)
    same_head = (row2 < HEAD_DIM) == (col2 < HEAD_DIM)
    diag2 = row2 == col2
    zeros_cp = jnp.zeros((ch, PAIR), F32)

    def bdiag(y):
        return jnp.concatenate([jnp.where(first, y, 0.0), jnp.where(first, 0.0, y)], axis=0)

    ps = RWKV_PASSES
    units = []
    for bi in range(nbat):
        ld = ld_ref[bi]
        cum = _dot3(tril, ld)
        tot = cum[ch - 1:ch, :]
        a_t = -kk_ref[bi] * jnp.exp(cum - ld)
        pinv = jnp.exp(-cum)
        b_t = b_ref[bi] * pinv
        k_t = kp_ref[bi] * pinv
        r_t = r_ref[bi] * jnp.exp(cum)
        pend = jnp.exp(tot - cum)
        b_end = b_ref[bi] * pend
        k_end = kp_ref[bi] * pend
        p_tot = jnp.exp(tot)
        v_all = v_ref[bi]
        for p in range(npairs):
            sl = slice(p * PAIR, (p + 1) * PAIR)
            units.append(dict(h=bi * npairs + p, a=a_t[:, sl], r=r_t[:, sl], b=b_t[:, sl], k=k_t[:, sl],
                              v=v_all[:, sl], be=b_end[:, sl], ke=k_end[:, sl], pt=p_tot[:, sl]))

    for u in units:
        u["s8"] = _dotf(jnp.concatenate([u["a"], u["r"]], axis=0),
                        jnp.concatenate([bdiag(u["b"]), bdiag(u["k"])], axis=0), nt=True, passes=ps["s8"])
    for u in units:
        s8 = u.pop("s8")
        u["l"] = jnp.where(strict, s8[0:ch, 0:PAIR], 0.0)
        u["ak"] = jnp.where(strict, s8[0:ch, PAIR:], 0.0)
        u["mrb"] = jnp.where(incl, s8[ch:, 0:PAIR], 0.0)
        u["mrk"] = jnp.where(incl, s8[ch:, PAIR:], 0.0)
        u["t"] = eye + u["l"]
    for u in units:
        u["lk"] = _dotf(u["l"], bdiag(u["l"]), passes=ps["inv"])
        u["w1"] = _dotf(u["ak"], bdiag(u["v"]), passes=ps["w1"])
    for _ in range(max(ch.bit_length() - 3, 0)):
        for u in units:
            u["both"] = _dotf(jnp.concatenate([u["t"], u["lk"]], axis=0), bdiag(u["lk"]), passes=ps["inv"])
        for u in units:
            both = u.pop("both")
            u["t"] = u["t"] + both[0:ch]
            u["lk"] = both[ch:]
    for u in units:
        u["t"] = u["t"] + _dotf(u["t"], bdiag(u["lk"]), passes=ps["inv"])
    for u in units:
        u["au"] = _dotf(u["t"], jnp.concatenate([bdiag(u["a"]), bdiag(u["w1"])], axis=1), passes=ps["au"])
    for u in units:
        a_hat, u_loc = u["au"][:, 0:PAIR], u["au"][:, PAIR:]
        rhs = jnp.concatenate([
            jnp.concatenate([bdiag(a_hat), bdiag(u_loc)], axis=1),
            jnp.concatenate([jnp.zeros((2 * ch, PAIR), F32), bdiag(u["v"])], axis=1)], axis=0)
        u["ry"] = _dotf(jnp.concatenate([u["mrb"], u["mrk"]], axis=1), rhs, passes=ps["ry"])
        bkt = jnp.concatenate([u["be"], u["ke"]], axis=0).T
        u["gh"] = _dotf(bkt, jnp.concatenate([u["au"], jnp.concatenate([zeros_cp, u["v"]], axis=1)], axis=0),
                        passes=ps["gh"])
    for u in units:
        r_hat = u["r"] + u["ry"][:, 0:PAIR]
        g_mat = jnp.where(same_head, u["gh"][:, 0:PAIR], 0.0) + jnp.where(diag2, u["pt"], 0.0)
        u["yh"] = _dotf(jnp.concatenate([r_hat, g_mat], axis=0), h_ref[u["h"]], passes=ps["yh"])
    for u in units:
        h_ref[u["h"]] = u["yh"][ch:] + jnp.where(same_head, u["gh"][:, PAIR:], 0.0)
        u["y"] = u["yh"][0:ch] + u["ry"][:, PAIR:]

    inv_n = 1.0 / HEAD_DIM

    def head_mean(z):
        s0 = jnp.sum(jnp.where(first, z, 0.0), axis=1, keepdims=True)
        s1 = jnp.sum(jnp.where(first, 0.0, z), axis=1, keepdims=True)
        return jnp.where(first, s0, s1) * inv_n

    for u in units:
        yc = u["y"] - head_mean(u["y"])
        u["yn"] = yc * lax.rsqrt(head_mean(yc * yc) + GN_EPS)
    for bi in range(nbat):
        yn = jnp.concatenate([u["yn"] for u in units[bi * npairs:(bi + 1) * npairs]], axis=1)
        o_ref[bi] = (yn * gn_ref[0:1, :] + gn_ref[1:2, :] + bon_ref[bi]) * g_ref[bi]


def _rwkv(feats, bsz, seq, lnx_g, lnx_b, nbat):
    t, rw = feats[0].shape
    ch = RWKV_CHUNK
    gn = jnp.concatenate([lnx_g[None], lnx_b[None], jnp.zeros((SUBLANES - 2, rw), F32)], axis=0)
    tok = pl.BlockSpec((nbat, ch, rw), lambda bb, c: (bb, c, 0))
    const = lambda a: pl.BlockSpec(a.shape, lambda bb, c: (0, 0))
    out = pl.pallas_call(
        _rwkv_kernel,
        grid=(bsz // nbat, seq // ch),
        in_specs=[tok] * 8 + [const(gn)],
        out_specs=tok,
        out_shape=jax.ShapeDtypeStruct((bsz, seq, rw), F32),
        scratch_shapes=[pltpu.VMEM((nbat * (rw // PAIR), PAIR, PAIR), F32)],
        compiler_params=_cparams("arbitrary", "arbitrary"),
        name="rwkv",
    )(*[f.reshape(bsz, seq, rw) for f in feats], gn)
    return out.reshape(t, rw)


NEG_BIG = -1e30


def _attn_kernel(n_back, q_ref, kc_ref, kp_ref, vc_ref, vp_ref, o_ref, lse_ref):
    n = pl.program_id(2)
    qb = kp_ref.shape[0]
    nq = q_ref.shape[0] // qb
    qi = lax.broadcasted_iota(I32, (qb, 2 * qb), 0)
    ki = lax.broadcasted_iota(I32, (qb, 2 * qb), 1)
    dist = qb + qi - ki
    band = (dist >= 0) & (dist <= n_back)
    has_prev = jnp.where(n > 0, 0, qb)
    valid = [band & (ki >= has_prev)] + [band] * (nq - 1)
    first = lax.broadcasted_iota(I32, (qb, PAIR), 1) < HEAD_DIM
    npairs = q_ref.shape[1] // PAIR
    heads = []
    for u in range(nq):
        rows = slice(u * qb, (u + 1) * qb)
        for p in range(npairs):
            sl = slice(p * PAIR, (p + 1) * PAIR)
            q2 = q_ref[rows, sl]
            keys = jnp.concatenate([kp_ref[:, sl], kc_ref[:, sl]], axis=0)
            k2 = keys[u * qb:(u + 2) * qb]
            for hh in range(2):
                keep = first if hh == 0 else jnp.logical_not(first)
                s = _dot_nt(jnp.where(keep, q2, jnp.zeros_like(q2)), k2)
                heads.append(dict(u=u, rows=rows, sl=sl, s=s))
    for h in heads:
        ok = valid[h["u"]]
        s = jnp.where(ok, h.pop("s"), NEG_BIG)
        m = jnp.max(s, axis=1, keepdims=True)
        pe = jnp.where(ok, jnp.exp(s - m), 0.0)
        l = jnp.sum(pe, axis=1, keepdims=True)
        h["pe"] = pe.astype(BF16)
        h["l"] = l
        h["lse"] = m + jnp.log(l)
    for h in heads:
        sl, u = h["sl"], h["u"]
        vals = jnp.concatenate([vp_ref[:, sl], vc_ref[:, sl]], axis=0)
        h["o"] = _dot(h.pop("pe"), vals[u * qb:(u + 2) * qb]) / h["l"]
    for i in range(0, len(heads), 2):
        h0, h1 = heads[i], heads[i + 1]
        o_ref[h0["rows"], h0["sl"]] = jnp.where(first, h0["o"], h1["o"])
        lse_ref[h0["rows"], h0["sl"]] = jnp.where(first, h0["lse"], h1["lse"])


ATTN_QBLOCKS = 4


def _attention(q, k, v, bsz, seq, window, dilation):
    aw = q.shape[1] // dilation
    nq = min(ATTN_QBLOCKS, seq // dilation // Q_BLOCK)
    nb = seq // dilation // (nq * Q_BLOCK)
    cur = pl.BlockSpec((nq * Q_BLOCK, aw), lambda b, r, n: (b * nb + n, r))
    prv = pl.BlockSpec((Q_BLOCK, aw), lambda b, r, n: (jnp.maximum((b * nb + n) * nq - 1, b * nb * nq), r))
    shp = jax.ShapeDtypeStruct(q.shape, F32)
    return pl.pallas_call(
        functools.partial(_attn_kernel, window // dilation),
        grid=(bsz, dilation, nb),
        in_specs=[cur, cur, prv, cur, prv],
        out_specs=[cur, cur],
        out_shape=[shp, shp],
        compiler_params=_cparams("arbitrary", "arbitrary", "arbitrary"),
        name=f"attn_d{dilation}",
    )(q, k, k, v, v)


def _layer_norm(y, g, b):
    mu = jnp.mean(y, axis=-1, keepdims=True)
    yc = y - mu
    var = jnp.mean(yc * yc, axis=-1, keepdims=True)
    return yc * lax.rsqrt(var + LN_EPS) * g + b


def _outproj_kernel(dils, *refs):
    npat = len(dils)
    o_refs = refs[0:npat]
    lse_refs = refs[npat:2 * npat]
    rw_ref, x_ref, wo_ref, ln_ref, y_ref, yb_ref, scr = refs[2 * npat:]
    tm, aw = rw_ref.shape

    def token_major(ref, d, slot):
        if d == 1:
            return ref[...]
        nslab = aw // LANES
        for res in range(d):
            for j in range(nslab):
                scr[slot * nslab + j, pl.ds(res, tm // d, stride=d), :] = (
                    ref[:, res * aw + j * LANES:res * aw + (j + 1) * LANES])
        return jnp.concatenate([scr[slot * nslab + j] for j in range(nslab)], axis=1)

    lses = [token_major(r, d, 2 * i) for i, (r, d) in enumerate(zip(lse_refs, dils))]
    outs = [token_major(r, d, 2 * i + 1) for i, (r, d) in enumerate(zip(o_refs, dils))]
    m = functools.reduce(jnp.maximum, lses)
    es = [jnp.exp(z - m) for z in lses]
    den = functools.reduce(lambda a, b: a + b, es)
    attn = functools.reduce(lambda a, b: a + b, [(e / den) * o for e, o in zip(es, outs)])
    mix = _dot(attn.astype(BF16), wo_ref[0:aw, :]) + _dot(rw_ref[...].astype(BF16), wo_ref[aw:, :])
    y = _layer_norm(ALPHA * x_ref[...] + mix, ln_ref[0:1, :], ln_ref[1:2, :])
    y_ref[...] = y
    yb_ref[...] = y.astype(yb_ref.dtype)


def _outproj(os_, lses, dils, rw, x2, w_out, ln_g, ln_b, tm):
    t, d = x2.shape
    aw = rw.shape[1]
    ln = jnp.concatenate([ln_g[None], ln_b[None], jnp.zeros((SUBLANES - 2, d), F32)], axis=0)
    wo = w_out.astype(BF16)
    tok = lambda w: pl.BlockSpec((tm, w), lambda i: (i, 0))
    view = [pl.BlockSpec((tm // dl, dl * aw), lambda i: (i, 0)) for dl in dils]
    const = lambda a: pl.BlockSpec(a.shape, lambda i: (0, 0))
    return pl.pallas_call(
        functools.partial(_outproj_kernel, dils),
        grid=(t // tm,),
        in_specs=view + view + [tok(aw), tok(d), const(wo), const(ln)],
        out_specs=[tok(d), tok(d)],
        out_shape=[jax.ShapeDtypeStruct((t, d), F32), jax.ShapeDtypeStruct((t, d), BF16)],
        scratch_shapes=[pltpu.VMEM((2 * len(dils) * aw // LANES, tm, LANES), F32)],
        compiler_params=_cparams("arbitrary"),
        name="outproj",
    )(*os_, *lses, rw, x2, wo, ln)


def _lanes(col_rep, n):
    return jnp.concatenate([col_rep] * (n // LANES), axis=1)


MOE_TILE = 512
ROW_CHUNK = 8
SORT_ROWS = 512
COMBINE_ROWS = 1024
EXPERT_ROWS = 256


def _tile_rows(ne):
    raw = MOE_TILE * TOP_K + ne * ROW_CHUNK
    unit = max(SORT_ROWS, COMBINE_ROWS)
    return -(-raw // unit) * unit


def _route_kernel(x_ref, rwt_ref, bias_ref, tri_ref, ones_ref, low_ref, dloc_ref, gate_ref, tab_ref):
    ne = rwt_ref.shape[0]
    tm = x_ref.shape[0]
    gsz = ne // N_GROUPS
    ninf = -jnp.inf

    aff = _sigmoid(_dotf(rwt_ref[...], x_ref[...], nt=True))
    sel = aff + _lanes(bias_ref[...], tm)

    sel3 = sel.reshape(N_GROUPS, gsz, tm)
    rid = lax.broadcasted_iota(I32, sel3.shape, 1).astype(F32)
    m1 = jnp.max(sel3, axis=1, keepdims=True)
    i1 = jnp.min(jnp.where(sel3 == m1, rid, float(gsz)), axis=1, keepdims=True)
    m2 = jnp.max(jnp.where(rid == i1, ninf, sel3), axis=1, keepdims=True)
    gsc = (m1 + m2).reshape(N_GROUPS, tm)
    gid = lax.broadcasted_iota(I32, gsc.shape, 0).astype(F32)
    keep = jnp.zeros(gsc.shape, F32)
    for _ in range(TOPK_GROUPS):
        gm = jnp.max(gsc, axis=0, keepdims=True)
        gi = jnp.min(jnp.where(gsc == gm, gid, float(N_GROUPS)), axis=0, keepdims=True)
        hit = gid == gi
        keep = jnp.where(hit, 1.0, keep)
        gsc = jnp.where(hit, ninf, gsc)
    cand = jnp.where(keep.reshape(N_GROUPS, 1, tm) > 0.0, sel3, ninf).reshape(ne, tm)

    eid = lax.broadcasted_iota(I32, (ne, tm), 0).astype(F32)
    hits, graw = [], []
    for _ in range(TOP_K):
        m = jnp.max(cand, axis=0, keepdims=True)
        ij = jnp.min(jnp.where(cand == m, eid, float(ne)), axis=0, keepdims=True)
        hit = eid == ij
        hits.append(hit)
        graw.append(jnp.sum(jnp.where(hit, aff, 0.0), axis=0, keepdims=True))
        cand = jnp.where(hit, ninf, cand)
    gsum = functools.reduce(lambda a, b: a + b, graw)
    gate_ref[...] = jnp.concatenate([g / gsum * ROUTED_SCALE for g in graw], axis=0)

    onehot = functools.reduce(lambda a, b: a + b, [h.astype(F32) for h in hits]).astype(BF16)
    before = _dot(onehot, tri_ref[...])
    count = _dot(onehot, ones_ref[...])
    padded = jnp.maximum(jnp.ceil(count * (1.0 / ROW_CHUNK)) * ROW_CHUNK, ROW_CHUNK)
    start = _dot3(low_ref[...], padded)
    row = _lanes(start, tm) + before
    dloc_ref[...] = jnp.concatenate(
        [jnp.sum(jnp.where(h, row, 0.0), axis=0, keepdims=True) for h in hits], axis=0).astype(I32)
    lane = lax.broadcasted_iota(I32, padded.shape, 1)
    both = jnp.where(lane == 0, padded, jnp.where(lane == 1, start, 0.0))
    tab_ref[...] = both.T[0:SUBLANES, :].astype(I32)


def _route(x1, router_w, router_bias):
    t, d = x1.shape
    ne = router_w.shape[1]
    tm = MOE_TILE
    rwt = router_w.T
    bias = jnp.broadcast_to(router_bias[:, None], (ne, LANES))
    pos = jnp.arange(tm)
    tri = (pos[:, None] < pos[None, :]).astype(BF16)
    ones = jnp.ones((tm, LANES), BF16)
    eid = jnp.arange(ne)
    low = (eid[:, None] > eid[None, :]).astype(BF16)
    const = lambda a: pl.BlockSpec(a.shape, lambda i: (0, 0))
    tokt = pl.BlockSpec((TOP_K, tm), lambda i: (0, i))
    return pl.pallas_call(
        _route_kernel,
        grid=(t // tm,),
        in_specs=[pl.BlockSpec((tm, d), lambda i: (i, 0)), const(rwt), const(bias), const(tri), const(ones),
                  const(low)],
        out_specs=[tokt, tokt, pl.BlockSpec((None, SUBLANES, ne), lambda i: (i, 0, 0))],
        out_shape=[jax.ShapeDtypeStruct((TOP_K, t), I32), jax.ShapeDtypeStruct((TOP_K, t), F32),
                   jax.ShapeDtypeStruct((t // tm, SUBLANES, ne), I32)],
        compiler_params=_cparams("arbitrary"),
        name="route",
    )(x1, rwt, bias, tri, ones, low)


def _pack_bf16_pairs(v, exact=False):
    half = v.shape[1] // 2
    u = pltpu.bitcast(v if exact else v.astype(BF16).astype(F32), U32)
    return (u[:, 0:half] >> 16) | (u[:, half:] & jnp.uint32(0xFFFF0000))


def _unpack_bf16_pairs(w):
    lo = pltpu.bitcast(w << 16, F32)
    hi = pltpu.bitcast(w & jnp.uint32(0xFFFF0000), F32)
    return jnp.concatenate([lo, hi], axis=1)


def _dot3l(a, b_bf16):
    a1 = a.astype(BF16)
    r1 = a - a1.astype(F32)
    a2 = r1.astype(BF16)
    a3 = (r1 - a2.astype(F32)).astype(BF16)
    return _dot(a1, b_bf16) + (_dot(a2, b_bf16) + _dot(a3, b_bf16))


def _plan_kernel(nblk, cnt_ref, low_ref, upper_ref, gstart_ref, meta_ref, emeta_ref):
    c = cnt_ref[...].astype(F32)
    nt, ne = c.shape
    earlier = _dot3(low_ref[...], c)
    total = jnp.sum(c, axis=0, keepdims=True)
    padded = jnp.ceil(total * (1.0 / EXPERT_ROWS)) * EXPERT_ROWS
    pad_end = _dot3l(jnp.broadcast_to(padded, (SUBLANES, ne)), upper_ref[...])
    pad_start = (pad_end - padded)[0:1, :]
    gstart_ref[...] = (pad_start + earlier).astype(I32)
    emeta_ref[...] = jnp.concatenate(
        [pad_start, padded * (1.0 / EXPERT_ROWS), jnp.zeros((SUBLANES - 2, ne), F32)], axis=0).astype(I32)
    end_col = jnp.broadcast_to(pad_end.T[:, 0:1], (ne, nblk))
    blk_start = (lax.broadcasted_iota(I32, (ne, nblk), 1) * EXPERT_ROWS).astype(F32)
    owner = jnp.minimum(jnp.sum((end_col <= blk_start).astype(F32), axis=0, keepdims=True), float(ne - 1))
    used = jnp.max(end_col, axis=0, keepdims=True) * (1.0 / EXPERT_ROWS)
    meta_ref[...] = jnp.concatenate([owner, used, jnp.zeros((SUBLANES - 2, nblk), F32)], axis=0).astype(I32)


def _plan(cnt, nblk_pad):
    nt, ne = cnt.shape
    ti = jnp.arange(nt)
    low = (ti[:, None] > ti[None, :]).astype(BF16)
    ei = jnp.arange(ne)
    upper = (ei[:, None] <= ei[None, :]).astype(BF16)
    return pl.pallas_call(
        functools.partial(_plan_kernel, nblk_pad),
        out_shape=[jax.ShapeDtypeStruct((nt, ne), I32), jax.ShapeDtypeStruct((SUBLANES, nblk_pad), I32),
                   jax.ShapeDtypeStruct((SUBLANES, ne), I32)],
        compiler_params=pltpu.CompilerParams(vmem_limit_bytes=VMEM_LIMIT_BYTES),
        name="plan",
    )(cnt, low, upper)


def _group_copies(cnt_ref, loc_ref, gstart_ref, tile, ne, local, remote, sem, to_remote):
    def one(e, carry):
        n = pl.multiple_of(cnt_ref[tile * ne + e], ROW_CHUNK)
        lo = pl.multiple_of(loc_ref[tile * ne + e], ROW_CHUNK)
        go = pl.multiple_of(gstart_ref[tile * ne + e], ROW_CHUNK)
        a, b = local.at[pl.ds(lo, n), :], remote.at[pl.ds(go, n), :]
        (pltpu.make_async_copy(a, b, sem) if to_remote else pltpu.make_async_copy(b, a, sem)).start()
        return carry

    lax.fori_loop(0, ne, one, 0, unroll=8)


def _group_wait(rows, local, remote, sem, to_remote):
    n = pl.multiple_of(rows, ROW_CHUNK)

    @pl.when(n > 0)
    def _():
        a, b = local.at[pl.ds(0, n), :], remote.at[pl.ds(0, n), :]
        (pltpu.make_async_copy(a, b, sem) if to_remote else pltpu.make_async_copy(b, a, sem)).wait()


def _dispatch_kernel(ne, cnt_ref, loc_ref, gstart_ref, owner_ref, used_ref, urows_ref, dloc_ref, x_ref, xs_ref,
                     buf_ref, zero_ref, sems, zsem):
    step = pl.program_id(0)
    nsteps = pl.num_programs(0)
    slot = step % 2
    tm = x_ref.shape[0]
    nblk = owner_ref.shape[0]
    used = used_ref[0]

    @pl.when(step == 0)
    def _():
        zero_ref[...] = jnp.zeros_like(zero_ref)

        def block_copy(i):
            return pltpu.make_async_copy(zero_ref, xs_ref.at[pl.ds(i * EXPERT_ROWS, EXPERT_ROWS), :], zsem)

        def is_last(i):
            return (i == used - 1) | (owner_ref[jnp.minimum(i + 1, nblk - 1)] != owner_ref[i])

        def start(i, carry):
            @pl.when(is_last(i))
            def _():
                block_copy(i).start()
            return carry

        def wait(i, carry):
            @pl.when(is_last(i))
            def _():
                block_copy(i).wait()
            return carry

        lax.fori_loop(0, used, start, 0)
        lax.fori_loop(0, used, wait, 0)

    x = x_ref[...]
    dl = dloc_ref[...].astype(I16)
    riota = lax.broadcasted_iota(I32, (SORT_ROWS, tm), 0).astype(I16)
    one = jnp.ones((SORT_ROWS, tm), BF16)

    def chunk(c, carry):
        r0 = pl.multiple_of(c * SORT_ROWS, SORT_ROWS)
        r = riota + r0.astype(I16)
        p = jnp.zeros((SORT_ROWS, tm), BF16)
        for j in range(TOP_K):
            p = jnp.where(r == dl[j:j + 1, :], one, p)
        buf_ref[slot, pl.ds(r0, SORT_ROWS), :] = _pack_bf16_pairs(_dot(p, x), exact=True)
        return carry

    lax.fori_loop(0, (urows_ref[step] + SORT_ROWS - 1) // SORT_ROWS, chunk, 0)

    copies = functools.partial(_group_copies, cnt_ref, loc_ref, gstart_ref)
    copies(step, ne, buf_ref.at[slot], xs_ref, sems.at[slot], True)

    @pl.when(step > 0)
    def _():
        _group_wait(urows_ref[jnp.maximum(step - 1, 0)], buf_ref.at[1 - slot], xs_ref, sems.at[1 - slot], True)

    @pl.when(step == nsteps - 1)
    def _():
        _group_wait(urows_ref[step], buf_ref.at[slot], xs_ref, sems.at[slot], True)


def _dispatch(x1b, dloc, cnt, loc, gstart, owner, used, used_rows, nrows):
    t, d = x1b.shape
    nt, ne = cnt.shape
    tm = MOE_TILE
    rt = _tile_rows(ne)
    grid_spec = pltpu.PrefetchScalarGridSpec(
        num_scalar_prefetch=6,
        grid=(nt,),
        in_specs=[pl.BlockSpec((TOP_K, tm), lambda i, *_: (0, i)), pl.BlockSpec((tm, d), lambda i, *_: (i, 0))],
        out_specs=pl.BlockSpec(memory_space=pl.ANY),
        scratch_shapes=[pltpu.VMEM((2, rt, d // 2), U32), pltpu.VMEM((EXPERT_ROWS, d // 2), U32),
                        pltpu.SemaphoreType.DMA((2,)), pltpu.SemaphoreType.DMA(())],
    )
    return pl.pallas_call(
        functools.partial(_dispatch_kernel, ne),
        grid_spec=grid_spec,
        out_shape=jax.ShapeDtypeStruct((nrows, d // 2), U32),
        compiler_params=_cparams("arbitrary"),
        name="dispatch",
    )(cnt.reshape(-1), loc.reshape(-1), gstart.reshape(-1), owner, used, used_rows, dloc, x1b)


EXPERT_PIECE = 1024


def _expert_kernel(first_ref, nblk_ref, xs_ref, wg_ref, wu_ref, wd_ref, ys_ref,
                   xbuf, ybuf, wgb_ref, wub_ref, wdb_ref, sem_in, sem_out, done_ref, pend_ref):
    e = pl.program_id(0)
    last = pl.num_programs(0) - 1
    per_piece = EXPERT_PIECE // EXPERT_ROWS

    def pieces(ex):
        return (nblk_ref[ex] + per_piece - 1) // per_piece

    def span(ex, s):
        n = jnp.minimum(EXPERT_PIECE, nblk_ref[ex] * EXPERT_ROWS - s * EXPERT_PIECE)
        n = pl.multiple_of(n, EXPERT_ROWS)
        return pl.ds(pl.multiple_of(first_ref[ex] + s * EXPERT_PIECE, EXPERT_ROWS), n), pl.ds(0, n)

    def in_copy(ex, s, slot):
        far, near = span(ex, s)
        return pltpu.make_async_copy(xs_ref.at[far, :], xbuf.at[slot, near, :], sem_in.at[slot])

    def out_copy(ex, s, slot):
        far, near = span(ex, s)
        return pltpu.make_async_copy(ybuf.at[slot, near, :], ys_ref.at[far, :], sem_out.at[slot])

    @pl.when(e == 0)
    def _():
        done_ref[0] = 0
        for slot in range(2):
            pend_ref[2 * slot] = -1

        @pl.when(nblk_ref[0] > 0)
        def _():
            in_copy(0, 0, 0).start()

    nb = nblk_ref[e]
    ns = pieces(e)
    g0 = done_ref[0]
    nxt = jnp.minimum(e + 1, last)
    has_next = (e < last) & (nblk_ref[nxt] > 0)

    def release(slot):
        pe = pend_ref[2 * slot]

        @pl.when(pe >= 0)
        def _():
            out_copy(pe, pend_ref[2 * slot + 1], slot).wait()

    @pl.when(nb > 0)
    def _():
        wgb_ref[...] = wg_ref[...].astype(BF16)
        wub_ref[...] = wu_ref[...].astype(BF16)
        wdb_ref[...] = wd_ref[...].astype(BF16)

        def piece(s, carry):
            slot = (g0 + s) % 2

            @pl.when(s + 1 < ns)
            def _():
                in_copy(e, s + 1, 1 - slot).start()

            @pl.when((s + 1 == ns) & has_next)
            def _():
                in_copy(nxt, 0, 1 - slot).start()

            in_copy(e, s, slot).wait()
            release(slot)

            blocks_here = jnp.minimum(per_piece, nb - s * per_piece)
            for nblocks in range(1, per_piece + 1):
                @pl.when(blocks_here == nblocks)
                def _():
                    rows = pl.ds(0, nblocks * EXPERT_ROWS)
                    xb = _unpack_bf16_pairs(xbuf[slot, rows, :]).astype(BF16)
                    hg = _dot(xb, wgb_ref[...])
                    hb = hg * _sigmoid(hg) * _dot(xb, wub_ref[...])
                    ybuf[slot, rows, :] = _pack_bf16_pairs(_dot(hb.astype(BF16), wdb_ref[...]))
            out_copy(e, s, slot).start()
            pend_ref[2 * slot] = e
            pend_ref[2 * slot + 1] = s
            return carry

        lax.fori_loop(0, ns, piece, 0)

    @pl.when((nb == 0) & has_next)
    def _():
        in_copy(nxt, 0, g0 % 2).start()

    @pl.when(e == last)
    def _():
        for slot in range(2):
            release(slot)

    done_ref[0] = g0 + ns


def _experts(xs, first_row, nblocks, e_gate, e_up, e_down):
    nrows, dh = xs.shape
    ne, d, ff = e_gate.shape
    wsel = lambda e, *_: (e, 0, 0)
    grid_spec = pltpu.PrefetchScalarGridSpec(
        num_scalar_prefetch=2,
        grid=(ne,),
        in_specs=[pl.BlockSpec(memory_space=pl.ANY),
                  pl.BlockSpec((None, d, ff), wsel), pl.BlockSpec((None, d, ff), wsel),
                  pl.BlockSpec((None, ff, d), wsel)],
        out_specs=pl.BlockSpec(memory_space=pl.ANY),
        scratch_shapes=[pltpu.VMEM((2, EXPERT_PIECE, dh), U32), pltpu.VMEM((2, EXPERT_PIECE, dh), U32),
                        pltpu.VMEM((d, ff), BF16), pltpu.VMEM((d, ff), BF16), pltpu.VMEM((ff, d), BF16),
                        pltpu.SemaphoreType.DMA((2,)), pltpu.SemaphoreType.DMA((2,)),
                        pltpu.SMEM((1,), I32), pltpu.SMEM((4,), I32)],
    )
    return pl.pallas_call(
        _expert_kernel,
        grid_spec=grid_spec,
        out_shape=jax.ShapeDtypeStruct((nrows, dh), U32),
        compiler_params=_cparams("arbitrary"),
        name="experts",
    )(first_row, nblocks, xs, e_gate, e_up, e_down)


def _final_kernel(ne, cnt_ref, loc_ref, gstart_ref, used_ref, dloc_ref, gate_ref, x_ref, xb_ref, ys_ref,
                  sg_ref, su_ref, sd_ref, ln_ref, o_ref, buf_ref, acc_ref, sems):
    step = pl.program_id(0)
    nsteps = pl.num_programs(0)
    slot = step % 2
    tm = x_ref.shape[0]
    used = used_ref[step]
    copies = functools.partial(_group_copies, cnt_ref, loc_ref, gstart_ref)

    @pl.when(step == 0)
    def _():
        copies(step, ne, buf_ref.at[0], ys_ref, sems.at[0], False)

    @pl.when(step + 1 < nsteps)
    def _():
        copies(step + 1, ne, buf_ref.at[1 - slot], ys_ref, sems.at[1 - slot], False)

    xb = xb_ref[...]
    hg = _dot(xb, sg_ref[...])
    hs = hg * _sigmoid(hg) * _dot(xb, su_ref[...])
    acc_ref[...] = _dot(hs.astype(BF16), sd_ref[...])

    _group_wait(used, buf_ref.at[slot], ys_ref, sems.at[slot], False)

    dl = dloc_ref[...].astype(I16)
    gt = gate_ref[...].astype(BF16)
    rows16 = lax.broadcasted_iota(I32, (COMBINE_ROWS, tm), 0).astype(I16)
    gts = [jnp.broadcast_to(gt[j:j + 1, :], (COMBINE_ROWS, tm)) for j in range(TOP_K)]
    riota = lax.broadcasted_iota(I32, (COMBINE_ROWS, buf_ref.shape[2]), 0)

    def chunk(c, partial):
        r0 = pl.multiple_of(c * COMBINE_ROWS, COMBINE_ROWS)
        r = rows16 + r0.astype(I16)
        g = jnp.zeros((COMBINE_ROWS, tm), BF16)
        for j in range(TOP_K):
            g = jnp.where(r == dl[j:j + 1, :], gts[j], g)
        w = buf_ref[slot, pl.ds(r0, COMBINE_ROWS), :]
        if partial:
            w = jnp.where(riota + r0 < used, w, jnp.zeros_like(w))
        y = _unpack_bf16_pairs(w).astype(BF16)
        acc_ref[...] += lax.dot_general(g, y, (((0,), (0,)), ((), ())), preferred_element_type=F32)

    nfull = used // COMBINE_ROWS

    def full_chunk(c, carry):
        chunk(c, False)
        return carry

    lax.fori_loop(0, nfull, full_chunk, 0)

    @pl.when(used > nfull * COMBINE_ROWS)
    def _():
        chunk(nfull, True)
    o_ref[...] = _layer_norm(ALPHA * x_ref[...] + acc_ref[...], ln_ref[0:1, :], ln_ref[1:2, :])


def _final(x1, x1b, dloc, gate, cnt, loc, gstart, used_rows, ys, s_gate, s_up, s_down, ln_g, ln_b):
    t, d = x1.shape
    nt, ne = cnt.shape
    tm = MOE_TILE
    rt = _tile_rows(ne)
    ln = jnp.concatenate([ln_g[None], ln_b[None], jnp.zeros((SUBLANES - 2, d), F32)], axis=0)
    sg, su, sd = s_gate.astype(BF16), s_up.astype(BF16), s_down.astype(BF16)
    tok = pl.BlockSpec((tm, d), lambda i, *_: (i, 0))
    tokt = pl.BlockSpec((TOP_K, tm), lambda i, *_: (0, i))
    const = lambda a: pl.BlockSpec(a.shape, lambda i, *_: (0, 0))
    grid_spec = pltpu.PrefetchScalarGridSpec(
        num_scalar_prefetch=4,
        grid=(nt,),
        in_specs=[tokt, tokt, tok, tok, pl.BlockSpec(memory_space=pl.ANY), const(sg), const(su), const(sd), const(ln)],
        out_specs=tok,
        scratch_shapes=[pltpu.VMEM((2, rt, d // 2), U32), pltpu.VMEM((tm, d), F32), pltpu.SemaphoreType.DMA((2,))],
    )
    return pl.pallas_call(
        functools.partial(_final_kernel, ne),
        grid_spec=grid_spec,
        out_shape=jax.ShapeDtypeStruct((t, d), F32),
        compiler_params=_cparams("arbitrary"),
        name="final",
    )(cnt.reshape(-1), loc.reshape(-1), gstart.reshape(-1), used_rows, dloc, gate, x1, x1b, ys, sg, su, sd, ln)


def _moe(x1, x1b, router_w, router_bias, e_gate, e_up, e_down, s_gate, s_up, s_down, ln_g, ln_b):
    t = x1.shape[0]
    ne = router_w.shape[1]
    nt = t // MOE_TILE
    nblk = -(-(nt * _tile_rows(ne)) // EXPERT_ROWS) + ne
    nblk_pad = -(-nblk // LANES) * LANES
    dloc, gate, tab = _route(x1, router_w, router_bias)
    cnt, loc = tab[:, 0, :], tab[:, 1, :]
    used_rows = loc[:, ne - 1] + cnt[:, ne - 1]
    gstart, meta, emeta = _plan(cnt, nblk_pad)
    owner, used = meta[0], meta[1, 0:1]
    xs = _dispatch(x1b, dloc, cnt, loc, gstart, owner, used, used_rows, nblk * EXPERT_ROWS)
    ys = _experts(xs, emeta[0], emeta[1], e_gate, e_up, e_down)
    return _final(x1, x1b, dloc, gate, cnt, loc, gstart, used_rows, ys, s_gate, s_up, s_down, ln_g, ln_b)


def kernel(x, w_in, mu_shift, w0, w_decay_up, a0, w_aaa_up, w_gate_up, k_k, k_a, r_k, lnx_g, lnx_b, w_out,
           ln1_g, ln1_b, router_w, router_bias, e_gate, e_up, e_down, s_gate, s_up, s_down, ln2_g, ln2_b):
    bsz, seq, d = x.shape
    x2 = x.reshape(bsz * seq, d)
    qkvs, feats = _inproj(x2, seq, w_in[0], mu_shift[0], w0[0], w_decay_up[0], a0[0], w_aaa_up[0], w_gate_up[0],
                          k_k[0], k_a[0], r_k[0], tm=256)
    rw = _rwkv(feats, bsz, seq, lnx_g[0], lnx_b[0], nbat=8)
    x1, x1b = _mixer_tail(qkvs, rw, x2, bsz, seq, w_out[0], ln1_g[0], ln1_b[0])
    out = _moe(x1, x1b, router_w[0], router_bias[0], e_gate[0], e_up[0], e_down[0], s_gate[0], s_up[0], s_down[0],
               ln2_g[0], ln2_b[0])
    return out.reshape(bsz, seq, d)


def _mixer_tail(qkvs, rw, x2, bsz, seq, w_out, ln_g, ln_b):
    res = [_attention(q, k, v, bsz, seq, window, dilation)
           for (q, k, v), (window, dilation) in zip(qkvs, ATTN_PATTERNS)]
    dils = tuple(dl for _, dl in ATTN_PATTERNS)
    return _outproj([o for o, _ in res], [l for _, l in res], dils, rw, x2, w_out, ln_g, ln_b, tm=512)
```
